```python
import jax, jax.numpy as jnp
from jax import lax
import numpy as np

D_MODEL = 1024
BATCH = 1
SEQ = 16384
DEPTH = 1
DEC_BATCH = 128
DEC_SEQ = 1
PAST_LEN = 16384
PAGE_SIZE = 128

D_CONV = D_MODEL
CONV_W = 3
N_HEADS = 16
N_KV = 4
GROUP = N_HEADS // N_KV
HEAD_DIM = 64
WINDOW = 128
BLOCK = 128
D_FF = 2816
EPS = 1e-6
N_MOD = 6
ATTN_SCALE = HEAD_DIM ** -0.5
SPLIT_SIZES = (D_CONV, D_CONV, D_CONV, N_HEADS * HEAD_DIM, N_KV * HEAD_DIM, N_KV * HEAD_DIM, D_MODEL, D_MODEL)
IN_COLS = sum(SPLIT_SIZES)
SPLIT_POINTS = tuple(int(v) for v in np.cumsum(SPLIT_SIZES)[:-1])

kernel_name = 'adaln_hybrid_shortconv_swa_sink_convffn_step'


def rmsnorm(x, g):
    xf = x.astype(jnp.float32)
    y = xf * lax.rsqrt(jnp.mean(xf * xf, axis=-1, keepdims=True) + EPS)
    return (y * g.astype(jnp.float32)).astype(x.dtype)


def causal_dwconv(prev, u, w):
    t = u.shape[1]
    full = jnp.concatenate([prev.astype(u.dtype), u], axis=1)
    out = sum(w[i] * full[:, i:i + t] for i in range(CONV_W))
    return out, full[:, -(CONV_W - 1):]


def sink_softmax(s, mask, sink):
    s = jnp.where(mask, s, -jnp.inf)
    m = jnp.maximum(jnp.max(s, axis=-1, keepdims=True), sink)
    p = jnp.exp(s - m)
    return p / (jnp.sum(p, axis=-1, keepdims=True) + jnp.exp(sink - m))


def window_attn_prompt(q, k, v, sinks):
    n, s = q.shape[:2]
    nb = s // BLOCK
    qb = q.reshape(n, nb, BLOCK, N_KV, GROUP, HEAD_DIM)
    kb = k.reshape(n, nb, BLOCK, N_KV, HEAD_DIM)
    vb = v.reshape(n, nb, BLOCK, N_KV, HEAD_DIM)
    pad = ((0, 0), (1, 0), (0, 0), (0, 0), (0, 0))
    kk = jnp.concatenate([jnp.pad(kb, pad)[:, :-1], kb], axis=2)
    vv = jnp.concatenate([jnp.pad(vb, pad)[:, :-1], vb], axis=2)
    d = (jnp.arange(BLOCK) + BLOCK)[:, None] - jnp.arange(2 * BLOCK)[None, :]
    band = (d >= 0) & (d <= WINDOW)
    real = (jnp.arange(nb)[:, None] > 0) | (jnp.arange(2 * BLOCK)[None, :] >= BLOCK)
    mask = band[None] & real[:, None, :]
    sc = jnp.einsum('bnqkgd,bnskd->bnkgqs', qb, kk, preferred_element_type=jnp.float32) * ATTN_SCALE
    sink = sinks.astype(jnp.float32).reshape(N_KV, GROUP)[None, None, :, :, None, None]
    p = sink_softmax(sc, mask[None, :, None, None], sink)
    o = jnp.einsum('bnkgqs,bnskd->bnqkgd', p.astype(vv.dtype), vv)
    return o.reshape(n, s, N_HEADS, HEAD_DIM), k[:, -WINDOW:], v[:, -WINDOW:]


def window_attn_decode(q, k, v, k_prev, v_prev, sinks):
    n, t = q.shape[:2]
    r = k_prev.shape[1]
    kk = jnp.concatenate([k_prev.astype(k.dtype), k], axis=1)
    vv = jnp.concatenate([v_prev.astype(v.dtype), v], axis=1)
    qg = q.reshape(n, t, N_KV, GROUP, HEAD_DIM)
    sc = jnp.einsum('btkgd,bskd->bkgts', qg, kk, preferred_element_type=jnp.float32) * ATTN_SCALE
    d = (r + jnp.arange(t))[:, None] - jnp.arange(r + t)[None, :]
    mask = (d >= 0) & (d <= WINDOW)
    sink = sinks.astype(jnp.float32).reshape(N_KV, GROUP)[None, :, :, None, None]
    p = sink_softmax(sc, mask, sink)
    o = jnp.einsum('bkgts,bskd->btkgd', p.astype(vv.dtype), vv)
    return o.reshape(n, t, N_HEADS, HEAD_DIM), kk[:, -r:], vv[:, -r:]


def decoder_layer(x, c, conv_a_prev, k_prev, v_prev, ffn_prev, w_ada, b_ada, g_mix, w_in, conv_a_w,
                  sinks, w_a_out, w_b_out, w_o, g_ffn, w_up, ffn_conv_w, ffn_conv_b, w_down):
    n, t, _ = x.shape
    ada = jax.nn.silu(c) @ w_ada + b_ada
    sh1, sc1, gt1, sh2, sc2, gt2 = jnp.split(ada[:, None, :], N_MOD, axis=-1)
    h = rmsnorm(x, g_mix) * (1 + sc1) + sh1
    proj = h @ w_in
    xin, b_gate, c_gate, q, k, v, ga, gb = jnp.split(proj, SPLIT_POINTS, axis=-1)
    conv_out, conv_a_new = causal_dwconv(conv_a_prev, c_gate * xin, conv_a_w)
    y_a = (b_gate * conv_out) @ w_a_out
    q = q.reshape(n, t, N_HEADS, HEAD_DIM)
    k = k.reshape(n, t, N_KV, HEAD_DIM)
    v = v.reshape(n, t, N_KV, HEAD_DIM)
    if k_prev is None:
        att, k_new, v_new = window_attn_prompt(q, k, v, sinks)
    else:
        att, k_new, v_new = window_attn_decode(q, k, v, k_prev, v_prev, sinks)
    y_b = att.reshape(n, t, N_HEADS * HEAD_DIM) @ w_b_out
    x = x + gt1 * ((jax.nn.sigmoid(ga) * y_a + jax.nn.sigmoid(gb) * y_b) @ w_o)
    h = rmsnorm(x, g_ffn) * (1 + sc2) + sh2
    up, ffn_new = causal_dwconv(ffn_prev, h @ w_up, ffn_conv_w)
    a_g, a_v = jnp.split(up + ffn_conv_b, 2, axis=-1)
    x = x + gt2 * ((jax.nn.silu(a_g) * a_v) @ w_down)
    return x, conv_a_new, k_new, v_new, ffn_new


def setup_inputs(seed: int = 0) -> dict:
    key = jax.random.key(seed)
    ks = jax.random.split(key, 24)
    f32 = jnp.float32

    def nrm(k, shape, scale):
        return jax.random.normal(k, shape, f32) * scale

    rows = min(WINDOW, PAST_LEN)
    return {
        'x_prompt': nrm(ks[0], (BATCH, SEQ, D_MODEL), 1.0),
        'x_sample': nrm(ks[1], (DEC_BATCH, DEC_SEQ, D_MODEL), 1.0),
        'c_prompt': nrm(ks[2], (BATCH, D_MODEL), 1.0),
        'c_sample': nrm(ks[3], (DEC_BATCH, D_MODEL), 1.0),
        'state_conv_a': nrm(ks[4], (DEPTH, DEC_BATCH, CONV_W - 1, D_CONV), 1.0),
        'cache_k_win': nrm(ks[5], (DEPTH, DEC_BATCH, rows, N_KV, HEAD_DIM), 1.0),
        'cache_v_win': nrm(ks[6], (DEPTH, DEC_BATCH, rows, N_KV, HEAD_DIM), 1.0),
        'state_ffn_conv': nrm(ks[7], (DEPTH, DEC_BATCH, CONV_W - 1, 2 * D_FF), 1.0),
        'w_ada': nrm(ks[8], (DEPTH, D_MODEL, N_MOD * D_MODEL), D_MODEL ** -0.5),
        'b_ada': nrm(ks[9], (DEPTH, N_MOD * D_MODEL), 0.02),
        'g_mix': 1.0 + nrm(ks[10], (DEPTH, D_MODEL), 0.02),
        'w_in': nrm(ks[11], (DEPTH, D_MODEL, IN_COLS), D_MODEL ** -0.5),
        'conv_a_w': nrm(ks[12], (DEPTH, CONV_W, D_CONV), CONV_W ** -0.5),
        'attn_sinks': nrm(ks[13], (DEPTH, N_HEADS), 1.0),
        'w_a_out': nrm(ks[14], (DEPTH, D_CONV, D_MODEL), D_CONV ** -0.5),
        'w_b_out': nrm(ks[15], (DEPTH, N_HEADS * HEAD_DIM, D_MODEL), (N_HEADS * HEAD_DIM) ** -0.5),
        'w_o': nrm(ks[16], (DEPTH, D_MODEL, D_MODEL), D_MODEL ** -0.5),
        'g_ffn': 1.0 + nrm(ks[17], (DEPTH, D_MODEL), 0.02),
        'w_up': nrm(ks[18], (DEPTH, D_MODEL, 2 * D_FF), D_MODEL ** -0.5),
        'ffn_conv_w': nrm(ks[19], (DEPTH, CONV_W, 2 * D_FF), CONV_W ** -0.5),
        'ffn_conv_b': nrm(ks[20], (DEPTH, 2 * D_FF), 0.01),
        'w_down': nrm(ks[21], (DEPTH, D_FF, D_MODEL), D_FF ** -0.5),
        'g_final': 1.0 + nrm(ks[22], (D_MODEL,), 0.02),
    }


def reference(x_prompt, x_sample, c_prompt, c_sample, state_conv_a, cache_k_win, cache_v_win, state_ffn_conv,
              w_ada, b_ada, g_mix, w_in, conv_a_w, attn_sinks, w_a_out, w_b_out, w_o, g_ffn, w_up,
              ffn_conv_w, ffn_conv_b, w_down, g_final):
    xp, xs = x_prompt, x_sample
    nb_p = x_prompt.shape[0]
    ca_p, ca_s, kp, ks_, vp, vs, fp, fs = [], [], [], [], [], [], [], []
    for l in range(DEPTH):
        w = (w_ada[l], b_ada[l], g_mix[l], w_in[l], conv_a_w[l], attn_sinks[l], w_a_out[l], w_b_out[l],
             w_o[l], g_ffn[l], w_up[l], ffn_conv_w[l], ffn_conv_b[l], w_down[l])
        zero_a = jnp.zeros((nb_p, CONV_W - 1, D_CONV), xp.dtype)
        zero_f = jnp.zeros((nb_p, CONV_W - 1, 2 * D_FF), xp.dtype)
        xp, a1, k1, v1, f1 = decoder_layer(xp, c_prompt, zero_a, None, None, zero_f, *w)
        xs, a2, k2, v2, f2 = decoder_layer(xs, c_sample, state_conv_a[l], cache_k_win[l], cache_v_win[l],
                                           state_ffn_conv[l], *w)
        ca_p.append(a1); ca_s.append(a2); kp.append(k1); ks_.append(k2)
        vp.append(v1); vs.append(v2); fp.append(f1); fs.append(f2)
    y_prompt = rmsnorm(xp, g_final)
    y_sample = rmsnorm(xs, g_final)
    return (y_prompt, y_sample, jnp.stack(ca_p), jnp.stack(ca_s), jnp.stack(kp), jnp.stack(ks_),
            jnp.stack(vp), jnp.stack(vs), jnp.stack(fp), jnp.stack(fs))
```

```python
import functools

import jax
import jax.numpy as jnp
from jax import lax
from jax.experimental import pallas as pl
from jax.experimental.pallas import tpu as pltpu

f32 = jnp.float32
bf16 = jnp.bfloat16

D_MODEL = 1024
D_CONV = D_MODEL
CONV_W = 3
N_HEADS = 16
N_KV = 4
GROUP = N_HEADS // N_KV
HEAD_DIM = 64
WINDOW = 128
D_FF = 2816
EPS = 1e-6
N_MOD = 6
ATTN_SCALE = HEAD_DIM ** -0.5
KV_COLS = N_KV * HEAD_DIM
C_XIN, C_B, C_C = 0, D_CONV, 2 * D_CONV
C_Q = 3 * D_CONV
C_K = C_Q + N_HEADS * HEAD_DIM
C_V = C_K + KV_COLS
C_GA = C_V + KV_COLS
C_GB = C_GA + D_MODEL
IN_COLS = C_GB + D_MODEL

LANES = 128
SUBLANES = 8
Q_SUB = 128
HEAD_PAIRS = N_HEADS // 2
VMEM_BYTES_V7X = 64 * 1024 * 1024


def _rms(x, g):
    ms = jnp.mean(x * x, axis=-1, keepdims=True)
    return x * lax.rsqrt(ms + EPS) * g


def _silu(x):
    return x * jax.nn.sigmoid(x)


def _dot(a, b):
    return jnp.dot(a, b, preferred_element_type=f32)


def _const_spec(shape):
    nd = len(shape)
    return pl.BlockSpec(shape, lambda i: (0,) * nd, pipeline_mode=pl.Buffered(1))


def _ada_kernel(c_ref, w_ref, b_ref, o_ref):
    c = c_ref[...]
    o_ref[...] = _dot(_silu(c).astype(bf16), w_ref[...].astype(bf16)) + b_ref[...]


def _ada(c_all, w_ada, b_ada):
    rows = c_all.shape[0]
    return pl.pallas_call(
        _ada_kernel,
        grid=(N_MOD,),
        in_specs=[
            pl.BlockSpec((rows, D_MODEL), lambda j: (0, 0)),
            pl.BlockSpec((D_MODEL, D_MODEL), lambda j: (0, j)),
            pl.BlockSpec((1, D_MODEL), lambda j: (0, j)),
        ],
        out_specs=pl.BlockSpec((rows, D_MODEL), lambda j: (0, j)),
        out_shape=jax.ShapeDtypeStruct((rows, N_MOD * D_MODEL), f32),
        compiler_params=pltpu.CompilerParams(dimension_semantics=("arbitrary",)),
        name="ada",
    )(c_all, w_ada, b_ada.reshape(1, -1))


def _mixer_prompt_kernel(sinks_ref, x_ref, ada_ref, g_ref, w_in_ref, cw_ref, w_a_ref, w_b_ref, w_o_ref,
                         x1_ref, conv_ref, knew_ref, vnew_ref,
                         ubuf, klo, khi, vlo, vhi, attbuf, *, tb):
    i = pl.program_id(0)

    @pl.when(i == 0)
    def _():
        ubuf[0:SUBLANES, :] = jnp.zeros((SUBLANES, D_CONV), f32)
        for r in (klo, khi, vlo, vhi):
            r[:, 0:WINDOW, :] = jnp.zeros((N_KV, WINDOW, LANES), bf16)

    x = x_ref[...]
    sh1, sc1, gt1 = ada_ref[0:1, :], ada_ref[1:2, :], ada_ref[2:3, :]
    h = (_rms(x, g_ref[...]) * (1.0 + sc1) + sh1).astype(bf16)

    def proj(c0, n):
        return _dot(h, w_in_ref[:, c0:c0 + n])

    u = proj(C_C, D_CONV) * proj(C_XIN, D_CONV)
    ubuf[SUBLANES:SUBLANES + tb, :] = u
    conv = (cw_ref[0:1, :] * ubuf[SUBLANES - 2:SUBLANES - 2 + tb, :]
            + cw_ref[1:2, :] * ubuf[SUBLANES - 1:SUBLANES - 1 + tb, :]
            + cw_ref[2:3, :] * u)
    ya = _dot((proj(C_B, D_CONV) * conv).astype(bf16), w_a_ref[...])

    q = (proj(C_Q, N_HEADS * HEAD_DIM) * ATTN_SCALE).astype(bf16)
    k = proj(C_K, KV_COLS)
    v = proj(C_V, KV_COLS)
    knew_ref[...] = k[tb - WINDOW:, :]
    vnew_ref[...] = v[tb - WINDOW:, :]

    lo = lax.broadcasted_iota(jnp.int32, (tb, LANES), 1) < HEAD_DIM
    for j in range(N_KV // 2):
        for src, dlo, dhi in ((k, klo, khi), (v, vlo, vhi)):
            pair = src[:, LANES * j:LANES * (j + 1)]
            rolled = pltpu.roll(pair, HEAD_DIM, axis=1)
            zero = jnp.zeros_like(pair)
            dlo[2 * j, WINDOW:WINDOW + tb, :] = jnp.where(lo, pair, zero).astype(bf16)
            dhi[2 * j, WINDOW:WINDOW + tb, :] = jnp.where(lo, zero, rolled).astype(bf16)
            dlo[2 * j + 1, WINDOW:WINDOW + tb, :] = jnp.where(lo, rolled, zero).astype(bf16)
            dhi[2 * j + 1, WINDOW:WINDOW + tb, :] = jnp.where(lo, zero, pair).astype(bf16)

    rr = lax.broadcasted_iota(jnp.int32, (Q_SUB, Q_SUB + WINDOW), 0)
    cc = lax.broadcasted_iota(jnp.int32, (Q_SUB, Q_SUB + WINDOW), 1)
    band = (cc >= rr) & (cc <= rr + WINDOW)
    lo_q = lax.broadcasted_iota(jnp.int32, (Q_SUB, LANES), 1) < HEAD_DIM
    for j in range(tb // Q_SUB):
        r0 = Q_SUB * j
        mask = band & (cc >= jnp.where(i == 0, WINDOW, 0)) if j == 0 else band
        for p in range(HEAD_PAIRS):
            g = (2 * p) // GROUP
            qp = q[r0:r0 + Q_SUB, LANES * p:LANES * (p + 1)]
            o = None
            linv = []
            for t, (kr, vr) in enumerate(((klo, vlo), (khi, vhi))):
                sink = sinks_ref[2 * p + t]
                s = lax.dot_general(qp, kr[g, r0:r0 + Q_SUB + WINDOW, :], (((1,), (1,)), ((), ())),
                                    preferred_element_type=f32)
                s = jnp.where(mask, s, -jnp.inf)
                m = jnp.maximum(jnp.max(s, axis=-1, keepdims=True), sink)
                e = jnp.exp(s - m)
                linv.append(1.0 / (jnp.sum(e, axis=-1, keepdims=True) + jnp.exp(sink - m)))
                pv = _dot(e.astype(bf16), vr[g, r0:r0 + Q_SUB + WINDOW, :])
                o = pv if o is None else o + pv
            attbuf[r0:r0 + Q_SUB, LANES * p:LANES * (p + 1)] = (o * jnp.where(lo_q, linv[0], linv[1])).astype(bf16)

    yb = _dot(attbuf[...], w_b_ref[...])
    mix = jax.nn.sigmoid(proj(C_GA, D_MODEL)) * ya + jax.nn.sigmoid(proj(C_GB, D_MODEL)) * yb
    x1_ref[...] = x + gt1 * _dot(mix.astype(bf16), w_o_ref[...])

    conv_ref[...] = ubuf[SUBLANES + tb - (CONV_W - 1):SUBLANES + tb, :]
    ubuf[0:SUBLANES, :] = ubuf[tb:tb + SUBLANES, :]
    for r in (klo, khi, vlo, vhi):
        r[:, 0:WINDOW, :] = r[:, tb:tb + WINDOW, :]


def _mixer_prompt(x, ada8, g_mix, w_in, conv_w, sinks, w_a, w_b, w_o, *, tb):
    s = x.shape[0]
    assert s % tb == 0 and tb % Q_SUB == 0 and tb >= WINDOW
    kv_scratch = pltpu.VMEM((N_KV, WINDOW + tb, LANES), bf16)
    grid_spec = pltpu.PrefetchScalarGridSpec(
        num_scalar_prefetch=1,
        grid=(s // tb,),
        in_specs=[
            pl.BlockSpec((tb, D_MODEL), lambda i, sk: (i, 0)),
            pl.BlockSpec((SUBLANES, D_MODEL), lambda i, sk: (0, 0)),
            pl.BlockSpec((1, D_MODEL), lambda i, sk: (0, 0)),
            pl.BlockSpec((D_MODEL, IN_COLS), lambda i, sk: (0, 0), pipeline_mode=pl.Buffered(1)),
            pl.BlockSpec((CONV_W, D_CONV), lambda i, sk: (0, 0)),
            pl.BlockSpec((D_CONV, D_MODEL), lambda i, sk: (0, 0), pipeline_mode=pl.Buffered(1)),
            pl.BlockSpec((N_HEADS * HEAD_DIM, D_MODEL), lambda i, sk: (0, 0), pipeline_mode=pl.Buffered(1)),
            pl.BlockSpec((D_MODEL, D_MODEL), lambda i, sk: (0, 0), pipeline_mode=pl.Buffered(1)),
        ],
        out_specs=[
            pl.BlockSpec((tb, D_MODEL), lambda i, sk: (i, 0)),
            pl.BlockSpec((CONV_W - 1, D_CONV), lambda i, sk: (0, 0)),
            pl.BlockSpec((WINDOW, KV_COLS), lambda i, sk: (0, 0)),
            pl.BlockSpec((WINDOW, KV_COLS), lambda i, sk: (0, 0)),
        ],
        scratch_shapes=[
            pltpu.VMEM((SUBLANES + tb, D_CONV), f32),
            kv_scratch, kv_scratch, kv_scratch, kv_scratch,
            pltpu.VMEM((tb, N_HEADS * HEAD_DIM), bf16),
        ],
    )
    return pl.pallas_call(
        functools.partial(_mixer_prompt_kernel, tb=tb),
        grid_spec=grid_spec,
        out_shape=[
            jax.ShapeDtypeStruct((s, D_MODEL), f32),
            jax.ShapeDtypeStruct((CONV_W - 1, D_CONV), f32),
            jax.ShapeDtypeStruct((WINDOW, KV_COLS), f32),
            jax.ShapeDtypeStruct((WINDOW, KV_COLS), f32),
        ],
        compiler_params=pltpu.CompilerParams(
            dimension_semantics=("arbitrary",),
            vmem_limit_bytes=VMEM_BYTES_V7X * 7 // 8),
        name="mixer_prompt",
    )(sinks, x, ada8, g_mix, w_in, conv_w, w_a, w_b, w_o)


def _ffn_prompt_kernel(x_ref, ada_ref, g_ref, w_up_ref, fcw_ref, fcb_ref, w_down_ref, gf_ref,
                       y_ref, fst_ref, upbuf, actbuf, *, tb, ch):
    i = pl.program_id(0)

    @pl.when(i == 0)
    def _():
        upbuf[0:SUBLANES, :] = jnp.zeros((SUBLANES, 2 * D_FF), f32)

    x = x_ref[...]
    sh2, sc2, gt2 = ada_ref[3:4, :], ada_ref[4:5, :], ada_ref[5:6, :]
    h = (_rms(x, g_ref[...]) * (1.0 + sc2) + sh2).astype(bf16)

    def conv_cols(c0):
        up = _dot(h, w_up_ref[:, c0:c0 + ch])
        upbuf[SUBLANES:SUBLANES + tb, c0:c0 + ch] = up
        return (fcw_ref[0:1, c0:c0 + ch] * upbuf[SUBLANES - 2:SUBLANES - 2 + tb, c0:c0 + ch]
                + fcw_ref[1:2, c0:c0 + ch] * upbuf[SUBLANES - 1:SUBLANES - 1 + tb, c0:c0 + ch]
                + fcw_ref[2:3, c0:c0 + ch] * up
                + fcb_ref[0:1, c0:c0 + ch])

    for c0 in range(0, D_FF, ch):
        actbuf[:, c0:c0 + ch] = (_silu(conv_cols(c0)) * conv_cols(D_FF + c0)).astype(bf16)

    x2 = x + gt2 * _dot(actbuf[...], w_down_ref[...])
    y_ref[...] = _rms(x2, gf_ref[...])

    fst_ref[...] = upbuf[SUBLANES + tb - (CONV_W - 1):SUBLANES + tb, :]
    upbuf[0:SUBLANES, :] = upbuf[tb:tb + SUBLANES, :]


def _ffn_prompt(x1, ada8, g_ffn, w_up, fcw, fcb, w_down, g_final, *, tb, ch):
    s = x1.shape[0]
    assert s % tb == 0 and D_FF % ch == 0 and ch % LANES == 0
    return pl.pallas_call(
        functools.partial(_ffn_prompt_kernel, tb=tb, ch=ch),
        grid=(s // tb,),
        in_specs=[
            pl.BlockSpec((tb, D_MODEL), lambda i: (i, 0)),
            pl.BlockSpec((SUBLANES, D_MODEL), lambda i: (0, 0)),
            pl.BlockSpec((1, D_MODEL), lambda i: (0, 0)),
            _const_spec((D_MODEL, 2 * D_FF)),
            pl.BlockSpec((CONV_W, 2 * D_FF), lambda i: (0, 0)),
            pl.BlockSpec((1, 2 * D_FF), lambda i: (0, 0)),
            _const_spec((D_FF, D_MODEL)),
            pl.BlockSpec((1, D_MODEL), lambda i: (0, 0)),
        ],
        out_specs=[
            pl.BlockSpec((tb, D_MODEL), lambda i: (i, 0)),
            pl.BlockSpec((CONV_W - 1, 2 * D_FF), lambda i: (0, 0)),
        ],
        out_shape=[
            jax.ShapeDtypeStruct((s, D_MODEL), f32),
            jax.ShapeDtypeStruct((CONV_W - 1, 2 * D_FF), f32),
        ],
        scratch_shapes=[
            pltpu.VMEM((SUBLANES + tb, 2 * D_FF), f32),
            pltpu.VMEM((tb, D_FF), bf16),
        ],
        compiler_params=pltpu.CompilerParams(
            dimension_semantics=("arbitrary",),
            vmem_limit_bytes=VMEM_BYTES_V7X * 7 // 8),
        name="ffn_prompt",
    )(x1, ada8, g_ffn, w_up, fcw, fcb, w_down, g_final)


def _mixer_decode_pre_kernel(x_ref, ada_ref, g_ref, w_in_ref, cw_ref, st_ref, w_a_ref,
                             q_ref, k_ref, v_ref, za_ref, sgb_ref, stn_ref):
    x = x_ref[...]
    sh1 = ada_ref[:, 0:D_MODEL]
    sc1 = ada_ref[:, D_MODEL:2 * D_MODEL]
    h = (_rms(x, g_ref[...]) * (1.0 + sc1) + sh1).astype(bf16)

    def proj(c0, n):
        return _dot(h, w_in_ref[:, c0:c0 + n])

    u = proj(C_C, D_CONV) * proj(C_XIN, D_CONV)
    prev0 = st_ref[:, 0:D_CONV]
    prev1 = st_ref[:, D_CONV:2 * D_CONV]
    conv = cw_ref[0:1, :] * prev0 + cw_ref[1:2, :] * prev1 + cw_ref[2:3, :] * u
    stn_ref[:, 0:D_CONV] = prev1
    stn_ref[:, D_CONV:2 * D_CONV] = u
    ya = _dot((proj(C_B, D_CONV) * conv).astype(bf16), w_a_ref[...])
    q_ref[...] = proj(C_Q, N_HEADS * HEAD_DIM) * ATTN_SCALE
    k_ref[...] = proj(C_K, KV_COLS)
    v_ref[...] = proj(C_V, KV_COLS)
    za_ref[...] = jax.nn.sigmoid(proj(C_GA, D_MODEL)) * ya
    sgb_ref[...] = jax.nn.sigmoid(proj(C_GB, D_MODEL))


def _mixer_decode_pre(x, ada, g_mix, w_in, conv_w, state, w_a):
    n = x.shape[0]
    shapes = [(n, N_HEADS * HEAD_DIM), (n, KV_COLS), (n, KV_COLS), (n, D_MODEL), (n, D_MODEL),
              (n, (CONV_W - 1) * D_CONV)]
    return pl.pallas_call(
        _mixer_decode_pre_kernel,
        out_shape=[jax.ShapeDtypeStruct(s, f32) for s in shapes],
        compiler_params=pltpu.CompilerParams(vmem_limit_bytes=VMEM_BYTES_V7X * 5 // 8),
        name="mixer_decode_pre",
    )(x, ada, g_mix, w_in, conv_w, state, w_a)


def _attn_decode_kernel(qr_ref, kn_ref, vn_ref, ck_ref, cv_ref, sink_ref, e_ref, et_ref,
                        att_ref, ok_ref, ov_ref, *, bb):
    ok_ref[:, 0:WINDOW - 1, :] = ck_ref[:, 1:WINDOW, :]
    ok_ref[:, WINDOW - 1:WINDOW, :] = kn_ref[...]
    ov_ref[:, 0:WINDOW - 1, :] = cv_ref[:, 1:WINDOW, :]
    ov_ref[:, WINDOW - 1:WINDOW, :] = vn_ref[...]
    kw = ok_ref[...]
    vw = ov_ref[...]
    k0 = ck_ref[:, 0:1, :]
    v0 = cv_ref[:, 0:1, :]
    e = e_ref[...]
    et = et_ref[...]

    def seg_sum(prod):
        r = prod.shape[1]
        return _dot(prod.astype(bf16).reshape(bb * r, KV_COLS), e).reshape(bb, r, LANES)

    def seg_expand(p):
        r = p.shape[1]
        return _dot(p.astype(bf16).reshape(bb * r, LANES), et).reshape(bb, r, KV_COLS)

    for i in range(GROUP):
        qi = qr_ref[:, i:i + 1, :]
        sink = sink_ref[i:i + 1, :].reshape(1, 1, LANES)
        s_w = seg_sum(kw * qi)
        s_0 = seg_sum(jnp.broadcast_to(k0 * qi, (bb, SUBLANES, KV_COLS)))
        s_0 = s_0[:, 0:1, :]
        m = jnp.maximum(jnp.maximum(jnp.max(s_w, axis=1, keepdims=True), s_0), sink)
        e_w = jnp.exp(s_w - m)
        e_0 = jnp.exp(s_0 - m)
        linv = 1.0 / (jnp.sum(e_w, axis=1, keepdims=True) + e_0 + jnp.exp(sink - m))
        p_w = seg_expand(e_w * linv)
        p_0 = seg_expand(jnp.broadcast_to(e_0 * linv, (bb, SUBLANES, LANES)))[:, 0:1, :]
        att_ref[:, i:i + 1, :] = jnp.sum(p_w * vw, axis=1, keepdims=True) + p_0 * v0


def _attn_decode(qr, kn, vn, ck, cv, sink_r, *, bb):
    n = qr.shape[0]
    assert n % bb == 0
    lane_group = jnp.arange(KV_COLS, dtype=jnp.int32) // HEAD_DIM
    e = (lane_group[:, None] == jnp.arange(LANES, dtype=jnp.int32)[None, :]).astype(bf16)
    cache_spec = pl.BlockSpec((bb, WINDOW, KV_COLS), lambda b: (b, 0, 0))
    row_spec = pl.BlockSpec((bb, 1, KV_COLS), lambda b: (b, 0, 0))
    q_spec = pl.BlockSpec((bb, GROUP, KV_COLS), lambda b: (b, 0, 0))
    return pl.pallas_call(
        functools.partial(_attn_decode_kernel, bb=bb),
        grid=(n // bb,),
        in_specs=[q_spec, row_spec, row_spec, cache_spec, cache_spec,
                  pl.BlockSpec((GROUP, LANES), lambda b: (0, 0)),
                  pl.BlockSpec((KV_COLS, LANES), lambda b: (0, 0)),
                  pl.BlockSpec((LANES, KV_COLS), lambda b: (0, 0))],
        out_specs=[q_spec, cache_spec, cache_spec],
        out_shape=[jax.ShapeDtypeStruct((n, GROUP, KV_COLS), f32),
                   jax.ShapeDtypeStruct((n, WINDOW, KV_COLS), f32),
                   jax.ShapeDtypeStruct((n, WINDOW, KV_COLS), f32)],
        compiler_params=pltpu.CompilerParams(
            dimension_semantics=("arbitrary",),
            vmem_limit_bytes=VMEM_BYTES_V7X // 2),
        name="attn_decode",
    )(qr, kn, vn, ck, cv, sink_r, e, e.T)


def _post_decode_kernel(x_ref, ada_ref, za_ref, sgb_ref, att_ref, w_b_ref, w_o_ref, g_ref, w_up_ref,
                        fcw_ref, fcb_ref, w_down_ref, gf_ref, fst_ref, y_ref, fstn_ref):
    def mod(j):
        return ada_ref[:, j * D_MODEL:(j + 1) * D_MODEL]

    yb = _dot(att_ref[...].astype(bf16), w_b_ref[...])
    mix = za_ref[...] + sgb_ref[...] * yb
    x1 = x_ref[...] + mod(2) * _dot(mix.astype(bf16), w_o_ref[...])
    h = (_rms(x1, g_ref[...]) * (1.0 + mod(4)) + mod(3)).astype(bf16)
    up = _dot(h, w_up_ref[...])
    prev0 = fst_ref[:, 0:2 * D_FF]
    prev1 = fst_ref[:, 2 * D_FF:4 * D_FF]
    conv = fcw_ref[0:1, :] * prev0 + fcw_ref[1:2, :] * prev1 + fcw_ref[2:3, :] * up + fcb_ref[...]
    fstn_ref[:, 0:2 * D_FF] = prev1
    fstn_ref[:, 2 * D_FF:4 * D_FF] = up
    act = (_silu(conv[:, 0:D_FF]) * conv[:, D_FF:2 * D_FF]).astype(bf16)
    x2 = x1 + mod(5) * _dot(act, w_down_ref[...])
    y_ref[...] = _rms(x2, gf_ref[...])


def _post_decode(x, ada, za, sgb, att, w_b, w_o, g_ffn, w_up, fcw, fcb, w_down, g_final, fstate):
    n = x.shape[0]
    return pl.pallas_call(
        _post_decode_kernel,
        out_shape=[jax.ShapeDtypeStruct((n, D_MODEL), f32),
                   jax.ShapeDtypeStruct((n, (CONV_W - 1) * 2 * D_FF), f32)],
        compiler_params=pltpu.CompilerParams(vmem_limit_bytes=VMEM_BYTES_V7X * 7 // 8),
        name="post_decode",
    )(x, ada, za, sgb, att, w_b, w_o, g_ffn, w_up, fcw, fcb, w_down, g_final, fstate)


PROMPT_MIXER_ROWS = 256
PROMPT_FFN_ROWS = 256
PROMPT_FFN_COLS = 256
DECODE_ATTN_BATCH = 8


def kernel(x_prompt, x_sample, c_prompt, c_sample, state_conv_a, cache_k_win, cache_v_win, state_ffn_conv, w_ada, b_ada, g_mix, w_in, conv_a_w, attn_sinks, w_a_out, w_b_out, w_o, g_ffn, w_up, ffn_conv_w, ffn_conv_b, w_down, g_final):
    depth = w_in.shape[0]
    n_p, seq, _ = x_prompt.shape
    n_s, t_s, _ = x_sample.shape
    assert n_p == 1 and t_s == 1, "one prompt sequence and single-token decode only"
    xp = x_prompt.reshape(seq, D_MODEL)
    xs = x_sample.reshape(n_s, D_MODEL)
    pad = (-(n_s + n_p)) % SUBLANES
    c_all = jnp.concatenate([c_sample, c_prompt, jnp.zeros((pad, D_MODEL), f32)], axis=0)
    gf = g_final.reshape(1, D_MODEL)
    outs = [[] for _ in range(8)]
    for l in range(depth):
        ada = _ada(c_all, w_ada[l], b_ada[l])
        ada_s = ada[:n_s]
        ada_p = jnp.pad(ada[n_s].reshape(N_MOD, D_MODEL), ((0, SUBLANES - N_MOD), (0, 0)))
        w_in_b, w_a_b, w_b_b, w_o_b = (w[l].astype(bf16) for w in (w_in, w_a_out, w_b_out, w_o))
        w_up_b, w_down_b = w_up[l].astype(bf16), w_down[l].astype(bf16)
        gm, gn = g_mix[l].reshape(1, D_MODEL), g_ffn[l].reshape(1, D_MODEL)
        fcb = ffn_conv_b[l].reshape(1, 2 * D_FF)

        x1, conv_p, k_p, v_p = _mixer_prompt(xp, ada_p, gm, w_in_b, conv_a_w[l], attn_sinks[l],
                                             w_a_b, w_b_b, w_o_b, tb=PROMPT_MIXER_ROWS)
        xp, ffn_p = _ffn_prompt(x1, ada_p, gn, w_up_b, ffn_conv_w[l], fcb, w_down_b, gf,
                                tb=PROMPT_FFN_ROWS, ch=PROMPT_FFN_COLS)

        q, k_n, v_n, za, sgb, conv_s = _mixer_decode_pre(
            xs, ada_s, gm, w_in_b, conv_a_w[l], state_conv_a[l].reshape(n_s, (CONV_W - 1) * D_CONV), w_a_b)
        qr = q.reshape(n_s, N_KV, GROUP, HEAD_DIM).transpose(0, 2, 1, 3).reshape(n_s, GROUP, KV_COLS)
        sink_r = jnp.pad(attn_sinks[l].reshape(N_KV, GROUP).T, ((0, 0), (0, LANES - N_KV)))
        att_r, k_s, v_s = _attn_decode(
            qr, k_n.reshape(n_s, 1, KV_COLS), v_n.reshape(n_s, 1, KV_COLS),
            cache_k_win[l].reshape(n_s, WINDOW, KV_COLS), cache_v_win[l].reshape(n_s, WINDOW, KV_COLS),
            sink_r, bb=DECODE_ATTN_BATCH)
        att = att_r.reshape(n_s, GROUP, N_KV, HEAD_DIM).transpose(0, 2, 1, 3).reshape(n_s, N_HEADS * HEAD_DIM)
        xs, ffn_s = _post_decode(xs, ada_s, za, sgb, att, w_b_b, w_o_b, gn, w_up_b, ffn_conv_w[l], fcb,
                                 w_down_b, gf, state_ffn_conv[l].reshape(n_s, (CONV_W - 1) * 2 * D_FF))

        for lst, val in zip(outs, (
                conv_p.reshape(n_p, CONV_W - 1, D_CONV), conv_s.reshape(n_s, CONV_W - 1, D_CONV),
                k_p.reshape(n_p, WINDOW, N_KV, HEAD_DIM), k_s.reshape(n_s, WINDOW, N_KV, HEAD_DIM),
                v_p.reshape(n_p, WINDOW, N_KV, HEAD_DIM), v_s.reshape(n_s, WINDOW, N_KV, HEAD_DIM),
                ffn_p.reshape(n_p, CONV_W - 1, 2 * D_FF), ffn_s.reshape(n_s, CONV_W - 1, 2 * D_FF))):
            lst.append(val)
    assert depth == 1, "final RMSNorm is fused into the single layer's FFN kernels"
    return (xp.reshape(n_p, seq, D_MODEL), xs.reshape(n_s, t_s, D_MODEL)) + tuple(jnp.stack(o) for o in outs)
```

```python
import functools

import jax
import jax.numpy as jnp
from jax import lax
from jax.experimental import pallas as pl
from jax.experimental.pallas import tpu as pltpu

f32 = jnp.float32
bf16 = jnp.bfloat16

D_MODEL = 1024
D_CONV = D_MODEL
CONV_W = 3
N_HEADS = 16
N_KV = 4
GROUP = N_HEADS // N_KV
HEAD_DIM = 64
WINDOW = 128
D_FF = 2816
EPS = 1e-6
N_MOD = 6
ATTN_SCALE = HEAD_DIM ** -0.5
KV_COLS = N_KV * HEAD_DIM
C_XIN, C_B, C_C = 0, D_CONV, 2 * D_CONV
C_Q = 3 * D_CONV
C_K = C_Q + N_HEADS * HEAD_DIM
C_V = C_K + KV_COLS
C_GA = C_V + KV_COLS
C_GB = C_GA + D_MODEL
IN_COLS = C_GB + D_MODEL

LANES = 128
SUBLANES = 8
Q_SUB = 128
ATTN_LOOKAHEAD = 3
VMEM_BYTES_V7X = 64 * 1024 * 1024


def _rms(x, g):
    ms = jnp.mean(x * x, axis=-1, keepdims=True)
    return x * lax.rsqrt(ms + EPS) * g


def _silu(x):
    return x * jax.nn.sigmoid(x)


def _dot(a, b):
    return jnp.dot(a, b, preferred_element_type=f32)


def _const_spec(shape):
    nd = len(shape)
    return pl.BlockSpec(shape, lambda i: (0,) * nd, pipeline_mode=pl.Buffered(1))


def _ada_kernel(c_ref, w_ref, b_ref, o_ref):
    c = c_ref[...]
    o_ref[...] = _dot(_silu(c).astype(bf16), w_ref[...].astype(bf16)) + b_ref[...]


def _ada(c_all, w_ada, b_ada):
    rows = c_all.shape[0]
    return pl.pallas_call(
        _ada_kernel,
        grid=(N_MOD,),
        in_specs=[
            pl.BlockSpec((rows, D_MODEL), lambda j: (0, 0)),
            pl.BlockSpec((D_MODEL, D_MODEL), lambda j: (0, j)),
            pl.BlockSpec((1, D_MODEL), lambda j: (0, j)),
        ],
        out_specs=pl.BlockSpec((rows, D_MODEL), lambda j: (0, j)),
        out_shape=jax.ShapeDtypeStruct((rows, N_MOD * D_MODEL), f32),
        compiler_params=pltpu.CompilerParams(dimension_semantics=("arbitrary",)),
        name="ada",
    )(c_all, w_ada, b_ada.reshape(1, -1))


def _mixer_prompt_kernel(sinks_ref, x_ref, ada_ref, g_ref, w_in_ref, cw_ref, w_a_ref, w_b_ref, w_o_ref,
                         x1_ref, conv_ref, knew_ref, vnew_ref,
                         ubuf, klo, khi, vt, attbuf, *, tb):
    i = pl.program_id(0)

    @pl.when(i == 0)
    def _():
        ubuf[0:SUBLANES, :] = jnp.zeros((SUBLANES, D_CONV), f32)
        for r in (klo, khi):
            r[:, 0:WINDOW, :] = jnp.zeros((N_KV, WINDOW, LANES), bf16)
        vt[:, 0:WINDOW] = jnp.zeros((KV_COLS, WINDOW), bf16)

    x = x_ref[...]
    sh1, sc1, gt1 = ada_ref[0:1, :], ada_ref[1:2, :], ada_ref[2:3, :]
    h = (_rms(x, g_ref[...]) * (1.0 + sc1) + sh1).astype(bf16)

    def proj(c0, n):
        return _dot(h, w_in_ref[:, c0:c0 + n])

    u = proj(C_C, D_CONV) * proj(C_XIN, D_CONV)
    ubuf[SUBLANES:SUBLANES + tb, :] = u
    conv = (cw_ref[0:1, :] * ubuf[SUBLANES - 2:SUBLANES - 2 + tb, :]
            + cw_ref[1:2, :] * ubuf[SUBLANES - 1:SUBLANES - 1 + tb, :]
            + cw_ref[2:3, :] * u)
    ya = _dot((proj(C_B, D_CONV) * conv).astype(bf16), w_a_ref[...])

    q = (proj(C_Q, N_HEADS * HEAD_DIM) * ATTN_SCALE).astype(bf16)
    k = proj(C_K, KV_COLS)
    v = proj(C_V, KV_COLS)
    knew_ref[...] = k[tb - WINDOW:, :]
    vnew_ref[...] = v[tb - WINDOW:, :]

    lo = lax.broadcasted_iota(jnp.int32, (tb, LANES), 1) < HEAD_DIM
    for j in range(N_KV // 2):
        pair = k[:, LANES * j:LANES * (j + 1)]
        rolled = pltpu.roll(pair, HEAD_DIM, axis=1)
        zero = jnp.zeros_like(pair)
        klo[2 * j, WINDOW:WINDOW + tb, :] = jnp.where(lo, pair, zero).astype(bf16)
        khi[2 * j, WINDOW:WINDOW + tb, :] = jnp.where(lo, zero, rolled).astype(bf16)
        klo[2 * j + 1, WINDOW:WINDOW + tb, :] = jnp.where(lo, rolled, zero).astype(bf16)
        khi[2 * j + 1, WINDOW:WINDOW + tb, :] = jnp.where(lo, zero, pair).astype(bf16)
    vt[:, WINDOW:WINDOW + tb] = v.T.astype(bf16)

    cc = lax.broadcasted_iota(jnp.int32, (Q_SUB + WINDOW, 2 * Q_SUB), 0)
    col = lax.broadcasted_iota(jnp.int32, (Q_SUB + WINDOW, 2 * Q_SUB), 1)
    rr = col & (Q_SUB - 1)
    band = (cc >= rr) & (cc <= rr + WINDOW)
    first_head = lax.broadcasted_iota(jnp.int32, (1, 2 * Q_SUB), 1) < Q_SUB
    mask0 = band & (cc >= jnp.where(i == 0, WINDOW, 0))

    def scores(j, g, t):
        r0 = Q_SUB * j
        qs = jnp.concatenate([q[r0:r0 + Q_SUB, 2 * LANES * g:2 * LANES * g + LANES],
                              q[r0:r0 + Q_SUB, 2 * LANES * g + LANES:2 * LANES * (g + 1)]], axis=0)
        kr = (klo, khi)[t]
        return lax.dot_general(kr[g, r0:r0 + Q_SUB + WINDOW, :], qs, (((1,), (1,)), ((), ())),
                               preferred_element_type=f32)

    def finish(j, g, t, st):
        r0 = Q_SUB * j
        h0, h1 = GROUP * g + t, GROUP * g + 2 + t
        sink = jnp.where(first_head, sinks_ref[h0], sinks_ref[h1])
        st = jnp.where(mask0 if j == 0 else band, st, -jnp.inf)
        m = jnp.maximum(jnp.max(st, axis=0, keepdims=True), sink)
        e = jnp.exp(st - m)
        linv = 1.0 / (jnp.sum(e, axis=0, keepdims=True) + jnp.exp(sink - m))
        vtg = vt[HEAD_DIM * g:HEAD_DIM * (g + 1), r0:r0 + Q_SUB + WINDOW]
        ot = _dot(vtg, e.astype(bf16)) * linv
        attbuf[HEAD_DIM * h0:HEAD_DIM * (h0 + 1), r0:r0 + Q_SUB] = ot[:, 0:Q_SUB]
        attbuf[HEAD_DIM * h1:HEAD_DIM * (h1 + 1), r0:r0 + Q_SUB] = ot[:, Q_SUB:2 * Q_SUB]

    chains = [(j, g, t) for j in range(tb // Q_SUB) for g in range(N_KV) for t in range(2)]
    pending = [scores(*c) for c in chains[:ATTN_LOOKAHEAD]]
    for n, chain in enumerate(chains):
        if n + ATTN_LOOKAHEAD < len(chains):
            pending.append(scores(*chains[n + ATTN_LOOKAHEAD]))
        finish(*chain, pending.pop(0))

    yb = _dot(attbuf[...].T.astype(bf16), w_b_ref[...])
    mix = jax.nn.sigmoid(proj(C_GA, D_MODEL)) * ya + jax.nn.sigmoid(proj(C_GB, D_MODEL)) * yb
    x1_ref[...] = x + gt1 * _dot(mix.astype(bf16), w_o_ref[...])

    conv_ref[...] = ubuf[SUBLANES + tb - (CONV_W - 1):SUBLANES + tb, :]
    ubuf[0:SUBLANES, :] = ubuf[tb:tb + SUBLANES, :]
    for r in (klo, khi):
        r[:, 0:WINDOW, :] = r[:, tb:tb + WINDOW, :]
    vt[:, 0:WINDOW] = vt[:, tb:tb + WINDOW]


def _mixer_prompt(x, ada8, g_mix, w_in, conv_w, sinks, w_a, w_b, w_o, *, tb):
    s = x.shape[0]
    assert s % tb == 0 and tb % Q_SUB == 0 and tb >= WINDOW
    kv_scratch = pltpu.VMEM((N_KV, WINDOW + tb, LANES), bf16)
    grid_spec = pltpu.PrefetchScalarGridSpec(
        num_scalar_prefetch=1,
        grid=(s // tb,),
        in_specs=[
            pl.BlockSpec((tb, D_MODEL), lambda i, sk: (i, 0)),
            pl.BlockSpec((SUBLANES, D_MODEL), lambda i, sk: (0, 0)),
            pl.BlockSpec((1, D_MODEL), lambda i, sk: (0, 0)),
            pl.BlockSpec((D_MODEL, IN_COLS), lambda i, sk: (0, 0), pipeline_mode=pl.Buffered(1)),
            pl.BlockSpec((CONV_W, D_CONV), lambda i, sk: (0, 0)),
            pl.BlockSpec((D_CONV, D_MODEL), lambda i, sk: (0, 0), pipeline_mode=pl.Buffered(1)),
            pl.BlockSpec((N_HEADS * HEAD_DIM, D_MODEL), lambda i, sk: (0, 0), pipeline_mode=pl.Buffered(1)),
            pl.BlockSpec((D_MODEL, D_MODEL), lambda i, sk: (0, 0), pipeline_mode=pl.Buffered(1)),
        ],
        out_specs=[
            pl.BlockSpec((tb, D_MODEL), lambda i, sk: (i, 0)),
            pl.BlockSpec((CONV_W - 1, D_CONV), lambda i, sk: (0, 0)),
            pl.BlockSpec((WINDOW, KV_COLS), lambda i, sk: (0, 0)),
            pl.BlockSpec((WINDOW, KV_COLS), lambda i, sk: (0, 0)),
        ],
        scratch_shapes=[
            pltpu.VMEM((SUBLANES + tb, D_CONV), f32),
            kv_scratch, kv_scratch,
            pltpu.VMEM((KV_COLS, WINDOW + tb), bf16),
            pltpu.VMEM((N_HEADS * HEAD_DIM, tb), f32),
        ],
    )
    return pl.pallas_call(
        functools.partial(_mixer_prompt_kernel, tb=tb),
        grid_spec=grid_spec,
        out_shape=[
            jax.ShapeDtypeStruct((s, D_MODEL), f32),
            jax.ShapeDtypeStruct((CONV_W - 1, D_CONV), f32),
            jax.ShapeDtypeStruct((WINDOW, KV_COLS), f32),
            jax.ShapeDtypeStruct((WINDOW, KV_COLS), f32),
        ],
        compiler_params=pltpu.CompilerParams(
            dimension_semantics=("arbitrary",),
            vmem_limit_bytes=VMEM_BYTES_V7X * 7 // 8),
        name="mixer_prompt",
    )(sinks, x, ada8, g_mix, w_in, conv_w, w_a, w_b, w_o)


def _ffn_prompt_kernel(x_ref, ada_ref, g_ref, w_up_ref, fcw_ref, fcb_ref, w_down_ref, gf_ref,
                       y_ref, fst_ref, upbuf, actbuf, *, tb, ch):
    i = pl.program_id(0)

    @pl.when(i == 0)
    def _():
        upbuf[0:SUBLANES, :] = jnp.zeros((SUBLANES, 2 * D_FF), f32)

    x = x_ref[...]
    sh2, sc2, gt2 = ada_ref[3:4, :], ada_ref[4:5, :], ada_ref[5:6, :]
    h = (_rms(x, g_ref[...]) * (1.0 + sc2) + sh2).astype(bf16)

    def conv_cols(c0):
        up = _dot(h, w_up_ref[:, c0:c0 + ch])
        upbuf[SUBLANES:SUBLANES + tb, c0:c0 + ch] = up
        return (fcw_ref[0:1, c0:c0 + ch] * upbuf[SUBLANES - 2:SUBLANES - 2 + tb, c0:c0 + ch]
                + fcw_ref[1:2, c0:c0 + ch] * upbuf[SUBLANES - 1:SUBLANES - 1 + tb, c0:c0 + ch]
                + fcw_ref[2:3, c0:c0 + ch] * up
                + fcb_ref[0:1, c0:c0 + ch])

    for c0 in range(0, D_FF, ch):
        actbuf[:, c0:c0 + ch] = (_silu(conv_cols(c0)) * conv_cols(D_FF + c0)).astype(bf16)

    x2 = x + gt2 * _dot(actbuf[...], w_down_ref[...])
    y_ref[...] = _rms(x2, gf_ref[...])

    fst_ref[...] = upbuf[SUBLANES + tb - (CONV_W - 1):SUBLANES + tb, :]
    upbuf[0:SUBLANES, :] = upbuf[tb:tb + SUBLANES, :]


def _ffn_prompt(x1, ada8, g_ffn, w_up, fcw, fcb, w_down, g_final, *, tb, ch):
    s = x1.shape[0]
    assert s % tb == 0 and D_FF % ch == 0 and ch % LANES == 0
    return pl.pallas_call(
        functools.partial(_ffn_prompt_kernel, tb=tb, ch=ch),
        grid=(s // tb,),
        in_specs=[
            pl.BlockSpec((tb, D_MODEL), lambda i: (i, 0)),
            pl.BlockSpec((SUBLANES, D_MODEL), lambda i: (0, 0)),
            pl.BlockSpec((1, D_MODEL), lambda i: (0, 0)),
            _const_spec((D_MODEL, 2 * D_FF)),
            pl.BlockSpec((CONV_W, 2 * D_FF), lambda i: (0, 0)),
            pl.BlockSpec((1, 2 * D_FF), lambda i: (0, 0)),
            _const_spec((D_FF, D_MODEL)),
            pl.BlockSpec((1, D_MODEL), lambda i: (0, 0)),
        ],
        out_specs=[
            pl.BlockSpec((tb, D_MODEL), lambda i: (i, 0)),
            pl.BlockSpec((CONV_W - 1, 2 * D_FF), lambda i: (0, 0)),
        ],
        out_shape=[
            jax.ShapeDtypeStruct((s, D_MODEL), f32),
            jax.ShapeDtypeStruct((CONV_W - 1, 2 * D_FF), f32),
        ],
        scratch_shapes=[
            pltpu.VMEM((SUBLANES + tb, 2 * D_FF), f32),
            pltpu.VMEM((tb, D_FF), bf16),
        ],
        compiler_params=pltpu.CompilerParams(
            dimension_semantics=("arbitrary",),
            vmem_limit_bytes=VMEM_BYTES_V7X * 7 // 8),
        name="ffn_prompt",
    )(x1, ada8, g_ffn, w_up, fcw, fcb, w_down, g_final)


def _mixer_decode_pre_kernel(x_ref, ada_ref, g_ref, w_in_ref, cw_ref, st_ref, w_a_ref,
                             q_ref, k_ref, v_ref, za_ref, sgb_ref, stn_ref):
    x = x_ref[...]
    sh1 = ada_ref[:, 0:D_MODEL]
    sc1 = ada_ref[:, D_MODEL:2 * D_MODEL]
    h = (_rms(x, g_ref[...]) * (1.0 + sc1) + sh1).astype(bf16)

    def proj(c0, n):
        return _dot(h, w_in_ref[:, c0:c0 + n])

    u = proj(C_C, D_CONV) * proj(C_XIN, D_CONV)
    prev0 = st_ref[:, 0:D_CONV]
    prev1 = st_ref[:, D_CONV:2 * D_CONV]
    conv = cw_ref[0:1, :] * prev0 + cw_ref[1:2, :] * prev1 + cw_ref[2:3, :] * u
    stn_ref[:, 0:D_CONV] = prev1
    stn_ref[:, D_CONV:2 * D_CONV] = u
    ya = _dot((proj(C_B, D_CONV) * conv).astype(bf16), w_a_ref[...])
    q_ref[...] = proj(C_Q, N_HEADS * HEAD_DIM) * ATTN_SCALE
    k_ref[...] = proj(C_K, KV_COLS)
    v_ref[...] = proj(C_V, KV_COLS)
    za_ref[...] = jax.nn.sigmoid(proj(C_GA, D_MODEL)) * ya
    sgb_ref[...] = jax.nn.sigmoid(proj(C_GB, D_MODEL))


def _mixer_decode_pre(x, ada, g_mix, w_in, conv_w, state, w_a):
    n = x.shape[0]
    shapes = [(n, N_HEADS * HEAD_DIM), (n, KV_COLS), (n, KV_COLS), (n, D_MODEL), (n, D_MODEL),
              (n, (CONV_W - 1) * D_CONV)]
    return pl.pallas_call(
        _mixer_decode_pre_kernel,
        out_shape=[jax.ShapeDtypeStruct(s, f32) for s in shapes],
        compiler_params=pltpu.CompilerParams(vmem_limit_bytes=VMEM_BYTES_V7X * 5 // 8),
        name="mixer_decode_pre",
    )(x, ada, g_mix, w_in, conv_w, state, w_a)


def _attn_decode_kernel(qr_ref, kn_ref, vn_ref, ck_ref, cv_ref, sink_ref, e_ref, et_ref,
                        att_ref, ok_ref, ov_ref, *, bb):
    ok_ref[:, 0:WINDOW - 1, :] = ck_ref[:, 1:WINDOW, :]
    ok_ref[:, WINDOW - 1:WINDOW, :] = kn_ref[...]
    ov_ref[:, 0:WINDOW - 1, :] = cv_ref[:, 1:WINDOW, :]
    ov_ref[:, WINDOW - 1:WINDOW, :] = vn_ref[...]
    kw = ok_ref[...]
    vw = ov_ref[...]
    k0 = ck_ref[:, 0:1, :]
    v0 = cv_ref[:, 0:1, :]
    e = e_ref[...]
    et = et_ref[...]

    def seg_sum(prod):
        r = prod.shape[1]
        return _dot(prod.astype(bf16).reshape(bb * r, KV_COLS), e).reshape(bb, r, LANES)

    def seg_expand(p):
        r = p.shape[1]
        return _dot(p.astype(bf16).reshape(bb * r, LANES), et).reshape(bb, r, KV_COLS)

    for i in range(GROUP):
        qi = qr_ref[:, i:i + 1, :]
        sink = sink_ref[i:i + 1, :].reshape(1, 1, LANES)
        s_w = seg_sum(kw * qi)
        s_0 = seg_sum(jnp.broadcast_to(k0 * qi, (bb, SUBLANES, KV_COLS)))
        s_0 = s_0[:, 0:1, :]
        m = jnp.maximum(jnp.maximum(jnp.max(s_w, axis=1, keepdims=True), s_0), sink)
        e_w = jnp.exp(s_w - m)
        e_0 = jnp.exp(s_0 - m)
        linv = 1.0 / (jnp.sum(e_w, axis=1, keepdims=True) + e_0 + jnp.exp(sink - m))
        p_w = seg_expand(e_w * linv)
        p_0 = seg_expand(jnp.broadcast_to(e_0 * linv, (bb, SUBLANES, LANES)))[:, 0:1, :]
        att_ref[:, i:i + 1, :] = jnp.sum(p_w * vw, axis=1, keepdims=True) + p_0 * v0


def _attn_decode(qr, kn, vn, ck, cv, sink_r, *, bb):
    n = qr.shape[0]
    assert n % bb == 0
    lane_group = jnp.arange(KV_COLS, dtype=jnp.int32) // HEAD_DIM
    e = (lane_group[:, None] == jnp.arange(LANES, dtype=jnp.int32)[None, :]).astype(bf16)
    cache_spec = pl.BlockSpec((bb, WINDOW, KV_COLS), lambda b: (b, 0, 0))
    row_spec = pl.BlockSpec((bb, 1, KV_COLS), lambda b: (b, 0, 0))
    q_spec = pl.BlockSpec((bb, GROUP, KV_COLS), lambda b: (b, 0, 0))
    return pl.pallas_call(
        functools.partial(_attn_decode_kernel, bb=bb),
        grid=(n // bb,),
        in_specs=[q_spec, row_spec, row_spec, cache_spec, cache_spec,
                  pl.BlockSpec((GROUP, LANES), lambda b: (0, 0)),
                  pl.BlockSpec((KV_COLS, LANES), lambda b: (0, 0)),
                  pl.BlockSpec((LANES, KV_COLS), lambda b: (0, 0))],
        out_specs=[q_spec, cache_spec, cache_spec],
        out_shape=[jax.ShapeDtypeStruct((n, GROUP, KV_COLS), f32),
                   jax.ShapeDtypeStruct((n, WINDOW, KV_COLS), f32),
                   jax.ShapeDtypeStruct((n, WINDOW, KV_COLS), f32)],
        compiler_params=pltpu.CompilerParams(
            dimension_semantics=("arbitrary",),
            vmem_limit_bytes=VMEM_BYTES_V7X // 2),
        name="attn_decode",
    )(qr, kn, vn, ck, cv, sink_r, e, e.T)


def _post_decode_kernel(x_ref, ada_ref, za_ref, sgb_ref, att_ref, w_b_ref, w_o_ref, g_ref, w_up_ref,
                        fcw_ref, fcb_ref, w_down_ref, gf_ref, fst_ref, y_ref, fstn_ref):
    def mod(j):
        return ada_ref[:, j * D_MODEL:(j + 1) * D_MODEL]

    yb = _dot(att_ref[...].astype(bf16), w_b_ref[...])
    mix = za_ref[...] + sgb_ref[...] * yb
    x1 = x_ref[...] + mod(2) * _dot(mix.astype(bf16), w_o_ref[...])
    h = (_rms(x1, g_ref[...]) * (1.0 + mod(4)) + mod(3)).astype(bf16)
    up = _dot(h, w_up_ref[...])
    prev0 = fst_ref[:, 0:2 * D_FF]
    prev1 = fst_ref[:, 2 * D_FF:4 * D_FF]
    conv = fcw_ref[0:1, :] * prev0 + fcw_ref[1:2, :] * prev1 + fcw_ref[2:3, :] * up + fcb_ref[...]
    fstn_ref[:, 0:2 * D_FF] = prev1
    fstn_ref[:, 2 * D_FF:4 * D_FF] = up
    act = (_silu(conv[:, 0:D_FF]) * conv[:, D_FF:2 * D_FF]).astype(bf16)
    x2 = x1 + mod(5) * _dot(act, w_down_ref[...])
    y_ref[...] = _rms(x2, gf_ref[...])


def _post_decode(x, ada, za, sgb, att, w_b, w_o, g_ffn, w_up, fcw, fcb, w_down, g_final, fstate):
    n = x.shape[0]
    return pl.pallas_call(
        _post_decode_kernel,
        out_shape=[jax.ShapeDtypeStruct((n, D_MODEL), f32),
                   jax.ShapeDtypeStruct((n, (CONV_W - 1) * 2 * D_FF), f32)],
        compiler_params=pltpu.CompilerParams(vmem_limit_bytes=VMEM_BYTES_V7X * 7 // 8),
        name="post_decode",
    )(x, ada, za, sgb, att, w_b, w_o, g_ffn, w_up, fcw, fcb, w_down, g_final, fstate)


PROMPT_MIXER_ROWS = 256
PROMPT_FFN_ROWS = 256
PROMPT_FFN_COLS = 256
DECODE_ATTN_BATCH = 8


def kernel(x_prompt, x_sample, c_prompt, c_sample, state_conv_a, cache_k_win, cache_v_win, state_ffn_conv, w_ada, b_ada, g_mix, w_in, conv_a_w, attn_sinks, w_a_out, w_b_out, w_o, g_ffn, w_up, ffn_conv_w, ffn_conv_b, w_down, g_final):
    depth = w_in.shape[0]
    n_p, seq, _ = x_prompt.shape
    n_s, t_s, _ = x_sample.shape
    assert n_p == 1 and t_s == 1, "one prompt sequence and single-token decode only"
    xp = x_prompt.reshape(seq, D_MODEL)
    xs = x_sample.reshape(n_s, D_MODEL)
    pad = (-(n_s + n_p)) % SUBLANES
    c_all = jnp.concatenate([c_sample, c_prompt, jnp.zeros((pad, D_MODEL), f32)], axis=0)
    gf = g_final.reshape(1, D_MODEL)
    outs = [[] for _ in range(8)]
    for l in range(depth):
        ada = _ada(c_all, w_ada[l], b_ada[l])
        ada_s = ada[:n_s]
        ada_p = jnp.pad(ada[n_s].reshape(N_MOD, D_MODEL), ((0, SUBLANES - N_MOD), (0, 0)))
        w_in_b, w_a_b, w_b_b, w_o_b = (w[l].astype(bf16) for w in (w_in, w_a_out, w_b_out, w_o))
        w_up_b, w_down_b = w_up[l].astype(bf16), w_down[l].astype(bf16)
        gm, gn = g_mix[l].reshape(1, D_MODEL), g_ffn[l].reshape(1, D_MODEL)
        fcb = ffn_conv_b[l].reshape(1, 2 * D_FF)

        x1, conv_p, k_p, v_p = _mixer_prompt(xp, ada_p, gm, w_in_b, conv_a_w[l], attn_sinks[l],
                                             w_a_b, w_b_b, w_o_b, tb=PROMPT_MIXER_ROWS)
        xp, ffn_p = _ffn_prompt(x1, ada_p, gn, w_up_b, ffn_conv_w[l], fcb, w_down_b, gf,
                                tb=PROMPT_FFN_ROWS, ch=PROMPT_FFN_COLS)

        q, k_n, v_n, za, sgb, conv_s = _mixer_decode_pre(
            xs, ada_s, gm, w_in_b, conv_a_w[l], state_conv_a[l].reshape(n_s, (CONV_W - 1) * D_CONV), w_a_b)
        qr = q.reshape(n_s, N_KV, GROUP, HEAD_DIM).transpose(0, 2, 1, 3).reshape(n_s, GROUP, KV_COLS)
        sink_r = jnp.pad(attn_sinks[l].reshape(N_KV, GROUP).T, ((0, 0), (0, LANES - N_KV)))
        att_r, k_s, v_s = _attn_decode(
            qr, k_n.reshape(n_s, 1, KV_COLS), v_n.reshape(n_s, 1, KV_COLS),
            cache_k_win[l].reshape(n_s, WINDOW, KV_COLS), cache_v_win[l].reshape(n_s, WINDOW, KV_COLS),
            sink_r, bb=DECODE_ATTN_BATCH)
        att = att_r.reshape(n_s, GROUP, N_KV, HEAD_DIM).transpose(0, 2, 1, 3).reshape(n_s, N_HEADS * HEAD_DIM)
        xs, ffn_s = _post_decode(xs, ada_s, za, sgb, att, w_b_b, w_o_b, gn, w_up_b, ffn_conv_w[l], fcb,
                                 w_down_b, gf, state_ffn_conv[l].reshape(n_s, (CONV_W - 1) * 2 * D_FF))

        for lst, val in zip(outs, (
                conv_p.reshape(n_p, CONV_W - 1, D_CONV), conv_s.reshape(n_s, CONV_W - 1, D_CONV),
                k_p.reshape(n_p, WINDOW, N_KV, HEAD_DIM), k_s.reshape(n_s, WINDOW, N_KV, HEAD_DIM),
                v_p.reshape(n_p, WINDOW, N_KV, HEAD_DIM), v_s.reshape(n_s, WINDOW, N_KV, HEAD_DIM),
                ffn_p.reshape(n_p, CONV_W - 1, 2 * D_FF), ffn_s.reshape(n_s, CONV_W - 1, 2 * D_FF))):
            lst.append(val)
    assert depth == 1, "final RMSNorm is fused into the single layer's FFN kernels"
    return (xp.reshape(n_p, seq, D_MODEL), xs.reshape(n_s, t_s, D_MODEL)) + tuple(jnp.stack(o) for o in outs)
```

```python
import functools

import jax
import jax.numpy as jnp
from jax import lax
from jax.experimental import pallas as pl
from jax.experimental.pallas import tpu as pltpu

f32 = jnp.float32
bf16 = jnp.bfloat16

D_MODEL = 1024
D_CONV = D_MODEL
CONV_W = 3
N_HEADS = 16
N_KV = 4
GROUP = N_HEADS // N_KV
HEAD_DIM = 64
WINDOW = 128
D_FF = 2816
EPS = 1e-6
N_MOD = 6
ATTN_SCALE = HEAD_DIM ** -0.5
KV_COLS = N_KV * HEAD_DIM
C_XIN, C_B, C_C = 0, D_CONV, 2 * D_CONV
C_Q = 3 * D_CONV
C_K = C_Q + N_HEADS * HEAD_DIM
C_V = C_K + KV_COLS
C_GA = C_V + KV_COLS
C_GB = C_GA + D_MODEL
IN_COLS = C_GB + D_MODEL

LANES = 128
SUBLANES = 8
Q_SUB = 128
ATTN_LOOKAHEAD = 3
FFN_LOOKAHEAD = 2
VMEM_BYTES_V7X = 64 * 1024 * 1024


def _rms(x, g):
    ms = jnp.mean(x * x, axis=-1, keepdims=True)
    return x * lax.rsqrt(ms + EPS) * g


def _silu(x):
    return x * jax.nn.sigmoid(x)


def _dot(a, b):
    return jnp.dot(a, b, preferred_element_type=f32)


def _const_spec(shape):
    nd = len(shape)
    return pl.BlockSpec(shape, lambda i: (0,) * nd, pipeline_mode=pl.Buffered(1))


def _ada_kernel(c_ref, w_ref, b_ref, o_ref):
    c = c_ref[...]
    o_ref[...] = _dot(_silu(c).astype(bf16), w_ref[...].astype(bf16)) + b_ref[...]


def _ada(c_all, w_ada, b_ada):
    rows = c_all.shape[0]
    return pl.pallas_call(
        _ada_kernel,
        grid=(N_MOD,),
        in_specs=[
            pl.BlockSpec((rows, D_MODEL), lambda j: (0, 0)),
            pl.BlockSpec((D_MODEL, D_MODEL), lambda j: (0, j)),
            pl.BlockSpec((1, D_MODEL), lambda j: (0, j)),
        ],
        out_specs=pl.BlockSpec((rows, D_MODEL), lambda j: (0, j)),
        out_shape=jax.ShapeDtypeStruct((rows, N_MOD * D_MODEL), f32),
        compiler_params=pltpu.CompilerParams(dimension_semantics=("arbitrary",)),
        name="ada",
    )(c_all, w_ada, b_ada.reshape(1, -1))


def _mixer_prompt_kernel(sinks_ref, x_ref, ada_ref, g_ref, w_in_ref, cw_ref, w_a_ref, w_b_ref, w_o_ref,
                         x1_ref, conv_ref, knew_ref, vnew_ref,
                         ubuf, klo, khi, vt, attbuf, *, tb):
    i = pl.program_id(0)

    @pl.when(i == 0)
    def _():
        ubuf[0:SUBLANES, :] = jnp.zeros((SUBLANES, D_CONV), f32)
        for r in (klo, khi):
            r[:, 0:WINDOW, :] = jnp.zeros((N_KV, WINDOW, LANES), bf16)
        vt[:, 0:WINDOW] = jnp.zeros((KV_COLS, WINDOW), bf16)

    x = x_ref[...]
    sh1, sc1, gt1 = ada_ref[0:1, :], ada_ref[1:2, :], ada_ref[2:3, :]
    h = (_rms(x, g_ref[...]) * (1.0 + sc1) + sh1).astype(bf16)

    def proj(c0, n):
        return _dot(h, w_in_ref[:, c0:c0 + n])

    u = proj(C_C, D_CONV) * proj(C_XIN, D_CONV)
    ubuf[SUBLANES:SUBLANES + tb, :] = u
    conv = (cw_ref[0:1, :] * ubuf[SUBLANES - 2:SUBLANES - 2 + tb, :]
            + cw_ref[1:2, :] * ubuf[SUBLANES - 1:SUBLANES - 1 + tb, :]
            + cw_ref[2:3, :] * u)
    ya = _dot((proj(C_B, D_CONV) * conv).astype(bf16), w_a_ref[...])

    q = (proj(C_Q, N_HEADS * HEAD_DIM) * ATTN_SCALE).astype(bf16)
    k = proj(C_K, KV_COLS)
    v = proj(C_V, KV_COLS)
    knew_ref[...] = k[tb - WINDOW:, :]
    vnew_ref[...] = v[tb - WINDOW:, :]

    lo = lax.broadcasted_iota(jnp.int32, (tb, LANES), 1) < HEAD_DIM
    for j in range(N_KV // 2):
        pair = k[:, LANES * j:LANES * (j + 1)]
        rolled = pltpu.roll(pair, HEAD_DIM, axis=1)
        zero = jnp.zeros_like(pair)
        klo[2 * j, WINDOW:WINDOW + tb, :] = jnp.where(lo, pair, zero).astype(bf16)
        khi[2 * j, WINDOW:WINDOW + tb, :] = jnp.where(lo, zero, rolled).astype(bf16)
        klo[2 * j + 1, WINDOW:WINDOW + tb, :] = jnp.where(lo, rolled, zero).astype(bf16)
        khi[2 * j + 1, WINDOW:WINDOW + tb, :] = jnp.where(lo, zero, pair).astype(bf16)
    vt[:, WINDOW:WINDOW + tb] = v.T.astype(bf16)

    cc = lax.broadcasted_iota(jnp.int32, (Q_SUB + WINDOW, 2 * Q_SUB), 0)
    col = lax.broadcasted_iota(jnp.int32, (Q_SUB + WINDOW, 2 * Q_SUB), 1)
    rr = col & (Q_SUB - 1)
    band = (cc >= rr) & (cc <= rr + WINDOW)
    first_head = lax.broadcasted_iota(jnp.int32, (1, 2 * Q_SUB), 1) < Q_SUB
    mask0 = band & (cc >= jnp.where(i == 0, WINDOW, 0))

    def scores(j, g, t):
        r0 = Q_SUB * j
        qs = jnp.concatenate([q[r0:r0 + Q_SUB, 2 * LANES * g:2 * LANES * g + LANES],
                              q[r0:r0 + Q_SUB, 2 * LANES * g + LANES:2 * LANES * (g + 1)]], axis=0)
        kr = (klo, khi)[t]
        return lax.dot_general(kr[g, r0:r0 + Q_SUB + WINDOW, :], qs, (((1,), (1,)), ((), ())),
                               preferred_element_type=f32)

    def finish(j, g, t, st):
        r0 = Q_SUB * j
        h0, h1 = GROUP * g + t, GROUP * g + 2 + t
        sink = jnp.where(first_head, sinks_ref[h0], sinks_ref[h1])
        st = jnp.where(mask0 if j == 0 else band, st, -jnp.inf)
        m = jnp.maximum(jnp.max(st, axis=0, keepdims=True), sink)
        e = jnp.exp(st - m)
        linv = 1.0 / (jnp.sum(e, axis=0, keepdims=True) + jnp.exp(sink - m))
        vtg = vt[HEAD_DIM * g:HEAD_DIM * (g + 1), r0:r0 + Q_SUB + WINDOW]
        ot = _dot(vtg, e.astype(bf16)) * linv
        attbuf[HEAD_DIM * h0:HEAD_DIM * (h0 + 1), r0:r0 + Q_SUB] = ot[:, 0:Q_SUB]
        attbuf[HEAD_DIM * h1:HEAD_DIM * (h1 + 1), r0:r0 + Q_SUB] = ot[:, Q_SUB:2 * Q_SUB]

    chains = [(j, g, t) for j in range(tb // Q_SUB) for g in range(N_KV) for t in range(2)]
    pending = [scores(*c) for c in chains[:ATTN_LOOKAHEAD]]
    for n, chain in enumerate(chains):
        if n + ATTN_LOOKAHEAD < len(chains):
            pending.append(scores(*chains[n + ATTN_LOOKAHEAD]))
        finish(*chain, pending.pop(0))

    yb = _dot(attbuf[...].T.astype(bf16), w_b_ref[...])
    mix = jax.nn.sigmoid(proj(C_GA, D_MODEL)) * ya + jax.nn.sigmoid(proj(C_GB, D_MODEL)) * yb
    x1_ref[...] = x + gt1 * _dot(mix.astype(bf16), w_o_ref[...])

    conv_ref[...] = ubuf[SUBLANES + tb - (CONV_W - 1):SUBLANES + tb, :]
    ubuf[0:SUBLANES, :] = ubuf[tb:tb + SUBLANES, :]
    for r in (klo, khi):
        r[:, 0:WINDOW, :] = r[:, tb:tb + WINDOW, :]
    vt[:, 0:WINDOW] = vt[:, tb:tb + WINDOW]


def _mixer_prompt(x, ada8, g_mix, w_in, conv_w, sinks, w_a, w_b, w_o, *, tb):
    s = x.shape[0]
    assert s % tb == 0 and tb % Q_SUB == 0 and tb >= WINDOW
    kv_scratch = pltpu.VMEM((N_KV, WINDOW + tb, LANES), bf16)
    grid_spec = pltpu.PrefetchScalarGridSpec(
        num_scalar_prefetch=1,
        grid=(s // tb,),
        in_specs=[
            pl.BlockSpec((tb, D_MODEL), lambda i, sk: (i, 0)),
            pl.BlockSpec((SUBLANES, D_MODEL), lambda i, sk: (0, 0)),
            pl.BlockSpec((1, D_MODEL), lambda i, sk: (0, 0)),
            pl.BlockSpec((D_MODEL, IN_COLS), lambda i, sk: (0, 0), pipeline_mode=pl.Buffered(1)),
            pl.BlockSpec((CONV_W, D_CONV), lambda i, sk: (0, 0)),
            pl.BlockSpec((D_CONV, D_MODEL), lambda i, sk: (0, 0), pipeline_mode=pl.Buffered(1)),
            pl.BlockSpec((N_HEADS * HEAD_DIM, D_MODEL), lambda i, sk: (0, 0), pipeline_mode=pl.Buffered(1)),
            pl.BlockSpec((D_MODEL, D_MODEL), lambda i, sk: (0, 0), pipeline_mode=pl.Buffered(1)),
        ],
        out_specs=[
            pl.BlockSpec((tb, D_MODEL), lambda i, sk: (i, 0)),
            pl.BlockSpec((CONV_W - 1, D_CONV), lambda i, sk: (0, 0)),
            pl.BlockSpec((WINDOW, KV_COLS), lambda i, sk: (0, 0)),
            pl.BlockSpec((WINDOW, KV_COLS), lambda i, sk: (0, 0)),
        ],
        scratch_shapes=[
            pltpu.VMEM((SUBLANES + tb, D_CONV), f32),
            kv_scratch, kv_scratch,
            pltpu.VMEM((KV_COLS, WINDOW + tb), bf16),
            pltpu.VMEM((N_HEADS * HEAD_DIM, tb), f32),
        ],
    )
    return pl.pallas_call(
        functools.partial(_mixer_prompt_kernel, tb=tb),
        grid_spec=grid_spec,
        out_shape=[
            jax.ShapeDtypeStruct((s, D_MODEL), f32),
            jax.ShapeDtypeStruct((CONV_W - 1, D_CONV), f32),
            jax.ShapeDtypeStruct((WINDOW, KV_COLS), f32),
            jax.ShapeDtypeStruct((WINDOW, KV_COLS), f32),
        ],
        compiler_params=pltpu.CompilerParams(
            dimension_semantics=("arbitrary",),
            vmem_limit_bytes=VMEM_BYTES_V7X * 7 // 8),
        name="mixer_prompt",
    )(sinks, x, ada8, g_mix, w_in, conv_w, w_a, w_b, w_o)


def _ffn_prompt_kernel(x_ref, ada_ref, g_ref, w_up_ref, fcw_ref, fcb_ref, w_down_ref, gf_ref,
                       y_ref, fst_ref, upbuf, actbuf, *, tb, ch, nw):
    i = pl.program_id(0)

    @pl.when(i == 0)
    def _():
        upbuf[:, 0:SUBLANES, :] = jnp.zeros((2 * D_FF // LANES, SUBLANES, LANES), f32)

    hb = tb // 2
    sh2, sc2, gt2 = ada_ref[3:4, :], ada_ref[4:5, :], ada_ref[5:6, :]
    h = (_rms(x_ref[...], g_ref[...]) * (1.0 + sc2) + sh2).astype(bf16)

    def up_cols(half, c0):
        up = _dot(h[half * hb:(half + 1) * hb, :], w_up_ref[:, c0:c0 + ch])
        for s in range(ch // LANES):
            upbuf[c0 // LANES + s, SUBLANES:SUBLANES + hb, :] = up[:, s * LANES:(s + 1) * LANES]
        return up

    def conv_cols(c0, up):
        pieces = []
        for s in range(ch // LANES):
            slab = c0 // LANES + s
            cols = slice(c0 + s * LANES, c0 + (s + 1) * LANES)
            pieces.append(fcw_ref[0:1, cols] * upbuf[slab, SUBLANES - 2:SUBLANES - 2 + hb, :]
                          + fcw_ref[1:2, cols] * upbuf[slab, SUBLANES - 1:SUBLANES - 1 + hb, :]
                          + fcw_ref[2:3, cols] * up[:, s * LANES:(s + 1) * LANES]
                          + fcb_ref[0:1, cols])
            upbuf[slab, 0:SUBLANES, :] = upbuf[slab, hb:hb + SUBLANES, :]
        return jnp.concatenate(pieces, axis=1)

    def down_cols(half, n0):
        return _dot(actbuf[half], w_down_ref[:, n0:n0 + nw])

    def finish(half, parts):
        rows = slice(half * hb, (half + 1) * hb)
        x2 = x_ref[rows, :] + gt2 * jnp.concatenate(parts, axis=1)
        y_ref[rows, :] = _rms(x2, gf_ref[...])

    chunks = list(range(0, D_FF, ch))
    down_starts = list(range(0, D_MODEL, nw))
    down_at = {len(chunks) * (k + 1) // (len(down_starts) + 1): n0 for k, n0 in enumerate(down_starts)}
    assert len(down_at) == len(down_starts)
    for half in range(2):
        parts = []
        pending = [(up_cols(half, c0), up_cols(half, D_FF + c0)) for c0 in chunks[:FFN_LOOKAHEAD]]
        for n, c0 in enumerate(chunks):
            if n + FFN_LOOKAHEAD < len(chunks):
                c1 = chunks[n + FFN_LOOKAHEAD]
                pending.append((up_cols(half, c1), up_cols(half, D_FF + c1)))
            if half == 1 and n in down_at:
                parts.append(down_cols(0, down_at[n]))
            up_g, up_v = pending.pop(0)
            actbuf[half, :, c0:c0 + ch] = (_silu(conv_cols(c0, up_g))
                                           * conv_cols(D_FF + c0, up_v)).astype(bf16)
        if half == 1:
            finish(0, parts)
    finish(1, [down_cols(1, n0) for n0 in down_starts])

    for slab in range(2 * D_FF // LANES):
        fst_ref[:, slab * LANES:(slab + 1) * LANES] = upbuf[slab, SUBLANES - (CONV_W - 1):SUBLANES, :]


def _ffn_prompt(x1, ada8, g_ffn, w_up, fcw, fcb, w_down, g_final, *, tb, ch, nw):
    s = x1.shape[0]
    assert s % tb == 0 and tb % (4 * SUBLANES) == 0
    assert D_FF % ch == 0 and ch % LANES == 0 and D_MODEL % nw == 0 and nw % LANES == 0
    return pl.pallas_call(
        functools.partial(_ffn_prompt_kernel, tb=tb, ch=ch, nw=nw),
        grid=(s // tb,),
        in_specs=[
            pl.BlockSpec((tb, D_MODEL), lambda i: (i, 0)),
            pl.BlockSpec((SUBLANES, D_MODEL), lambda i: (0, 0)),
            pl.BlockSpec((1, D_MODEL), lambda i: (0, 0)),
            _const_spec((D_MODEL, 2 * D_FF)),
            pl.BlockSpec((CONV_W, 2 * D_FF), lambda i: (0, 0)),
            pl.BlockSpec((1, 2 * D_FF), lambda i: (0, 0)),
            _const_spec((D_FF, D_MODEL)),
            pl.BlockSpec((1, D_MODEL), lambda i: (0, 0)),
        ],
        out_specs=[
            pl.BlockSpec((tb, D_MODEL), lambda i: (i, 0)),
            pl.BlockSpec((CONV_W - 1, 2 * D_FF), lambda i: (0, 0)),
        ],
        out_shape=[
            jax.ShapeDtypeStruct((s, D_MODEL), f32),
            jax.ShapeDtypeStruct((CONV_W - 1, 2 * D_FF), f32),
        ],
        scratch_shapes=[
            pltpu.VMEM((2 * D_FF // LANES, SUBLANES + tb // 2, LANES), f32),
            pltpu.VMEM((2, tb // 2, D_FF), bf16),
        ],
        compiler_params=pltpu.CompilerParams(
            dimension_semantics=("arbitrary",),
            vmem_limit_bytes=VMEM_BYTES_V7X * 7 // 8),
        name="ffn_prompt",
    )(x1, ada8, g_ffn, w_up, fcw, fcb, w_down, g_final)


def _mixer_decode_pre_kernel(x_ref, ada_ref, g_ref, w_in_ref, cw_ref, st_ref, w_a_ref,
                             q_ref, k_ref, v_ref, za_ref, sgb_ref, stn_ref):
    x = x_ref[...]
    sh1 = ada_ref[:, 0:D_MODEL]
    sc1 = ada_ref[:, D_MODEL:2 * D_MODEL]
    h = (_rms(x, g_ref[...]) * (1.0 + sc1) + sh1).astype(bf16)

    def proj(c0, n):
        return _dot(h, w_in_ref[:, c0:c0 + n])

    u = proj(C_C, D_CONV) * proj(C_XIN, D_CONV)
    prev0 = st_ref[:, 0:D_CONV]
    prev1 = st_ref[:, D_CONV:2 * D_CONV]
    conv = cw_ref[0:1, :] * prev0 + cw_ref[1:2, :] * prev1 + cw_ref[2:3, :] * u
    stn_ref[:, 0:D_CONV] = prev1
    stn_ref[:, D_CONV:2 * D_CONV] = u
    ya = _dot((proj(C_B, D_CONV) * conv).astype(bf16), w_a_ref[...])
    q_ref[...] = proj(C_Q, N_HEADS * HEAD_DIM) * ATTN_SCALE
    k_ref[...] = proj(C_K, KV_COLS)
    v_ref[...] = proj(C_V, KV_COLS)
    za_ref[...] = jax.nn.sigmoid(proj(C_GA, D_MODEL)) * ya
    sgb_ref[...] = jax.nn.sigmoid(proj(C_GB, D_MODEL))


def _mixer_decode_pre(x, ada, g_mix, w_in, conv_w, state, w_a):
    n = x.shape[0]
    shapes = [(n, N_HEADS * HEAD_DIM), (n, KV_COLS), (n, KV_COLS), (n, D_MODEL), (n, D_MODEL),
              (n, (CONV_W - 1) * D_CONV)]
    return pl.pallas_call(
        _mixer_decode_pre_kernel,
        out_shape=[jax.ShapeDtypeStruct(s, f32) for s in shapes],
        compiler_params=pltpu.CompilerParams(vmem_limit_bytes=VMEM_BYTES_V7X * 5 // 8),
        name="mixer_decode_pre",
    )(x, ada, g_mix, w_in, conv_w, state, w_a)


def _attn_decode_kernel(qr_ref, kn_ref, vn_ref, ck_ref, cv_ref, sink_ref, e_ref, et_ref,
                        att_ref, ok_ref, ov_ref, *, bb):
    ok_ref[:, 0:WINDOW - 1, :] = ck_ref[:, 1:WINDOW, :]
    ok_ref[:, WINDOW - 1:WINDOW, :] = kn_ref[...]
    ov_ref[:, 0:WINDOW - 1, :] = cv_ref[:, 1:WINDOW, :]
    ov_ref[:, WINDOW - 1:WINDOW, :] = vn_ref[...]
    kw = ok_ref[...]
    vw = ov_ref[...]
    k0 = ck_ref[:, 0:1, :]
    v0 = cv_ref[:, 0:1, :]
    e = e_ref[...]
    et = et_ref[...]

    def seg_sum(prod):
        r = prod.shape[1]
        return _dot(prod.astype(bf16).reshape(bb * r, KV_COLS), e).reshape(bb, r, LANES)

    def seg_expand(p):
        r = p.shape[1]
        return _dot(p.astype(bf16).reshape(bb * r, LANES), et).reshape(bb, r, KV_COLS)

    for i in range(GROUP):
        qi = qr_ref[:, i:i + 1, :]
        sink = sink_ref[i:i + 1, :].reshape(1, 1, LANES)
        s_w = seg_sum(kw * qi)
        s_0 = seg_sum(jnp.broadcast_to(k0 * qi, (bb, SUBLANES, KV_COLS)))
        s_0 = s_0[:, 0:1, :]
        m = jnp.maximum(jnp.maximum(jnp.max(s_w, axis=1, keepdims=True), s_0), sink)
        e_w = jnp.exp(s_w - m)
        e_0 = jnp.exp(s_0 - m)
        linv = 1.0 / (jnp.sum(e_w, axis=1, keepdims=True) + e_0 + jnp.exp(sink - m))
        p_w = seg_expand(e_w * linv)
        p_0 = seg_expand(jnp.broadcast_to(e_0 * linv, (bb, SUBLANES, LANES)))[:, 0:1, :]
        att_ref[:, i:i + 1, :] = jnp.sum(p_w * vw, axis=1, keepdims=True) + p_0 * v0


def _attn_decode(qr, kn, vn, ck, cv, sink_r, *, bb):
    n = qr.shape[0]
    assert n % bb == 0
    lane_group = jnp.arange(KV_COLS, dtype=jnp.int32) // HEAD_DIM
    e = (lane_group[:, None] == jnp.arange(LANES, dtype=jnp.int32)[None, :]).astype(bf16)
    cache_spec = pl.BlockSpec((bb, WINDOW, KV_COLS), lambda b: (b, 0, 0))
    row_spec = pl.BlockSpec((bb, 1, KV_COLS), lambda b: (b, 0, 0))
    q_spec = pl.BlockSpec((bb, GROUP, KV_COLS), lambda b: (b, 0, 0))
    return pl.pallas_call(
        functools.partial(_attn_decode_kernel, bb=bb),
        grid=(n // bb,),
        in_specs=[q_spec, row_spec, row_spec, cache_spec, cache_spec,
                  pl.BlockSpec((GROUP, LANES), lambda b: (0, 0)),
                  pl.BlockSpec((KV_COLS, LANES), lambda b: (0, 0)),
                  pl.BlockSpec((LANES, KV_COLS), lambda b: (0, 0))],
        out_specs=[q_spec, cache_spec, cache_spec],
        out_shape=[jax.ShapeDtypeStruct((n, GROUP, KV_COLS), f32),
                   jax.ShapeDtypeStruct((n, WINDOW, KV_COLS), f32),
                   jax.ShapeDtypeStruct((n, WINDOW, KV_COLS), f32)],
        compiler_params=pltpu.CompilerParams(
            dimension_semantics=("arbitrary",),
            vmem_limit_bytes=VMEM_BYTES_V7X // 2),
        name="attn_decode",
    )(qr, kn, vn, ck, cv, sink_r, e, e.T)


def _post_decode_kernel(x_ref, ada_ref, za_ref, sgb_ref, att_ref, w_b_ref, w_o_ref, g_ref, w_up_ref,
                        fcw_ref, fcb_ref, w_down_ref, gf_ref, fst_ref, y_ref, fstn_ref):
    def mod(j):
        return ada_ref[:, j * D_MODEL:(j + 1) * D_MODEL]

    yb = _dot(att_ref[...].astype(bf16), w_b_ref[...])
    mix = za_ref[...] + sgb_ref[...] * yb
    x1 = x_ref[...] + mod(2) * _dot(mix.astype(bf16), w_o_ref[...])
    h = (_rms(x1, g_ref[...]) * (1.0 + mod(4)) + mod(3)).astype(bf16)
    up = _dot(h, w_up_ref[...])
    prev0 = fst_ref[:, 0:2 * D_FF]
    prev1 = fst_ref[:, 2 * D_FF:4 * D_FF]
    conv = fcw_ref[0:1, :] * prev0 + fcw_ref[1:2, :] * prev1 + fcw_ref[2:3, :] * up + fcb_ref[...]
    fstn_ref[:, 0:2 * D_FF] = prev1
    fstn_ref[:, 2 * D_FF:4 * D_FF] = up
    act = (_silu(conv[:, 0:D_FF]) * conv[:, D_FF:2 * D_FF]).astype(bf16)
    x2 = x1 + mod(5) * _dot(act, w_down_ref[...])
    y_ref[...] = _rms(x2, gf_ref[...])


def _post_decode(x, ada, za, sgb, att, w_b, w_o, g_ffn, w_up, fcw, fcb, w_down, g_final, fstate):
    n = x.shape[0]
    return pl.pallas_call(
        _post_decode_kernel,
        out_shape=[jax.ShapeDtypeStruct((n, D_MODEL), f32),
                   jax.ShapeDtypeStruct((n, (CONV_W - 1) * 2 * D_FF), f32)],
        compiler_params=pltpu.CompilerParams(vmem_limit_bytes=VMEM_BYTES_V7X * 7 // 8),
        name="post_decode",
    )(x, ada, za, sgb, att, w_b, w_o, g_ffn, w_up, fcw, fcb, w_down, g_final, fstate)


PROMPT_MIXER_ROWS = 512
PROMPT_FFN_ROWS = 512
PROMPT_FFN_COLS = 256
PROMPT_FFN_DOWN_COLS = 256
DECODE_ATTN_BATCH = 8


def kernel(x_prompt, x_sample, c_prompt, c_sample, state_conv_a, cache_k_win, cache_v_win, state_ffn_conv, w_ada, b_ada, g_mix, w_in, conv_a_w, attn_sinks, w_a_out, w_b_out, w_o, g_ffn, w_up, ffn_conv_w, ffn_conv_b, w_down, g_final):
    depth = w_in.shape[0]
    n_p, seq, _ = x_prompt.shape
    n_s, t_s, _ = x_sample.shape
    assert n_p == 1 and t_s == 1, "one prompt sequence and single-token decode only"
    xp = x_prompt.reshape(seq, D_MODEL)
    xs = x_sample.reshape(n_s, D_MODEL)
    pad = (-(n_s + n_p)) % SUBLANES
    c_all = jnp.concatenate([c_sample, c_prompt, jnp.zeros((pad, D_MODEL), f32)], axis=0)
    gf = g_final.reshape(1, D_MODEL)
    outs = [[] for _ in range(8)]
    for l in range(depth):
        ada = _ada(c_all, w_ada[l], b_ada[l])
        ada_s = ada[:n_s]
        ada_p = jnp.pad(ada[n_s].reshape(N_MOD, D_MODEL), ((0, SUBLANES - N_MOD), (0, 0)))
        w_in_b, w_a_b, w_b_b, w_o_b = (w[l].astype(bf16) for w in (w_in, w_a_out, w_b_out, w_o))
        w_up_b, w_down_b = w_up[l].astype(bf16), w_down[l].astype(bf16)
        gm, gn = g_mix[l].reshape(1, D_MODEL), g_ffn[l].reshape(1, D_MODEL)
        fcb = ffn_conv_b[l].reshape(1, 2 * D_FF)

        x1, conv_p, k_p, v_p = _mixer_prompt(xp, ada_p, gm, w_in_b, conv_a_w[l], attn_sinks[l],
                                             w_a_b, w_b_b, w_o_b, tb=PROMPT_MIXER_ROWS)
        xp, ffn_p = _ffn_prompt(x1, ada_p, gn, w_up_b, ffn_conv_w[l], fcb, w_down_b, gf,
                                tb=PROMPT_FFN_ROWS, ch=PROMPT_FFN_COLS, nw=PROMPT_FFN_DOWN_COLS)

        q, k_n, v_n, za, sgb, conv_s = _mixer_decode_pre(
            xs, ada_s, gm, w_in_b, conv_a_w[l], state_conv_a[l].reshape(n_s, (CONV_W - 1) * D_CONV), w_a_b)
        qr = q.reshape(n_s, N_KV, GROUP, HEAD_DIM).transpose(0, 2, 1, 3).reshape(n_s, GROUP, KV_COLS)
        sink_r = jnp.pad(attn_sinks[l].reshape(N_KV, GROUP).T, ((0, 0), (0, LANES - N_KV)))
        att_r, k_s, v_s = _attn_decode(
            qr, k_n.reshape(n_s, 1, KV_COLS), v_n.reshape(n_s, 1, KV_COLS),
            cache_k_win[l].reshape(n_s, WINDOW, KV_COLS), cache_v_win[l].reshape(n_s, WINDOW, KV_COLS),
            sink_r, bb=DECODE_ATTN_BATCH)
        att = att_r.reshape(n_s, GROUP, N_KV, HEAD_DIM).transpose(0, 2, 1, 3).reshape(n_s, N_HEADS * HEAD_DIM)
        xs, ffn_s = _post_decode(xs, ada_s, za, sgb, att, w_b_b, w_o_b, gn, w_up_b, ffn_conv_w[l], fcb,
                                 w_down_b, gf, state_ffn_conv[l].reshape(n_s, (CONV_W - 1) * 2 * D_FF))

        for lst, val in zip(outs, (
                conv_p.reshape(n_p, CONV_W - 1, D_CONV), conv_s.reshape(n_s, CONV_W - 1, D_CONV),
                k_p.reshape(n_p, WINDOW, N_KV, HEAD_DIM), k_s.reshape(n_s, WINDOW, N_KV, HEAD_DIM),
                v_p.reshape(n_p, WINDOW, N_KV, HEAD_DIM), v_s.reshape(n_s, WINDOW, N_KV, HEAD_DIM),
                ffn_p.reshape(n_p, CONV_W - 1, 2 * D_FF), ffn_s.reshape(n_s, CONV_W - 1, 2 * D_FF))):
            lst.append(val)
    assert depth == 1, "final RMSNorm is fused into the single layer's FFN kernels"
    return (xp.reshape(n_p, seq, D_MODEL), xs.reshape(n_s, t_s, D_MODEL)) + tuple(jnp.stack(o) for o in outs)
```

```python
import functools

import jax
import jax.numpy as jnp
from jax import lax
from jax.experimental import pallas as pl
from jax.experimental.pallas import tpu as pltpu

f32 = jnp.float32
bf16 = jnp.bfloat16

D_MODEL = 1024
D_CONV = D_MODEL
CONV_W = 3
N_HEADS = 16
N_KV = 4
GROUP = N_HEADS // N_KV
HEAD_DIM = 64
WINDOW = 128
D_FF = 2816
EPS = 1e-6
N_MOD = 6
ATTN_SCALE = HEAD_DIM ** -0.5
KV_COLS = N_KV * HEAD_DIM
C_XIN, C_B, C_C = 0, D_CONV, 2 * D_CONV
C_Q = 3 * D_CONV
C_K = C_Q + N_HEADS * HEAD_DIM
C_V = C_K + KV_COLS
C_GA = C_V + KV_COLS
C_GB = C_GA + D_MODEL
IN_COLS = C_GB + D_MODEL

LANES = 128
SUBLANES = 8
Q_SUB = 128
ATTN_LOOKAHEAD = 3
FFN_LOOKAHEAD = 2
VMEM_BYTES_V7X = 64 * 1024 * 1024


def _rms(x, g):
    ms = jnp.mean(x * x, axis=-1, keepdims=True)
    return x * lax.rsqrt(ms + EPS) * g


def _silu(x):
    return x * jax.nn.sigmoid(x)


def _dot(a, b):
    return jnp.dot(a, b, preferred_element_type=f32)


def _const_spec(shape):
    nd = len(shape)
    return pl.BlockSpec(shape, lambda i: (0,) * nd, pipeline_mode=pl.Buffered(1))


def _ada_kernel(c_ref, w_ref, b_ref, o_ref):
    c = c_ref[...]
    o_ref[...] = _dot(_silu(c).astype(bf16), w_ref[...].astype(bf16)) + b_ref[...]


def _ada(c_all, w_ada, b_ada):
    rows = c_all.shape[0]
    return pl.pallas_call(
        _ada_kernel,
        grid=(N_MOD,),
        in_specs=[
            pl.BlockSpec((rows, D_MODEL), lambda j: (0, 0)),
            pl.BlockSpec((D_MODEL, D_MODEL), lambda j: (0, j)),
            pl.BlockSpec((1, D_MODEL), lambda j: (0, j)),
        ],
        out_specs=pl.BlockSpec((rows, D_MODEL), lambda j: (0, j)),
        out_shape=jax.ShapeDtypeStruct((rows, N_MOD * D_MODEL), f32),
        compiler_params=pltpu.CompilerParams(dimension_semantics=("arbitrary",)),
        name="ada",
    )(c_all, w_ada, b_ada.reshape(1, -1))


def _mixer_prompt_kernel(sinks_ref, x_ref, ada_ref, g_ref, w_in_ref, cw_ref, w_a_ref, w_b_ref, w_o_ref,
                         x1_ref, conv_ref, knew_ref, vnew_ref,
                         ubuf, klo, khi, vt, attbuf, *, tb):
    i = pl.program_id(0)

    @pl.when(i == 0)
    def _():
        ubuf[0:SUBLANES, :] = jnp.zeros((SUBLANES, D_CONV), f32)
        for r in (klo, khi):
            r[:, 0:WINDOW, :] = jnp.zeros((N_KV, WINDOW, LANES), bf16)
        vt[:, 0:WINDOW] = jnp.zeros((KV_COLS, WINDOW), bf16)

    x = x_ref[...]
    sh1, sc1, gt1 = ada_ref[0:1, :], ada_ref[1:2, :], ada_ref[2:3, :]
    h = (_rms(x, g_ref[...]) * (1.0 + sc1) + sh1).astype(bf16)

    def proj(c0, n):
        return _dot(h, w_in_ref[:, c0:c0 + n])

    u = proj(C_C, D_CONV) * proj(C_XIN, D_CONV)
    ubuf[SUBLANES:SUBLANES + tb, :] = u
    conv = (cw_ref[0:1, :] * ubuf[SUBLANES - 2:SUBLANES - 2 + tb, :]
            + cw_ref[1:2, :] * ubuf[SUBLANES - 1:SUBLANES - 1 + tb, :]
            + cw_ref[2:3, :] * u)
    ya = _dot((proj(C_B, D_CONV) * conv).astype(bf16), w_a_ref[...])

    q = (proj(C_Q, N_HEADS * HEAD_DIM) * ATTN_SCALE).astype(bf16)
    k = proj(C_K, KV_COLS)
    v = proj(C_V, KV_COLS)
    knew_ref[...] = k[tb - WINDOW:, :]
    vnew_ref[...] = v[tb - WINDOW:, :]

    lo = lax.broadcasted_iota(jnp.int32, (tb, LANES), 1) < HEAD_DIM
    for j in range(N_KV // 2):
        pair = k[:, LANES * j:LANES * (j + 1)]
        rolled = pltpu.roll(pair, HEAD_DIM, axis=1)
        zero = jnp.zeros_like(pair)
        klo[2 * j, WINDOW:WINDOW + tb, :] = jnp.where(lo, pair, zero).astype(bf16)
        khi[2 * j, WINDOW:WINDOW + tb, :] = jnp.where(lo, zero, rolled).astype(bf16)
        klo[2 * j + 1, WINDOW:WINDOW + tb, :] = jnp.where(lo, rolled, zero).astype(bf16)
        khi[2 * j + 1, WINDOW:WINDOW + tb, :] = jnp.where(lo, zero, pair).astype(bf16)
    vt[:, WINDOW:WINDOW + tb] = v.T.astype(bf16)

    cc = lax.broadcasted_iota(jnp.int32, (Q_SUB + WINDOW, 2 * Q_SUB), 0)
    col = lax.broadcasted_iota(jnp.int32, (Q_SUB + WINDOW, 2 * Q_SUB), 1)
    rr = col & (Q_SUB - 1)
    band = (cc >= rr) & (cc <= rr + WINDOW)
    first_head = lax.broadcasted_iota(jnp.int32, (1, 2 * Q_SUB), 1) < Q_SUB
    mask0 = band & (cc >= jnp.where(i == 0, WINDOW, 0))

    def scores(j, g, t):
        r0 = Q_SUB * j
        qs = jnp.concatenate([q[r0:r0 + Q_SUB, 2 * LANES * g:2 * LANES * g + LANES],
                              q[r0:r0 + Q_SUB, 2 * LANES * g + LANES:2 * LANES * (g + 1)]], axis=0)
        kr = (klo, khi)[t]
        return lax.dot_general(kr[g, r0:r0 + Q_SUB + WINDOW, :], qs, (((1,), (1,)), ((), ())),
                               preferred_element_type=f32)

    def finish(j, g, t, st):
        r0 = Q_SUB * j
        h0, h1 = GROUP * g + t, GROUP * g + 2 + t
        sink = jnp.where(first_head, sinks_ref[h0], sinks_ref[h1])
        st = jnp.where(mask0 if j == 0 else band, st, -jnp.inf)
        m = jnp.maximum(jnp.max(st, axis=0, keepdims=True), sink)
        e = jnp.exp(st - m)
        linv = 1.0 / (jnp.sum(e, axis=0, keepdims=True) + jnp.exp(sink - m))
        vtg = vt[HEAD_DIM * g:HEAD_DIM * (g + 1), r0:r0 + Q_SUB + WINDOW]
        ot = _dot(vtg, e.astype(bf16)) * linv
        attbuf[HEAD_DIM * h0:HEAD_DIM * (h0 + 1), r0:r0 + Q_SUB] = ot[:, 0:Q_SUB]
        attbuf[HEAD_DIM * h1:HEAD_DIM * (h1 + 1), r0:r0 + Q_SUB] = ot[:, Q_SUB:2 * Q_SUB]

    chains = [(j, g, t) for j in range(tb // Q_SUB) for g in range(N_KV) for t in range(2)]
    pending = [scores(*c) for c in chains[:ATTN_LOOKAHEAD]]
    for n, chain in enumerate(chains):
        if n + ATTN_LOOKAHEAD < len(chains):
            pending.append(scores(*chains[n + ATTN_LOOKAHEAD]))
        finish(*chain, pending.pop(0))

    yb = _dot(attbuf[...].T.astype(bf16), w_b_ref[...])
    mix = jax.nn.sigmoid(proj(C_GA, D_MODEL)) * ya + jax.nn.sigmoid(proj(C_GB, D_MODEL)) * yb
    x1_ref[...] = x + gt1 * _dot(mix.astype(bf16), w_o_ref[...])

    conv_ref[...] = ubuf[SUBLANES + tb - (CONV_W - 1):SUBLANES + tb, :]
    ubuf[0:SUBLANES, :] = ubuf[tb:tb + SUBLANES, :]
    for r in (klo, khi):
        r[:, 0:WINDOW, :] = r[:, tb:tb + WINDOW, :]
    vt[:, 0:WINDOW] = vt[:, tb:tb + WINDOW]


def _mixer_prompt(x, ada8, g_mix, w_in, conv_w, sinks, w_a, w_b, w_o, *, tb):
    s = x.shape[0]
    assert s % tb == 0 and tb % Q_SUB == 0 and tb >= WINDOW
    kv_scratch = pltpu.VMEM((N_KV, WINDOW + tb, LANES), bf16)
    grid_spec = pltpu.PrefetchScalarGridSpec(
        num_scalar_prefetch=1,
        grid=(s // tb,),
        in_specs=[
            pl.BlockSpec((tb, D_MODEL), lambda i, sk: (i, 0)),
            pl.BlockSpec((SUBLANES, D_MODEL), lambda i, sk: (0, 0)),
            pl.BlockSpec((1, D_MODEL), lambda i, sk: (0, 0)),
            pl.BlockSpec((D_MODEL, IN_COLS), lambda i, sk: (0, 0), pipeline_mode=pl.Buffered(1)),
            pl.BlockSpec((CONV_W, D_CONV), lambda i, sk: (0, 0)),
            pl.BlockSpec((D_CONV, D_MODEL), lambda i, sk: (0, 0), pipeline_mode=pl.Buffered(1)),
            pl.BlockSpec((N_HEADS * HEAD_DIM, D_MODEL), lambda i, sk: (0, 0), pipeline_mode=pl.Buffered(1)),
            pl.BlockSpec((D_MODEL, D_MODEL), lambda i, sk: (0, 0), pipeline_mode=pl.Buffered(1)),
        ],
        out_specs=[
            pl.BlockSpec((tb, D_MODEL), lambda i, sk: (i, 0)),
            pl.BlockSpec((CONV_W - 1, D_CONV), lambda i, sk: (0, 0)),
            pl.BlockSpec((WINDOW, KV_COLS), lambda i, sk: (0, 0)),
            pl.BlockSpec((WINDOW, KV_COLS), lambda i, sk: (0, 0)),
        ],
        scratch_shapes=[
            pltpu.VMEM((SUBLANES + tb, D_CONV), f32),
            kv_scratch, kv_scratch,
            pltpu.VMEM((KV_COLS, WINDOW + tb), bf16),
            pltpu.VMEM((N_HEADS * HEAD_DIM, tb), f32),
        ],
    )
    return pl.pallas_call(
        functools.partial(_mixer_prompt_kernel, tb=tb),
        grid_spec=grid_spec,
        out_shape=[
            jax.ShapeDtypeStruct((s, D_MODEL), f32),
            jax.ShapeDtypeStruct((CONV_W - 1, D_CONV), f32),
            jax.ShapeDtypeStruct((WINDOW, KV_COLS), f32),
            jax.ShapeDtypeStruct((WINDOW, KV_COLS), f32),
        ],
        compiler_params=pltpu.CompilerParams(
            dimension_semantics=("arbitrary",),
            vmem_limit_bytes=VMEM_BYTES_V7X * 7 // 8),
        name="mixer_prompt",
    )(sinks, x, ada8, g_mix, w_in, conv_w, w_a, w_b, w_o)


def _ffn_prompt_kernel(x_ref, ada_ref, g_ref, w_up_ref, fcw_ref, fcb_ref, w_down_ref, gf_ref,
                       y_ref, fst_ref, upbuf, actbuf, *, tb, ch, nw):
    i = pl.program_id(0)

    @pl.when(i == 0)
    def _():
        upbuf[:, 0:SUBLANES, :] = jnp.zeros((2 * D_FF // LANES, SUBLANES, LANES), f32)

    hb = tb // 2
    sh2, sc2, gt2 = ada_ref[3:4, :], ada_ref[4:5, :], ada_ref[5:6, :]
    h = (_rms(x_ref[...], g_ref[...]) * (1.0 + sc2) + sh2).astype(bf16)

    def up_cols(half, c0):
        up = _dot(h[half * hb:(half + 1) * hb, :], w_up_ref[:, c0:c0 + ch])
        for s in range(ch // LANES):
            upbuf[c0 // LANES + s, SUBLANES:SUBLANES + hb, :] = up[:, s * LANES:(s + 1) * LANES]
        return up

    def conv_cols(c0, up):
        pieces = []
        for s in range(ch // LANES):
            slab = c0 // LANES + s
            cols = slice(c0 + s * LANES, c0 + (s + 1) * LANES)
            pieces.append(fcw_ref[0:1, cols] * upbuf[slab, SUBLANES - 2:SUBLANES - 2 + hb, :]
                          + fcw_ref[1:2, cols] * upbuf[slab, SUBLANES - 1:SUBLANES - 1 + hb, :]
                          + fcw_ref[2:3, cols] * up[:, s * LANES:(s + 1) * LANES]
                          + fcb_ref[0:1, cols])
            upbuf[slab, 0:SUBLANES, :] = upbuf[slab, hb:hb + SUBLANES, :]
        return jnp.concatenate(pieces, axis=1)

    def down_cols(half, n0):
        return _dot(actbuf[half], w_down_ref[:, n0:n0 + nw])

    def finish(half, parts):
        rows = slice(half * hb, (half + 1) * hb)
        x2 = x_ref[rows, :] + gt2 * jnp.concatenate(parts, axis=1)
        y_ref[rows, :] = _rms(x2, gf_ref[...])

    chunks = list(range(0, D_FF, ch))
    down_starts = list(range(0, D_MODEL, nw))
    down_at = {len(chunks) * (k + 1) // (len(down_starts) + 1): n0 for k, n0 in enumerate(down_starts)}
    assert len(down_at) == len(down_starts)
    for half in range(2):
        parts = []
        pending = [(up_cols(half, c0), up_cols(half, D_FF + c0)) for c0 in chunks[:FFN_LOOKAHEAD]]
        for n, c0 in enumerate(chunks):
            if n + FFN_LOOKAHEAD < len(chunks):
                c1 = chunks[n + FFN_LOOKAHEAD]
                pending.append((up_cols(half, c1), up_cols(half, D_FF + c1)))
            if half == 1 and n in down_at:
                parts.append(down_cols(0, down_at[n]))
            up_g, up_v = pending.pop(0)
            actbuf[half, :, c0:c0 + ch] = (_silu(conv_cols(c0, up_g))
                                           * conv_cols(D_FF + c0, up_v)).astype(bf16)
        if half == 1:
            finish(0, parts)
    finish(1, [down_cols(1, n0) for n0 in down_starts])

    for slab in range(2 * D_FF // LANES):
        fst_ref[:, slab * LANES:(slab + 1) * LANES] = upbuf[slab, SUBLANES - (CONV_W - 1):SUBLANES, :]


def _ffn_prompt(x1, ada8, g_ffn, w_up, fcw, fcb, w_down, g_final, *, tb, ch, nw):
    s = x1.shape[0]
    assert s % tb == 0 and tb % (4 * SUBLANES) == 0
    assert D_FF % ch == 0 and ch % LANES == 0 and D_MODEL % nw == 0 and nw % LANES == 0
    return pl.pallas_call(
        functools.partial(_ffn_prompt_kernel, tb=tb, ch=ch, nw=nw),
        grid=(s // tb,),
        in_specs=[
            pl.BlockSpec((tb, D_MODEL), lambda i: (i, 0)),
            pl.BlockSpec((SUBLANES, D_MODEL), lambda i: (0, 0)),
            pl.BlockSpec((1, D_MODEL), lambda i: (0, 0)),
            _const_spec((D_MODEL, 2 * D_FF)),
            pl.BlockSpec((CONV_W, 2 * D_FF), lambda i: (0, 0)),
            pl.BlockSpec((1, 2 * D_FF), lambda i: (0, 0)),
            _const_spec((D_FF, D_MODEL)),
            pl.BlockSpec((1, D_MODEL), lambda i: (0, 0)),
        ],
        out_specs=[
            pl.BlockSpec((tb, D_MODEL), lambda i: (i, 0)),
            pl.BlockSpec((CONV_W - 1, 2 * D_FF), lambda i: (0, 0)),
        ],
        out_shape=[
            jax.ShapeDtypeStruct((s, D_MODEL), f32),
            jax.ShapeDtypeStruct((CONV_W - 1, 2 * D_FF), f32),
        ],
        scratch_shapes=[
            pltpu.VMEM((2 * D_FF // LANES, SUBLANES + tb // 2, LANES), f32),
            pltpu.VMEM((2, tb // 2, D_FF), bf16),
        ],
        compiler_params=pltpu.CompilerParams(
            dimension_semantics=("arbitrary",),
            vmem_limit_bytes=VMEM_BYTES_V7X * 7 // 8),
        name="ffn_prompt",
    )(x1, ada8, g_ffn, w_up, fcw, fcb, w_down, g_final)


def _mixer_decode_pre_kernel(x_ref, ada_ref, g_ref, w_in_ref, cw_ref, st_ref, w_a_ref,
                             q_ref, k_ref, v_ref, za_ref, sgb_ref, stn_ref):
    x = x_ref[...]
    sh1 = ada_ref[:, 0:D_MODEL]
    sc1 = ada_ref[:, D_MODEL:2 * D_MODEL]
    h = (_rms(x, g_ref[...]) * (1.0 + sc1) + sh1).astype(bf16)

    def proj(c0, n):
        return _dot(h, w_in_ref[:, c0:c0 + n])

    u = proj(C_C, D_CONV) * proj(C_XIN, D_CONV)
    prev0 = st_ref[:, 0, :]
    prev1 = st_ref[:, 1, :]
    conv = cw_ref[0:1, :] * prev0 + cw_ref[1:2, :] * prev1 + cw_ref[2:3, :] * u
    stn_ref[:, 0, :] = prev1
    stn_ref[:, 1, :] = u
    ya = _dot((proj(C_B, D_CONV) * conv).astype(bf16), w_a_ref[...])
    q_ref[...] = proj(C_Q, N_HEADS * HEAD_DIM) * ATTN_SCALE
    k_ref[...] = proj(C_K, KV_COLS)
    v_ref[...] = proj(C_V, KV_COLS)
    za_ref[...] = jax.nn.sigmoid(proj(C_GA, D_MODEL)) * ya
    sgb_ref[...] = jax.nn.sigmoid(proj(C_GB, D_MODEL))


def _mixer_decode_pre(x, ada, g_mix, w_in, conv_w, state, w_a):
    n = x.shape[0]
    shapes = [(n, N_HEADS * HEAD_DIM), (n, KV_COLS), (n, KV_COLS), (n, D_MODEL), (n, D_MODEL),
              (n, CONV_W - 1, D_CONV)]
    return pl.pallas_call(
        _mixer_decode_pre_kernel,
        out_shape=[jax.ShapeDtypeStruct(s, f32) for s in shapes],
        compiler_params=pltpu.CompilerParams(vmem_limit_bytes=VMEM_BYTES_V7X * 5 // 8),
        name="mixer_decode_pre",
    )(x, ada, g_mix, w_in, conv_w, state, w_a)


def _attn_decode_kernel(q_ref, kn_ref, vn_ref, knt_ref, vnt_ref, ck_ref, cv_ref, sink_ref,
                        att_ref, ok_ref, ov_ref, *, bb):
    step = pl.program_id(0)
    last = lax.broadcasted_iota(jnp.int32, (HEAD_DIM, WINDOW), 1) == WINDOW - 1
    to_front = (LANES - step * bb) % LANES
    kstep = [pltpu.roll(knt_ref[g], to_front, axis=1) for g in range(N_KV)]
    vstep = [pltpu.roll(vnt_ref[g], to_front, axis=1) for g in range(N_KV)]

    pairs = [(b, g) for b in range(bb) for g in range(N_KV)]
    sinks = [sink_ref[g][:, 0:1] for g in range(N_KV)]
    s_old = [_dot(q_ref[b, g].astype(bf16), ck_ref[b, g].astype(bf16)) for b, g in pairs]
    s_new = [jnp.sum(q_ref[b, g] * kn_ref[b, g], axis=-1, keepdims=True) for b, g in pairs]
    m = [jnp.maximum(jnp.maximum(jnp.max(so, axis=-1, keepdims=True), sn), sinks[g])
         for (b, g), so, sn in zip(pairs, s_old, s_new)]
    e_old = [jnp.exp(so - mm) for so, mm in zip(s_old, m)]
    e_new = [jnp.exp(sn - mm) for sn, mm in zip(s_new, m)]
    den = [jnp.sum(eo, axis=-1, keepdims=True) + en + jnp.exp(sinks[g] - mm)
           for (b, g), eo, en, mm in zip(pairs, e_old, e_new, m)]
    o_old = [lax.dot_general(eo.astype(bf16), cv_ref[b, g].astype(bf16), (((1,), (1,)), ((), ())),
                             preferred_element_type=f32) for (b, g), eo in zip(pairs, e_old)]
    for (b, g), oo, en, dd in zip(pairs, o_old, e_new, den):
        att_ref[b, g] = (oo + en * vn_ref[b, g]) / dd
    for b, g in pairs:
        ok_ref[b, g] = jnp.where(last, pltpu.roll(kstep[g], WINDOW - 1 - b, axis=1),
                                 pltpu.roll(ck_ref[b, g], WINDOW - 1, axis=1))
        ov_ref[b, g] = jnp.where(last, pltpu.roll(vstep[g], WINDOW - 1 - b, axis=1),
                                 pltpu.roll(cv_ref[b, g], WINDOW - 1, axis=1))


def _attn_decode(q4, kn4, vn4, knt, vnt, ck, cv, sink_b, *, bb):
    n = q4.shape[0]
    assert n % bb == 0 and n == LANES and WINDOW == LANES
    cache_spec = pl.BlockSpec((bb, N_KV, HEAD_DIM, WINDOW), lambda b: (b, 0, 0, 0))
    row_spec = pl.BlockSpec((bb, N_KV, 1, HEAD_DIM), lambda b: (b, 0, 0, 0))
    q_spec = pl.BlockSpec((bb, N_KV, GROUP, HEAD_DIM), lambda b: (b, 0, 0, 0))
    new_t_spec = pl.BlockSpec((N_KV, HEAD_DIM, n), lambda b: (0, 0, 0))
    return pl.pallas_call(
        functools.partial(_attn_decode_kernel, bb=bb),
        grid=(n // bb,),
        in_specs=[q_spec, row_spec, row_spec, new_t_spec, new_t_spec, cache_spec, cache_spec,
                  pl.BlockSpec((N_KV, GROUP, LANES), lambda b: (0, 0, 0))],
        out_specs=[q_spec, cache_spec, cache_spec],
        out_shape=[jax.ShapeDtypeStruct((n, N_KV, GROUP, HEAD_DIM), f32),
                   jax.ShapeDtypeStruct((n, N_KV, HEAD_DIM, WINDOW), f32),
                   jax.ShapeDtypeStruct((n, N_KV, HEAD_DIM, WINDOW), f32)],
        compiler_params=pltpu.CompilerParams(dimension_semantics=("arbitrary",)),
        name="attn_decode",
    )(q4, kn4, vn4, knt, vnt, ck, cv, sink_b)


def _post_decode_kernel(x_ref, ada_ref, za_ref, sgb_ref, att_ref, w_b_ref, w_o_ref, g_ref, w_up_ref,
                        fcw_ref, fcb_ref, w_down_ref, gf_ref, fst_ref, y_ref, fstn_ref):
    def mod(j):
        return ada_ref[:, j * D_MODEL:(j + 1) * D_MODEL]

    yb = _dot(att_ref[...].astype(bf16), w_b_ref[...])
    mix = za_ref[...] + sgb_ref[...] * yb
    x1 = x_ref[...] + mod(2) * _dot(mix.astype(bf16), w_o_ref[...])
    h = (_rms(x1, g_ref[...]) * (1.0 + mod(4)) + mod(3)).astype(bf16)
    up = _dot(h, w_up_ref[...])
    prev0 = fst_ref[:, 0, :]
    prev1 = fst_ref[:, 1, :]
    conv = fcw_ref[0:1, :] * prev0 + fcw_ref[1:2, :] * prev1 + fcw_ref[2:3, :] * up + fcb_ref[...]
    fstn_ref[:, 0, :] = prev1
    fstn_ref[:, 1, :] = up
    act = (_silu(conv[:, 0:D_FF]) * conv[:, D_FF:2 * D_FF]).astype(bf16)
    x2 = x1 + mod(5) * _dot(act, w_down_ref[...])
    y_ref[...] = _rms(x2, gf_ref[...])


def _post_decode(x, ada, za, sgb, att, w_b, w_o, g_ffn, w_up, fcw, fcb, w_down, g_final, fstate):
    n = x.shape[0]
    return pl.pallas_call(
        _post_decode_kernel,
        out_shape=[jax.ShapeDtypeStruct((n, D_MODEL), f32),
                   jax.ShapeDtypeStruct((n, CONV_W - 1, 2 * D_FF), f32)],
        compiler_params=pltpu.CompilerParams(vmem_limit_bytes=VMEM_BYTES_V7X * 7 // 8),
        name="post_decode",
    )(x, ada, za, sgb, att, w_b, w_o, g_ffn, w_up, fcw, fcb, w_down, g_final, fstate)


PROMPT_MIXER_ROWS = 512
PROMPT_FFN_ROWS = 512
PROMPT_FFN_COLS = 256
PROMPT_FFN_DOWN_COLS = 256
DECODE_ATTN_BATCH = 8


def kernel(x_prompt, x_sample, c_prompt, c_sample, state_conv_a, cache_k_win, cache_v_win, state_ffn_conv, w_ada, b_ada, g_mix, w_in, conv_a_w, attn_sinks, w_a_out, w_b_out, w_o, g_ffn, w_up, ffn_conv_w, ffn_conv_b, w_down, g_final):
    depth = w_in.shape[0]
    n_p, seq, _ = x_prompt.shape
    n_s, t_s, _ = x_sample.shape
    assert n_p == 1 and t_s == 1, "one prompt sequence and single-token decode only"
    xp = x_prompt.reshape(seq, D_MODEL)
    xs = x_sample.reshape(n_s, D_MODEL)
    pad = (-(n_s + n_p)) % SUBLANES
    c_all = jnp.concatenate([c_sample, c_prompt, jnp.zeros((pad, D_MODEL), f32)], axis=0)
    gf = g_final.reshape(1, D_MODEL)
    outs = [[] for _ in range(8)]
    for l in range(depth):
        ada = _ada(c_all, w_ada[l], b_ada[l])
        ada_s = ada[:n_s]
        ada_p = jnp.pad(ada[n_s].reshape(N_MOD, D_MODEL), ((0, SUBLANES - N_MOD), (0, 0)))
        w_in_b, w_a_b, w_b_b, w_o_b = (w[l].astype(bf16) for w in (w_in, w_a_out, w_b_out, w_o))
        w_up_b, w_down_b = w_up[l].astype(bf16), w_down[l].astype(bf16)
        gm, gn = g_mix[l].reshape(1, D_MODEL), g_ffn[l].reshape(1, D_MODEL)
        fcb = ffn_conv_b[l].reshape(1, 2 * D_FF)

        x1, conv_p, k_p, v_p = _mixer_prompt(xp, ada_p, gm, w_in_b, conv_a_w[l], attn_sinks[l],
                                             w_a_b, w_b_b, w_o_b, tb=PROMPT_MIXER_ROWS)
        xp, ffn_p = _ffn_prompt(x1, ada_p, gn, w_up_b, ffn_conv_w[l], fcb, w_down_b, gf,
                                tb=PROMPT_FFN_ROWS, ch=PROMPT_FFN_COLS, nw=PROMPT_FFN_DOWN_COLS)

        q, k_n, v_n, za, sgb, conv_s = _mixer_decode_pre(
            xs, ada_s, gm, w_in_b, conv_a_w[l], state_conv_a[l], w_a_b)
        to_native = lambda c: c.transpose(0, 2, 3, 1)
        from_native = lambda c: c.transpose(0, 3, 1, 2)
        sink_b = jnp.broadcast_to(attn_sinks[l].reshape(N_KV, GROUP, 1), (N_KV, GROUP, LANES))
        att4, k_s, v_s = _attn_decode(
            q.reshape(n_s, N_KV, GROUP, HEAD_DIM),
            k_n.reshape(n_s, N_KV, 1, HEAD_DIM), v_n.reshape(n_s, N_KV, 1, HEAD_DIM),
            k_n.T.reshape(N_KV, HEAD_DIM, n_s), v_n.T.reshape(N_KV, HEAD_DIM, n_s),
            to_native(cache_k_win[l]), to_native(cache_v_win[l]), sink_b, bb=DECODE_ATTN_BATCH)
        k_s, v_s = from_native(k_s), from_native(v_s)
        xs, ffn_s = _post_decode(xs, ada_s, za, sgb, att4.reshape(n_s, N_HEADS * HEAD_DIM), w_b_b, w_o_b, gn,
                                 w_up_b, ffn_conv_w[l], fcb, w_down_b, gf, state_ffn_conv[l])

        for lst, val in zip(outs, (
                conv_p.reshape(n_p, CONV_W - 1, D_CONV), conv_s,
                k_p.reshape(n_p, WINDOW, N_KV, HEAD_DIM), k_s,
                v_p.reshape(n_p, WINDOW, N_KV, HEAD_DIM), v_s,
                ffn_p.reshape(n_p, CONV_W - 1, 2 * D_FF), ffn_s)):
            lst.append(val)
    assert depth == 1, "final RMSNorm is fused into the single layer's FFN kernels"
    return (xp.reshape(n_p, seq, D_MODEL), xs.reshape(n_s, t_s, D_MODEL)) + tuple(jnp.stack(o) for o in outs)
```

```python
import functools

import jax
import jax.numpy as jnp
from jax import lax
from jax.experimental import pallas as pl
from jax.experimental.pallas import tpu as pltpu

f32 = jnp.float32
bf16 = jnp.bfloat16

D_MODEL = 1024
D_CONV = D_MODEL
CONV_W = 3
N_HEADS = 16
N_KV = 4
GROUP = N_HEADS // N_KV
HEAD_DIM = 64
WINDOW = 128
D_FF = 2816
EPS = 1e-6
N_MOD = 6
ATTN_SCALE = HEAD_DIM ** -0.5
KV_COLS = N_KV * HEAD_DIM
C_XIN, C_B, C_C = 0, D_CONV, 2 * D_CONV
C_Q = 3 * D_CONV
C_K = C_Q + N_HEADS * HEAD_DIM
C_V = C_K + KV_COLS
C_GA = C_V + KV_COLS
C_GB = C_GA + D_MODEL
IN_COLS = C_GB + D_MODEL

LANES = 128
SUBLANES = 8
Q_SUB = 128
ATTN_LOOKAHEAD = 3
FFN_LOOKAHEAD = 2
MXU_COLS = 256
VMEM_BYTES_V7X = 64 * 1024 * 1024


def _rms(x, g):
    ms = jnp.mean(x * x, axis=-1, keepdims=True)
    return x * lax.rsqrt(ms + EPS) * g


def _silu(x):
    return x * jax.nn.sigmoid(x)


def _dot(a, b):
    return jnp.dot(a, b, preferred_element_type=f32)


def _const_spec(shape):
    nd = len(shape)
    return pl.BlockSpec(shape, lambda i: (0,) * nd, pipeline_mode=pl.Buffered(1))


def _ada_kernel(c_ref, w_ref, b_ref, o_ref):
    c = c_ref[...]
    o_ref[...] = _dot(_silu(c).astype(bf16), w_ref[...].astype(bf16)) + b_ref[...]


def _ada(c_all, w_ada, b_ada):
    rows = c_all.shape[0]
    return pl.pallas_call(
        _ada_kernel,
        grid=(N_MOD,),
        in_specs=[
            pl.BlockSpec((rows, D_MODEL), lambda j: (0, 0)),
            pl.BlockSpec((D_MODEL, D_MODEL), lambda j: (0, j)),
            pl.BlockSpec((1, D_MODEL), lambda j: (0, j)),
        ],
        out_specs=pl.BlockSpec((rows, D_MODEL), lambda j: (0, j)),
        out_shape=jax.ShapeDtypeStruct((rows, N_MOD * D_MODEL), f32),
        compiler_params=pltpu.CompilerParams(dimension_semantics=("arbitrary",)),
        name="ada",
    )(c_all, w_ada, b_ada.reshape(1, -1))


def _mixer_prompt_kernel(sinks_ref, x_ref, ada_ref, g_ref, w_in_ref, cw_ref, w_a_ref, w_b_ref, w_o_ref,
                         x1_ref, conv_ref, knew_ref, vnew_ref,
                         ubuf, klo, khi, vt, attbuf, *, tb):
    i = pl.program_id(0)

    @pl.when(i == 0)
    def _():
        ubuf[0:SUBLANES, :] = jnp.zeros((SUBLANES, D_CONV), f32)
        for r in (klo, khi):
            r[:, 0:WINDOW, :] = jnp.zeros((N_KV, WINDOW, LANES), bf16)
        vt[:, 0:WINDOW] = jnp.zeros((KV_COLS, WINDOW), bf16)

    x = x_ref[...]
    sh1, sc1, gt1 = ada_ref[0:1, :], ada_ref[1:2, :], ada_ref[2:3, :]
    h = (_rms(x, g_ref[...]) * (1.0 + sc1) + sh1).astype(bf16)

    def proj(c0, n):
        return _dot(h, w_in_ref[:, c0:c0 + n])

    u = proj(C_C, D_CONV) * proj(C_XIN, D_CONV)
    ubuf[SUBLANES:SUBLANES + tb, :] = u
    b_gate = proj(C_B, D_CONV)
    q = (proj(C_Q, N_HEADS * HEAD_DIM) * ATTN_SCALE).astype(bf16)
    k = proj(C_K, KV_COLS)
    v = proj(C_V, KV_COLS)
    conv = (cw_ref[0:1, :] * ubuf[SUBLANES - 2:SUBLANES - 2 + tb, :]
            + cw_ref[1:2, :] * ubuf[SUBLANES - 1:SUBLANES - 1 + tb, :]
            + cw_ref[2:3, :] * u)
    conv_in = (b_gate * conv).astype(bf16)
    knew_ref[...] = k[tb - WINDOW:, :]
    vnew_ref[...] = v[tb - WINDOW:, :]

    lo = lax.broadcasted_iota(jnp.int32, (tb, LANES), 1) < HEAD_DIM
    for j in range(N_KV // 2):
        pair = k[:, LANES * j:LANES * (j + 1)]
        rolled = pltpu.roll(pair, HEAD_DIM, axis=1)
        zero = jnp.zeros_like(pair)
        klo[2 * j, WINDOW:WINDOW + tb, :] = jnp.where(lo, pair, zero).astype(bf16)
        khi[2 * j, WINDOW:WINDOW + tb, :] = jnp.where(lo, zero, rolled).astype(bf16)
        klo[2 * j + 1, WINDOW:WINDOW + tb, :] = jnp.where(lo, rolled, zero).astype(bf16)
        khi[2 * j + 1, WINDOW:WINDOW + tb, :] = jnp.where(lo, zero, pair).astype(bf16)
    vt[:, WINDOW:WINDOW + tb] = v.T.astype(bf16)

    cc = lax.broadcasted_iota(jnp.int32, (Q_SUB + WINDOW, 2 * Q_SUB), 0)
    col = lax.broadcasted_iota(jnp.int32, (Q_SUB + WINDOW, 2 * Q_SUB), 1)
    rr = col & (Q_SUB - 1)
    band = (cc >= rr) & (cc <= rr + WINDOW)
    first_head = lax.broadcasted_iota(jnp.int32, (1, 2 * Q_SUB), 1) < Q_SUB
    mask0 = band & (cc >= jnp.where(i == 0, WINDOW, 0))

    def scores(j, g, t):
        r0 = Q_SUB * j
        qs = jnp.concatenate([q[r0:r0 + Q_SUB, 2 * LANES * g:2 * LANES * g + LANES],
                              q[r0:r0 + Q_SUB, 2 * LANES * g + LANES:2 * LANES * (g + 1)]], axis=0)
        kr = (klo, khi)[t]
        return lax.dot_general(kr[g, r0:r0 + Q_SUB + WINDOW, :], qs, (((1,), (1,)), ((), ())),
                               preferred_element_type=f32)

    def finish(j, g, t, st):
        r0 = Q_SUB * j
        h0, h1 = GROUP * g + t, GROUP * g + 2 + t
        sink = jnp.where(first_head, sinks_ref[h0], sinks_ref[h1])
        st = jnp.where(mask0 if j == 0 else band, st, -jnp.inf)
        m = jnp.maximum(jnp.max(st, axis=0, keepdims=True), sink)
        e = jnp.exp(st - m)
        linv = 1.0 / (jnp.sum(e, axis=0, keepdims=True) + jnp.exp(sink - m))
        vtg = vt[HEAD_DIM * g:HEAD_DIM * (g + 1), r0:r0 + Q_SUB + WINDOW]
        ot = _dot(vtg, e.astype(bf16)) * linv
        attbuf[HEAD_DIM * h0:HEAD_DIM * (h0 + 1), r0:r0 + Q_SUB] = ot[:, 0:Q_SUB]
        attbuf[HEAD_DIM * h1:HEAD_DIM * (h1 + 1), r0:r0 + Q_SUB] = ot[:, Q_SUB:2 * Q_SUB]

    chains = [(j, g, t) for j in range(tb // Q_SUB) for g in range(N_KV) for t in range(2)]
    fillers = ([(lambda c=c: _dot(conv_in, w_a_ref[:, c:c + MXU_COLS])) for c in range(0, D_MODEL, MXU_COLS)]
               + [functools.partial(proj, c, MXU_COLS) for c in range(C_GA, C_GB + D_MODEL, MXU_COLS)])
    fill_every = -(-len(chains) // len(fillers))
    filled = []
    pending = [scores(*c) for c in chains[:ATTN_LOOKAHEAD]]
    for n, chain in enumerate(chains):
        if n + ATTN_LOOKAHEAD < len(chains):
            pending.append(scores(*chains[n + ATTN_LOOKAHEAD]))
        if n % fill_every == 0:
            filled.extend(f() for f in fillers[len(filled):len(filled) + 1])
        finish(*chain, pending.pop(0))
    filled.extend(f() for f in fillers[len(filled):])
    ya, ga, gb = (jnp.concatenate(filled[j:j + D_MODEL // MXU_COLS], axis=1)
                  for j in range(0, len(filled), D_MODEL // MXU_COLS))

    yb = _dot(attbuf[...].T.astype(bf16), w_b_ref[...])
    mix = (jax.nn.sigmoid(ga) * ya + jax.nn.sigmoid(gb) * yb).astype(bf16)
    for r0 in range(0, tb, tb // 2):
        x1_ref[r0:r0 + tb // 2, :] = x[r0:r0 + tb // 2, :] + gt1 * _dot(mix[r0:r0 + tb // 2, :], w_o_ref[...])

    conv_ref[...] = ubuf[SUBLANES + tb - (CONV_W - 1):SUBLANES + tb, :]
    ubuf[0:SUBLANES, :] = ubuf[tb:tb + SUBLANES, :]
    for r in (klo, khi):
        r[:, 0:WINDOW, :] = r[:, tb:tb + WINDOW, :]
    vt[:, 0:WINDOW] = vt[:, tb:tb + WINDOW]


def _mixer_prompt(x, ada8, g_mix, w_in, conv_w, sinks, w_a, w_b, w_o, *, tb):
    s = x.shape[0]
    assert s % tb == 0 and tb % Q_SUB == 0 and tb >= WINDOW
    kv_scratch = pltpu.VMEM((N_KV, WINDOW + tb, LANES), bf16)
    grid_spec = pltpu.PrefetchScalarGridSpec(
        num_scalar_prefetch=1,
        grid=(s // tb,),
        in_specs=[
            pl.BlockSpec((tb, D_MODEL), lambda i, sk: (i, 0)),
            pl.BlockSpec((SUBLANES, D_MODEL), lambda i, sk: (0, 0)),
            pl.BlockSpec((1, D_MODEL), lambda i, sk: (0, 0)),
            pl.BlockSpec((D_MODEL, IN_COLS), lambda i, sk: (0, 0), pipeline_mode=pl.Buffered(1)),
            pl.BlockSpec((CONV_W, D_CONV), lambda i, sk: (0, 0)),
            pl.BlockSpec((D_CONV, D_MODEL), lambda i, sk: (0, 0), pipeline_mode=pl.Buffered(1)),
            pl.BlockSpec((N_HEADS * HEAD_DIM, D_MODEL), lambda i, sk: (0, 0), pipeline_mode=pl.Buffered(1)),
            pl.BlockSpec((D_MODEL, D_MODEL), lambda i, sk: (0, 0), pipeline_mode=pl.Buffered(1)),
        ],
        out_specs=[
            pl.BlockSpec((tb, D_MODEL), lambda i, sk: (i, 0)),
            pl.BlockSpec((CONV_W - 1, D_CONV), lambda i, sk: (0, 0)),
            pl.BlockSpec((WINDOW, KV_COLS), lambda i, sk: (0, 0)),
            pl.BlockSpec((WINDOW, KV_COLS), lambda i, sk: (0, 0)),
        ],
        scratch_shapes=[
            pltpu.VMEM((SUBLANES + tb, D_CONV), f32),
            kv_scratch, kv_scratch,
            pltpu.VMEM((KV_COLS, WINDOW + tb), bf16),
            pltpu.VMEM((N_HEADS * HEAD_DIM, tb), f32),
        ],
    )
    return pl.pallas_call(
        functools.partial(_mixer_prompt_kernel, tb=tb),
        grid_spec=grid_spec,
        out_shape=[
            jax.ShapeDtypeStruct((s, D_MODEL), f32),
            jax.ShapeDtypeStruct((CONV_W - 1, D_CONV), f32),
            jax.ShapeDtypeStruct((WINDOW, KV_COLS), f32),
            jax.ShapeDtypeStruct((WINDOW, KV_COLS), f32),
        ],
        compiler_params=pltpu.CompilerParams(
            dimension_semantics=("arbitrary",),
            vmem_limit_bytes=VMEM_BYTES_V7X * 7 // 8),
        name="mixer_prompt",
    )(sinks, x, ada8, g_mix, w_in, conv_w, w_a, w_b, w_o)


def _ffn_prompt_kernel(x_ref, ada_ref, g_ref, w_up_ref, fcw_ref, fcb_ref, w_down_ref, gf_ref,
                       y_ref, fst_ref, upbuf, actbuf, *, tb, ch, nw):
    i = pl.program_id(0)

    @pl.when(i == 0)
    def _():
        upbuf[:, 0:SUBLANES, :] = jnp.zeros((2 * D_FF // LANES, SUBLANES, LANES), f32)

    hb = tb // 2
    sh2, sc2, gt2 = ada_ref[3:4, :], ada_ref[4:5, :], ada_ref[5:6, :]
    h = (_rms(x_ref[...], g_ref[...]) * (1.0 + sc2) + sh2).astype(bf16)

    def up_cols(half, c0):
        up = _dot(h[half * hb:(half + 1) * hb, :], w_up_ref[:, c0:c0 + ch])
        for s in range(ch // LANES):
            upbuf[c0 // LANES + s, SUBLANES:SUBLANES + hb, :] = up[:, s * LANES:(s + 1) * LANES]
        return up

    def conv_cols(c0, up):
        pieces = []
        for s in range(ch // LANES):
            slab = c0 // LANES + s
            cols = slice(c0 + s * LANES, c0 + (s + 1) * LANES)
            pieces.append(fcw_ref[0:1, cols] * upbuf[slab, SUBLANES - 2:SUBLANES - 2 + hb, :]
                          + fcw_ref[1:2, cols] * upbuf[slab, SUBLANES - 1:SUBLANES - 1 + hb, :]
                          + fcw_ref[2:3, cols] * up[:, s * LANES:(s + 1) * LANES]
                          + fcb_ref[0:1, cols])
            upbuf[slab, 0:SUBLANES, :] = upbuf[slab, hb:hb + SUBLANES, :]
        return jnp.concatenate(pieces, axis=1)

    def down_cols(half, n0):
        return _dot(actbuf[half], w_down_ref[:, n0:n0 + nw])

    def finish(half, parts):
        rows = slice(half * hb, (half + 1) * hb)
        x2 = x_ref[rows, :] + gt2 * jnp.concatenate(parts, axis=1)
        y_ref[rows, :] = _rms(x2, gf_ref[...])

    chunks = list(range(0, D_FF, ch))
    down_starts = list(range(0, D_MODEL, nw))
    down_at = {len(chunks) * (k + 1) // (len(down_starts) + 1): n0 for k, n0 in enumerate(down_starts)}
    assert len(down_at) == len(down_starts)
    for half in range(2):
        parts = []
        pending = [(up_cols(half, c0), up_cols(half, D_FF + c0)) for c0 in chunks[:FFN_LOOKAHEAD]]
        for n, c0 in enumerate(chunks):
            if n + FFN_LOOKAHEAD < len(chunks):
                c1 = chunks[n + FFN_LOOKAHEAD]
                pending.append((up_cols(half, c1), up_cols(half, D_FF + c1)))
            if half == 1 and n in down_at:
                parts.append(down_cols(0, down_at[n]))
            up_g, up_v = pending.pop(0)
            actbuf[half, :, c0:c0 + ch] = (_silu(conv_cols(c0, up_g))
                                           * conv_cols(D_FF + c0, up_v)).astype(bf16)
        if half == 1:
            finish(0, parts)
    finish(1, [down_cols(1, n0) for n0 in down_starts])

    for slab in range(2 * D_FF // LANES):
        fst_ref[:, slab * LANES:(slab + 1) * LANES] = upbuf[slab, SUBLANES - (CONV_W - 1):SUBLANES, :]


def _ffn_prompt(x1, ada8, g_ffn, w_up, fcw, fcb, w_down, g_final, *, tb, ch, nw):
    s = x1.shape[0]
    assert s % tb == 0 and tb % (4 * SUBLANES) == 0
    assert D_FF % ch == 0 and ch % LANES == 0 and D_MODEL % nw == 0 and nw % LANES == 0
    return pl.pallas_call(
        functools.partial(_ffn_prompt_kernel, tb=tb, ch=ch, nw=nw),
        grid=(s // tb,),
        in_specs=[
            pl.BlockSpec((tb, D_MODEL), lambda i: (i, 0)),
            pl.BlockSpec((SUBLANES, D_MODEL), lambda i: (0, 0)),
            pl.BlockSpec((1, D_MODEL), lambda i: (0, 0)),
            _const_spec((D_MODEL, 2 * D_FF)),
            pl.BlockSpec((CONV_W, 2 * D_FF), lambda i: (0, 0)),
            pl.BlockSpec((1, 2 * D_FF), lambda i: (0, 0)),
            _const_spec((D_FF, D_MODEL)),
            pl.BlockSpec((1, D_MODEL), lambda i: (0, 0)),
        ],
        out_specs=[
            pl.BlockSpec((tb, D_MODEL), lambda i: (i, 0)),
            pl.BlockSpec((CONV_W - 1, 2 * D_FF), lambda i: (0, 0)),
        ],
        out_shape=[
            jax.ShapeDtypeStruct((s, D_MODEL), f32),
            jax.ShapeDtypeStruct((CONV_W - 1, 2 * D_FF), f32),
        ],
        scratch_shapes=[
            pltpu.VMEM((2 * D_FF // LANES, SUBLANES + tb // 2, LANES), f32),
            pltpu.VMEM((2, tb // 2, D_FF), bf16),
        ],
        compiler_params=pltpu.CompilerParams(
            dimension_semantics=("arbitrary",),
            vmem_limit_bytes=VMEM_BYTES_V7X * 7 // 8),
        name="ffn_prompt",
    )(x1, ada8, g_ffn, w_up, fcw, fcb, w_down, g_final)


def _mixer_decode_pre_kernel(x_ref, ada_ref, g_ref, w_in_ref, cw_ref, st_ref, w_a_ref,
                             q_ref, k_ref, v_ref, za_ref, sgb_ref, stn_ref):
    x = x_ref[...]
    sh1 = ada_ref[:, 0:D_MODEL]
    sc1 = ada_ref[:, D_MODEL:2 * D_MODEL]
    h = (_rms(x, g_ref[...]) * (1.0 + sc1) + sh1).astype(bf16)

    def proj(c0, n):
        return _dot(h, w_in_ref[:, c0:c0 + n])

    u = proj(C_C, D_CONV) * proj(C_XIN, D_CONV)
    prev0 = st_ref[:, 0, :]
    prev1 = st_ref[:, 1, :]
    conv = cw_ref[0:1, :] * prev0 + cw_ref[1:2, :] * prev1 + cw_ref[2:3, :] * u
    stn_ref[:, 0, :] = prev1
    stn_ref[:, 1, :] = u
    ya = _dot((proj(C_B, D_CONV) * conv).astype(bf16), w_a_ref[...])
    q_ref[...] = proj(C_Q, N_HEADS * HEAD_DIM) * ATTN_SCALE
    k_ref[...] = proj(C_K, KV_COLS)
    v_ref[...] = proj(C_V, KV_COLS)
    za_ref[...] = jax.nn.sigmoid(proj(C_GA, D_MODEL)) * ya
    sgb_ref[...] = jax.nn.sigmoid(proj(C_GB, D_MODEL))


def _mixer_decode_pre(x, ada, g_mix, w_in, conv_w, state, w_a):
    n = x.shape[0]
    shapes = [(n, N_HEADS * HEAD_DIM), (n, KV_COLS), (n, KV_COLS), (n, D_MODEL), (n, D_MODEL),
              (n, CONV_W - 1, D_CONV)]
    return pl.pallas_call(
        _mixer_decode_pre_kernel,
        out_shape=[jax.ShapeDtypeStruct(s, f32) for s in shapes],
        compiler_params=pltpu.CompilerParams(vmem_limit_bytes=VMEM_BYTES_V7X * 5 // 8),
        name="mixer_decode_pre",
    )(x, ada, g_mix, w_in, conv_w, state, w_a)


def _attn_decode_kernel(q_ref, kn_ref, vn_ref, knt_ref, vnt_ref, ck_ref, cv_ref, sink_ref,
                        att_ref, ok_ref, ov_ref, *, bb):
    step = pl.program_id(0)
    last = lax.broadcasted_iota(jnp.int32, (HEAD_DIM, WINDOW), 1) == WINDOW - 1
    to_front = (LANES - step * bb) % LANES
    kstep = [pltpu.roll(knt_ref[g], to_front, axis=1) for g in range(N_KV)]
    vstep = [pltpu.roll(vnt_ref[g], to_front, axis=1) for g in range(N_KV)]

    pairs = [(b, g) for b in range(bb) for g in range(N_KV)]
    sinks = [sink_ref[g][:, 0:1] for g in range(N_KV)]
    s_old = [_dot(q_ref[b, g].astype(bf16), ck_ref[b, g].astype(bf16)) for b, g in pairs]
    s_new = [jnp.sum(q_ref[b, g] * kn_ref[b, g], axis=-1, keepdims=True) for b, g in pairs]
    m = [jnp.maximum(jnp.maximum(jnp.max(so, axis=-1, keepdims=True), sn), sinks[g])
         for (b, g), so, sn in zip(pairs, s_old, s_new)]
    e_old = [jnp.exp(so - mm) for so, mm in zip(s_old, m)]
    e_new = [jnp.exp(sn - mm) for sn, mm in zip(s_new, m)]
    den = [jnp.sum(eo, axis=-1, keepdims=True) + en + jnp.exp(sinks[g] - mm)
           for (b, g), eo, en, mm in zip(pairs, e_old, e_new, m)]
    o_old = [lax.dot_general(eo.astype(bf16), cv_ref[b, g].astype(bf16), (((1,), (1,)), ((), ())),
                             preferred_element_type=f32) for (b, g), eo in zip(pairs, e_old)]
    for (b, g), oo, en, dd in zip(pairs, o_old, e_new, den):
        att_ref[b, g] = (oo + en * vn_ref[b, g]) / dd
    for b, g in pairs:
        ok_ref[b, g] = jnp.where(last, pltpu.roll(kstep[g], WINDOW - 1 - b, axis=1),
                                 pltpu.roll(ck_ref[b, g], WINDOW - 1, axis=1))
        ov_ref[b, g] = jnp.where(last, pltpu.roll(vstep[g], WINDOW - 1 - b, axis=1),
                                 pltpu.roll(cv_ref[b, g], WINDOW - 1, axis=1))


def _attn_decode(q4, kn4, vn4, knt, vnt, ck, cv, sink_b, *, bb):
    n = q4.shape[0]
    assert n % bb == 0 and n == LANES and WINDOW == LANES
    cache_spec = pl.BlockSpec((bb, N_KV, HEAD_DIM, WINDOW), lambda b: (b, 0, 0, 0))
    row_spec = pl.BlockSpec((bb, N_KV, 1, HEAD_DIM), lambda b: (b, 0, 0, 0))
    q_spec = pl.BlockSpec((bb, N_KV, GROUP, HEAD_DIM), lambda b: (b, 0, 0, 0))
    new_t_spec = pl.BlockSpec((N_KV, HEAD_DIM, n), lambda b: (0, 0, 0))
    return pl.pallas_call(
        functools.partial(_attn_decode_kernel, bb=bb),
        grid=(n // bb,),
        in_specs=[q_spec, row_spec, row_spec, new_t_spec, new_t_spec, cache_spec, cache_spec,
                  pl.BlockSpec((N_KV, GROUP, LANES), lambda b: (0, 0, 0))],
        out_specs=[q_spec, cache_spec, cache_spec],
        out_shape=[jax.ShapeDtypeStruct((n, N_KV, GROUP, HEAD_DIM), f32),
                   jax.ShapeDtypeStruct((n, N_KV, HEAD_DIM, WINDOW), f32),
                   jax.ShapeDtypeStruct((n, N_KV, HEAD_DIM, WINDOW), f32)],
        compiler_params=pltpu.CompilerParams(dimension_semantics=("arbitrary",)),
        name="attn_decode",
    )(q4, kn4, vn4, knt, vnt, ck, cv, sink_b)


def _post_decode_kernel(x_ref, ada_ref, za_ref, sgb_ref, att_ref, w_b_ref, w_o_ref, g_ref, w_up_ref,
                        fcw_ref, fcb_ref, w_down_ref, gf_ref, fst_ref, y_ref, fstn_ref):
    def mod(j):
        return ada_ref[:, j * D_MODEL:(j + 1) * D_MODEL]

    yb = _dot(att_ref[...].astype(bf16), w_b_ref[...])
    mix = za_ref[...] + sgb_ref[...] * yb
    x1 = x_ref[...] + mod(2) * _dot(mix.astype(bf16), w_o_ref[...])
    h = (_rms(x1, g_ref[...]) * (1.0 + mod(4)) + mod(3)).astype(bf16)
    up = _dot(h, w_up_ref[...])
    prev0 = fst_ref[:, 0, :]
    prev1 = fst_ref[:, 1, :]
    conv = fcw_ref[0:1, :] * prev0 + fcw_ref[1:2, :] * prev1 + fcw_ref[2:3, :] * up + fcb_ref[...]
    fstn_ref[:, 0, :] = prev1
    fstn_ref[:, 1, :] = up
    act = (_silu(conv[:, 0:D_FF]) * conv[:, D_FF:2 * D_FF]).astype(bf16)
    x2 = x1 + mod(5) * _dot(act, w_down_ref[...])
    y_ref[...] = _rms(x2, gf_ref[...])


def _post_decode(x, ada, za, sgb, att, w_b, w_o, g_ffn, w_up, fcw, fcb, w_down, g_final, fstate):
    n = x.shape[0]
    return pl.pallas_call(
        _post_decode_kernel,
        out_shape=[jax.ShapeDtypeStruct((n, D_MODEL), f32),
                   jax.ShapeDtypeStruct((n, CONV_W - 1, 2 * D_FF), f32)],
        compiler_params=pltpu.CompilerParams(vmem_limit_bytes=VMEM_BYTES_V7X * 7 // 8),
        name="post_decode",
    )(x, ada, za, sgb, att, w_b, w_o, g_ffn, w_up, fcw, fcb, w_down, g_final, fstate)


PROMPT_MIXER_ROWS = 512
PROMPT_FFN_ROWS = 512
PROMPT_FFN_COLS = 256
PROMPT_FFN_DOWN_COLS = 256
DECODE_ATTN_BATCH = 8


def kernel(x_prompt, x_sample, c_prompt, c_sample, state_conv_a, cache_k_win, cache_v_win, state_ffn_conv, w_ada, b_ada, g_mix, w_in, conv_a_w, attn_sinks, w_a_out, w_b_out, w_o, g_ffn, w_up, ffn_conv_w, ffn_conv_b, w_down, g_final):
    depth = w_in.shape[0]
    n_p, seq, _ = x_prompt.shape
    n_s, t_s, _ = x_sample.shape
    assert n_p == 1 and t_s == 1, "one prompt sequence and single-token decode only"
    xp = x_prompt.reshape(seq, D_MODEL)
    xs = x_sample.reshape(n_s, D_MODEL)
    pad = (-(n_s + n_p)) % SUBLANES
    c_all = jnp.concatenate([c_sample, c_prompt, jnp.zeros((pad, D_MODEL), f32)], axis=0)
    gf = g_final.reshape(1, D_MODEL)
    outs = [[] for _ in range(8)]
    for l in range(depth):
        ada = _ada(c_all, w_ada[l], b_ada[l])
        ada_s = ada[:n_s]
        ada_p = jnp.pad(ada[n_s].reshape(N_MOD, D_MODEL), ((0, SUBLANES - N_MOD), (0, 0)))
        w_in_b, w_a_b, w_b_b, w_o_b = (w[l].astype(bf16) for w in (w_in, w_a_out, w_b_out, w_o))
        w_up_b, w_down_b = w_up[l].astype(bf16), w_down[l].astype(bf16)
        gm, gn = g_mix[l].reshape(1, D_MODEL), g_ffn[l].reshape(1, D_MODEL)
        fcb = ffn_conv_b[l].reshape(1, 2 * D_FF)

        x1, conv_p, k_p, v_p = _mixer_prompt(xp, ada_p, gm, w_in_b, conv_a_w[l], attn_sinks[l],
                                             w_a_b, w_b_b, w_o_b, tb=PROMPT_MIXER_ROWS)
        xp, ffn_p = _ffn_prompt(x1, ada_p, gn, w_up_b, ffn_conv_w[l], fcb, w_down_b, gf,
                                tb=PROMPT_FFN_ROWS, ch=PROMPT_FFN_COLS, nw=PROMPT_FFN_DOWN_COLS)

        q, k_n, v_n, za, sgb, conv_s = _mixer_decode_pre(
            xs, ada_s, gm, w_in_b, conv_a_w[l], state_conv_a[l], w_a_b)
        to_native = lambda c: c.transpose(0, 2, 3, 1)
        from_native = lambda c: c.transpose(0, 3, 1, 2)
        sink_b = jnp.broadcast_to(attn_sinks[l].reshape(N_KV, GROUP, 1), (N_KV, GROUP, LANES))
        att4, k_s, v_s = _attn_decode(
            q.reshape(n_s, N_KV, GROUP, HEAD_DIM),
            k_n.reshape(n_s, N_KV, 1, HEAD_DIM), v_n.reshape(n_s, N_KV, 1, HEAD_DIM),
            k_n.T.reshape(N_KV, HEAD_DIM, n_s), v_n.T.reshape(N_KV, HEAD_DIM, n_s),
            to_native(cache_k_win[l]), to_native(cache_v_win[l]), sink_b, bb=DECODE_ATTN_BATCH)
        k_s, v_s = from_native(k_s), from_native(v_s)
        xs, ffn_s = _post_decode(xs, ada_s, za, sgb, att4.reshape(n_s, N_HEADS * HEAD_DIM), w_b_b, w_o_b, gn,
                                 w_up_b, ffn_conv_w[l], fcb, w_down_b, gf, state_ffn_conv[l])

        for lst, val in zip(outs, (
                conv_p.reshape(n_p, CONV_W - 1, D_CONV), conv_s,
                k_p.reshape(n_p, WINDOW, N_KV, HEAD_DIM), k_s,
                v_p.reshape(n_p, WINDOW, N_KV, HEAD_DIM), v_s,
                ffn_p.reshape(n_p, CONV_W - 1, 2 * D_FF), ffn_s)):
            lst.append(val)
    assert depth == 1, "final RMSNorm is fused into the single layer's FFN kernels"
    return (xp.reshape(n_p, seq, D_MODEL), xs.reshape(n_s, t_s, D_MODEL)) + tuple(jnp.stack(o) for o in outs)
```

```python
import functools

import jax
import jax.numpy as jnp
from jax import lax
from jax.experimental import pallas as pl
from jax.experimental.pallas import tpu as pltpu

f32 = jnp.float32
bf16 = jnp.bfloat16

D_MODEL = 1024
D_CONV = D_MODEL
CONV_W = 3
N_HEADS = 16
N_KV = 4
GROUP = N_HEADS // N_KV
HEAD_DIM = 64
WINDOW = 128
D_FF = 2816
EPS = 1e-6
N_MOD = 6
ATTN_SCALE = HEAD_DIM ** -0.5
KV_COLS = N_KV * HEAD_DIM
C_XIN, C_B, C_C = 0, D_CONV, 2 * D_CONV
C_Q = 3 * D_CONV
C_K = C_Q + N_HEADS * HEAD_DIM
C_V = C_K + KV_COLS
C_GA = C_V + KV_COLS
C_GB = C_GA + D_MODEL
IN_COLS = C_GB + D_MODEL

LANES = 128
SUBLANES = 8
Q_SUB = 128
ATTN_LOOKAHEAD = 3
FFN_LOOKAHEAD = 2
MXU_COLS = 256
VMEM_BYTES_V7X = 64 * 1024 * 1024


def _rms(x, g):
    ms = jnp.mean(x * x, axis=-1, keepdims=True)
    return x * lax.rsqrt(ms + EPS) * g


def _silu(x):
    return x * jax.nn.sigmoid(x)


def _dot(a, b):
    return jnp.dot(a, b, preferred_element_type=f32)


def _const_spec(shape):
    nd = len(shape)
    return pl.BlockSpec(shape, lambda i: (0,) * nd, pipeline_mode=pl.Buffered(1))


def _ada_kernel(c_ref, w_ref, b_ref, o_ref):
    c = c_ref[...]
    o_ref[...] = _dot(_silu(c).astype(bf16), w_ref[...].astype(bf16)) + b_ref[...]


def _ada(c_all, w_ada, b_ada):
    rows = c_all.shape[0]
    return pl.pallas_call(
        _ada_kernel,
        grid=(N_MOD,),
        in_specs=[
            pl.BlockSpec((rows, D_MODEL), lambda j: (0, 0)),
            pl.BlockSpec((D_MODEL, D_MODEL), lambda j: (0, j)),
            pl.BlockSpec((1, D_MODEL), lambda j: (0, j)),
        ],
        out_specs=pl.BlockSpec((rows, D_MODEL), lambda j: (0, j)),
        out_shape=jax.ShapeDtypeStruct((rows, N_MOD * D_MODEL), f32),
        compiler_params=pltpu.CompilerParams(dimension_semantics=("arbitrary",)),
        name="ada",
    )(c_all, w_ada, b_ada.reshape(1, -1))


def _mixer_prompt_kernel(sinks_ref, x_ref, ada_ref, g_ref, w_in_ref, cw_ref, w_a_ref, w_b_ref, w_o_ref,
                         x1_ref, conv_ref, knew_ref, vnew_ref,
                         ubuf, klo, khi, vt, attbuf, *, tb):
    i = pl.program_id(0)

    @pl.when(i == 0)
    def _():
        ubuf[0:SUBLANES, :] = jnp.zeros((SUBLANES, D_CONV), f32)
        for r in (klo, khi):
            r[:, 0:WINDOW, :] = jnp.zeros((N_KV, WINDOW, LANES), bf16)
        vt[:, 0:WINDOW] = jnp.zeros((KV_COLS, WINDOW), bf16)

    x = x_ref[...]
    sh1, sc1, gt1 = ada_ref[0:1, :], ada_ref[1:2, :], ada_ref[2:3, :]
    h = (_rms(x, g_ref[...]) * (1.0 + sc1) + sh1).astype(bf16)

    def proj(c0, n):
        return _dot(h, w_in_ref[:, c0:c0 + n])

    u = proj(C_C, D_CONV) * proj(C_XIN, D_CONV)
    ubuf[SUBLANES:SUBLANES + tb, :] = u
    b_gate = proj(C_B, D_CONV)
    q = (proj(C_Q, N_HEADS * HEAD_DIM) * ATTN_SCALE).astype(bf16)
    k = proj(C_K, KV_COLS)
    v = proj(C_V, KV_COLS)
    conv = (cw_ref[0:1, :] * ubuf[SUBLANES - 2:SUBLANES - 2 + tb, :]
            + cw_ref[1:2, :] * ubuf[SUBLANES - 1:SUBLANES - 1 + tb, :]
            + cw_ref[2:3, :] * u)
    conv_in = (b_gate * conv).astype(bf16)
    knew_ref[...] = k[tb - WINDOW:, :]
    vnew_ref[...] = v[tb - WINDOW:, :]

    lo = lax.broadcasted_iota(jnp.int32, (tb, LANES), 1) < HEAD_DIM
    for j in range(N_KV // 2):
        pair = k[:, LANES * j:LANES * (j + 1)]
        rolled = pltpu.roll(pair, HEAD_DIM, axis=1)
        zero = jnp.zeros_like(pair)
        klo[2 * j, WINDOW:WINDOW + tb, :] = jnp.where(lo, pair, zero).astype(bf16)
        khi[2 * j, WINDOW:WINDOW + tb, :] = jnp.where(lo, zero, rolled).astype(bf16)
        klo[2 * j + 1, WINDOW:WINDOW + tb, :] = jnp.where(lo, rolled, zero).astype(bf16)
        khi[2 * j + 1, WINDOW:WINDOW + tb, :] = jnp.where(lo, zero, pair).astype(bf16)
    vt[:, WINDOW:WINDOW + tb] = v.T.astype(bf16)

    cc = lax.broadcasted_iota(jnp.int32, (Q_SUB + WINDOW, 2 * Q_SUB), 0)
    col = lax.broadcasted_iota(jnp.int32, (Q_SUB + WINDOW, 2 * Q_SUB), 1)
    rr = col & (Q_SUB - 1)
    band = (cc >= rr) & (cc <= rr + WINDOW)
    first_head = lax.broadcasted_iota(jnp.int32, (1, 2 * Q_SUB), 1) < Q_SUB
    mask0 = band & (cc >= jnp.where(i == 0, WINDOW, 0))

    def scores(j, g, t):
        r0 = Q_SUB * j
        qs = jnp.concatenate([q[r0:r0 + Q_SUB, 2 * LANES * g:2 * LANES * g + LANES],
                              q[r0:r0 + Q_SUB, 2 * LANES * g + LANES:2 * LANES * (g + 1)]], axis=0)
        kr = (klo, khi)[t]
        return lax.dot_general(kr[g, r0:r0 + Q_SUB + WINDOW, :], qs, (((1,), (1,)), ((), ())),
                               preferred_element_type=f32)

    def finish(j, g, t, st):
        r0 = Q_SUB * j
        h0, h1 = GROUP * g + t, GROUP * g + 2 + t
        sink = jnp.where(first_head, sinks_ref[h0], sinks_ref[h1])
        st = jnp.where(mask0 if j == 0 else band, st, -jnp.inf)
        m = jnp.maximum(jnp.max(st, axis=0, keepdims=True), sink)
        e = jnp.exp(st - m)
        linv = 1.0 / (jnp.sum(e, axis=0, keepdims=True) + jnp.exp(sink - m))
        vtg = vt[HEAD_DIM * g:HEAD_DIM * (g + 1), r0:r0 + Q_SUB + WINDOW]
        ot = _dot(vtg, e.astype(bf16)) * linv
        attbuf[HEAD_DIM * h0:HEAD_DIM * (h0 + 1), r0:r0 + Q_SUB] = ot[:, 0:Q_SUB]
        attbuf[HEAD_DIM * h1:HEAD_DIM * (h1 + 1), r0:r0 + Q_SUB] = ot[:, Q_SUB:2 * Q_SUB]

    chains = [(j, g, t) for j in range(tb // Q_SUB) for g in range(N_KV) for t in range(2)]
    fillers = ([(lambda c=c: _dot(conv_in, w_a_ref[:, c:c + MXU_COLS])) for c in range(0, D_MODEL, MXU_COLS)]
               + [functools.partial(proj, c, MXU_COLS) for c in range(C_GA, C_GB + D_MODEL, MXU_COLS)])
    fill_every = -(-len(chains) // len(fillers))
    filled = []
    pending = [scores(*c) for c in chains[:ATTN_LOOKAHEAD]]
    for n, chain in enumerate(chains):
        if n + ATTN_LOOKAHEAD < len(chains):
            pending.append(scores(*chains[n + ATTN_LOOKAHEAD]))
        if n % fill_every == 0:
            filled.extend(f() for f in fillers[len(filled):len(filled) + 1])
        finish(*chain, pending.pop(0))
    filled.extend(f() for f in fillers[len(filled):])
    ya, ga, gb = (jnp.concatenate(filled[j:j + D_MODEL // MXU_COLS], axis=1)
                  for j in range(0, len(filled), D_MODEL // MXU_COLS))

    yb = _dot(attbuf[...].T.astype(bf16), w_b_ref[...])
    mix = (jax.nn.sigmoid(ga) * ya + jax.nn.sigmoid(gb) * yb).astype(bf16)
    for r0 in range(0, tb, tb // 2):
        x1_ref[r0:r0 + tb // 2, :] = x[r0:r0 + tb // 2, :] + gt1 * _dot(mix[r0:r0 + tb // 2, :], w_o_ref[...])

    conv_ref[...] = ubuf[SUBLANES + tb - (CONV_W - 1):SUBLANES + tb, :]
    ubuf[0:SUBLANES, :] = ubuf[tb:tb + SUBLANES, :]
    for r in (klo, khi):
        r[:, 0:WINDOW, :] = r[:, tb:tb + WINDOW, :]
    vt[:, 0:WINDOW] = vt[:, tb:tb + WINDOW]


def _mixer_prompt(x, ada8, g_mix, w_in, conv_w, sinks, w_a, w_b, w_o, *, tb):
    s = x.shape[0]
    assert s % tb == 0 and tb % Q_SUB == 0 and tb >= WINDOW
    kv_scratch = pltpu.VMEM((N_KV, WINDOW + tb, LANES), bf16)
    grid_spec = pltpu.PrefetchScalarGridSpec(
        num_scalar_prefetch=1,
        grid=(s // tb,),
        in_specs=[
            pl.BlockSpec((tb, D_MODEL), lambda i, sk: (i, 0)),
            pl.BlockSpec((SUBLANES, D_MODEL), lambda i, sk: (0, 0)),
            pl.BlockSpec((1, D_MODEL), lambda i, sk: (0, 0)),
            pl.BlockSpec((D_MODEL, IN_COLS), lambda i, sk: (0, 0), pipeline_mode=pl.Buffered(1)),
            pl.BlockSpec((CONV_W, D_CONV), lambda i, sk: (0, 0)),
            pl.BlockSpec((D_CONV, D_MODEL), lambda i, sk: (0, 0), pipeline_mode=pl.Buffered(1)),
            pl.BlockSpec((N_HEADS * HEAD_DIM, D_MODEL), lambda i, sk: (0, 0), pipeline_mode=pl.Buffered(1)),
            pl.BlockSpec((D_MODEL, D_MODEL), lambda i, sk: (0, 0), pipeline_mode=pl.Buffered(1)),
        ],
        out_specs=[
            pl.BlockSpec((tb, D_MODEL), lambda i, sk: (i, 0)),
            pl.BlockSpec((CONV_W - 1, D_CONV), lambda i, sk: (0, 0)),
            pl.BlockSpec((WINDOW, KV_COLS), lambda i, sk: (0, 0)),
            pl.BlockSpec((WINDOW, KV_COLS), lambda i, sk: (0, 0)),
        ],
        scratch_shapes=[
            pltpu.VMEM((SUBLANES + tb, D_CONV), f32),
            kv_scratch, kv_scratch,
            pltpu.VMEM((KV_COLS, WINDOW + tb), bf16),
            pltpu.VMEM((N_HEADS * HEAD_DIM, tb), f32),
        ],
    )
    return pl.pallas_call(
        functools.partial(_mixer_prompt_kernel, tb=tb),
        grid_spec=grid_spec,
        out_shape=[
            jax.ShapeDtypeStruct((s, D_MODEL), f32),
            jax.ShapeDtypeStruct((CONV_W - 1, D_CONV), f32),
            jax.ShapeDtypeStruct((WINDOW, KV_COLS), f32),
            jax.ShapeDtypeStruct((WINDOW, KV_COLS), f32),
        ],
        compiler_params=pltpu.CompilerParams(
            dimension_semantics=("arbitrary",),
            vmem_limit_bytes=VMEM_BYTES_V7X * 7 // 8),
        name="mixer_prompt",
    )(sinks, x, ada8, g_mix, w_in, conv_w, w_a, w_b, w_o)


def _ffn_prompt_kernel(x_ref, ada_ref, g_ref, w_up_ref, fcw_ref, fcb_ref, w_down_ref, gf_ref,
                       y_ref, fst_ref, upbuf, actbuf, *, tb, ch, nw):
    i = pl.program_id(0)

    @pl.when(i == 0)
    def _():
        upbuf[:, 0:SUBLANES, :] = jnp.zeros((2 * D_FF // LANES, SUBLANES, LANES), f32)

    hb = tb // 2
    sh2, sc2, gt2 = ada_ref[3:4, :], ada_ref[4:5, :], ada_ref[5:6, :]
    h = (_rms(x_ref[...], g_ref[...]) * (1.0 + sc2) + sh2).astype(bf16)

    def up_cols(half, c0):
        up = _dot(h[half * hb:(half + 1) * hb, :], w_up_ref[:, c0:c0 + ch])
        for s in range(ch // LANES):
            upbuf[c0 // LANES + s, SUBLANES:SUBLANES + hb, :] = up[:, s * LANES:(s + 1) * LANES]
        return up

    def conv_cols(c0, up):
        pieces = []
        for s in range(ch // LANES):
            slab = c0 // LANES + s
            cols = slice(c0 + s * LANES, c0 + (s + 1) * LANES)
            pieces.append(fcw_ref[0:1, cols] * upbuf[slab, SUBLANES - 2:SUBLANES - 2 + hb, :]
                          + fcw_ref[1:2, cols] * upbuf[slab, SUBLANES - 1:SUBLANES - 1 + hb, :]
                          + fcw_ref[2:3, cols] * up[:, s * LANES:(s + 1) * LANES]
                          + fcb_ref[0:1, cols])
            upbuf[slab, 0:SUBLANES, :] = upbuf[slab, hb:hb + SUBLANES, :]
        return jnp.concatenate(pieces, axis=1)

    def down_cols(half, n0):
        return _dot(actbuf[half], w_down_ref[:, n0:n0 + nw])

    def finish(half, parts):
        rows = slice(half * hb, (half + 1) * hb)
        x2 = x_ref[rows, :] + gt2 * jnp.concatenate(parts, axis=1)
        y_ref[rows, :] = _rms(x2, gf_ref[...])

    chunks = list(range(0, D_FF, ch))
    down_starts = list(range(0, D_MODEL, nw))
    down_at = {len(chunks) * (k + 1) // (len(down_starts) + 1): n0 for k, n0 in enumerate(down_starts)}
    assert len(down_at) == len(down_starts)
    for half in range(2):
        parts = []
        pending = [(up_cols(half, c0), up_cols(half, D_FF + c0)) for c0 in chunks[:FFN_LOOKAHEAD]]
        for n, c0 in enumerate(chunks):
            if n + FFN_LOOKAHEAD < len(chunks):
                c1 = chunks[n + FFN_LOOKAHEAD]
                pending.append((up_cols(half, c1), up_cols(half, D_FF + c1)))
            if half == 1 and n in down_at:
                parts.append(down_cols(0, down_at[n]))
            up_g, up_v = pending.pop(0)
            actbuf[half, :, c0:c0 + ch] = (_silu(conv_cols(c0, up_g))
                                           * conv_cols(D_FF + c0, up_v)).astype(bf16)
        if half == 1:
            finish(0, parts)
    finish(1, [down_cols(1, n0) for n0 in down_starts])

    for slab in range(2 * D_FF // LANES):
        fst_ref[:, slab * LANES:(slab + 1) * LANES] = upbuf[slab, SUBLANES - (CONV_W - 1):SUBLANES, :]


def _ffn_prompt(x1, ada8, g_ffn, w_up, fcw, fcb, w_down, g_final, *, tb, ch, nw):
    s = x1.shape[0]
    assert s % tb == 0 and tb % (4 * SUBLANES) == 0
    assert D_FF % ch == 0 and ch % LANES == 0 and D_MODEL % nw == 0 and nw % LANES == 0
    return pl.pallas_call(
        functools.partial(_ffn_prompt_kernel, tb=tb, ch=ch, nw=nw),
        grid=(s // tb,),
        in_specs=[
            pl.BlockSpec((tb, D_MODEL), lambda i: (i, 0)),
            pl.BlockSpec((SUBLANES, D_MODEL), lambda i: (0, 0)),
            pl.BlockSpec((1, D_MODEL), lambda i: (0, 0)),
            _const_spec((D_MODEL, 2 * D_FF)),
            pl.BlockSpec((CONV_W, 2 * D_FF), lambda i: (0, 0)),
            pl.BlockSpec((1, 2 * D_FF), lambda i: (0, 0)),
            _const_spec((D_FF, D_MODEL)),
            pl.BlockSpec((1, D_MODEL), lambda i: (0, 0)),
        ],
        out_specs=[
            pl.BlockSpec((tb, D_MODEL), lambda i: (i, 0)),
            pl.BlockSpec((CONV_W - 1, 2 * D_FF), lambda i: (0, 0)),
        ],
        out_shape=[
            jax.ShapeDtypeStruct((s, D_MODEL), f32),
            jax.ShapeDtypeStruct((CONV_W - 1, 2 * D_FF), f32),
        ],
        scratch_shapes=[
            pltpu.VMEM((2 * D_FF // LANES, SUBLANES + tb // 2, LANES), f32),
            pltpu.VMEM((2, tb // 2, D_FF), bf16),
        ],
        compiler_params=pltpu.CompilerParams(
            dimension_semantics=("arbitrary",),
            vmem_limit_bytes=VMEM_BYTES_V7X * 7 // 8),
        name="ffn_prompt",
    )(x1, ada8, g_ffn, w_up, fcw, fcb, w_down, g_final)


def _mixer_decode_pre_kernel(x_ref, ada_ref, g_ref, w_in_ref, cw_ref, st_ref, w_a_ref,
                             w_in_bf_ref, w_a_bf_ref, q_ref, k_ref, v_ref, za_ref, sgb_ref, stn_ref,
                             h_scr, proj_scr, *, chunk):
    j = pl.program_id(0)
    n_chunks = IN_COLS // chunk

    @pl.when(j == 0)
    def _():
        sh1 = ada_ref[:, 0:D_MODEL]
        sc1 = ada_ref[:, D_MODEL:2 * D_MODEL]
        h_scr[...] = (_rms(x_ref[...], g_ref[...]) * (1.0 + sc1) + sh1).astype(bf16)

    w_chunk = w_in_ref[...].astype(bf16)
    w_in_bf_ref[...] = w_chunk
    part = _dot(h_scr[...], w_chunk)
    for c in range(n_chunks):
        @pl.when(j == c)
        def _(c=c):
            proj_scr[:, c * chunk:(c + 1) * chunk] = part

    @pl.when(j == n_chunks - 1)
    def _():
        def proj(c0, n):
            return proj_scr[:, c0:c0 + n]

        w_a = w_a_ref[...].astype(bf16)
        w_a_bf_ref[...] = w_a
        u = proj(C_C, D_CONV) * proj(C_XIN, D_CONV)
        prev0 = st_ref[:, 0, :]
        prev1 = st_ref[:, 1, :]
        conv = cw_ref[0:1, :] * prev0 + cw_ref[1:2, :] * prev1 + cw_ref[2:3, :] * u
        stn_ref[:, 0, :] = prev1
        stn_ref[:, 1, :] = u
        ya = _dot((proj(C_B, D_CONV) * conv).astype(bf16), w_a)
        q_ref[...] = proj(C_Q, N_HEADS * HEAD_DIM) * ATTN_SCALE
        k_ref[...] = proj(C_K, KV_COLS)
        v_ref[...] = proj(C_V, KV_COLS)
        za_ref[...] = jax.nn.sigmoid(proj(C_GA, D_MODEL)) * ya
        sgb_ref[...] = jax.nn.sigmoid(proj(C_GB, D_MODEL))


def _mixer_decode_pre(x, ada, g_mix, w_in, conv_w, state, w_a, *, chunk):
    n = x.shape[0]
    assert IN_COLS % chunk == 0 and chunk % LANES == 0
    const2 = lambda j: (0, 0)
    const3 = lambda j: (0, 0, 0)
    row_block = lambda cols: pl.BlockSpec((n, cols), const2)
    state_block = pl.BlockSpec((n, CONV_W - 1, D_CONV), const3)
    shapes = [((D_MODEL, IN_COLS), bf16), ((D_CONV, D_MODEL), bf16),
              ((n, N_HEADS * HEAD_DIM), f32), ((n, KV_COLS), f32), ((n, KV_COLS), f32),
              ((n, D_MODEL), f32), ((n, D_MODEL), f32), ((n, CONV_W - 1, D_CONV), f32)]
    return pl.pallas_call(
        functools.partial(_mixer_decode_pre_kernel, chunk=chunk),
        grid=(IN_COLS // chunk,),
        in_specs=[row_block(D_MODEL), row_block(2 * D_MODEL), pl.BlockSpec((1, D_MODEL), const2),
                  pl.BlockSpec((D_MODEL, chunk), lambda j: (0, j)),
                  pl.BlockSpec((CONV_W, D_CONV), const2), state_block,
                  pl.BlockSpec((D_CONV, D_MODEL), const2)],
        out_specs=[pl.BlockSpec((D_MODEL, chunk), lambda j: (0, j)), pl.BlockSpec((D_CONV, D_MODEL), const2),
                   row_block(N_HEADS * HEAD_DIM), row_block(KV_COLS), row_block(KV_COLS),
                   row_block(D_MODEL), row_block(D_MODEL), state_block],
        out_shape=[jax.ShapeDtypeStruct(shp, dt) for shp, dt in shapes],
        scratch_shapes=[pltpu.VMEM((n, D_MODEL), bf16), pltpu.VMEM((n, IN_COLS), f32)],
        compiler_params=pltpu.CompilerParams(
            dimension_semantics=("arbitrary",),
            vmem_limit_bytes=VMEM_BYTES_V7X * 5 // 8),
        name="mixer_decode_pre",
    )(x, ada, g_mix, w_in, conv_w, state, w_a)


def _attn_decode_kernel(q_ref, kn_ref, vn_ref, knt_ref, vnt_ref, ck_ref, cv_ref, sink_ref,
                        att_ref, ok_ref, ov_ref, *, bb):
    step = pl.program_id(0)
    last = lax.broadcasted_iota(jnp.int32, (HEAD_DIM, WINDOW), 1) == WINDOW - 1
    to_front = (LANES - step * bb) % LANES
    kstep = [pltpu.roll(knt_ref[g], to_front, axis=1) for g in range(N_KV)]
    vstep = [pltpu.roll(vnt_ref[g], to_front, axis=1) for g in range(N_KV)]

    pairs = [(b, g) for b in range(bb) for g in range(N_KV)]
    sinks = [sink_ref[g][:, 0:1] for g in range(N_KV)]
    s_old = [_dot(q_ref[b, g].astype(bf16), ck_ref[b, g].astype(bf16)) for b, g in pairs]
    s_new = [jnp.sum(q_ref[b, g] * kn_ref[b, g], axis=-1, keepdims=True) for b, g in pairs]
    m = [jnp.maximum(jnp.maximum(jnp.max(so, axis=-1, keepdims=True), sn), sinks[g])
         for (b, g), so, sn in zip(pairs, s_old, s_new)]
    e_old = [jnp.exp(so - mm) for so, mm in zip(s_old, m)]
    e_new = [jnp.exp(sn - mm) for sn, mm in zip(s_new, m)]
    den = [jnp.sum(eo, axis=-1, keepdims=True) + en + jnp.exp(sinks[g] - mm)
           for (b, g), eo, en, mm in zip(pairs, e_old, e_new, m)]
    o_old = [lax.dot_general(eo.astype(bf16), cv_ref[b, g].astype(bf16), (((1,), (1,)), ((), ())),
                             preferred_element_type=f32) for (b, g), eo in zip(pairs, e_old)]
    for (b, g), oo, en, dd in zip(pairs, o_old, e_new, den):
        att_ref[b, g] = (oo + en * vn_ref[b, g]) / dd
    for b, g in pairs:
        ok_ref[b, g] = jnp.where(last, pltpu.roll(kstep[g], WINDOW - 1 - b, axis=1),
                                 pltpu.roll(ck_ref[b, g], WINDOW - 1, axis=1))
        ov_ref[b, g] = jnp.where(last, pltpu.roll(vstep[g], WINDOW - 1 - b, axis=1),
                                 pltpu.roll(cv_ref[b, g], WINDOW - 1, axis=1))


def _attn_decode(q4, kn4, vn4, knt, vnt, ck, cv, sink_b, *, bb):
    n = q4.shape[0]
    assert n % bb == 0 and n == LANES and WINDOW == LANES
    cache_spec = pl.BlockSpec((bb, N_KV, HEAD_DIM, WINDOW), lambda b: (b, 0, 0, 0))
    row_spec = pl.BlockSpec((bb, N_KV, 1, HEAD_DIM), lambda b: (b, 0, 0, 0))
    q_spec = pl.BlockSpec((bb, N_KV, GROUP, HEAD_DIM), lambda b: (b, 0, 0, 0))
    new_t_spec = pl.BlockSpec((N_KV, HEAD_DIM, n), lambda b: (0, 0, 0))
    return pl.pallas_call(
        functools.partial(_attn_decode_kernel, bb=bb),
        grid=(n // bb,),
        in_specs=[q_spec, row_spec, row_spec, new_t_spec, new_t_spec, cache_spec, cache_spec,
                  pl.BlockSpec((N_KV, GROUP, LANES), lambda b: (0, 0, 0))],
        out_specs=[q_spec, cache_spec, cache_spec],
        out_shape=[jax.ShapeDtypeStruct((n, N_KV, GROUP, HEAD_DIM), f32),
                   jax.ShapeDtypeStruct((n, N_KV, HEAD_DIM, WINDOW), f32),
                   jax.ShapeDtypeStruct((n, N_KV, HEAD_DIM, WINDOW), f32)],
        compiler_params=pltpu.CompilerParams(dimension_semantics=("arbitrary",)),
        name="attn_decode",
    )(q4, kn4, vn4, knt, vnt, ck, cv, sink_b)


def _post_decode_kernel(x_ref, ada_ref, za_ref, sgb_ref, att_ref, w_b_ref, w_o_ref, g_ref, w_up_ref,
                        fcw_ref, fcb_ref, w_down_ref, gf_ref, fst_ref,
                        w_b_bf_ref, w_o_bf_ref, w_up_bf_ref, w_down_bf_ref, y_ref, fstn_ref,
                        yb_scr, x1_scr, h_scr, up_scr, act_scr, acc_scr, *, phases):
    j = pl.program_id(0)
    (b0, nb, cb), (o0, no, co), (u0, nu, cu), (d0, nd, cd) = phases

    def mod(k):
        return ada_ref[:, k * D_MODEL:(k + 1) * D_MODEL]

    for c in range(nb):
        @pl.when(j == b0 + c)
        def _(c=c):
            w = w_b_ref[...].astype(bf16)
            w_b_bf_ref[...] = w
            yb_scr[:, c * cb:(c + 1) * cb] = _dot(att_ref[...].astype(bf16), w)

    for c in range(no):
        @pl.when(j == o0 + c)
        def _(c=c):
            w = w_o_ref[...].astype(bf16)
            w_o_bf_ref[...] = w
            mix = (za_ref[...] + sgb_ref[...] * yb_scr[...]).astype(bf16)
            cols = slice(c * co, (c + 1) * co)
            x1_scr[:, cols] = x_ref[:, cols] + mod(2)[:, cols] * _dot(mix, w)

    for c in range(nu):
        @pl.when(j == u0 + c)
        def _(c=c):
            if c == 0:
                h_scr[...] = (_rms(x1_scr[...], g_ref[...]) * (1.0 + mod(4)) + mod(3)).astype(bf16)
            w = w_up_ref[...].astype(bf16)
            w_up_bf_ref[...] = w
            up_scr[:, c * cu:(c + 1) * cu] = _dot(h_scr[...], w)

    for c in range(nd):
        @pl.when(j == d0 + c)
        def _(c=c):
            if c == 0:
                up = up_scr[...]
                prev0 = fst_ref[:, 0, :]
                prev1 = fst_ref[:, 1, :]
                conv = fcw_ref[0:1, :] * prev0 + fcw_ref[1:2, :] * prev1 + fcw_ref[2:3, :] * up + fcb_ref[...]
                fstn_ref[:, 0, :] = prev1
                fstn_ref[:, 1, :] = up
                act_scr[...] = (_silu(conv[:, 0:D_FF]) * conv[:, D_FF:2 * D_FF]).astype(bf16)
            w = w_down_ref[...].astype(bf16)
            w_down_bf_ref[...] = w
            part = _dot(act_scr[:, c * cd:(c + 1) * cd], w)
            acc_scr[...] = part if c == 0 else acc_scr[...] + part
            if c == nd - 1:
                y_ref[...] = _rms(x1_scr[...] + mod(5) * acc_scr[...], gf_ref[...])


def _post_decode(x, ada, za, sgb, att, w_b, w_o, g_ffn, w_up, fcw, fcb, w_down, g_final, fstate,
                 *, proj_chunk, up_chunk, down_chunk):
    n = x.shape[0]
    assert D_MODEL % proj_chunk == 0 and (2 * D_FF) % up_chunk == 0 and D_FF % down_chunk == 0
    assert proj_chunk % LANES == 0 and up_chunk % LANES == 0 and down_chunk % LANES == 0
    nb = no = D_MODEL // proj_chunk
    nu, nd = 2 * D_FF // up_chunk, D_FF // down_chunk
    b0, o0, u0, d0 = 0, nb, nb + no, nb + no + nu
    phases = ((b0, nb, proj_chunk), (o0, no, proj_chunk), (u0, nu, up_chunk), (d0, nd, down_chunk))

    def chunk_index(start, count):
        return lambda j: jnp.clip(j - start, 0, count - 1)

    ib, io, iu, idn = (chunk_index(s0, cnt) for s0, cnt, _ in phases)
    const2 = lambda j: (0, 0)
    const3 = lambda j: (0, 0, 0)
    rows = lambda cols: pl.BlockSpec((n, cols), const2, pipeline_mode=pl.Buffered(1))
    state_block = pl.BlockSpec((n, CONV_W - 1, 2 * D_FF), const3, pipeline_mode=pl.Buffered(1))
    col_chunk = lambda k, width, idx: pl.BlockSpec((k, width), lambda j: (0, idx(j)))
    w_down_block = pl.BlockSpec((down_chunk, D_MODEL), lambda j: (idn(j), 0))
    return pl.pallas_call(
        functools.partial(_post_decode_kernel, phases=phases),
        grid=(d0 + nd,),
        in_specs=[rows(D_MODEL), rows(N_MOD * D_MODEL), rows(D_MODEL), rows(D_MODEL), rows(N_HEADS * HEAD_DIM),
                  col_chunk(N_HEADS * HEAD_DIM, proj_chunk, ib), col_chunk(D_MODEL, proj_chunk, io),
                  pl.BlockSpec((1, D_MODEL), const2), col_chunk(D_MODEL, up_chunk, iu),
                  pl.BlockSpec((CONV_W, 2 * D_FF), const2), pl.BlockSpec((1, 2 * D_FF), const2),
                  w_down_block, pl.BlockSpec((1, D_MODEL), const2), state_block],
        out_specs=[col_chunk(N_HEADS * HEAD_DIM, proj_chunk, ib), col_chunk(D_MODEL, proj_chunk, io),
                   col_chunk(D_MODEL, up_chunk, iu), w_down_block, rows(D_MODEL), state_block],
        out_shape=[jax.ShapeDtypeStruct((N_HEADS * HEAD_DIM, D_MODEL), bf16),
                   jax.ShapeDtypeStruct((D_MODEL, D_MODEL), bf16),
                   jax.ShapeDtypeStruct((D_MODEL, 2 * D_FF), bf16),
                   jax.ShapeDtypeStruct((D_FF, D_MODEL), bf16),
                   jax.ShapeDtypeStruct((n, D_MODEL), f32),
                   jax.ShapeDtypeStruct((n, CONV_W - 1, 2 * D_FF), f32)],
        scratch_shapes=[pltpu.VMEM((n, D_MODEL), f32), pltpu.VMEM((n, D_MODEL), f32), pltpu.VMEM((n, D_MODEL), bf16),
                        pltpu.VMEM((n, 2 * D_FF), f32), pltpu.VMEM((n, D_FF), bf16), pltpu.VMEM((n, D_MODEL), f32)],
        compiler_params=pltpu.CompilerParams(
            dimension_semantics=("arbitrary",),
            vmem_limit_bytes=VMEM_BYTES_V7X * 7 // 8),
        name="post_decode",
    )(x, ada, za, sgb, att, w_b, w_o, g_ffn, w_up, fcw, fcb, w_down, g_final, fstate)


PROMPT_MIXER_ROWS = 512
PROMPT_FFN_ROWS = 512
PROMPT_FFN_COLS = 256
PROMPT_FFN_DOWN_COLS = 256
DECODE_ATTN_BATCH = 8
DECODE_IN_CHUNK = 1664
DECODE_PROJ_CHUNK = 512
DECODE_UP_CHUNK = 512
DECODE_DOWN_CHUNK = 256


def kernel(x_prompt, x_sample, c_prompt, c_sample, state_conv_a, cache_k_win, cache_v_win, state_ffn_conv, w_ada, b_ada, g_mix, w_in, conv_a_w, attn_sinks, w_a_out, w_b_out, w_o, g_ffn, w_up, ffn_conv_w, ffn_conv_b, w_down, g_final):
    depth = w_in.shape[0]
    n_p, seq, _ = x_prompt.shape
    n_s, t_s, _ = x_sample.shape
    assert n_p == 1 and t_s == 1, "one prompt sequence and single-token decode only"
    xp = x_prompt.reshape(seq, D_MODEL)
    xs = x_sample.reshape(n_s, D_MODEL)
    pad = (-(n_s + n_p)) % SUBLANES
    c_all = jnp.concatenate([c_sample, c_prompt, jnp.zeros((pad, D_MODEL), f32)], axis=0)
    gf = g_final.reshape(1, D_MODEL)
    outs = [[] for _ in range(8)]
    for l in range(depth):
        ada = _ada(c_all, w_ada[l], b_ada[l])
        ada_s = ada[:n_s]
        ada_p = jnp.pad(ada[n_s].reshape(N_MOD, D_MODEL), ((0, SUBLANES - N_MOD), (0, 0)))
        gm, gn = g_mix[l].reshape(1, D_MODEL), g_ffn[l].reshape(1, D_MODEL)
        fcb = ffn_conv_b[l].reshape(1, 2 * D_FF)

        w_in_b, w_a_b, q, k_n, v_n, za, sgb, conv_s = _mixer_decode_pre(
            xs, ada_s, gm, w_in[l], conv_a_w[l], state_conv_a[l], w_a_out[l], chunk=DECODE_IN_CHUNK)
        to_native = lambda c: c.transpose(0, 2, 3, 1)
        from_native = lambda c: c.transpose(0, 3, 1, 2)
        sink_b = jnp.broadcast_to(attn_sinks[l].reshape(N_KV, GROUP, 1), (N_KV, GROUP, LANES))
        att4, k_s, v_s = _attn_decode(
            q.reshape(n_s, N_KV, GROUP, HEAD_DIM),
            k_n.reshape(n_s, N_KV, 1, HEAD_DIM), v_n.reshape(n_s, N_KV, 1, HEAD_DIM),
            k_n.T.reshape(N_KV, HEAD_DIM, n_s), v_n.T.reshape(N_KV, HEAD_DIM, n_s),
            to_native(cache_k_win[l]), to_native(cache_v_win[l]), sink_b, bb=DECODE_ATTN_BATCH)
        k_s, v_s = from_native(k_s), from_native(v_s)
        w_b_b, w_o_b, w_up_b, w_down_b, xs, ffn_s = _post_decode(
            xs, ada_s, za, sgb, att4.reshape(n_s, N_HEADS * HEAD_DIM), w_b_out[l], w_o[l], gn, w_up[l],
            ffn_conv_w[l], fcb, w_down[l], gf, state_ffn_conv[l],
            proj_chunk=DECODE_PROJ_CHUNK, up_chunk=DECODE_UP_CHUNK, down_chunk=DECODE_DOWN_CHUNK)

        x1, conv_p, k_p, v_p = _mixer_prompt(xp, ada_p, gm, w_in_b, conv_a_w[l], attn_sinks[l],
                                             w_a_b, w_b_b, w_o_b, tb=PROMPT_MIXER_ROWS)
        xp, ffn_p = _ffn_prompt(x1, ada_p, gn, w_up_b, ffn_conv_w[l], fcb, w_down_b, gf,
                                tb=PROMPT_FFN_ROWS, ch=PROMPT_FFN_COLS, nw=PROMPT_FFN_DOWN_COLS)

        for lst, val in zip(outs, (
                conv_p.reshape(n_p, CONV_W - 1, D_CONV), conv_s,
                k_p.reshape(n_p, WINDOW, N_KV, HEAD_DIM), k_s,
                v_p.reshape(n_p, WINDOW, N_KV, HEAD_DIM), v_s,
                ffn_p.reshape(n_p, CONV_W - 1, 2 * D_FF), ffn_s)):
            lst.append(val)
    assert depth == 1, "final RMSNorm is fused into the single layer's FFN kernels"
    return (xp.reshape(n_p, seq, D_MODEL), xs.reshape(n_s, t_s, D_MODEL)) + tuple(jnp.stack(o) for o in outs)
```

```python
import functools

import jax
import jax.numpy as jnp
from jax import lax
from jax.experimental import pallas as pl
from jax.experimental.pallas import tpu as pltpu

f32 = jnp.float32
bf16 = jnp.bfloat16

D_MODEL = 1024
D_CONV = D_MODEL
CONV_W = 3
N_HEADS = 16
N_KV = 4
GROUP = N_HEADS // N_KV
HEAD_DIM = 64
WINDOW = 128
D_FF = 2816
EPS = 1e-6
N_MOD = 6
ATTN_SCALE = HEAD_DIM ** -0.5
KV_COLS = N_KV * HEAD_DIM
C_XIN, C_B, C_C = 0, D_CONV, 2 * D_CONV
C_Q = 3 * D_CONV
C_K = C_Q + N_HEADS * HEAD_DIM
C_V = C_K + KV_COLS
C_GA = C_V + KV_COLS
C_GB = C_GA + D_MODEL
IN_COLS = C_GB + D_MODEL

LANES = 128
SUBLANES = 8
Q_SUB = 128
ATTN_LOOKAHEAD = 3
FFN_LOOKAHEAD = 2
MXU_COLS = 256
VMEM_BYTES_V7X = 64 * 1024 * 1024


def _rms(x, g):
    ms = jnp.mean(x * x, axis=-1, keepdims=True)
    return x * lax.rsqrt(ms + EPS) * g


def _silu(x):
    return x * jax.nn.sigmoid(x)


def _dot(a, b):
    return jnp.dot(a, b, preferred_element_type=f32)


def _const_spec(shape):
    nd = len(shape)
    return pl.BlockSpec(shape, lambda i: (0,) * nd, pipeline_mode=pl.Buffered(1))


def _ada_kernel(c_ref, w_ref, b_ref, o_ref):
    c = c_ref[...]
    o_ref[...] = _dot(_silu(c).astype(bf16), w_ref[...].astype(bf16)) + b_ref[...]


def _ada(c_all, w_ada, b_ada):
    rows = c_all.shape[0]
    return pl.pallas_call(
        _ada_kernel,
        grid=(N_MOD,),
        in_specs=[
            pl.BlockSpec((rows, D_MODEL), lambda j: (0, 0)),
            pl.BlockSpec((D_MODEL, D_MODEL), lambda j: (0, j)),
            pl.BlockSpec((1, D_MODEL), lambda j: (0, j)),
        ],
        out_specs=pl.BlockSpec((rows, D_MODEL), lambda j: (0, j)),
        out_shape=jax.ShapeDtypeStruct((rows, N_MOD * D_MODEL), f32),
        compiler_params=pltpu.CompilerParams(dimension_semantics=("arbitrary",)),
        name="ada",
    )(c_all, w_ada, b_ada.reshape(1, -1))


def _mixer_prompt_kernel(sinks_ref, x_ref, ada_ref, g_ref, w_in_ref, cw_ref, w_a_ref, w_b_ref, w_o_ref,
                         x1_ref, conv_ref, knew_ref, vnew_ref,
                         ubuf, klo, khi, vt, attbuf, *, tb):
    i = pl.program_id(0)

    @pl.when(i == 0)
    def _():
        ubuf[0:SUBLANES, :] = jnp.zeros((SUBLANES, D_CONV), f32)
        for r in (klo, khi):
            r[:, 0:WINDOW, :] = jnp.zeros((N_KV, WINDOW, LANES), bf16)
        vt[:, 0:WINDOW] = jnp.zeros((KV_COLS, WINDOW), bf16)

    x = x_ref[...]
    sh1, sc1, gt1 = ada_ref[0:1, :], ada_ref[1:2, :], ada_ref[2:3, :]
    h = (_rms(x, g_ref[...]) * (1.0 + sc1) + sh1).astype(bf16)

    def proj(c0, n):
        return _dot(h, w_in_ref[:, c0:c0 + n])

    u = proj(C_C, D_CONV) * proj(C_XIN, D_CONV)
    ubuf[SUBLANES:SUBLANES + tb, :] = u
    b_gate = proj(C_B, D_CONV)
    q = (proj(C_Q, N_HEADS * HEAD_DIM) * ATTN_SCALE).astype(bf16)
    k = proj(C_K, KV_COLS)
    v = proj(C_V, KV_COLS)
    conv = (cw_ref[0:1, :] * ubuf[SUBLANES - 2:SUBLANES - 2 + tb, :]
            + cw_ref[1:2, :] * ubuf[SUBLANES - 1:SUBLANES - 1 + tb, :]
            + cw_ref[2:3, :] * u)
    conv_in = (b_gate * conv).astype(bf16)
    knew_ref[...] = k[tb - WINDOW:, :]
    vnew_ref[...] = v[tb - WINDOW:, :]

    lo = lax.broadcasted_iota(jnp.int32, (tb, LANES), 1) < HEAD_DIM
    for j in range(N_KV // 2):
        pair = k[:, LANES * j:LANES * (j + 1)]
        rolled = pltpu.roll(pair, HEAD_DIM, axis=1)
        zero = jnp.zeros_like(pair)
        klo[2 * j, WINDOW:WINDOW + tb, :] = jnp.where(lo, pair, zero).astype(bf16)
        khi[2 * j, WINDOW:WINDOW + tb, :] = jnp.where(lo, zero, rolled).astype(bf16)
        klo[2 * j + 1, WINDOW:WINDOW + tb, :] = jnp.where(lo, rolled, zero).astype(bf16)
        khi[2 * j + 1, WINDOW:WINDOW + tb, :] = jnp.where(lo, zero, pair).astype(bf16)
    vt[:, WINDOW:WINDOW + tb] = v.T.astype(bf16)

    cc = lax.broadcasted_iota(jnp.int32, (Q_SUB + WINDOW, 2 * Q_SUB), 0)
    col = lax.broadcasted_iota(jnp.int32, (Q_SUB + WINDOW, 2 * Q_SUB), 1)
    rr = col & (Q_SUB - 1)
    band = (cc >= rr) & (cc <= rr + WINDOW)
    first_head = lax.broadcasted_iota(jnp.int32, (1, 2 * Q_SUB), 1) < Q_SUB
    mask0 = band & (cc >= jnp.where(i == 0, WINDOW, 0))

    def scores(j, g, t):
        r0 = Q_SUB * j
        qs = jnp.concatenate([q[r0:r0 + Q_SUB, 2 * LANES * g:2 * LANES * g + LANES],
                              q[r0:r0 + Q_SUB, 2 * LANES * g + LANES:2 * LANES * (g + 1)]], axis=0)
        kr = (klo, khi)[t]
        return lax.dot_general(kr[g, r0:r0 + Q_SUB + WINDOW, :], qs, (((1,), (1,)), ((), ())),
                               preferred_element_type=f32)

    def finish(j, g, t, st):
        r0 = Q_SUB * j
        h0, h1 = GROUP * g + t, GROUP * g + 2 + t
        sink = jnp.where(first_head, sinks_ref[h0], sinks_ref[h1])
        st = jnp.where(mask0 if j == 0 else band, st, -jnp.inf)
        m = jnp.maximum(jnp.max(st, axis=0, keepdims=True), sink)
        e = jnp.exp(st - m)
        linv = 1.0 / (jnp.sum(e, axis=0, keepdims=True) + jnp.exp(sink - m))
        vtg = vt[HEAD_DIM * g:HEAD_DIM * (g + 1), r0:r0 + Q_SUB + WINDOW]
        ot = _dot(vtg, e.astype(bf16)) * linv
        attbuf[HEAD_DIM * h0:HEAD_DIM * (h0 + 1), r0:r0 + Q_SUB] = ot[:, 0:Q_SUB]
        attbuf[HEAD_DIM * h1:HEAD_DIM * (h1 + 1), r0:r0 + Q_SUB] = ot[:, Q_SUB:2 * Q_SUB]

    chains = [(j, g, t) for j in range(tb // Q_SUB) for g in range(N_KV) for t in range(2)]
    fillers = ([(lambda c=c: _dot(conv_in, w_a_ref[:, c:c + MXU_COLS])) for c in range(0, D_MODEL, MXU_COLS)]
               + [functools.partial(proj, c, MXU_COLS) for c in range(C_GA, C_GB + D_MODEL, MXU_COLS)])
    fill_every = -(-len(chains) // len(fillers))
    filled = []
    pending = [scores(*c) for c in chains[:ATTN_LOOKAHEAD]]
    for n, chain in enumerate(chains):
        if n + ATTN_LOOKAHEAD < len(chains):
            pending.append(scores(*chains[n + ATTN_LOOKAHEAD]))
        if n % fill_every == 0:
            filled.extend(f() for f in fillers[len(filled):len(filled) + 1])
        finish(*chain, pending.pop(0))
    filled.extend(f() for f in fillers[len(filled):])
    ya, ga, gb = (jnp.concatenate(filled[j:j + D_MODEL // MXU_COLS], axis=1)
                  for j in range(0, len(filled), D_MODEL // MXU_COLS))

    yb = _dot(attbuf[...].T.astype(bf16), w_b_ref[...])
    mix = (jax.nn.sigmoid(ga) * ya + jax.nn.sigmoid(gb) * yb).astype(bf16)
    for r0 in range(0, tb, tb // 2):
        x1_ref[r0:r0 + tb // 2, :] = x[r0:r0 + tb // 2, :] + gt1 * _dot(mix[r0:r0 + tb // 2, :], w_o_ref[...])

    conv_ref[...] = ubuf[SUBLANES + tb - (CONV_W - 1):SUBLANES + tb, :]
    ubuf[0:SUBLANES, :] = ubuf[tb:tb + SUBLANES, :]
    for r in (klo, khi):
        r[:, 0:WINDOW, :] = r[:, tb:tb + WINDOW, :]
    vt[:, 0:WINDOW] = vt[:, tb:tb + WINDOW]


def _mixer_prompt(x, ada8, g_mix, w_in, conv_w, sinks, w_a, w_b, w_o, *, tb):
    s = x.shape[0]
    assert s % tb == 0 and tb % Q_SUB == 0 and tb >= WINDOW
    kv_scratch = pltpu.VMEM((N_KV, WINDOW + tb, LANES), bf16)
    grid_spec = pltpu.PrefetchScalarGridSpec(
        num_scalar_prefetch=1,
        grid=(s // tb,),
        in_specs=[
            pl.BlockSpec((tb, D_MODEL), lambda i, sk: (i, 0)),
            pl.BlockSpec((SUBLANES, D_MODEL), lambda i, sk: (0, 0)),
            pl.BlockSpec((1, D_MODEL), lambda i, sk: (0, 0)),
            pl.BlockSpec((D_MODEL, IN_COLS), lambda i, sk: (0, 0), pipeline_mode=pl.Buffered(1)),
            pl.BlockSpec((CONV_W, D_CONV), lambda i, sk: (0, 0)),
            pl.BlockSpec((D_CONV, D_MODEL), lambda i, sk: (0, 0), pipeline_mode=pl.Buffered(1)),
            pl.BlockSpec((N_HEADS * HEAD_DIM, D_MODEL), lambda i, sk: (0, 0), pipeline_mode=pl.Buffered(1)),
            pl.BlockSpec((D_MODEL, D_MODEL), lambda i, sk: (0, 0), pipeline_mode=pl.Buffered(1)),
        ],
        out_specs=[
            pl.BlockSpec((tb, D_MODEL), lambda i, sk: (i, 0)),
            pl.BlockSpec((CONV_W - 1, D_CONV), lambda i, sk: (0, 0)),
            pl.BlockSpec((WINDOW, KV_COLS), lambda i, sk: (0, 0)),
            pl.BlockSpec((WINDOW, KV_COLS), lambda i, sk: (0, 0)),
        ],
        scratch_shapes=[
            pltpu.VMEM((SUBLANES + tb, D_CONV), f32),
            kv_scratch, kv_scratch,
            pltpu.VMEM((KV_COLS, WINDOW + tb), bf16),
            pltpu.VMEM((N_HEADS * HEAD_DIM, tb), f32),
        ],
    )
    return pl.pallas_call(
        functools.partial(_mixer_prompt_kernel, tb=tb),
        grid_spec=grid_spec,
        out_shape=[
            jax.ShapeDtypeStruct((s, D_MODEL), f32),
            jax.ShapeDtypeStruct((CONV_W - 1, D_CONV), f32),
            jax.ShapeDtypeStruct((WINDOW, KV_COLS), f32),
            jax.ShapeDtypeStruct((WINDOW, KV_COLS), f32),
        ],
        compiler_params=pltpu.CompilerParams(
            dimension_semantics=("arbitrary",),
            vmem_limit_bytes=VMEM_BYTES_V7X * 7 // 8),
        name="mixer_prompt",
    )(sinks, x, ada8, g_mix, w_in, conv_w, w_a, w_b, w_o)


def _ffn_prompt_kernel(x_ref, ada_ref, g_ref, w_up_ref, fcw_ref, fcb_ref, w_down_ref, gf_ref,
                       y_ref, fst_ref, upbuf, actbuf, *, tb, ch, nw):
    i = pl.program_id(0)

    @pl.when(i == 0)
    def _():
        upbuf[:, 0:SUBLANES, :] = jnp.zeros((2 * D_FF // LANES, SUBLANES, LANES), f32)

    hb = tb // 2
    sh2, sc2, gt2 = ada_ref[3:4, :], ada_ref[4:5, :], ada_ref[5:6, :]
    h = (_rms(x_ref[...], g_ref[...]) * (1.0 + sc2) + sh2).astype(bf16)

    def up_cols(half, c0):
        up = _dot(h[half * hb:(half + 1) * hb, :], w_up_ref[:, c0:c0 + ch])
        for s in range(ch // LANES):
            upbuf[c0 // LANES + s, SUBLANES:SUBLANES + hb, :] = up[:, s * LANES:(s + 1) * LANES]
        return up

    def conv_cols(c0, up):
        pieces = []
        for s in range(ch // LANES):
            slab = c0 // LANES + s
            cols = slice(c0 + s * LANES, c0 + (s + 1) * LANES)
            pieces.append(fcw_ref[0:1, cols] * upbuf[slab, SUBLANES - 2:SUBLANES - 2 + hb, :]
                          + fcw_ref[1:2, cols] * upbuf[slab, SUBLANES - 1:SUBLANES - 1 + hb, :]
                          + fcw_ref[2:3, cols] * up[:, s * LANES:(s + 1) * LANES]
                          + fcb_ref[0:1, cols])
            upbuf[slab, 0:SUBLANES, :] = upbuf[slab, hb:hb + SUBLANES, :]
        return jnp.concatenate(pieces, axis=1)

    def down_cols(half, n0):
        return _dot(actbuf[half], w_down_ref[:, n0:n0 + nw])

    def finish(half, parts):
        rows = slice(half * hb, (half + 1) * hb)
        x2 = x_ref[rows, :] + gt2 * jnp.concatenate(parts, axis=1)
        y_ref[rows, :] = _rms(x2, gf_ref[...])

    chunks = list(range(0, D_FF, ch))
    down_starts = list(range(0, D_MODEL, nw))
    down_at = {len(chunks) * (k + 1) // (len(down_starts) + 1): n0 for k, n0 in enumerate(down_starts)}
    assert len(down_at) == len(down_starts)
    for half in range(2):
        parts = []
        pending = [(up_cols(half, c0), up_cols(half, D_FF + c0)) for c0 in chunks[:FFN_LOOKAHEAD]]
        for n, c0 in enumerate(chunks):
            if n + FFN_LOOKAHEAD < len(chunks):
                c1 = chunks[n + FFN_LOOKAHEAD]
                pending.append((up_cols(half, c1), up_cols(half, D_FF + c1)))
            if half == 1 and n in down_at:
                parts.append(down_cols(0, down_at[n]))
            up_g, up_v = pending.pop(0)
            actbuf[half, :, c0:c0 + ch] = (_silu(conv_cols(c0, up_g))
                                           * conv_cols(D_FF + c0, up_v)).astype(bf16)
        if half == 1:
            finish(0, parts)
    finish(1, [down_cols(1, n0) for n0 in down_starts])

    for slab in range(2 * D_FF // LANES):
        fst_ref[:, slab * LANES:(slab + 1) * LANES] = upbuf[slab, SUBLANES - (CONV_W - 1):SUBLANES, :]


def _ffn_prompt(x1, ada8, g_ffn, w_up, fcw, fcb, w_down, g_final, *, tb, ch, nw):
    s = x1.shape[0]
    assert s % tb == 0 and tb % (4 * SUBLANES) == 0
    assert D_FF % ch == 0 and ch % LANES == 0 and D_MODEL % nw == 0 and nw % LANES == 0
    return pl.pallas_call(
        functools.partial(_ffn_prompt_kernel, tb=tb, ch=ch, nw=nw),
        grid=(s // tb,),
        in_specs=[
            pl.BlockSpec((tb, D_MODEL), lambda i: (i, 0)),
            pl.BlockSpec((SUBLANES, D_MODEL), lambda i: (0, 0)),
            pl.BlockSpec((1, D_MODEL), lambda i: (0, 0)),
            _const_spec((D_MODEL, 2 * D_FF)),
            pl.BlockSpec((CONV_W, 2 * D_FF), lambda i: (0, 0)),
            pl.BlockSpec((1, 2 * D_FF), lambda i: (0, 0)),
            _const_spec((D_FF, D_MODEL)),
            pl.BlockSpec((1, D_MODEL), lambda i: (0, 0)),
        ],
        out_specs=[
            pl.BlockSpec((tb, D_MODEL), lambda i: (i, 0)),
            pl.BlockSpec((CONV_W - 1, 2 * D_FF), lambda i: (0, 0)),
        ],
        out_shape=[
            jax.ShapeDtypeStruct((s, D_MODEL), f32),
            jax.ShapeDtypeStruct((CONV_W - 1, 2 * D_FF), f32),
        ],
        scratch_shapes=[
            pltpu.VMEM((2 * D_FF // LANES, SUBLANES + tb // 2, LANES), f32),
            pltpu.VMEM((2, tb // 2, D_FF), bf16),
        ],
        compiler_params=pltpu.CompilerParams(
            dimension_semantics=("arbitrary",),
            vmem_limit_bytes=VMEM_BYTES_V7X * 7 // 8),
        name="ffn_prompt",
    )(x1, ada8, g_ffn, w_up, fcw, fcb, w_down, g_final)


def _mixer_decode_pre_kernel(x_ref, ada_ref, g_ref, w_in_ref, cw_ref, st_ref, w_a_ref,
                             w_in_bf_ref, w_a_bf_ref, q_ref, k_ref, v_ref, za_ref, sgb_ref, stn_ref,
                             h_scr, proj_scr, *, chunk):
    j = pl.program_id(0)
    n_chunks = IN_COLS // chunk

    @pl.when(j == 0)
    def _():
        sh1 = ada_ref[:, 0:D_MODEL]
        sc1 = ada_ref[:, D_MODEL:2 * D_MODEL]
        h_scr[...] = (_rms(x_ref[...], g_ref[...]) * (1.0 + sc1) + sh1).astype(bf16)

    w_chunk = w_in_ref[...].astype(bf16)
    w_in_bf_ref[...] = w_chunk
    part = _dot(h_scr[...], w_chunk)
    for c in range(n_chunks):
        @pl.when(j == c)
        def _(c=c):
            proj_scr[:, c * chunk:(c + 1) * chunk] = part

    @pl.when(j == n_chunks - 1)
    def _():
        def proj(c0, n):
            return proj_scr[:, c0:c0 + n]

        w_a = w_a_ref[...].astype(bf16)
        w_a_bf_ref[...] = w_a
        u = proj(C_C, D_CONV) * proj(C_XIN, D_CONV)
        prev0 = st_ref[:, 0, :]
        prev1 = st_ref[:, 1, :]
        conv = cw_ref[0:1, :] * prev0 + cw_ref[1:2, :] * prev1 + cw_ref[2:3, :] * u
        stn_ref[:, 0, :] = prev1
        stn_ref[:, 1, :] = u
        ya = _dot((proj(C_B, D_CONV) * conv).astype(bf16), w_a)
        q_ref[...] = proj(C_Q, N_HEADS * HEAD_DIM) * ATTN_SCALE
        k_ref[...] = proj(C_K, KV_COLS)
        v_ref[...] = proj(C_V, KV_COLS)
        za_ref[...] = jax.nn.sigmoid(proj(C_GA, D_MODEL)) * ya
        sgb_ref[...] = jax.nn.sigmoid(proj(C_GB, D_MODEL))


def _mixer_decode_pre(x, ada, g_mix, w_in, conv_w, state, w_a, *, chunk):
    n = x.shape[0]
    assert IN_COLS % chunk == 0 and chunk % LANES == 0
    const2 = lambda j: (0, 0)
    const3 = lambda j: (0, 0, 0)
    row_block = lambda cols: pl.BlockSpec((n, cols), const2)
    state_block = pl.BlockSpec((n, CONV_W - 1, D_CONV), const3)
    shapes = [((D_MODEL, IN_COLS), bf16), ((D_CONV, D_MODEL), bf16),
              ((n, N_HEADS * HEAD_DIM), f32), ((n, KV_COLS), f32), ((n, KV_COLS), f32),
              ((n, D_MODEL), f32), ((n, D_MODEL), f32), ((n, CONV_W - 1, D_CONV), f32)]
    return pl.pallas_call(
        functools.partial(_mixer_decode_pre_kernel, chunk=chunk),
        grid=(IN_COLS // chunk,),
        in_specs=[row_block(D_MODEL), row_block(2 * D_MODEL), pl.BlockSpec((1, D_MODEL), const2),
                  pl.BlockSpec((D_MODEL, chunk), lambda j: (0, j)),
                  pl.BlockSpec((CONV_W, D_CONV), const2), state_block,
                  pl.BlockSpec((D_CONV, D_MODEL), const2)],
        out_specs=[pl.BlockSpec((D_MODEL, chunk), lambda j: (0, j)), pl.BlockSpec((D_CONV, D_MODEL), const2),
                   row_block(N_HEADS * HEAD_DIM), row_block(KV_COLS), row_block(KV_COLS),
                   row_block(D_MODEL), row_block(D_MODEL), state_block],
        out_shape=[jax.ShapeDtypeStruct(shp, dt) for shp, dt in shapes],
        scratch_shapes=[pltpu.VMEM((n, D_MODEL), bf16), pltpu.VMEM((n, IN_COLS), f32)],
        compiler_params=pltpu.CompilerParams(
            dimension_semantics=("arbitrary",),
            vmem_limit_bytes=VMEM_BYTES_V7X * 5 // 8),
        name="mixer_decode_pre",
    )(x, ada, g_mix, w_in, conv_w, state, w_a)


def _attn_decode_kernel(q_ref, kn_ref, vn_ref, knt_ref, vnt_ref, ck_ref, cv_ref, sink_ref,
                        att_ref, ok_ref, ov_ref, *, bb):
    step = pl.program_id(0)
    last = lax.broadcasted_iota(jnp.int32, (HEAD_DIM, WINDOW), 1) == WINDOW - 1
    to_front = (LANES - step * bb) % LANES
    kstep = [pltpu.roll(knt_ref[g], to_front, axis=1) for g in range(N_KV)]
    vstep = [pltpu.roll(vnt_ref[g], to_front, axis=1) for g in range(N_KV)]

    pairs = [(b, g) for b in range(bb) for g in range(N_KV)]
    sinks = [sink_ref[g][:, 0:1] for g in range(N_KV)]
    s_old = [_dot(q_ref[b, g].astype(bf16), ck_ref[b, g].astype(bf16)) for b, g in pairs]
    s_new = [jnp.sum(q_ref[b, g] * kn_ref[b, g], axis=-1, keepdims=True) for b, g in pairs]
    m = [jnp.maximum(jnp.maximum(jnp.max(so, axis=-1, keepdims=True), sn), sinks[g])
         for (b, g), so, sn in zip(pairs, s_old, s_new)]
    e_old = [jnp.exp(so - mm) for so, mm in zip(s_old, m)]
    e_new = [jnp.exp(sn - mm) for sn, mm in zip(s_new, m)]
    den = [jnp.sum(eo, axis=-1, keepdims=True) + en + jnp.exp(sinks[g] - mm)
           for (b, g), eo, en, mm in zip(pairs, e_old, e_new, m)]
    o_old = [lax.dot_general(eo.astype(bf16), cv_ref[b, g].astype(bf16), (((1,), (1,)), ((), ())),
                             preferred_element_type=f32) for (b, g), eo in zip(pairs, e_old)]
    for (b, g), oo, en, dd in zip(pairs, o_old, e_new, den):
        att_ref[b, g] = (oo + en * vn_ref[b, g]) / dd
    for b, g in pairs:
        ok_ref[b, g] = jnp.where(last, pltpu.roll(kstep[g], WINDOW - 1 - b, axis=1),
                                 pltpu.roll(ck_ref[b, g], WINDOW - 1, axis=1))
        ov_ref[b, g] = jnp.where(last, pltpu.roll(vstep[g], WINDOW - 1 - b, axis=1),
                                 pltpu.roll(cv_ref[b, g], WINDOW - 1, axis=1))


def _attn_decode(q4, kn4, vn4, knt, vnt, ck, cv, sink_b, *, bb):
    n = q4.shape[0]
    assert n % bb == 0 and n == LANES and WINDOW == LANES
    cache_spec = pl.BlockSpec((bb, N_KV, HEAD_DIM, WINDOW), lambda b: (b, 0, 0, 0))
    row_spec = pl.BlockSpec((bb, N_KV, 1, HEAD_DIM), lambda b: (b, 0, 0, 0))
    q_spec = pl.BlockSpec((bb, N_KV, GROUP, HEAD_DIM), lambda b: (b, 0, 0, 0))
    new_t_spec = pl.BlockSpec((N_KV, HEAD_DIM, n), lambda b: (0, 0, 0))
    return pl.pallas_call(
        functools.partial(_attn_decode_kernel, bb=bb),
        grid=(n // bb,),
        in_specs=[q_spec, row_spec, row_spec, new_t_spec, new_t_spec, cache_spec, cache_spec,
                  pl.BlockSpec((N_KV, GROUP, LANES), lambda b: (0, 0, 0))],
        out_specs=[q_spec, cache_spec, cache_spec],
        out_shape=[jax.ShapeDtypeStruct((n, N_KV, GROUP, HEAD_DIM), f32),
                   jax.ShapeDtypeStruct((n, N_KV, HEAD_DIM, WINDOW), f32),
                   jax.ShapeDtypeStruct((n, N_KV, HEAD_DIM, WINDOW), f32)],
        compiler_params=pltpu.CompilerParams(dimension_semantics=("arbitrary",)),
        name="attn_decode",
    )(q4, kn4, vn4, knt, vnt, ck, cv, sink_b)


def _post_decode_kernel(x_ref, ada_ref, za_ref, sgb_ref, att_ref, w_b_ref, w_o_ref, g_ref, w_up_ref,
                        fcw_ref, fcb_ref, w_down_ref, gf_ref, fst_ref,
                        w_b_bf_ref, w_o_bf_ref, w_up_bf_ref, w_down_bf_ref, y_ref, fstn_ref,
                        yb_scr, x1_scr, h_scr, up_scr, act_scr, acc_scr, *, phases):
    j = pl.program_id(0)
    (b0, nb, cb), (o0, no, co), (u0, nu, cu), (d0, nd, cd) = phases

    def mod(k):
        return ada_ref[:, k * D_MODEL:(k + 1) * D_MODEL]

    for c in range(nb):
        @pl.when(j == b0 + c)
        def _(c=c):
            w = w_b_ref[...].astype(bf16)
            w_b_bf_ref[...] = w
            yb_scr[:, c * cb:(c + 1) * cb] = _dot(att_ref[...].astype(bf16), w)

    for c in range(no):
        @pl.when(j == o0 + c)
        def _(c=c):
            w = w_o_ref[...].astype(bf16)
            w_o_bf_ref[...] = w
            mix = (za_ref[...] + sgb_ref[...] * yb_scr[...]).astype(bf16)
            cols = slice(c * co, (c + 1) * co)
            x1_scr[:, cols] = x_ref[:, cols] + mod(2)[:, cols] * _dot(mix, w)

    for c in range(nu):
        @pl.when(j == u0 + c)
        def _(c=c):
            if c == 0:
                h_scr[...] = (_rms(x1_scr[...], g_ref[...]) * (1.0 + mod(4)) + mod(3)).astype(bf16)
            w = w_up_ref[...].astype(bf16)
            w_up_bf_ref[...] = w
            up_scr[:, c * cu:(c + 1) * cu] = _dot(h_scr[...], w)

    for c in range(nd):
        @pl.when(j == d0 + c)
        def _(c=c):
            if c == 0:
                up = up_scr[...]
                prev0 = fst_ref[:, 0, :]
                prev1 = fst_ref[:, 1, :]
                conv = fcw_ref[0:1, :] * prev0 + fcw_ref[1:2, :] * prev1 + fcw_ref[2:3, :] * up + fcb_ref[...]
                fstn_ref[:, 0, :] = prev1
                fstn_ref[:, 1, :] = up
                act_scr[...] = (_silu(conv[:, 0:D_FF]) * conv[:, D_FF:2 * D_FF]).astype(bf16)
            w = w_down_ref[...].astype(bf16)
            w_down_bf_ref[...] = w
            part = _dot(act_scr[:, c * cd:(c + 1) * cd], w)
            acc_scr[...] = part if c == 0 else acc_scr[...] + part
            if c == nd - 1:
                y_ref[...] = _rms(x1_scr[...] + mod(5) * acc_scr[...], gf_ref[...])


def _post_decode(x, ada, za, sgb, att, w_b, w_o, g_ffn, w_up, fcw, fcb, w_down, g_final, fstate,
                 *, proj_chunk, up_chunk, down_chunk):
    n = x.shape[0]
    assert D_MODEL % proj_chunk == 0 and (2 * D_FF) % up_chunk == 0 and D_FF % down_chunk == 0
    assert proj_chunk % LANES == 0 and up_chunk % LANES == 0 and down_chunk % LANES == 0
    nb = no = D_MODEL // proj_chunk
    nu, nd = 2 * D_FF // up_chunk, D_FF // down_chunk
    b0, o0, u0, d0 = 0, nb, nb + no, nb + no + nu
    phases = ((b0, nb, proj_chunk), (o0, no, proj_chunk), (u0, nu, up_chunk), (d0, nd, down_chunk))

    def chunk_index(start, count):
        return lambda j: jnp.clip(j - start, 0, count - 1)

    ib, io, iu, idn = (chunk_index(s0, cnt) for s0, cnt, _ in phases)
    const2 = lambda j: (0, 0)
    const3 = lambda j: (0, 0, 0)
    rows = lambda cols: pl.BlockSpec((n, cols), const2, pipeline_mode=pl.Buffered(1))
    state_block = pl.BlockSpec((n, CONV_W - 1, 2 * D_FF), const3, pipeline_mode=pl.Buffered(1))
    col_chunk = lambda k, width, idx: pl.BlockSpec((k, width), lambda j: (0, idx(j)))
    w_down_block = pl.BlockSpec((down_chunk, D_MODEL), lambda j: (idn(j), 0))
    return pl.pallas_call(
        functools.partial(_post_decode_kernel, phases=phases),
        grid=(d0 + nd,),
        in_specs=[rows(D_MODEL), rows(N_MOD * D_MODEL), rows(D_MODEL), rows(D_MODEL), rows(N_HEADS * HEAD_DIM),
                  col_chunk(N_HEADS * HEAD_DIM, proj_chunk, ib), col_chunk(D_MODEL, proj_chunk, io),
                  pl.BlockSpec((1, D_MODEL), const2), col_chunk(D_MODEL, up_chunk, iu),
                  pl.BlockSpec((CONV_W, 2 * D_FF), const2), pl.BlockSpec((1, 2 * D_FF), const2),
                  w_down_block, pl.BlockSpec((1, D_MODEL), const2), state_block],
        out_specs=[col_chunk(N_HEADS * HEAD_DIM, proj_chunk, ib), col_chunk(D_MODEL, proj_chunk, io),
                   col_chunk(D_MODEL, up_chunk, iu), w_down_block, rows(D_MODEL), state_block],
        out_shape=[jax.ShapeDtypeStruct((N_HEADS * HEAD_DIM, D_MODEL), bf16),
                   jax.ShapeDtypeStruct((D_MODEL, D_MODEL), bf16),
                   jax.ShapeDtypeStruct((D_MODEL, 2 * D_FF), bf16),
                   jax.ShapeDtypeStruct((D_FF, D_MODEL), bf16),
                   jax.ShapeDtypeStruct((n, D_MODEL), f32),
                   jax.ShapeDtypeStruct((n, CONV_W - 1, 2 * D_FF), f32)],
        scratch_shapes=[pltpu.VMEM((n, D_MODEL), f32), pltpu.VMEM((n, D_MODEL), f32), pltpu.VMEM((n, D_MODEL), bf16),
                        pltpu.VMEM((n, 2 * D_FF), f32), pltpu.VMEM((n, D_FF), bf16), pltpu.VMEM((n, D_MODEL), f32)],
        compiler_params=pltpu.CompilerParams(
            dimension_semantics=("arbitrary",),
            vmem_limit_bytes=VMEM_BYTES_V7X * 7 // 8),
        name="post_decode",
    )(x, ada, za, sgb, att, w_b, w_o, g_ffn, w_up, fcw, fcb, w_down, g_final, fstate)


PROMPT_MIXER_ROWS = 512
PROMPT_FFN_ROWS = 1024
PROMPT_FFN_COLS = 256
PROMPT_FFN_DOWN_COLS = 256
DECODE_ATTN_BATCH = 8
DECODE_IN_CHUNK = 1664
DECODE_PROJ_CHUNK = 512
DECODE_UP_CHUNK = 512
DECODE_DOWN_CHUNK = 256


def kernel(x_prompt, x_sample, c_prompt, c_sample, state_conv_a, cache_k_win, cache_v_win, state_ffn_conv, w_ada, b_ada, g_mix, w_in, conv_a_w, attn_sinks, w_a_out, w_b_out, w_o, g_ffn, w_up, ffn_conv_w, ffn_conv_b, w_down, g_final):
    depth = w_in.shape[0]
    n_p, seq, _ = x_prompt.shape
    n_s, t_s, _ = x_sample.shape
    assert n_p == 1 and t_s == 1, "one prompt sequence and single-token decode only"
    xp = x_prompt.reshape(seq, D_MODEL)
    xs = x_sample.reshape(n_s, D_MODEL)
    pad = (-(n_s + n_p)) % SUBLANES
    c_all = jnp.concatenate([c_sample, c_prompt, jnp.zeros((pad, D_MODEL), f32)], axis=0)
    gf = g_final.reshape(1, D_MODEL)
    outs = [[] for _ in range(8)]
    for l in range(depth):
        ada = _ada(c_all, w_ada[l], b_ada[l])
        ada_s = ada[:n_s]
        ada_p = jnp.pad(ada[n_s].reshape(N_MOD, D_MODEL), ((0, SUBLANES - N_MOD), (0, 0)))
        gm, gn = g_mix[l].reshape(1, D_MODEL), g_ffn[l].reshape(1, D_MODEL)
        fcb = ffn_conv_b[l].reshape(1, 2 * D_FF)

        w_in_b, w_a_b, q, k_n, v_n, za, sgb, conv_s = _mixer_decode_pre(
            xs, ada_s, gm, w_in[l], conv_a_w[l], state_conv_a[l], w_a_out[l], chunk=DECODE_IN_CHUNK)
        to_native = lambda c: c.transpose(0, 2, 3, 1)
        from_native = lambda c: c.transpose(0, 3, 1, 2)
        sink_b = jnp.broadcast_to(attn_sinks[l].reshape(N_KV, GROUP, 1), (N_KV, GROUP, LANES))
        att4, k_s, v_s = _attn_decode(
            q.reshape(n_s, N_KV, GROUP, HEAD_DIM),
            k_n.reshape(n_s, N_KV, 1, HEAD_DIM), v_n.reshape(n_s, N_KV, 1, HEAD_DIM),
            k_n.T.reshape(N_KV, HEAD_DIM, n_s), v_n.T.reshape(N_KV, HEAD_DIM, n_s),
            to_native(cache_k_win[l]), to_native(cache_v_win[l]), sink_b, bb=DECODE_ATTN_BATCH)
        k_s, v_s = from_native(k_s), from_native(v_s)
        w_b_b, w_o_b, w_up_b, w_down_b, xs, ffn_s = _post_decode(
            xs, ada_s, za, sgb, att4.reshape(n_s, N_HEADS * HEAD_DIM), w_b_out[l], w_o[l], gn, w_up[l],
            ffn_conv_w[l], fcb, w_down[l], gf, state_ffn_conv[l],
            proj_chunk=DECODE_PROJ_CHUNK, up_chunk=DECODE_UP_CHUNK, down_chunk=DECODE_DOWN_CHUNK)

        x1, conv_p, k_p, v_p = _mixer_prompt(xp, ada_p, gm, w_in_b, conv_a_w[l], attn_sinks[l],
                                             w_a_b, w_b_b, w_o_b, tb=PROMPT_MIXER_ROWS)
        xp, ffn_p = _ffn_prompt(x1, ada_p, gn, w_up_b, ffn_conv_w[l], fcb, w_down_b, gf,
                                tb=PROMPT_FFN_ROWS, ch=PROMPT_FFN_COLS, nw=PROMPT_FFN_DOWN_COLS)

        for lst, val in zip(outs, (
                conv_p.reshape(n_p, CONV_W - 1, D_CONV), conv_s,
                k_p.reshape(n_p, WINDOW, N_KV, HEAD_DIM), k_s,
                v_p.reshape(n_p, WINDOW, N_KV, HEAD_DIM), v_s,
                ffn_p.reshape(n_p, CONV_W - 1, 2 * D_FF), ffn_s)):
            lst.append(val)
    assert depth == 1, "final RMSNorm is fused into the single layer's FFN kernels"
    return (xp.reshape(n_p, seq, D_MODEL), xs.reshape(n_s, t_s, D_MODEL)) + tuple(jnp.stack(o) for o in outs)
```

```python
import functools

import jax
import jax.numpy as jnp
from jax import lax
from jax.experimental import pallas as pl
from jax.experimental.pallas import tpu as pltpu

f32 = jnp.float32
bf16 = jnp.bfloat16

D_MODEL = 1024
D_CONV = D_MODEL
CONV_W = 3
N_HEADS = 16
N_KV = 4
GROUP = N_HEADS // N_KV
HEAD_DIM = 64
WINDOW = 128
D_FF = 2816
EPS = 1e-6
N_MOD = 6
ATTN_SCALE = HEAD_DIM ** -0.5
KV_COLS = N_KV * HEAD_DIM
C_XIN, C_B, C_C = 0, D_CONV, 2 * D_CONV
C_Q = 3 * D_CONV
C_K = C_Q + N_HEADS * HEAD_DIM
C_V = C_K + KV_COLS
C_GA = C_V + KV_COLS
C_GB = C_GA + D_MODEL
IN_COLS = C_GB + D_MODEL

LANES = 128
SUBLANES = 8
Q_SUB = 128
ATTN_LOOKAHEAD = 3
FFN_LOOKAHEAD = 2
MXU_COLS = 256
VMEM_BYTES_V7X = 64 * 1024 * 1024


def _rms(x, g):
    ms = jnp.mean(x * x, axis=-1, keepdims=True)
    return x * lax.rsqrt(ms + EPS) * g


def _silu(x):
    return x * jax.nn.sigmoid(x)


def _dot(a, b):
    return jnp.dot(a, b, preferred_element_type=f32)


def _const_spec(shape):
    nd = len(shape)
    return pl.BlockSpec(shape, lambda i: (0,) * nd, pipeline_mode=pl.Buffered(1))


def _ada_kernel(cs_ref, cp_ref, w_ref, b_ref, os_ref, op_ref):
    w = w_ref[...].astype(bf16)
    os_ref[...] = _dot(_silu(cs_ref[...]).astype(bf16), w) + b_ref[...]
    cp = jnp.broadcast_to(_silu(cp_ref[...]), (SUBLANES, D_MODEL)).astype(bf16)
    op_ref[0] = _dot(cp, w)[0:1, :] + b_ref[...]


def _ada(c_sample, c_prompt, w_ada, b_ada):
    n_s = c_sample.shape[0]
    assert c_prompt.shape[0] == 1
    return pl.pallas_call(
        _ada_kernel,
        grid=(N_MOD,),
        in_specs=[
            pl.BlockSpec((n_s, D_MODEL), lambda j: (0, 0)),
            pl.BlockSpec((1, D_MODEL), lambda j: (0, 0)),
            pl.BlockSpec((D_MODEL, D_MODEL), lambda j: (0, j)),
            pl.BlockSpec((1, D_MODEL), lambda j: (0, j)),
        ],
        out_specs=[pl.BlockSpec((n_s, D_MODEL), lambda j: (0, j)),
                   pl.BlockSpec((1, 1, D_MODEL), lambda j: (j, 0, 0))],
        out_shape=[jax.ShapeDtypeStruct((n_s, N_MOD * D_MODEL), f32),
                   jax.ShapeDtypeStruct((N_MOD, 1, D_MODEL), f32)],
        compiler_params=pltpu.CompilerParams(dimension_semantics=("arbitrary",)),
        name="ada",
    )(c_sample, c_prompt, w_ada, b_ada.reshape(1, -1))


def _mixer_prompt_kernel(sinks_ref, x_ref, ada_ref, g_ref, w_in_ref, cw_ref, w_a_ref, w_b_ref, w_o_ref,
                         x1_ref, conv_ref, knew_ref, vnew_ref,
                         ubuf, klo, khi, vt, attbuf, *, tb):
    i = pl.program_id(0)

    @pl.when(i == 0)
    def _():
        ubuf[0:SUBLANES, :] = jnp.zeros((SUBLANES, D_CONV), f32)
        for r in (klo, khi):
            r[:, 0:WINDOW, :] = jnp.zeros((N_KV, WINDOW, LANES), bf16)
        vt[:, 0:WINDOW] = jnp.zeros((KV_COLS, WINDOW), bf16)

    x = x_ref[...]
    sh1, sc1, gt1 = ada_ref[0], ada_ref[1], ada_ref[2]
    h = (_rms(x, g_ref[...]) * (1.0 + sc1) + sh1).astype(bf16)

    def proj(c0, n):
        return _dot(h, w_in_ref[:, c0:c0 + n])

    u = proj(C_C, D_CONV) * proj(C_XIN, D_CONV)
    ubuf[SUBLANES:SUBLANES + tb, :] = u
    b_gate = proj(C_B, D_CONV)
    q = (proj(C_Q, N_HEADS * HEAD_DIM) * ATTN_SCALE).astype(bf16)
    k = proj(C_K, KV_COLS)
    v = proj(C_V, KV_COLS)
    conv = (cw_ref[0:1, :] * ubuf[SUBLANES - 2:SUBLANES - 2 + tb, :]
            + cw_ref[1:2, :] * ubuf[SUBLANES - 1:SUBLANES - 1 + tb, :]
            + cw_ref[2:3, :] * u)
    conv_in = (b_gate * conv).astype(bf16)
    knew_ref[...] = k[tb - WINDOW:, :]
    vnew_ref[...] = v[tb - WINDOW:, :]

    lo = lax.broadcasted_iota(jnp.int32, (tb, LANES), 1) < HEAD_DIM
    for j in range(N_KV // 2):
        pair = k[:, LANES * j:LANES * (j + 1)]
        rolled = pltpu.roll(pair, HEAD_DIM, axis=1)
        zero = jnp.zeros_like(pair)
        klo[2 * j, WINDOW:WINDOW + tb, :] = jnp.where(lo, pair, zero).astype(bf16)
        khi[2 * j, WINDOW:WINDOW + tb, :] = jnp.where(lo, zero, rolled).astype(bf16)
        klo[2 * j + 1, WINDOW:WINDOW + tb, :] = jnp.where(lo, rolled, zero).astype(bf16)
        khi[2 * j + 1, WINDOW:WINDOW + tb, :] = jnp.where(lo, zero, pair).astype(bf16)
    vt[:, WINDOW:WINDOW + tb] = v.T.astype(bf16)

    cc = lax.broadcasted_iota(jnp.int32, (Q_SUB + WINDOW, 2 * Q_SUB), 0)
    col = lax.broadcasted_iota(jnp.int32, (Q_SUB + WINDOW, 2 * Q_SUB), 1)
    rr = col & (Q_SUB - 1)
    band = (cc >= rr) & (cc <= rr + WINDOW)
    first_head = lax.broadcasted_iota(jnp.int32, (1, 2 * Q_SUB), 1) < Q_SUB
    mask0 = band & (cc >= jnp.where(i == 0, WINDOW, 0))

    def scores(j, g, t):
        r0 = Q_SUB * j
        qs = jnp.concatenate([q[r0:r0 + Q_SUB, 2 * LANES * g:2 * LANES * g + LANES],
                              q[r0:r0 + Q_SUB, 2 * LANES * g + LANES:2 * LANES * (g + 1)]], axis=0)
        kr = (klo, khi)[t]
        return lax.dot_general(kr[g, r0:r0 + Q_SUB + WINDOW, :], qs, (((1,), (1,)), ((), ())),
                               preferred_element_type=f32)

    def finish(j, g, t, st):
        r0 = Q_SUB * j
        h0, h1 = GROUP * g + t, GROUP * g + 2 + t
        sink = jnp.where(first_head, sinks_ref[h0], sinks_ref[h1])
        st = jnp.where(mask0 if j == 0 else band, st, -jnp.inf)
        m = jnp.maximum(jnp.max(st, axis=0, keepdims=True), sink)
        e = jnp.exp(st - m)
        linv = 1.0 / (jnp.sum(e, axis=0, keepdims=True) + jnp.exp(sink - m))
        vtg = vt[HEAD_DIM * g:HEAD_DIM * (g + 1), r0:r0 + Q_SUB + WINDOW]
        ot = _dot(vtg, e.astype(bf16)) * linv
        attbuf[HEAD_DIM * h0:HEAD_DIM * (h0 + 1), r0:r0 + Q_SUB] = ot[:, 0:Q_SUB]
        attbuf[HEAD_DIM * h1:HEAD_DIM * (h1 + 1), r0:r0 + Q_SUB] = ot[:, Q_SUB:2 * Q_SUB]

    chains = [(j, g, t) for j in range(tb // Q_SUB) for g in range(N_KV) for t in range(2)]
    fillers = ([(lambda c=c: _dot(conv_in, w_a_ref[:, c:c + MXU_COLS])) for c in range(0, D_MODEL, MXU_COLS)]
               + [functools.partial(proj, c, MXU_COLS) for c in range(C_GA, C_GB + D_MODEL, MXU_COLS)])
    fill_every = -(-len(chains) // len(fillers))
    filled = []
    pending = [scores(*c) for c in chains[:ATTN_LOOKAHEAD]]
    for n, chain in enumerate(chains):
        if n + ATTN_LOOKAHEAD < len(chains):
            pending.append(scores(*chains[n + ATTN_LOOKAHEAD]))
        if n % fill_every == 0:
            filled.extend(f() for f in fillers[len(filled):len(filled) + 1])
        finish(*chain, pending.pop(0))
    filled.extend(f() for f in fillers[len(filled):])
    ya, ga, gb = (jnp.concatenate(filled[j:j + D_MODEL // MXU_COLS], axis=1)
                  for j in range(0, len(filled), D_MODEL // MXU_COLS))

    yb = _dot(attbuf[...].T.astype(bf16), w_b_ref[...])
    mix = (jax.nn.sigmoid(ga) * ya + jax.nn.sigmoid(gb) * yb).astype(bf16)
    for r0 in range(0, tb, tb // 2):
        x1_ref[r0:r0 + tb // 2, :] = x[r0:r0 + tb // 2, :] + gt1 * _dot(mix[r0:r0 + tb // 2, :], w_o_ref[...])

    conv_ref[...] = ubuf[SUBLANES + tb - (CONV_W - 1):SUBLANES + tb, :]
    ubuf[0:SUBLANES, :] = ubuf[tb:tb + SUBLANES, :]
    for r in (klo, khi):
        r[:, 0:WINDOW, :] = r[:, tb:tb + WINDOW, :]
    vt[:, 0:WINDOW] = vt[:, tb:tb + WINDOW]


def _mixer_prompt(x, ada, g_mix, w_in, conv_w, sinks, w_a, w_b, w_o, *, tb):
    s = x.shape[0]
    assert s % tb == 0 and tb % Q_SUB == 0 and tb >= WINDOW
    kv_scratch = pltpu.VMEM((N_KV, WINDOW + tb, LANES), bf16)
    grid_spec = pltpu.PrefetchScalarGridSpec(
        num_scalar_prefetch=1,
        grid=(s // tb,),
        in_specs=[
            pl.BlockSpec((tb, D_MODEL), lambda i, sk: (i, 0)),
            pl.BlockSpec((N_MOD, 1, D_MODEL), lambda i, sk: (0, 0, 0)),
            pl.BlockSpec((1, D_MODEL), lambda i, sk: (0, 0)),
            pl.BlockSpec((D_MODEL, IN_COLS), lambda i, sk: (0, 0), pipeline_mode=pl.Buffered(1)),
            pl.BlockSpec((CONV_W, D_CONV), lambda i, sk: (0, 0)),
            pl.BlockSpec((D_CONV, D_MODEL), lambda i, sk: (0, 0), pipeline_mode=pl.Buffered(1)),
            pl.BlockSpec((N_HEADS * HEAD_DIM, D_MODEL), lambda i, sk: (0, 0), pipeline_mode=pl.Buffered(1)),
            pl.BlockSpec((D_MODEL, D_MODEL), lambda i, sk: (0, 0), pipeline_mode=pl.Buffered(1)),
        ],
        out_specs=[
            pl.BlockSpec((tb, D_MODEL), lambda i, sk: (i, 0)),
            pl.BlockSpec((CONV_W - 1, D_CONV), lambda i, sk: (0, 0)),
            pl.BlockSpec((WINDOW, KV_COLS), lambda i, sk: (0, 0)),
            pl.BlockSpec((WINDOW, KV_COLS), lambda i, sk: (0, 0)),
        ],
        scratch_shapes=[
            pltpu.VMEM((SUBLANES + tb, D_CONV), f32),
            kv_scratch, kv_scratch,
            pltpu.VMEM((KV_COLS, WINDOW + tb), bf16),
            pltpu.VMEM((N_HEADS * HEAD_DIM, tb), f32),
        ],
    )
    return pl.pallas_call(
        functools.partial(_mixer_prompt_kernel, tb=tb),
        grid_spec=grid_spec,
        out_shape=[
            jax.ShapeDtypeStruct((s, D_MODEL), f32),
            jax.ShapeDtypeStruct((CONV_W - 1, D_CONV), f32),
            jax.ShapeDtypeStruct((WINDOW, KV_COLS), f32),
            jax.ShapeDtypeStruct((WINDOW, KV_COLS), f32),
        ],
        compiler_params=pltpu.CompilerParams(
            dimension_semantics=("arbitrary",),
            vmem_limit_bytes=VMEM_BYTES_V7X * 7 // 8),
        name="mixer_prompt",
    )(sinks, x, ada, g_mix, w_in, conv_w, w_a, w_b, w_o)


def _ffn_prompt_kernel(x_ref, ada_ref, g_ref, w_up_ref, fcw_ref, fcb_ref, w_down_ref, gf_ref,
                       y_ref, fst_ref, upbuf, actbuf, *, tb, ch, nw):
    i = pl.program_id(0)

    @pl.when(i == 0)
    def _():
        upbuf[:, 0:SUBLANES, :] = jnp.zeros((2 * D_FF // LANES, SUBLANES, LANES), f32)

    hb = tb // 2
    sh2, sc2, gt2 = ada_ref[3], ada_ref[4], ada_ref[5]
    h = (_rms(x_ref[...], g_ref[...]) * (1.0 + sc2) + sh2).astype(bf16)

    def up_cols(half, c0):
        up = _dot(h[half * hb:(half + 1) * hb, :], w_up_ref[:, c0:c0 + ch])
        for s in range(ch // LANES):
            upbuf[c0 // LANES + s, SUBLANES:SUBLANES + hb, :] = up[:, s * LANES:(s + 1) * LANES]
        return up

    def conv_cols(c0, up):
        pieces = []
        for s in range(ch // LANES):
            slab = c0 // LANES + s
            cols = slice(c0 + s * LANES, c0 + (s + 1) * LANES)
            pieces.append(fcw_ref[0:1, cols] * upbuf[slab, SUBLANES - 2:SUBLANES - 2 + hb, :]
                          + fcw_ref[1:2, cols] * upbuf[slab, SUBLANES - 1:SUBLANES - 1 + hb, :]
                          + fcw_ref[2:3, cols] * up[:, s * LANES:(s + 1) * LANES]
                          + fcb_ref[0:1, cols])
            upbuf[slab, 0:SUBLANES, :] = upbuf[slab, hb:hb + SUBLANES, :]
        return jnp.concatenate(pieces, axis=1)

    def down_cols(half, n0):
        return _dot(actbuf[half], w_down_ref[:, n0:n0 + nw])

    def finish(half, parts):
        rows = slice(half * hb, (half + 1) * hb)
        x2 = x_ref[rows, :] + gt2 * jnp.concatenate(parts, axis=1)
        y_ref[rows, :] = _rms(x2, gf_ref[...])

    chunks = list(range(0, D_FF, ch))
    down_starts = list(range(0, D_MODEL, nw))
    down_at = {len(chunks) * (k + 1) // (len(down_starts) + 1): n0 for k, n0 in enumerate(down_starts)}
    assert len(down_at) == len(down_starts)
    for half in range(2):
        parts = []
        pending = [(up_cols(half, c0), up_cols(half, D_FF + c0)) for c0 in chunks[:FFN_LOOKAHEAD]]
        for n, c0 in enumerate(chunks):
            if n + FFN_LOOKAHEAD < len(chunks):
                c1 = chunks[n + FFN_LOOKAHEAD]
                pending.append((up_cols(half, c1), up_cols(half, D_FF + c1)))
            if half == 1 and n in down_at:
                parts.append(down_cols(0, down_at[n]))
            up_g, up_v = pending.pop(0)
            actbuf[half, :, c0:c0 + ch] = (_silu(conv_cols(c0, up_g))
                                           * conv_cols(D_FF + c0, up_v)).astype(bf16)
        if half == 1:
            finish(0, parts)
    finish(1, [down_cols(1, n0) for n0 in down_starts])

    for slab in range(2 * D_FF // LANES):
        fst_ref[:, slab * LANES:(slab + 1) * LANES] = upbuf[slab, SUBLANES - (CONV_W - 1):SUBLANES, :]


def _ffn_prompt(x1, ada, g_ffn, w_up, fcw, fcb, w_down, g_final, *, tb, ch, nw):
    s = x1.shape[0]
    assert s % tb == 0 and tb % (4 * SUBLANES) == 0
    assert D_FF % ch == 0 and ch % LANES == 0 and D_MODEL % nw == 0 and nw % LANES == 0
    return pl.pallas_call(
        functools.partial(_ffn_prompt_kernel, tb=tb, ch=ch, nw=nw),
        grid=(s // tb,),
        in_specs=[
            pl.BlockSpec((tb, D_MODEL), lambda i: (i, 0)),
            pl.BlockSpec((N_MOD, 1, D_MODEL), lambda i: (0, 0, 0)),
            pl.BlockSpec((1, D_MODEL), lambda i: (0, 0)),
            _const_spec((D_MODEL, 2 * D_FF)),
            pl.BlockSpec((CONV_W, 2 * D_FF), lambda i: (0, 0)),
            pl.BlockSpec((1, 2 * D_FF), lambda i: (0, 0)),
            _const_spec((D_FF, D_MODEL)),
            pl.BlockSpec((1, D_MODEL), lambda i: (0, 0)),
        ],
        out_specs=[
            pl.BlockSpec((tb, D_MODEL), lambda i: (i, 0)),
            pl.BlockSpec((CONV_W - 1, 2 * D_FF), lambda i: (0, 0)),
        ],
        out_shape=[
            jax.ShapeDtypeStruct((s, D_MODEL), f32),
            jax.ShapeDtypeStruct((CONV_W - 1, 2 * D_FF), f32),
        ],
        scratch_shapes=[
            pltpu.VMEM((2 * D_FF // LANES, SUBLANES + tb // 2, LANES), f32),
            pltpu.VMEM((2, tb // 2, D_FF), bf16),
        ],
        compiler_params=pltpu.CompilerParams(
            dimension_semantics=("arbitrary",),
            vmem_limit_bytes=VMEM_BYTES_V7X * 7 // 8),
        name="ffn_prompt",
    )(x1, ada, g_ffn, w_up, fcw, fcb, w_down, g_final)


def _mixer_decode_pre_kernel(x_ref, ada_ref, g_ref, w_in_ref, cw_ref, st_ref, w_a_ref,
                             w_in_bf_ref, w_a_bf_ref, q_ref, k_ref, v_ref, za_ref, sgb_ref, stn_ref,
                             h_scr, proj_scr, *, chunk):
    j = pl.program_id(0)
    n_chunks = IN_COLS // chunk

    @pl.when(j == 0)
    def _():
        sh1 = ada_ref[:, 0:D_MODEL]
        sc1 = ada_ref[:, D_MODEL:2 * D_MODEL]
        h_scr[...] = (_rms(x_ref[:, 0, :], g_ref[...]) * (1.0 + sc1) + sh1).astype(bf16)

    w_chunk = w_in_ref[...].astype(bf16)
    w_in_bf_ref[...] = w_chunk
    part = _dot(h_scr[...], w_chunk)
    for c in range(n_chunks):
        @pl.when(j == c)
        def _(c=c):
            proj_scr[:, c * chunk:(c + 1) * chunk] = part

    @pl.when(j == n_chunks - 1)
    def _():
        def proj(c0, n):
            return proj_scr[:, c0:c0 + n]

        w_a = w_a_ref[...].astype(bf16)
        w_a_bf_ref[...] = w_a
        u = proj(C_C, D_CONV) * proj(C_XIN, D_CONV)
        prev0 = st_ref[:, 0, :]
        prev1 = st_ref[:, 1, :]
        conv = cw_ref[0:1, :] * prev0 + cw_ref[1:2, :] * prev1 + cw_ref[2:3, :] * u
        stn_ref[:, 0, :] = prev1
        stn_ref[:, 1, :] = u
        ya = _dot((proj(C_B, D_CONV) * conv).astype(bf16), w_a)
        q_ref[...] = proj(C_Q, N_HEADS * HEAD_DIM) * ATTN_SCALE
        k_ref[...] = proj(C_K, KV_COLS)
        v_ref[...] = proj(C_V, KV_COLS)
        za_ref[...] = jax.nn.sigmoid(proj(C_GA, D_MODEL)) * ya
        sgb_ref[...] = jax.nn.sigmoid(proj(C_GB, D_MODEL))


def _mixer_decode_pre(x, ada, g_mix, w_in, conv_w, state, w_a, *, chunk):
    n = x.shape[0]
    assert IN_COLS % chunk == 0 and chunk % LANES == 0
    const2 = lambda j: (0, 0)
    const3 = lambda j: (0, 0, 0)
    row_block = lambda cols: pl.BlockSpec((n, cols), const2)
    state_block = pl.BlockSpec((n, CONV_W - 1, D_CONV), const3)
    shapes = [((D_MODEL, IN_COLS), bf16), ((D_CONV, D_MODEL), bf16),
              ((n, N_HEADS * HEAD_DIM), f32), ((n, KV_COLS), f32), ((n, KV_COLS), f32),
              ((n, D_MODEL), f32), ((n, D_MODEL), f32), ((n, CONV_W - 1, D_CONV), f32)]
    return pl.pallas_call(
        functools.partial(_mixer_decode_pre_kernel, chunk=chunk),
        grid=(IN_COLS // chunk,),
        in_specs=[pl.BlockSpec((n, 1, D_MODEL), const3), row_block(2 * D_MODEL), pl.BlockSpec((1, D_MODEL), const2),
                  pl.BlockSpec((D_MODEL, chunk), lambda j: (0, j)),
                  pl.BlockSpec((CONV_W, D_CONV), const2), state_block,
                  pl.BlockSpec((D_CONV, D_MODEL), const2)],
        out_specs=[pl.BlockSpec((D_MODEL, chunk), lambda j: (0, j)), pl.BlockSpec((D_CONV, D_MODEL), const2),
                   row_block(N_HEADS * HEAD_DIM), row_block(KV_COLS), row_block(KV_COLS),
                   row_block(D_MODEL), row_block(D_MODEL), state_block],
        out_shape=[jax.ShapeDtypeStruct(shp, dt) for shp, dt in shapes],
        scratch_shapes=[pltpu.VMEM((n, D_MODEL), bf16), pltpu.VMEM((n, IN_COLS), f32)],
        compiler_params=pltpu.CompilerParams(
            dimension_semantics=("arbitrary",),
            vmem_limit_bytes=VMEM_BYTES_V7X * 5 // 8),
        name="mixer_decode_pre",
    )(x, ada, g_mix, w_in, conv_w, state, w_a)


def _attn_decode_kernel(q_ref, kn_ref, vn_ref, knt_ref, vnt_ref, ck_ref, cv_ref, sink_ref,
                        att_ref, ok_ref, ov_ref, *, bb):
    step = pl.program_id(0)
    last = lax.broadcasted_iota(jnp.int32, (HEAD_DIM, WINDOW), 1) == WINDOW - 1
    to_front = (LANES - step * bb) % LANES
    kstep = [pltpu.roll(knt_ref[g], to_front, axis=1) for g in range(N_KV)]
    vstep = [pltpu.roll(vnt_ref[g], to_front, axis=1) for g in range(N_KV)]

    pairs = [(b, g) for b in range(bb) for g in range(N_KV)]
    n_stage = 7
    cuts = [len(pairs) * s // n_stage for s in range(n_stage + 1)]

    def shift_caches(stage):
        for b, g in pairs[cuts[stage]:cuts[stage + 1]]:
            ok_ref[b, g] = jnp.where(last, pltpu.roll(kstep[g], WINDOW - 1 - b, axis=1),
                                     pltpu.roll(ck_ref[b, g], WINDOW - 1, axis=1))
            ov_ref[b, g] = jnp.where(last, pltpu.roll(vstep[g], WINDOW - 1 - b, axis=1),
                                     pltpu.roll(cv_ref[b, g], WINDOW - 1, axis=1))

    sinks = [sink_ref[g][:, 0:1] for g in range(N_KV)]
    shift_caches(0)
    s_old = [_dot(q_ref[b, g].astype(bf16), ck_ref[b, g].astype(bf16)) for b, g in pairs]
    shift_caches(1)
    s_new = [jnp.sum(q_ref[b, g] * kn_ref[b, g], axis=-1, keepdims=True) for b, g in pairs]
    shift_caches(2)
    m = [jnp.maximum(jnp.maximum(jnp.max(so, axis=-1, keepdims=True), sn), sinks[g])
         for (b, g), so, sn in zip(pairs, s_old, s_new)]
    shift_caches(3)
    e_old = [jnp.exp(so - mm) for so, mm in zip(s_old, m)]
    e_new = [jnp.exp(sn - mm) for sn, mm in zip(s_new, m)]
    den = [jnp.sum(eo, axis=-1, keepdims=True) + en + jnp.exp(sinks[g] - mm)
           for (b, g), eo, en, mm in zip(pairs, e_old, e_new, m)]
    shift_caches(4)
    o_old = [lax.dot_general(eo.astype(bf16), cv_ref[b, g].astype(bf16), (((1,), (1,)), ((), ())),
                             preferred_element_type=f32) for (b, g), eo in zip(pairs, e_old)]
    shift_caches(5)
    for (b, g), oo, en, dd in zip(pairs, o_old, e_new, den):
        att_ref[b, g] = (oo + en * vn_ref[b, g]) / dd
    shift_caches(6)


def _attn_decode(q4, kn4, vn4, knt, vnt, ck, cv, sink_b, *, bb):
    n = q4.shape[0]
    assert n % bb == 0 and n == LANES and WINDOW == LANES
    cache_spec = pl.BlockSpec((bb, N_KV, HEAD_DIM, WINDOW), lambda b: (b, 0, 0, 0))
    row_spec = pl.BlockSpec((bb, N_KV, 1, HEAD_DIM), lambda b: (b, 0, 0, 0))
    q_spec = pl.BlockSpec((bb, N_KV, GROUP, HEAD_DIM), lambda b: (b, 0, 0, 0))
    new_t_spec = pl.BlockSpec((N_KV, HEAD_DIM, n), lambda b: (0, 0, 0))
    return pl.pallas_call(
        functools.partial(_attn_decode_kernel, bb=bb),
        grid=(n // bb,),
        in_specs=[q_spec, row_spec, row_spec, new_t_spec, new_t_spec, cache_spec, cache_spec,
                  pl.BlockSpec((N_KV, GROUP, LANES), lambda b: (0, 0, 0))],
        out_specs=[q_spec, cache_spec, cache_spec],
        out_shape=[jax.ShapeDtypeStruct((n, N_KV, GROUP, HEAD_DIM), f32),
                   jax.ShapeDtypeStruct((n, N_KV, HEAD_DIM, WINDOW), f32),
                   jax.ShapeDtypeStruct((n, N_KV, HEAD_DIM, WINDOW), f32)],
        compiler_params=pltpu.CompilerParams(dimension_semantics=("arbitrary",)),
        name="attn_decode",
    )(q4, kn4, vn4, knt, vnt, ck, cv, sink_b)


def _post_decode_kernel(x_ref, ada_ref, za_ref, sgb_ref, att_ref, w_b_ref, w_o_ref, g_ref, w_up_ref,
                        fcw_ref, fcb_ref, w_down_ref, gf_ref, fst_ref,
                        w_b_bf_ref, w_o_bf_ref, w_up_bf_ref, w_down_bf_ref, y_ref, fstn_ref,
                        yb_scr, x1_scr, h_scr, up_scr, act_scr, acc_scr, *, phases):
    j = pl.program_id(0)
    (b0, nb, cb), (o0, no, co), (u0, nu, cu), (d0, nd, cd) = phases

    def mod(k):
        return ada_ref[:, k * D_MODEL:(k + 1) * D_MODEL]

    for c in range(nb):
        @pl.when(j == b0 + c)
        def _(c=c):
            w = w_b_ref[...].astype(bf16)
            w_b_bf_ref[...] = w
            yb_scr[:, c * cb:(c + 1) * cb] = _dot(att_ref[...].astype(bf16), w)

    for c in range(no):
        @pl.when(j == o0 + c)
        def _(c=c):
            w = w_o_ref[...].astype(bf16)
            w_o_bf_ref[...] = w
            mix = (za_ref[...] + sgb_ref[...] * yb_scr[...]).astype(bf16)
            cols = slice(c * co, (c + 1) * co)
            x1_scr[:, cols] = x_ref[:, 0, cols] + mod(2)[:, cols] * _dot(mix, w)

    for c in range(nu):
        @pl.when(j == u0 + c)
        def _(c=c):
            if c == 0:
                h_scr[...] = (_rms(x1_scr[...], g_ref[...]) * (1.0 + mod(4)) + mod(3)).astype(bf16)
            w = w_up_ref[...].astype(bf16)
            w_up_bf_ref[...] = w
            up_scr[:, c * cu:(c + 1) * cu] = _dot(h_scr[...], w)

    for c in range(nd):
        @pl.when(j == d0 + c)
        def _(c=c):
            if c == 0:
                up = up_scr[...]
                prev0 = fst_ref[:, 0, :]
                prev1 = fst_ref[:, 1, :]
                conv = fcw_ref[0:1, :] * prev0 + fcw_ref[1:2, :] * prev1 + fcw_ref[2:3, :] * up + fcb_ref[...]
                fstn_ref[:, 0, :] = prev1
                fstn_ref[:, 1, :] = up
                act_scr[...] = (_silu(conv[:, 0:D_FF]) * conv[:, D_FF:2 * D_FF]).astype(bf16)
            w = w_down_ref[...].astype(bf16)
            w_down_bf_ref[...] = w
            cols = slice(c * cd, (c + 1) * cd)
            acc_scr[:, cols] = x1_scr[:, cols] + mod(5)[:, cols] * _dot(act_scr[...], w)
            if c == nd - 1:
                y_ref[:, 0, :] = _rms(acc_scr[...], gf_ref[...])


def _post_decode(x, ada, za, sgb, att, w_b, w_o, g_ffn, w_up, fcw, fcb, w_down, g_final, fstate,
                 *, proj_chunk, up_chunk, down_chunk):
    n = x.shape[0]
    assert D_MODEL % proj_chunk == 0 and (2 * D_FF) % up_chunk == 0 and D_MODEL % down_chunk == 0
    assert proj_chunk % LANES == 0 and up_chunk % LANES == 0 and down_chunk % LANES == 0
    nb = no = D_MODEL // proj_chunk
    nu, nd = 2 * D_FF // up_chunk, D_MODEL // down_chunk
    b0, o0, u0, d0 = 0, nb, nb + no, nb + no + nu
    phases = ((b0, nb, proj_chunk), (o0, no, proj_chunk), (u0, nu, up_chunk), (d0, nd, down_chunk))

    def chunk_index(start, count):
        return lambda j: jnp.clip(j - start, 0, count - 1)

    ib, io, iu, idn = (chunk_index(s0, cnt) for s0, cnt, _ in phases)
    const2 = lambda j: (0, 0)
    const3 = lambda j: (0, 0, 0)
    rows = lambda cols: pl.BlockSpec((n, cols), const2, pipeline_mode=pl.Buffered(1))
    state_block = pl.BlockSpec((n, CONV_W - 1, 2 * D_FF), const3, pipeline_mode=pl.Buffered(1))
    token_block = pl.BlockSpec((n, 1, D_MODEL), const3, pipeline_mode=pl.Buffered(1))
    col_chunk = lambda k, width, idx: pl.BlockSpec((k, width), lambda j: (0, idx(j)))
    w_down_block = col_chunk(D_FF, down_chunk, idn)
    return pl.pallas_call(
        functools.partial(_post_decode_kernel, phases=phases),
        grid=(d0 + nd,),
        in_specs=[token_block, rows(N_MOD * D_MODEL), rows(D_MODEL), rows(D_MODEL), rows(N_HEADS * HEAD_DIM),
                  col_chunk(N_HEADS * HEAD_DIM, proj_chunk, ib), col_chunk(D_MODEL, proj_chunk, io),
                  pl.BlockSpec((1, D_MODEL), const2), col_chunk(D_MODEL, up_chunk, iu),
                  pl.BlockSpec((CONV_W, 2 * D_FF), const2), pl.BlockSpec((1, 2 * D_FF), const2),
                  w_down_block, pl.BlockSpec((1, D_MODEL), const2), state_block],
        out_specs=[col_chunk(N_HEADS * HEAD_DIM, proj_chunk, ib), col_chunk(D_MODEL, proj_chunk, io),
                   col_chunk(D_MODEL, up_chunk, iu), w_down_block, token_block, state_block],
        out_shape=[jax.ShapeDtypeStruct((N_HEADS * HEAD_DIM, D_MODEL), bf16),
                   jax.ShapeDtypeStruct((D_MODEL, D_MODEL), bf16),
                   jax.ShapeDtypeStruct((D_MODEL, 2 * D_FF), bf16),
                   jax.ShapeDtypeStruct((D_FF, D_MODEL), bf16),
                   jax.ShapeDtypeStruct((n, 1, D_MODEL), f32),
                   jax.ShapeDtypeStruct((n, CONV_W - 1, 2 * D_FF), f32)],
        scratch_shapes=[pltpu.VMEM((n, D_MODEL), f32), pltpu.VMEM((n, D_MODEL), f32), pltpu.VMEM((n, D_MODEL), bf16),
                        pltpu.VMEM((n, 2 * D_FF), f32), pltpu.VMEM((n, D_FF), bf16), pltpu.VMEM((n, D_MODEL), f32)],
        compiler_params=pltpu.CompilerParams(
            dimension_semantics=("arbitrary",),
            vmem_limit_bytes=VMEM_BYTES_V7X * 7 // 8),
        name="post_decode",
    )(x, ada, za, sgb, att, w_b, w_o, g_ffn, w_up, fcw, fcb, w_down, g_final, fstate)


PROMPT_MIXER_ROWS = 512
PROMPT_FFN_ROWS = 512
PROMPT_FFN_COLS = 256
PROMPT_FFN_DOWN_COLS = 256
DECODE_ATTN_BATCH = 16
DECODE_IN_CHUNK = 1664
DECODE_PROJ_CHUNK = 512
DECODE_UP_CHUNK = 512
DECODE_DOWN_CHUNK = 256


def kernel(x_prompt, x_sample, c_prompt, c_sample, state_conv_a, cache_k_win, cache_v_win, state_ffn_conv, w_ada, b_ada, g_mix, w_in, conv_a_w, attn_sinks, w_a_out, w_b_out, w_o, g_ffn, w_up, ffn_conv_w, ffn_conv_b, w_down, g_final):
    depth = w_in.shape[0]
    n_p, seq, _ = x_prompt.shape
    n_s, t_s, _ = x_sample.shape
    assert n_p == 1 and t_s == 1, "one prompt sequence and single-token decode only"
    xp = x_prompt.reshape(seq, D_MODEL)
    xs = x_sample
    gf = g_final.reshape(1, D_MODEL)
    outs = [[] for _ in range(8)]
    for l in range(depth):
        ada_s, ada_p = _ada(c_sample, c_prompt, w_ada[l], b_ada[l])
        gm, gn = g_mix[l].reshape(1, D_MODEL), g_ffn[l].reshape(1, D_MODEL)
        fcb = ffn_conv_b[l].reshape(1, 2 * D_FF)

        w_in_b, w_a_b, q, k_n, v_n, za, sgb, conv_s = _mixer_decode_pre(
            xs, ada_s, gm, w_in[l], conv_a_w[l], state_conv_a[l], w_a_out[l], chunk=DECODE_IN_CHUNK)
        to_native = lambda c: c.transpose(0, 2, 3, 1)
        from_native = lambda c: c.transpose(0, 3, 1, 2)
        sink_b = jnp.broadcast_to(attn_sinks[l].reshape(N_KV, GROUP, 1), (N_KV, GROUP, LANES))
        att4, k_s, v_s = _attn_decode(
            q.reshape(n_s, N_KV, GROUP, HEAD_DIM),
            k_n.reshape(n_s, N_KV, 1, HEAD_DIM), v_n.reshape(n_s, N_KV, 1, HEAD_DIM),
            k_n.T.reshape(N_KV, HEAD_DIM, n_s), v_n.T.reshape(N_KV, HEAD_DIM, n_s),
            to_native(cache_k_win[l]), to_native(cache_v_win[l]), sink_b, bb=DECODE_ATTN_BATCH)
        k_s, v_s = from_native(k_s), from_native(v_s)
        w_b_b, w_o_b, w_up_b, w_down_b, xs, ffn_s = _post_decode(
            xs, ada_s, za, sgb, att4.reshape(n_s, N_HEADS * HEAD_DIM), w_b_out[l], w_o[l], gn, w_up[l],
            ffn_conv_w[l], fcb, w_down[l], gf, state_ffn_conv[l],
            proj_chunk=DECODE_PROJ_CHUNK, up_chunk=DECODE_UP_CHUNK, down_chunk=DECODE_DOWN_CHUNK)

        x1, conv_p, k_p, v_p = _mixer_prompt(xp, ada_p, gm, w_in_b, conv_a_w[l], attn_sinks[l],
                                             w_a_b, w_b_b, w_o_b, tb=PROMPT_MIXER_ROWS)
        xp, ffn_p = _ffn_prompt(x1, ada_p, gn, w_up_b, ffn_conv_w[l], fcb, w_down_b, gf,
                                tb=PROMPT_FFN_ROWS, ch=PROMPT_FFN_COLS, nw=PROMPT_FFN_DOWN_COLS)

        for lst, val in zip(outs, (
                conv_p.reshape(n_p, CONV_W - 1, D_CONV), conv_s,
                k_p.reshape(n_p, WINDOW, N_KV, HEAD_DIM), k_s,
                v_p.reshape(n_p, WINDOW, N_KV, HEAD_DIM), v_s,
                ffn_p.reshape(n_p, CONV_W - 1, 2 * D_FF), ffn_s)):
            lst.append(val)
    assert depth == 1, "final RMSNorm is fused into the single layer's FFN kernels"
    return (xp.reshape(n_p, seq, D_MODEL), xs) + tuple(jnp.stack(o) for o in outs)
```

```python
import functools

import jax
import jax.numpy as jnp
from jax import lax
from jax.experimental import pallas as pl
from jax.experimental.pallas import tpu as pltpu

f32 = jnp.float32
bf16 = jnp.bfloat16

D_MODEL = 1024
D_CONV = D_MODEL
CONV_W = 3
N_HEADS = 16
N_KV = 4
GROUP = N_HEADS // N_KV
HEAD_DIM = 64
WINDOW = 128
D_FF = 2816
EPS = 1e-6
N_MOD = 6
ATTN_SCALE = HEAD_DIM ** -0.5
KV_COLS = N_KV * HEAD_DIM
C_XIN, C_B, C_C = 0, D_CONV, 2 * D_CONV
C_Q = 3 * D_CONV
C_K = C_Q + N_HEADS * HEAD_DIM
C_V = C_K + KV_COLS
C_GA = C_V + KV_COLS
C_GB = C_GA + D_MODEL
IN_COLS = C_GB + D_MODEL

LANES = 128
SUBLANES = 8
Q_SUB = 128
ATTN_LOOKAHEAD = 3
FFN_LOOKAHEAD = 2
MXU_COLS = 256
VMEM_BYTES_V7X = 64 * 1024 * 1024


def _rms(x, g):
    ms = jnp.mean(x * x, axis=-1, keepdims=True)
    return x * lax.rsqrt(ms + EPS) * g


def _silu(x):
    return x * jax.nn.sigmoid(x)


def _dot(a, b):
    return jnp.dot(a, b, preferred_element_type=f32)


def _const_spec(shape):
    nd = len(shape)
    return pl.BlockSpec(shape, lambda i: (0,) * nd, pipeline_mode=pl.Buffered(1))


def _ada_kernel(cs_ref, cp_ref, w_ref, b_ref, os_ref, op_ref):
    w = w_ref[...].astype(bf16)
    os_ref[...] = _dot(_silu(cs_ref[...]).astype(bf16), w) + b_ref[...]
    cp = jnp.broadcast_to(_silu(cp_ref[...]), (SUBLANES, D_MODEL)).astype(bf16)
    op_ref[0] = _dot(cp, w)[0:1, :] + b_ref[...]


def _ada(c_sample, c_prompt, w_ada, b_ada):
    n_s = c_sample.shape[0]
    assert c_prompt.shape[0] == 1
    return pl.pallas_call(
        _ada_kernel,
        grid=(N_MOD,),
        in_specs=[
            pl.BlockSpec((n_s, D_MODEL), lambda j: (0, 0)),
            pl.BlockSpec((1, D_MODEL), lambda j: (0, 0)),
            pl.BlockSpec((D_MODEL, D_MODEL), lambda j: (0, j)),
            pl.BlockSpec((1, D_MODEL), lambda j: (0, j)),
        ],
        out_specs=[pl.BlockSpec((n_s, D_MODEL), lambda j: (0, j)),
                   pl.BlockSpec((1, 1, D_MODEL), lambda j: (j, 0, 0))],
        out_shape=[jax.ShapeDtypeStruct((n_s, N_MOD * D_MODEL), f32),
                   jax.ShapeDtypeStruct((N_MOD, 1, D_MODEL), f32)],
        compiler_params=pltpu.CompilerParams(dimension_semantics=("arbitrary",)),
        name="ada",
    )(c_sample, c_prompt, w_ada, b_ada.reshape(1, -1))


def _mixer_prompt_kernel(sinks_ref, x_ref, ada_ref, g_ref, w_in_ref, cw_ref, w_a_ref, w_b_ref, w_o_ref,
                         x1_ref, conv_ref, knew_ref, vnew_ref,
                         ubuf, klo, khi, vt, attbuf, *, tb):
    i = pl.program_id(0)

    @pl.when(i == 0)
    def _():
        ubuf[0:SUBLANES, :] = jnp.zeros((SUBLANES, D_CONV), f32)
        for r in (klo, khi):
            r[:, 0:WINDOW, :] = jnp.zeros((N_KV, WINDOW, LANES), bf16)
        vt[:, 0:WINDOW] = jnp.zeros((KV_COLS, WINDOW), bf16)

    x = x_ref[...]
    sh1, sc1, gt1 = ada_ref[0], ada_ref[1], ada_ref[2]
    h = (_rms(x, g_ref[...]) * (1.0 + sc1) + sh1).astype(bf16)

    def proj(c0, n):
        return _dot(h, w_in_ref[:, c0:c0 + n])

    k = proj(C_K, KV_COLS)
    v = proj(C_V, KV_COLS)
    u = proj(C_C, D_CONV) * proj(C_XIN, D_CONV)
    ubuf[SUBLANES:SUBLANES + tb, :] = u
    b_gate = proj(C_B, D_CONV)
    q = (proj(C_Q, N_HEADS * HEAD_DIM) * ATTN_SCALE).astype(bf16)
    conv = (cw_ref[0:1, :] * ubuf[SUBLANES - 2:SUBLANES - 2 + tb, :]
            + cw_ref[1:2, :] * ubuf[SUBLANES - 1:SUBLANES - 1 + tb, :]
            + cw_ref[2:3, :] * u)
    conv_in = (b_gate * conv).astype(bf16)
    knew_ref[...] = k[tb - WINDOW:, :]
    vnew_ref[...] = v[tb - WINDOW:, :]

    lo = lax.broadcasted_iota(jnp.int32, (tb, LANES), 1) < HEAD_DIM
    for j in range(N_KV // 2):
        pair = k[:, LANES * j:LANES * (j + 1)]
        rolled = pltpu.roll(pair, HEAD_DIM, axis=1)
        zero = jnp.zeros_like(pair)
        klo[2 * j, WINDOW:WINDOW + tb, :] = jnp.where(lo, pair, zero).astype(bf16)
        khi[2 * j, WINDOW:WINDOW + tb, :] = jnp.where(lo, zero, rolled).astype(bf16)
        klo[2 * j + 1, WINDOW:WINDOW + tb, :] = jnp.where(lo, rolled, zero).astype(bf16)
        khi[2 * j + 1, WINDOW:WINDOW + tb, :] = jnp.where(lo, zero, pair).astype(bf16)
    vt[:, WINDOW:WINDOW + tb] = v.T.astype(bf16)

    cc = lax.broadcasted_iota(jnp.int32, (Q_SUB + WINDOW, 2 * Q_SUB), 0)
    col = lax.broadcasted_iota(jnp.int32, (Q_SUB + WINDOW, 2 * Q_SUB), 1)
    rr = col & (Q_SUB - 1)
    band = (cc >= rr) & (cc <= rr + WINDOW)
    first_head = lax.broadcasted_iota(jnp.int32, (1, 2 * Q_SUB), 1) < Q_SUB
    mask0 = band & (cc >= jnp.where(i == 0, WINDOW, 0))

    def scores(j, g, t):
        r0 = Q_SUB * j
        qs = jnp.concatenate([q[r0:r0 + Q_SUB, 2 * LANES * g:2 * LANES * g + LANES],
                              q[r0:r0 + Q_SUB, 2 * LANES * g + LANES:2 * LANES * (g + 1)]], axis=0)
        kr = (klo, khi)[t]
        return lax.dot_general(kr[g, r0:r0 + Q_SUB + WINDOW, :], qs, (((1,), (1,)), ((), ())),
                               preferred_element_type=f32)

    def finish(j, g, t, st):
        r0 = Q_SUB * j
        h0, h1 = GROUP * g + t, GROUP * g + 2 + t
        sink = jnp.where(first_head, sinks_ref[h0], sinks_ref[h1])
        st = jnp.where(mask0 if j == 0 else band, st, -jnp.inf)
        m = jnp.maximum(jnp.max(st, axis=0, keepdims=True), sink)
        e = jnp.exp(st - m)
        linv = 1.0 / (jnp.sum(e, axis=0, keepdims=True) + jnp.exp(sink - m))
        vtg = vt[HEAD_DIM * g:HEAD_DIM * (g + 1), r0:r0 + Q_SUB + WINDOW]
        ot = _dot(vtg, e.astype(bf16)) * linv
        attbuf[HEAD_DIM * h0:HEAD_DIM * (h0 + 1), r0:r0 + Q_SUB] = ot[:, 0:Q_SUB]
        attbuf[HEAD_DIM * h1:HEAD_DIM * (h1 + 1), r0:r0 + Q_SUB] = ot[:, Q_SUB:2 * Q_SUB]

    chains = [(j, g, t) for j in range(tb // Q_SUB) for g in range(N_KV) for t in range(2)]
    fillers = ([functools.partial(proj, c, MXU_COLS) for c in range(C_GA, C_GB + D_MODEL, MXU_COLS)]
               + [(lambda c=c: _dot(conv_in, w_a_ref[:, c:c + MXU_COLS])) for c in range(0, D_MODEL, MXU_COLS)])
    fill_every = -(-len(chains) // len(fillers))
    filled = []
    pending = [scores(*c) for c in chains[:ATTN_LOOKAHEAD]]
    for n, chain in enumerate(chains):
        if n + ATTN_LOOKAHEAD < len(chains):
            pending.append(scores(*chains[n + ATTN_LOOKAHEAD]))
        if n % fill_every == 0:
            filled.extend(f() for f in fillers[len(filled):len(filled) + 1])
        finish(*chain, pending.pop(0))
    filled.extend(f() for f in fillers[len(filled):])
    ga, gb, ya = (jnp.concatenate(filled[j:j + D_MODEL // MXU_COLS], axis=1)
                  for j in range(0, len(filled), D_MODEL // MXU_COLS))

    yb = _dot(attbuf[...].T.astype(bf16), w_b_ref[...])
    mix = (jax.nn.sigmoid(ga) * ya + jax.nn.sigmoid(gb) * yb).astype(bf16)
    for r0 in range(0, tb, tb // 2):
        x1_ref[r0:r0 + tb // 2, :] = x[r0:r0 + tb // 2, :] + gt1 * _dot(mix[r0:r0 + tb // 2, :], w_o_ref[...])

    conv_ref[...] = ubuf[SUBLANES + tb - (CONV_W - 1):SUBLANES + tb, :]
    ubuf[0:SUBLANES, :] = ubuf[tb:tb + SUBLANES, :]
    for r in (klo, khi):
        r[:, 0:WINDOW, :] = r[:, tb:tb + WINDOW, :]
    vt[:, 0:WINDOW] = vt[:, tb:tb + WINDOW]


def _mixer_prompt(x, ada, g_mix, w_in, conv_w, sinks, w_a, w_b, w_o, *, tb):
    s = x.shape[0]
    assert s % tb == 0 and tb % Q_SUB == 0 and tb >= WINDOW
    kv_scratch = pltpu.VMEM((N_KV, WINDOW + tb, LANES), bf16)
    grid_spec = pltpu.PrefetchScalarGridSpec(
        num_scalar_prefetch=1,
        grid=(s // tb,),
        in_specs=[
            pl.BlockSpec((tb, D_MODEL), lambda i, sk: (i, 0)),
            pl.BlockSpec((N_MOD, 1, D_MODEL), lambda i, sk: (0, 0, 0)),
            pl.BlockSpec((1, D_MODEL), lambda i, sk: (0, 0)),
            pl.BlockSpec((D_MODEL, IN_COLS), lambda i, sk: (0, 0), pipeline_mode=pl.Buffered(1)),
            pl.BlockSpec((CONV_W, D_CONV), lambda i, sk: (0, 0)),
            pl.BlockSpec((D_CONV, D_MODEL), lambda i, sk: (0, 0), pipeline_mode=pl.Buffered(1)),
            pl.BlockSpec((N_HEADS * HEAD_DIM, D_MODEL), lambda i, sk: (0, 0), pipeline_mode=pl.Buffered(1)),
            pl.BlockSpec((D_MODEL, D_MODEL), lambda i, sk: (0, 0), pipeline_mode=pl.Buffered(1)),
        ],
        out_specs=[
            pl.BlockSpec((tb, D_MODEL), lambda i, sk: (i, 0)),
            pl.BlockSpec((CONV_W - 1, D_CONV), lambda i, sk: (0, 0)),
            pl.BlockSpec((WINDOW, KV_COLS), lambda i, sk: (0, 0)),
            pl.BlockSpec((WINDOW, KV_COLS), lambda i, sk: (0, 0)),
        ],
        scratch_shapes=[
            pltpu.VMEM((SUBLANES + tb, D_CONV), f32),
            kv_scratch, kv_scratch,
            pltpu.VMEM((KV_COLS, WINDOW + tb), bf16),
            pltpu.VMEM((N_HEADS * HEAD_DIM, tb), f32),
        ],
    )
    return pl.pallas_call(
        functools.partial(_mixer_prompt_kernel, tb=tb),
        grid_spec=grid_spec,
        out_shape=[
            jax.ShapeDtypeStruct((s, D_MODEL), f32),
            jax.ShapeDtypeStruct((CONV_W - 1, D_CONV), f32),
            jax.ShapeDtypeStruct((WINDOW, KV_COLS), f32),
            jax.ShapeDtypeStruct((WINDOW, KV_COLS), f32),
        ],
        compiler_params=pltpu.CompilerParams(
            dimension_semantics=("arbitrary",),
            vmem_limit_bytes=VMEM_BYTES_V7X * 7 // 8),
        name="mixer_prompt",
    )(sinks, x, ada, g_mix, w_in, conv_w, w_a, w_b, w_o)


def _ffn_prompt_kernel(x_ref, ada_ref, g_ref, w_up_ref, fcw_ref, fcb_ref, w_down_ref, gf_ref,
                       y_ref, fst_ref, upbuf, actbuf, *, tb, ch, nw):
    i = pl.program_id(0)

    @pl.when(i == 0)
    def _():
        upbuf[:, 0:SUBLANES, :] = jnp.zeros((2 * D_FF // LANES, SUBLANES, LANES), f32)

    hb = tb // 2
    sh2, sc2, gt2 = ada_ref[3], ada_ref[4], ada_ref[5]
    h = (_rms(x_ref[...], g_ref[...]) * (1.0 + sc2) + sh2).astype(bf16)

    def up_cols(half, c0):
        up = _dot(h[half * hb:(half + 1) * hb, :], w_up_ref[:, c0:c0 + ch])
        for s in range(ch // LANES):
            upbuf[c0 // LANES + s, SUBLANES:SUBLANES + hb, :] = up[:, s * LANES:(s + 1) * LANES]
        return up

    def conv_cols(c0, up):
        pieces = []
        for s in range(ch // LANES):
            slab = c0 // LANES + s
            cols = slice(c0 + s * LANES, c0 + (s + 1) * LANES)
            pieces.append(fcw_ref[0:1, cols] * upbuf[slab, SUBLANES - 2:SUBLANES - 2 + hb, :]
                          + fcw_ref[1:2, cols] * upbuf[slab, SUBLANES - 1:SUBLANES - 1 + hb, :]
                          + fcw_ref[2:3, cols] * up[:, s * LANES:(s + 1) * LANES]
                          + fcb_ref[0:1, cols])
            upbuf[slab, 0:SUBLANES, :] = upbuf[slab, hb:hb + SUBLANES, :]
        return jnp.concatenate(pieces, axis=1)

    def down_cols(half, n0):
        return _dot(actbuf[half], w_down_ref[:, n0:n0 + nw])

    def finish(half, parts):
        rows = slice(half * hb, (half + 1) * hb)
        x2 = x_ref[rows, :] + gt2 * jnp.concatenate(parts, axis=1)
        y_ref[rows, :] = _rms(x2, gf_ref[...])

    chunks = list(range(0, D_FF, ch))
    down_starts = list(range(0, D_MODEL, nw))
    down_at = {len(chunks) * (k + 1) // (len(down_starts) + 1): n0 for k, n0 in enumerate(down_starts)}
    assert len(down_at) == len(down_starts)
    for half in range(2):
        parts = []
        pending = [(up_cols(half, c0), up_cols(half, D_FF + c0)) for c0 in chunks[:FFN_LOOKAHEAD]]
        for n, c0 in enumerate(chunks):
            if n + FFN_LOOKAHEAD < len(chunks):
                c1 = chunks[n + FFN_LOOKAHEAD]
                pending.append((up_cols(half, c1), up_cols(half, D_FF + c1)))
            if half == 1 and n in down_at:
                parts.append(down_cols(0, down_at[n]))
            up_g, up_v = pending.pop(0)
            actbuf[half, :, c0:c0 + ch] = (_silu(conv_cols(c0, up_g))
                                           * conv_cols(D_FF + c0, up_v)).astype(bf16)
        if half == 1:
            finish(0, parts)
    finish(1, [down_cols(1, n0) for n0 in down_starts])

    for slab in range(2 * D_FF // LANES):
        fst_ref[:, slab * LANES:(slab + 1) * LANES] = upbuf[slab, SUBLANES - (CONV_W - 1):SUBLANES, :]


def _ffn_prompt(x1, ada, g_ffn, w_up, fcw, fcb, w_down, g_final, *, tb, ch, nw):
    s = x1.shape[0]
    assert s % tb == 0 and tb % (4 * SUBLANES) == 0
    assert D_FF % ch == 0 and ch % LANES == 0 and D_MODEL % nw == 0 and nw % LANES == 0
    return pl.pallas_call(
        functools.partial(_ffn_prompt_kernel, tb=tb, ch=ch, nw=nw),
        grid=(s // tb,),
        in_specs=[
            pl.BlockSpec((tb, D_MODEL), lambda i: (i, 0)),
            pl.BlockSpec((N_MOD, 1, D_MODEL), lambda i: (0, 0, 0)),
            pl.BlockSpec((1, D_MODEL), lambda i: (0, 0)),
            _const_spec((D_MODEL, 2 * D_FF)),
            pl.BlockSpec((CONV_W, 2 * D_FF), lambda i: (0, 0)),
            pl.BlockSpec((1, 2 * D_FF), lambda i: (0, 0)),
            _const_spec((D_FF, D_MODEL)),
            pl.BlockSpec((1, D_MODEL), lambda i: (0, 0)),
        ],
        out_specs=[
            pl.BlockSpec((tb, D_MODEL), lambda i: (i, 0)),
            pl.BlockSpec((CONV_W - 1, 2 * D_FF), lambda i: (0, 0)),
        ],
        out_shape=[
            jax.ShapeDtypeStruct((s, D_MODEL), f32),
            jax.ShapeDtypeStruct((CONV_W - 1, 2 * D_FF), f32),
        ],
        scratch_shapes=[
            pltpu.VMEM((2 * D_FF // LANES, SUBLANES + tb // 2, LANES), f32),
            pltpu.VMEM((2, tb // 2, D_FF), bf16),
        ],
        compiler_params=pltpu.CompilerParams(
            dimension_semantics=("arbitrary",),
            vmem_limit_bytes=VMEM_BYTES_V7X * 7 // 8),
        name="ffn_prompt",
    )(x1, ada, g_ffn, w_up, fcw, fcb, w_down, g_final)


def _mixer_decode_pre_kernel(x_ref, ada_ref, g_ref, w_in_ref, cw_ref, st_ref, w_a_ref,
                             w_in_bf_ref, w_a_bf_ref, q_ref, k_ref, v_ref, za_ref, sgb_ref, stn_ref,
                             h_scr, proj_scr, *, chunk):
    j = pl.program_id(0)
    n_chunks = IN_COLS // chunk

    @pl.when(j == 0)
    def _():
        sh1 = ada_ref[:, 0:D_MODEL]
        sc1 = ada_ref[:, D_MODEL:2 * D_MODEL]
        h_scr[...] = (_rms(x_ref[:, 0, :], g_ref[...]) * (1.0 + sc1) + sh1).astype(bf16)

    w_chunk = w_in_ref[...].astype(bf16)
    w_in_bf_ref[...] = w_chunk
    part = _dot(h_scr[...], w_chunk)
    for c in range(n_chunks):
        @pl.when(j == c)
        def _(c=c):
            proj_scr[:, c * chunk:(c + 1) * chunk] = part

    @pl.when(j == n_chunks - 1)
    def _():
        def proj(c0, n):
            return proj_scr[:, c0:c0 + n]

        w_a = w_a_ref[...].astype(bf16)
        w_a_bf_ref[...] = w_a
        u = proj(C_C, D_CONV) * proj(C_XIN, D_CONV)
        prev0 = st_ref[:, 0, :]
        prev1 = st_ref[:, 1, :]
        conv = cw_ref[0:1, :] * prev0 + cw_ref[1:2, :] * prev1 + cw_ref[2:3, :] * u
        stn_ref[:, 0, :] = prev1
        stn_ref[:, 1, :] = u
        ya = _dot((proj(C_B, D_CONV) * conv).astype(bf16), w_a)
        q_ref[...] = proj(C_Q, N_HEADS * HEAD_DIM) * ATTN_SCALE
        k_ref[...] = proj(C_K, KV_COLS)
        v_ref[...] = proj(C_V, KV_COLS)
        za_ref[...] = jax.nn.sigmoid(proj(C_GA, D_MODEL)) * ya
        sgb_ref[...] = jax.nn.sigmoid(proj(C_GB, D_MODEL))


def _mixer_decode_pre(x, ada, g_mix, w_in, conv_w, state, w_a, *, chunk):
    n = x.shape[0]
    assert IN_COLS % chunk == 0 and chunk % LANES == 0
    const2 = lambda j: (0, 0)
    const3 = lambda j: (0, 0, 0)
    row_block = lambda cols: pl.BlockSpec((n, cols), const2)
    state_block = pl.BlockSpec((n, CONV_W - 1, D_CONV), const3)
    shapes = [((D_MODEL, IN_COLS), bf16), ((D_CONV, D_MODEL), bf16),
              ((n, N_HEADS * HEAD_DIM), f32), ((n, KV_COLS), f32), ((n, KV_COLS), f32),
              ((n, D_MODEL), f32), ((n, D_MODEL), f32), ((n, CONV_W - 1, D_CONV), f32)]
    return pl.pallas_call(
        functools.partial(_mixer_decode_pre_kernel, chunk=chunk),
        grid=(IN_COLS // chunk,),
        in_specs=[pl.BlockSpec((n, 1, D_MODEL), const3), row_block(2 * D_MODEL), pl.BlockSpec((1, D_MODEL), const2),
                  pl.BlockSpec((D_MODEL, chunk), lambda j: (0, j)),
                  pl.BlockSpec((CONV_W, D_CONV), const2), state_block,
                  pl.BlockSpec((D_CONV, D_MODEL), const2)],
        out_specs=[pl.BlockSpec((D_MODEL, chunk), lambda j: (0, j)), pl.BlockSpec((D_CONV, D_MODEL), const2),
                   row_block(N_HEADS * HEAD_DIM), row_block(KV_COLS), row_block(KV_COLS),
                   row_block(D_MODEL), row_block(D_MODEL), state_block],
        out_shape=[jax.ShapeDtypeStruct(shp, dt) for shp, dt in shapes],
        scratch_shapes=[pltpu.VMEM((n, D_MODEL), bf16), pltpu.VMEM((n, IN_COLS), f32)],
        compiler_params=pltpu.CompilerParams(
            dimension_semantics=("arbitrary",),
            vmem_limit_bytes=VMEM_BYTES_V7X * 5 // 8),
        name="mixer_decode_pre",
    )(x, ada, g_mix, w_in, conv_w, state, w_a)


def _attn_decode_kernel(q_ref, kn_ref, vn_ref, knt_ref, vnt_ref, ck_ref, cv_ref, sink_ref,
                        att_ref, ok_ref, ov_ref, *, bb):
    step = pl.program_id(0)
    last = lax.broadcasted_iota(jnp.int32, (HEAD_DIM, WINDOW), 1) == WINDOW - 1
    to_front = (LANES - step * bb) % LANES
    kstep = [pltpu.roll(knt_ref[g], to_front, axis=1) for g in range(N_KV)]
    vstep = [pltpu.roll(vnt_ref[g], to_front, axis=1) for g in range(N_KV)]

    pairs = [(b, g) for b in range(bb) for g in range(N_KV)]
    n_stage = 7
    cuts = [len(pairs) * s // n_stage for s in range(n_stage + 1)]

    def shift_caches(stage):
        for b, g in pairs[cuts[stage]:cuts[stage + 1]]:
            ok_ref[b, g] = jnp.where(last, pltpu.roll(kstep[g], WINDOW - 1 - b, axis=1),
                                     pltpu.roll(ck_ref[b, g], WINDOW - 1, axis=1))
            ov_ref[b, g] = jnp.where(last, pltpu.roll(vstep[g], WINDOW - 1 - b, axis=1),
                                     pltpu.roll(cv_ref[b, g], WINDOW - 1, axis=1))

    sinks = [sink_ref[g][:, 0:1] for g in range(N_KV)]
    shift_caches(0)
    s_old = [_dot(q_ref[b, g].astype(bf16), ck_ref[b, g].astype(bf16)) for b, g in pairs]
    shift_caches(1)
    s_new = [jnp.sum(q_ref[b, g] * kn_ref[b, g], axis=-1, keepdims=True) for b, g in pairs]
    shift_caches(2)
    m = [jnp.maximum(jnp.maximum(jnp.max(so, axis=-1, keepdims=True), sn), sinks[g])
         for (b, g), so, sn in zip(pairs, s_old, s_new)]
    shift_caches(3)
    e_old = [jnp.exp(so - mm) for so, mm in zip(s_old, m)]
    e_new = [jnp.exp(sn - mm) for sn, mm in zip(s_new, m)]
    den = [jnp.sum(eo, axis=-1, keepdims=True) + en + jnp.exp(sinks[g] - mm)
           for (b, g), eo, en, mm in zip(pairs, e_old, e_new, m)]
    shift_caches(4)
    o_old = [lax.dot_general(eo.astype(bf16), cv_ref[b, g].astype(bf16), (((1,), (1,)), ((), ())),
                             preferred_element_type=f32) for (b, g), eo in zip(pairs, e_old)]
    shift_caches(5)
    for (b, g), oo, en, dd in zip(pairs, o_old, e_new, den):
        att_ref[b, g] = (oo + en * vn_ref[b, g]) / dd
    shift_caches(6)


def _attn_decode(q4, kn4, vn4, knt, vnt, ck, cv, sink_b, *, bb):
    n = q4.shape[0]
    assert n % bb == 0 and n == LANES and WINDOW == LANES
    cache_spec = pl.BlockSpec((bb, N_KV, HEAD_DIM, WINDOW), lambda b: (b, 0, 0, 0))
    row_spec = pl.BlockSpec((bb, N_KV, 1, HEAD_DIM), lambda b: (b, 0, 0, 0))
    q_spec = pl.BlockSpec((bb, N_KV, GROUP, HEAD_DIM), lambda b: (b, 0, 0, 0))
    new_t_spec = pl.BlockSpec((N_KV, HEAD_DIM, n), lambda b: (0, 0, 0))
    return pl.pallas_call(
        functools.partial(_attn_decode_kernel, bb=bb),
        grid=(n // bb,),
        in_specs=[q_spec, row_spec, row_spec, new_t_spec, new_t_spec, cache_spec, cache_spec,
                  pl.BlockSpec((N_KV, GROUP, LANES), lambda b: (0, 0, 0))],
        out_specs=[q_spec, cache_spec, cache_spec],
        out_shape=[jax.ShapeDtypeStruct((n, N_KV, GROUP, HEAD_DIM), f32),
                   jax.ShapeDtypeStruct((n, N_KV, HEAD_DIM, WINDOW), f32),
                   jax.ShapeDtypeStruct((n, N_KV, HEAD_DIM, WINDOW), f32)],
        compiler_params=pltpu.CompilerParams(dimension_semantics=("arbitrary",)),
        name="attn_decode",
    )(q4, kn4, vn4, knt, vnt, ck, cv, sink_b)


def _post_decode_kernel(x_ref, ada_ref, za_ref, sgb_ref, att_ref, w_b_ref, w_o_ref, g_ref, w_up_ref,
                        fcw_ref, fcb_ref, w_down_ref, gf_ref, fst_ref,
                        w_b_bf_ref, w_o_bf_ref, w_up_bf_ref, w_down_bf_ref, y_ref, fstn_ref,
                        yb_scr, x1_scr, h_scr, up_scr, act_scr, acc_scr, *, phases):
    j = pl.program_id(0)
    (b0, nb, cb), (o0, no, co), (u0, nu, cu), (d0, nd, cd) = phases

    def mod(k):
        return ada_ref[:, k * D_MODEL:(k + 1) * D_MODEL]

    for c in range(nb):
        @pl.when(j == b0 + c)
        def _(c=c):
            w = w_b_ref[...].astype(bf16)
            w_b_bf_ref[...] = w
            yb_scr[:, c * cb:(c + 1) * cb] = _dot(att_ref[...].astype(bf16), w)

    for c in range(no):
        @pl.when(j == o0 + c)
        def _(c=c):
            w = w_o_ref[...].astype(bf16)
            w_o_bf_ref[...] = w
            mix = (za_ref[...] + sgb_ref[...] * yb_scr[...]).astype(bf16)
            cols = slice(c * co, (c + 1) * co)
            x1_scr[:, cols] = x_ref[:, 0, cols] + mod(2)[:, cols] * _dot(mix, w)

    for c in range(nu):
        @pl.when(j == u0 + c)
        def _(c=c):
            if c == 0:
                h_scr[...] = (_rms(x1_scr[...], g_ref[...]) * (1.0 + mod(4)) + mod(3)).astype(bf16)
            w = w_up_ref[...].astype(bf16)
            w_up_bf_ref[...] = w
            up_scr[:, c * cu:(c + 1) * cu] = _dot(h_scr[...], w)

    for c in range(nd):
        @pl.when(j == d0 + c)
        def _(c=c):
            if c == 0:
                up = up_scr[...]
                prev0 = fst_ref[:, 0, :]
                prev1 = fst_ref[:, 1, :]
                conv = fcw_ref[0:1, :] * prev0 + fcw_ref[1:2, :] * prev1 + fcw_ref[2:3, :] * up + fcb_ref[...]
                fstn_ref[:, 0, :] = prev1
                fstn_ref[:, 1, :] = up
                act_scr[...] = (_silu(conv[:, 0:D_FF]) * conv[:, D_FF:2 * D_FF]).astype(bf16)
            w = w_down_ref[...].astype(bf16)
            w_down_bf_ref[...] = w
            cols = slice(c * cd, (c + 1) * cd)
            acc_scr[:, cols] = x1_scr[:, cols] + mod(5)[:, cols] * _dot(act_scr[...], w)
            if c == nd - 1:
                y_ref[:, 0, :] = _rms(acc_scr[...], gf_ref[...])


def _post_decode(x, ada, za, sgb, att, w_b, w_o, g_ffn, w_up, fcw, fcb, w_down, g_final, fstate,
                 *, proj_chunk, up_chunk, down_chunk):
    n = x.shape[0]
    assert D_MODEL % proj_chunk == 0 and (2 * D_FF) % up_chunk == 0 and D_MODEL % down_chunk == 0
    assert proj_chunk % LANES == 0 and up_chunk % LANES == 0 and down_chunk % LANES == 0
    nb = no = D_MODEL // proj_chunk
    nu, nd = 2 * D_FF // up_chunk, D_MODEL // down_chunk
    b0, o0, u0, d0 = 0, nb, nb + no, nb + no + nu
    phases = ((b0, nb, proj_chunk), (o0, no, proj_chunk), (u0, nu, up_chunk), (d0, nd, down_chunk))

    def chunk_index(start, count):
        return lambda j: jnp.clip(j - start, 0, count - 1)

    ib, io, iu, idn = (chunk_index(s0, cnt) for s0, cnt, _ in phases)
    const2 = lambda j: (0, 0)
    const3 = lambda j: (0, 0, 0)
    rows = lambda cols: pl.BlockSpec((n, cols), const2, pipeline_mode=pl.Buffered(1))
    state_block = pl.BlockSpec((n, CONV_W - 1, 2 * D_FF), const3, pipeline_mode=pl.Buffered(1))
    token_block = pl.BlockSpec((n, 1, D_MODEL), const3, pipeline_mode=pl.Buffered(1))
    col_chunk = lambda k, width, idx: pl.BlockSpec((k, width), lambda j: (0, idx(j)))
    w_down_block = col_chunk(D_FF, down_chunk, idn)
    return pl.pallas_call(
        functools.partial(_post_decode_kernel, phases=phases),
        grid=(d0 + nd,),
        in_specs=[token_block, rows(N_MOD * D_MODEL), rows(D_MODEL), rows(D_MODEL), rows(N_HEADS * HEAD_DIM),
                  col_chunk(N_HEADS * HEAD_DIM, proj_chunk, ib), col_chunk(D_MODEL, proj_chunk, io),
                  pl.BlockSpec((1, D_MODEL), const2), col_chunk(D_MODEL, up_chunk, iu),
                  pl.BlockSpec((CONV_W, 2 * D_FF), const2), pl.BlockSpec((1, 2 * D_FF), const2),
                  w_down_block, pl.BlockSpec((1, D_MODEL), const2), state_block],
        out_specs=[col_chunk(N_HEADS * HEAD_DIM, proj_chunk, ib), col_chunk(D_MODEL, proj_chunk, io),
                   col_chunk(D_MODEL, up_chunk, iu), w_down_block, token_block, state_block],
        out_shape=[jax.ShapeDtypeStruct((N_HEADS * HEAD_DIM, D_MODEL), bf16),
                   jax.ShapeDtypeStruct((D_MODEL, D_MODEL), bf16),
                   jax.ShapeDtypeStruct((D_MODEL, 2 * D_FF), bf16),
                   jax.ShapeDtypeStruct((D_FF, D_MODEL), bf16),
                   jax.ShapeDtypeStruct((n, 1, D_MODEL), f32),
                   jax.ShapeDtypeStruct((n, CONV_W - 1, 2 * D_FF), f32)],
        scratch_shapes=[pltpu.VMEM((n, D_MODEL), f32), pltpu.VMEM((n, D_MODEL), f32), pltpu.VMEM((n, D_MODEL), bf16),
                        pltpu.VMEM((n, 2 * D_FF), f32), pltpu.VMEM((n, D_FF), bf16), pltpu.VMEM((n, D_MODEL), f32)],
        compiler_params=pltpu.CompilerParams(
            dimension_semantics=("arbitrary",),
            vmem_limit_bytes=VMEM_BYTES_V7X * 7 // 8),
        name="post_decode",
    )(x, ada, za, sgb, att, w_b, w_o, g_ffn, w_up, fcw, fcb, w_down, g_final, fstate)


PROMPT_MIXER_ROWS = 512
PROMPT_FFN_ROWS = 512
PROMPT_FFN_COLS = 256
PROMPT_FFN_DOWN_COLS = 512
DECODE_ATTN_BATCH = 16
DECODE_IN_CHUNK = 1664
DECODE_PROJ_CHUNK = 512
DECODE_UP_CHUNK = 512
DECODE_DOWN_CHUNK = 256


def kernel(x_prompt, x_sample, c_prompt, c_sample, state_conv_a, cache_k_win, cache_v_win, state_ffn_conv, w_ada, b_ada, g_mix, w_in, conv_a_w, attn_sinks, w_a_out, w_b_out, w_o, g_ffn, w_up, ffn_conv_w, ffn_conv_b, w_down, g_final):
    depth = w_in.shape[0]
    n_p, seq, _ = x_prompt.shape
    n_s, t_s, _ = x_sample.shape
    assert n_p == 1 and t_s == 1, "one prompt sequence and single-token decode only"
    xp = x_prompt.reshape(seq, D_MODEL)
    xs = x_sample
    gf = g_final.reshape(1, D_MODEL)
    outs = [[] for _ in range(8)]
    for l in range(depth):
        ada_s, ada_p = _ada(c_sample, c_prompt, w_ada[l], b_ada[l])
        gm, gn = g_mix[l].reshape(1, D_MODEL), g_ffn[l].reshape(1, D_MODEL)
        fcb = ffn_conv_b[l].reshape(1, 2 * D_FF)

        w_in_b, w_a_b, q, k_n, v_n, za, sgb, conv_s = _mixer_decode_pre(
            xs, ada_s, gm, w_in[l], conv_a_w[l], state_conv_a[l], w_a_out[l], chunk=DECODE_IN_CHUNK)
        to_native = lambda c: c.transpose(0, 2, 3, 1)
        from_native = lambda c: c.transpose(0, 3, 1, 2)
        sink_b = jnp.broadcast_to(attn_sinks[l].reshape(N_KV, GROUP, 1), (N_KV, GROUP, LANES))
        att4, k_s, v_s = _attn_decode(
            q.reshape(n_s, N_KV, GROUP, HEAD_DIM),
            k_n.reshape(n_s, N_KV, 1, HEAD_DIM), v_n.reshape(n_s, N_KV, 1, HEAD_DIM),
            k_n.T.reshape(N_KV, HEAD_DIM, n_s), v_n.T.reshape(N_KV, HEAD_DIM, n_s),
            to_native(cache_k_win[l]), to_native(cache_v_win[l]), sink_b, bb=DECODE_ATTN_BATCH)
        k_s, v_s = from_native(k_s), from_native(v_s)
        w_b_b, w_o_b, w_up_b, w_down_b, xs, ffn_s = _post_decode(
            xs, ada_s, za, sgb, att4.reshape(n_s, N_HEADS * HEAD_DIM), w_b_out[l], w_o[l], gn, w_up[l],
            ffn_conv_w[l], fcb, w_down[l], gf, state_ffn_conv[l],
            proj_chunk=DECODE_PROJ_CHUNK, up_chunk=DECODE_UP_CHUNK, down_chunk=DECODE_DOWN_CHUNK)

        x1, conv_p, k_p, v_p = _mixer_prompt(xp, ada_p, gm, w_in_b, conv_a_w[l], attn_sinks[l],
                                             w_a_b, w_b_b, w_o_b, tb=PROMPT_MIXER_ROWS)
        xp, ffn_p = _ffn_prompt(x1, ada_p, gn, w_up_b, ffn_conv_w[l], fcb, w_down_b, gf,
                                tb=PROMPT_FFN_ROWS, ch=PROMPT_FFN_COLS, nw=PROMPT_FFN_DOWN_COLS)

        for lst, val in zip(outs, (
                conv_p.reshape(n_p, CONV_W - 1, D_CONV), conv_s,
                k_p.reshape(n_p, WINDOW, N_KV, HEAD_DIM), k_s,
                v_p.reshape(n_p, WINDOW, N_KV, HEAD_DIM), v_s,
                ffn_p.reshape(n_p, CONV_W - 1, 2 * D_FF), ffn_s)):
            lst.append(val)
    assert depth == 1, "final RMSNorm is fused into the single layer's FFN kernels"
    return (xp.reshape(n_p, seq, D_MODEL), xs) + tuple(jnp.stack(o) for o in outs)
```

```python
import functools

import jax
import jax.numpy as jnp
from jax import lax
from jax.experimental import pallas as pl
from jax.experimental.pallas import tpu as pltpu

f32 = jnp.float32
bf16 = jnp.bfloat16

D_MODEL = 1024
D_CONV = D_MODEL
CONV_W = 3
N_HEADS = 16
N_KV = 4
GROUP = N_HEADS // N_KV
HEAD_DIM = 64
WINDOW = 128
D_FF = 2816
EPS = 1e-6
N_MOD = 6
ATTN_SCALE = HEAD_DIM ** -0.5
KV_COLS = N_KV * HEAD_DIM
C_XIN, C_B, C_C = 0, D_CONV, 2 * D_CONV
C_Q = 3 * D_CONV
C_K = C_Q + N_HEADS * HEAD_DIM
C_V = C_K + KV_COLS
C_GA = C_V + KV_COLS
C_GB = C_GA + D_MODEL
IN_COLS = C_GB + D_MODEL

LANES = 128
SUBLANES = 8
Q_SUB = 128
ATTN_LOOKAHEAD = 3
FFN_LOOKAHEAD = 2
MXU_COLS = 256
VMEM_BYTES_V7X = 64 * 1024 * 1024


def _rms(x, g):
    ms = jnp.mean(x * x, axis=-1, keepdims=True)
    return x * lax.rsqrt(ms + EPS) * g


def _silu(x):
    return x * jax.nn.sigmoid(x)


def _dot(a, b):
    return jnp.dot(a, b, preferred_element_type=f32)


def _const_spec(shape):
    nd = len(shape)
    return pl.BlockSpec(shape, lambda i: (0,) * nd, pipeline_mode=pl.Buffered(1))


def _ada_kernel(cs_ref, cp_ref, w_ref, b_ref, os_ref, op_ref):
    w = w_ref[...].astype(bf16)
    os_ref[...] = _dot(_silu(cs_ref[...]).astype(bf16), w) + b_ref[...]
    cp = jnp.broadcast_to(_silu(cp_ref[...]), (SUBLANES, D_MODEL)).astype(bf16)
    op_ref[0] = _dot(cp, w)[0:1, :] + b_ref[...]


def _ada(c_sample, c_prompt, w_ada, b_ada):
    n_s = c_sample.shape[0]
    assert c_prompt.shape[0] == 1
    return pl.pallas_call(
        _ada_kernel,
        grid=(N_MOD,),
        in_specs=[
            pl.BlockSpec((n_s, D_MODEL), lambda j: (0, 0)),
            pl.BlockSpec((1, D_MODEL), lambda j: (0, 0)),
            pl.BlockSpec((D_MODEL, D_MODEL), lambda j: (0, j)),
            pl.BlockSpec((1, D_MODEL), lambda j: (0, j)),
        ],
        out_specs=[pl.BlockSpec((n_s, D_MODEL), lambda j: (0, j)),
                   pl.BlockSpec((1, 1, D_MODEL), lambda j: (j, 0, 0))],
        out_shape=[jax.ShapeDtypeStruct((n_s, N_MOD * D_MODEL), f32),
                   jax.ShapeDtypeStruct((N_MOD, 1, D_MODEL), f32)],
        compiler_params=pltpu.CompilerParams(dimension_semantics=("arbitrary",)),
        name="ada",
    )(c_sample, c_prompt, w_ada, b_ada.reshape(1, -1))


def _mixer_prompt_kernel(sinks_ref, x_ref, ada_ref, g_ref, w_in_ref, cw_ref, w_a_ref, w_b_ref, w_o_ref,
                         x1_ref, conv_ref, knew_ref, vnew_ref,
                         ubuf, klo, khi, vt, attbuf, *, tb):
    i = pl.program_id(0)

    @pl.when(i == 0)
    def _():
        ubuf[0:SUBLANES, :] = jnp.zeros((SUBLANES, D_CONV), f32)
        for r in (klo, khi):
            r[:, 0:WINDOW, :] = jnp.zeros((N_KV, WINDOW, LANES), bf16)
        vt[:, 0:WINDOW] = jnp.zeros((KV_COLS, WINDOW), bf16)

    x = x_ref[...]
    sh1, sc1, gt1 = ada_ref[0], ada_ref[1], ada_ref[2]
    h = (_rms(x, g_ref[...]) * (1.0 + sc1) + sh1).astype(bf16)

    def proj(c0, n):
        return _dot(h, w_in_ref[:, c0:c0 + n])

    k = proj(C_K, KV_COLS)
    v = proj(C_V, KV_COLS)
    u = proj(C_C, D_CONV) * proj(C_XIN, D_CONV)
    ubuf[SUBLANES:SUBLANES + tb, :] = u
    b_gate = proj(C_B, D_CONV)
    q = (proj(C_Q, N_HEADS * HEAD_DIM) * ATTN_SCALE).astype(bf16)
    conv = (cw_ref[0:1, :] * ubuf[SUBLANES - 2:SUBLANES - 2 + tb, :]
            + cw_ref[1:2, :] * ubuf[SUBLANES - 1:SUBLANES - 1 + tb, :]
            + cw_ref[2:3, :] * u)
    conv_in = (b_gate * conv).astype(bf16)
    knew_ref[...] = k[tb - WINDOW:, :]
    vnew_ref[...] = v[tb - WINDOW:, :]

    lo = lax.broadcasted_iota(jnp.int32, (tb, LANES), 1) < HEAD_DIM
    for j in range(N_KV // 2):
        pair = k[:, LANES * j:LANES * (j + 1)]
        rolled = pltpu.roll(pair, HEAD_DIM, axis=1)
        zero = jnp.zeros_like(pair)
        klo[2 * j, WINDOW:WINDOW + tb, :] = jnp.where(lo, pair, zero).astype(bf16)
        khi[2 * j, WINDOW:WINDOW + tb, :] = jnp.where(lo, zero, rolled).astype(bf16)
        klo[2 * j + 1, WINDOW:WINDOW + tb, :] = jnp.where(lo, rolled, zero).astype(bf16)
        khi[2 * j + 1, WINDOW:WINDOW + tb, :] = jnp.where(lo, zero, pair).astype(bf16)
    vt[:, WINDOW:WINDOW + tb] = v.T.astype(bf16)

    cc = lax.broadcasted_iota(jnp.int32, (Q_SUB + WINDOW, 2 * Q_SUB), 0)
    col = lax.broadcasted_iota(jnp.int32, (Q_SUB + WINDOW, 2 * Q_SUB), 1)
    rr = col & (Q_SUB - 1)
    band = (cc >= rr) & (cc <= rr + WINDOW)
    first_head = lax.broadcasted_iota(jnp.int32, (1, 2 * Q_SUB), 1) < Q_SUB
    mask0 = band & (cc >= jnp.where(i == 0, WINDOW, 0))

    def scores(j, g, t):
        r0 = Q_SUB * j
        qs = jnp.concatenate([q[r0:r0 + Q_SUB, 2 * LANES * g:2 * LANES * g + LANES],
                              q[r0:r0 + Q_SUB, 2 * LANES * g + LANES:2 * LANES * (g + 1)]], axis=0)
        kr = (klo, khi)[t]
        return lax.dot_general(kr[g, r0:r0 + Q_SUB + WINDOW, :], qs, (((1,), (1,)), ((), ())),
                               preferred_element_type=f32)

    def finish(j, g, t, st):
        r0 = Q_SUB * j
        h0, h1 = GROUP * g + t, GROUP * g + 2 + t
        sink = jnp.where(first_head, sinks_ref[h0], sinks_ref[h1])
        st = jnp.where(mask0 if j == 0 else band, st, -jnp.inf)
        m = jnp.maximum(jnp.max(st, axis=0, keepdims=True), sink)
        e = jnp.exp(st - m)
        linv = 1.0 / (jnp.sum(e, axis=0, keepdims=True) + jnp.exp(sink - m))
        vtg = vt[HEAD_DIM * g:HEAD_DIM * (g + 1), r0:r0 + Q_SUB + WINDOW]
        ot = _dot(vtg, e.astype(bf16)) * linv
        attbuf[HEAD_DIM * h0:HEAD_DIM * (h0 + 1), r0:r0 + Q_SUB] = ot[:, 0:Q_SUB]
        attbuf[HEAD_DIM * h1:HEAD_DIM * (h1 + 1), r0:r0 + Q_SUB] = ot[:, Q_SUB:2 * Q_SUB]

    chains = [(j, g, t) for j in range(tb // Q_SUB) for g in range(N_KV) for t in range(2)]
    fillers = ([functools.partial(proj, c, MXU_COLS) for c in range(C_GA, C_GB + D_MODEL, MXU_COLS)]
               + [(lambda c=c: _dot(conv_in, w_a_ref[:, c:c + MXU_COLS])) for c in range(0, D_MODEL, MXU_COLS)])
    fill_every = -(-len(chains) // len(fillers))
    filled = []
    pending = [scores(*c) for c in chains[:ATTN_LOOKAHEAD]]
    for n, chain in enumerate(chains):
        if n + ATTN_LOOKAHEAD < len(chains):
            pending.append(scores(*chains[n + ATTN_LOOKAHEAD]))
        if n % fill_every == 0:
            filled.extend(f() for f in fillers[len(filled):len(filled) + 1])
        finish(*chain, pending.pop(0))
    filled.extend(f() for f in fillers[len(filled):])
    ga, gb, ya = (jnp.concatenate(filled[j:j + D_MODEL // MXU_COLS], axis=1)
                  for j in range(0, len(filled), D_MODEL // MXU_COLS))

    yb = _dot(attbuf[...].T.astype(bf16), w_b_ref[...])
    mix = (jax.nn.sigmoid(ga) * ya + jax.nn.sigmoid(gb) * yb).astype(bf16)
    for r0 in range(0, tb, tb // 2):
        x1_ref[r0:r0 + tb // 2, :] = x[r0:r0 + tb // 2, :] + gt1 * _dot(mix[r0:r0 + tb // 2, :], w_o_ref[...])

    conv_ref[...] = ubuf[SUBLANES + tb - (CONV_W - 1):SUBLANES + tb, :]
    ubuf[0:SUBLANES, :] = ubuf[tb:tb + SUBLANES, :]
    for r in (klo, khi):
        r[:, 0:WINDOW, :] = r[:, tb:tb + WINDOW, :]
    vt[:, 0:WINDOW] = vt[:, tb:tb + WINDOW]


def _mixer_prompt(x, ada, g_mix, w_in, conv_w, sinks, w_a, w_b, w_o, *, tb):
    s = x.shape[0]
    assert s % tb == 0 and tb % Q_SUB == 0 and tb >= WINDOW
    kv_scratch = pltpu.VMEM((N_KV, WINDOW + tb, LANES), bf16)
    grid_spec = pltpu.PrefetchScalarGridSpec(
        num_scalar_prefetch=1,
        grid=(s // tb,),
        in_specs=[
            pl.BlockSpec((tb, D_MODEL), lambda i, sk: (i, 0)),
            pl.BlockSpec((N_MOD, 1, D_MODEL), lambda i, sk: (0, 0, 0)),
            pl.BlockSpec((1, D_MODEL), lambda i, sk: (0, 0)),
            pl.BlockSpec((D_MODEL, IN_COLS), lambda i, sk: (0, 0), pipeline_mode=pl.Buffered(1)),
            pl.BlockSpec((CONV_W, D_CONV), lambda i, sk: (0, 0)),
            pl.BlockSpec((D_CONV, D_MODEL), lambda i, sk: (0, 0), pipeline_mode=pl.Buffered(1)),
            pl.BlockSpec((N_HEADS * HEAD_DIM, D_MODEL), lambda i, sk: (0, 0), pipeline_mode=pl.Buffered(1)),
            pl.BlockSpec((D_MODEL, D_MODEL), lambda i, sk: (0, 0), pipeline_mode=pl.Buffered(1)),
        ],
        out_specs=[
            pl.BlockSpec((tb, D_MODEL), lambda i, sk: (i, 0)),
            pl.BlockSpec((CONV_W - 1, D_CONV), lambda i, sk: (0, 0)),
            pl.BlockSpec((WINDOW, KV_COLS), lambda i, sk: (0, 0)),
            pl.BlockSpec((WINDOW, KV_COLS), lambda i, sk: (0, 0)),
        ],
        scratch_shapes=[
            pltpu.VMEM((SUBLANES + tb, D_CONV), f32),
            kv_scratch, kv_scratch,
            pltpu.VMEM((KV_COLS, WINDOW + tb), bf16),
            pltpu.VMEM((N_HEADS * HEAD_DIM, tb), f32),
        ],
    )
    return pl.pallas_call(
        functools.partial(_mixer_prompt_kernel, tb=tb),
        grid_spec=grid_spec,
        out_shape=[
            jax.ShapeDtypeStruct((s, D_MODEL), f32),
            jax.ShapeDtypeStruct((CONV_W - 1, D_CONV), f32),
            jax.ShapeDtypeStruct((WINDOW, KV_COLS), f32),
            jax.ShapeDtypeStruct((WINDOW, KV_COLS), f32),
        ],
        compiler_params=pltpu.CompilerParams(
            dimension_semantics=("arbitrary",),
            vmem_limit_bytes=VMEM_BYTES_V7X * 7 // 8),
        name="mixer_prompt",
    )(sinks, x, ada, g_mix, w_in, conv_w, w_a, w_b, w_o)


def _ffn_prompt_kernel(x_ref, ada_ref, g_ref, w_up_ref, fcw_ref, fcb_ref, w_down_ref, gf_ref,
                       y_ref, fst_ref, upbuf, actbuf, *, tb, ch, nw):
    i = pl.program_id(0)

    @pl.when(i == 0)
    def _():
        upbuf[:, 0:SUBLANES, :] = jnp.zeros((2 * D_FF // LANES, SUBLANES, LANES), f32)

    hb = tb // 2
    sh2, sc2, gt2 = ada_ref[3], ada_ref[4], ada_ref[5]
    h = (_rms(x_ref[...], g_ref[...]) * (1.0 + sc2) + sh2).astype(bf16)

    def up_cols(half, c0):
        up = _dot(h[half * hb:(half + 1) * hb, :], w_up_ref[:, c0:c0 + ch])
        for s in range(ch // LANES):
            upbuf[c0 // LANES + s, SUBLANES:SUBLANES + hb, :] = up[:, s * LANES:(s + 1) * LANES]
        return up

    def conv_cols(c0, up):
        pieces = []
        for s in range(ch // LANES):
            slab = c0 // LANES + s
            cols = slice(c0 + s * LANES, c0 + (s + 1) * LANES)
            pieces.append(fcw_ref[0:1, cols] * upbuf[slab, SUBLANES - 2:SUBLANES - 2 + hb, :]
                          + fcw_ref[1:2, cols] * upbuf[slab, SUBLANES - 1:SUBLANES - 1 + hb, :]
                          + fcw_ref[2:3, cols] * up[:, s * LANES:(s + 1) * LANES]
                          + fcb_ref[0:1, cols])
            upbuf[slab, 0:SUBLANES, :] = upbuf[slab, hb:hb + SUBLANES, :]
        return jnp.concatenate(pieces, axis=1)

    def down_cols(half, n0):
        return _dot(actbuf[half], w_down_ref[:, n0:n0 + nw])

    def finish(half, parts):
        rows = slice(half * hb, (half + 1) * hb)
        x2 = x_ref[rows, :] + gt2 * jnp.concatenate(parts, axis=1)
        y_ref[rows, :] = _rms(x2, gf_ref[...])

    chunks = list(range(0, D_FF, ch))
    down_starts = list(range(0, D_MODEL, nw))
    down_at = {len(chunks) * (k + 1) // (len(down_starts) + 1): n0 for k, n0 in enumerate(down_starts)}
    assert len(down_at) == len(down_starts)
    for half in range(2):
        parts = []
        pending = [(up_cols(half, c0), up_cols(half, D_FF + c0)) for c0 in chunks[:FFN_LOOKAHEAD]]
        for n, c0 in enumerate(chunks):
            if n + FFN_LOOKAHEAD < len(chunks):
                c1 = chunks[n + FFN_LOOKAHEAD]
                pending.append((up_cols(half, c1), up_cols(half, D_FF + c1)))
            if half == 1 and n in down_at:
                parts.append(down_cols(0, down_at[n]))
            up_g, up_v = pending.pop(0)
            actbuf[half, :, c0:c0 + ch] = (_silu(conv_cols(c0, up_g))
                                           * conv_cols(D_FF + c0, up_v)).astype(bf16)
        if half == 1:
            finish(0, parts)
    finish(1, [down_cols(1, n0) for n0 in down_starts])

    for slab in range(2 * D_FF // LANES):
        fst_ref[:, slab * LANES:(slab + 1) * LANES] = upbuf[slab, SUBLANES - (CONV_W - 1):SUBLANES, :]


def _ffn_prompt(x1, ada, g_ffn, w_up, fcw, fcb, w_down, g_final, *, tb, ch, nw):
    s = x1.shape[0]
    assert s % tb == 0 and tb % (4 * SUBLANES) == 0
    assert D_FF % ch == 0 and ch % LANES == 0 and D_MODEL % nw == 0 and nw % LANES == 0
    return pl.pallas_call(
        functools.partial(_ffn_prompt_kernel, tb=tb, ch=ch, nw=nw),
        grid=(s // tb,),
        in_specs=[
            pl.BlockSpec((tb, D_MODEL), lambda i: (i, 0)),
            pl.BlockSpec((N_MOD, 1, D_MODEL), lambda i: (0, 0, 0)),
            pl.BlockSpec((1, D_MODEL), lambda i: (0, 0)),
            _const_spec((D_MODEL, 2 * D_FF)),
            pl.BlockSpec((CONV_W, 2 * D_FF), lambda i: (0, 0)),
            pl.BlockSpec((1, 2 * D_FF), lambda i: (0, 0)),
            _const_spec((D_FF, D_MODEL)),
            pl.BlockSpec((1, D_MODEL), lambda i: (0, 0)),
        ],
        out_specs=[
            pl.BlockSpec((tb, D_MODEL), lambda i: (i, 0)),
            pl.BlockSpec((CONV_W - 1, 2 * D_FF), lambda i: (0, 0)),
        ],
        out_shape=[
            jax.ShapeDtypeStruct((s, D_MODEL), f32),
            jax.ShapeDtypeStruct((CONV_W - 1, 2 * D_FF), f32),
        ],
        scratch_shapes=[
            pltpu.VMEM((2 * D_FF // LANES, SUBLANES + tb // 2, LANES), f32),
            pltpu.VMEM((2, tb // 2, D_FF), bf16),
        ],
        compiler_params=pltpu.CompilerParams(
            dimension_semantics=("arbitrary",),
            vmem_limit_bytes=VMEM_BYTES_V7X * 7 // 8),
        name="ffn_prompt",
    )(x1, ada, g_ffn, w_up, fcw, fcb, w_down, g_final)


def _mixer_decode_pre_kernel(x_ref, ada_ref, g_ref, w_in_ref, cw_ref, st_ref, w_a_ref,
                             w_in_bf_ref, w_a_bf_ref, q_ref, k_ref, v_ref, za_ref, sgb_ref, stn_ref,
                             h_scr, proj_scr, *, chunk):
    j = pl.program_id(0)
    n_chunks = IN_COLS // chunk

    @pl.when(j == 0)
    def _():
        sh1 = ada_ref[:, 0:D_MODEL]
        sc1 = ada_ref[:, D_MODEL:2 * D_MODEL]
        h_scr[...] = (_rms(x_ref[:, 0, :], g_ref[...]) * (1.0 + sc1) + sh1).astype(bf16)

    w_chunk = w_in_ref[...].astype(bf16)
    w_in_bf_ref[...] = w_chunk
    part = _dot(h_scr[...], w_chunk)
    for c in range(n_chunks):
        @pl.when(j == c)
        def _(c=c):
            proj_scr[:, c * chunk:(c + 1) * chunk] = part

    @pl.when(j == n_chunks - 1)
    def _():
        def proj(c0, n):
            return proj_scr[:, c0:c0 + n]

        w_a = w_a_ref[...].astype(bf16)
        w_a_bf_ref[...] = w_a
        u = proj(C_C, D_CONV) * proj(C_XIN, D_CONV)
        prev0 = st_ref[:, 0, :]
        prev1 = st_ref[:, 1, :]
        conv = cw_ref[0:1, :] * prev0 + cw_ref[1:2, :] * prev1 + cw_ref[2:3, :] * u
        stn_ref[:, 0, :] = prev1
        stn_ref[:, 1, :] = u
        ya = _dot((proj(C_B, D_CONV) * conv).astype(bf16), w_a)
        q_ref[...] = proj(C_Q, N_HEADS * HEAD_DIM) * ATTN_SCALE
        k_ref[...] = proj(C_K, KV_COLS)
        v_ref[...] = proj(C_V, KV_COLS)
        za_ref[...] = jax.nn.sigmoid(proj(C_GA, D_MODEL)) * ya
        sgb_ref[...] = jax.nn.sigmoid(proj(C_GB, D_MODEL))


def _mixer_decode_pre(x, ada, g_mix, w_in, conv_w, state, w_a, *, chunk):
    n = x.shape[0]
    assert IN_COLS % chunk == 0 and chunk % LANES == 0
    const2 = lambda j: (0, 0)
    const3 = lambda j: (0, 0, 0)
    row_block = lambda cols: pl.BlockSpec((n, cols), const2)
    state_block = pl.BlockSpec((n, CONV_W - 1, D_CONV), const3)
    shapes = [((D_MODEL, IN_COLS), bf16), ((D_CONV, D_MODEL), bf16),
              ((n, N_HEADS * HEAD_DIM), f32), ((n, KV_COLS), f32), ((n, KV_COLS), f32),
              ((n, D_MODEL), f32), ((n, D_MODEL), f32), ((n, CONV_W - 1, D_CONV), f32)]
    return pl.pallas_call(
        functools.partial(_mixer_decode_pre_kernel, chunk=chunk),
        grid=(IN_COLS // chunk,),
        in_specs=[pl.BlockSpec((n, 1, D_MODEL), const3), row_block(2 * D_MODEL), pl.BlockSpec((1, D_MODEL), const2),
                  pl.BlockSpec((D_MODEL, chunk), lambda j: (0, j)),
                  pl.BlockSpec((CONV_W, D_CONV), const2), state_block,
                  pl.BlockSpec((D_CONV, D_MODEL), const2)],
        out_specs=[pl.BlockSpec((D_MODEL, chunk), lambda j: (0, j)), pl.BlockSpec((D_CONV, D_MODEL), const2),
                   row_block(N_HEADS * HEAD_DIM), row_block(KV_COLS), row_block(KV_COLS),
                   row_block(D_MODEL), row_block(D_MODEL), state_block],
        out_shape=[jax.ShapeDtypeStruct(shp, dt) for shp, dt in shapes],
        scratch_shapes=[pltpu.VMEM((n, D_MODEL), bf16), pltpu.VMEM((n, IN_COLS), f32)],
        compiler_params=pltpu.CompilerParams(
            dimension_semantics=("arbitrary",),
            vmem_limit_bytes=VMEM_BYTES_V7X * 5 // 8),
        name="mixer_decode_pre",
    )(x, ada, g_mix, w_in, conv_w, state, w_a)


def _attn_decode_kernel(q_ref, kn_ref, vn_ref, knt_ref, vnt_ref, ck_ref, cv_ref, sink_ref,
                        att_ref, ok_ref, ov_ref, *, bb):
    step = pl.program_id(0)
    last = lax.broadcasted_iota(jnp.int32, (HEAD_DIM, WINDOW), 1) == WINDOW - 1
    to_front = (LANES - step * bb) % LANES
    kstep = [pltpu.roll(knt_ref[g], to_front, axis=1) for g in range(N_KV)]
    vstep = [pltpu.roll(vnt_ref[g], to_front, axis=1) for g in range(N_KV)]

    pairs = [(b, g) for b in range(bb) for g in range(N_KV)]
    n_stage = 7
    cuts = [len(pairs) * s // n_stage for s in range(n_stage + 1)]

    def shift_caches(stage):
        for b, g in pairs[cuts[stage]:cuts[stage + 1]]:
            ok_ref[b, g] = jnp.where(last, pltpu.roll(kstep[g], WINDOW - 1 - b, axis=1),
                                     pltpu.roll(ck_ref[b, g], WINDOW - 1, axis=1))
            ov_ref[b, g] = jnp.where(last, pltpu.roll(vstep[g], WINDOW - 1 - b, axis=1),
                                     pltpu.roll(cv_ref[b, g], WINDOW - 1, axis=1))

    sinks = [sink_ref[g][:, 0:1] for g in range(N_KV)]
    shift_caches(0)
    s_old = [_dot(q_ref[b, g].astype(bf16), ck_ref[b, g].astype(bf16)) for b, g in pairs]
    shift_caches(1)
    s_new = [jnp.sum(q_ref[b, g] * kn_ref[b, g], axis=-1, keepdims=True) for b, g in pairs]
    shift_caches(2)
    m = [jnp.maximum(jnp.maximum(jnp.max(so, axis=-1, keepdims=True), sn), sinks[g])
         for (b, g), so, sn in zip(pairs, s_old, s_new)]
    shift_caches(3)
    e_old = [jnp.exp(so - mm) for so, mm in zip(s_old, m)]
    e_new = [jnp.exp(sn - mm) for sn, mm in zip(s_new, m)]
    den = [jnp.sum(eo, axis=-1, keepdims=True) + en + jnp.exp(sinks[g] - mm)
           for (b, g), eo, en, mm in zip(pairs, e_old, e_new, m)]
    shift_caches(4)
    o_old = [lax.dot_general(eo.astype(bf16), cv_ref[b, g].astype(bf16), (((1,), (1,)), ((), ())),
                             preferred_element_type=f32) for (b, g), eo in zip(pairs, e_old)]
    shift_caches(5)
    for (b, g), oo, en, dd in zip(pairs, o_old, e_new, den):
        att_ref[b, g] = (oo + en * vn_ref[b, g]) / dd
    shift_caches(6)


def _attn_decode(q4, kn4, vn4, knt, vnt, ck, cv, sink_b, *, bb):
    n = q4.shape[0]
    assert n % bb == 0 and n == LANES and WINDOW == LANES
    cache_spec = pl.BlockSpec((bb, N_KV, HEAD_DIM, WINDOW), lambda b: (b, 0, 0, 0))
    row_spec = pl.BlockSpec((bb, N_KV, 1, HEAD_DIM), lambda b: (b, 0, 0, 0))
    q_spec = pl.BlockSpec((bb, N_KV, GROUP, HEAD_DIM), lambda b: (b, 0, 0, 0))
    new_t_spec = pl.BlockSpec((N_KV, HEAD_DIM, n), lambda b: (0, 0, 0))
    return pl.pallas_call(
        functools.partial(_attn_decode_kernel, bb=bb),
        grid=(n // bb,),
        in_specs=[q_spec, row_spec, row_spec, new_t_spec, new_t_spec, cache_spec, cache_spec,
                  pl.BlockSpec((N_KV, GROUP, LANES), lambda b: (0, 0, 0))],
        out_specs=[q_spec, cache_spec, cache_spec],
        out_shape=[jax.ShapeDtypeStruct((n, N_KV, GROUP, HEAD_DIM), f32),
                   jax.ShapeDtypeStruct((n, N_KV, HEAD_DIM, WINDOW), f32),
                   jax.ShapeDtypeStruct((n, N_KV, HEAD_DIM, WINDOW), f32)],
        compiler_params=pltpu.CompilerParams(dimension_semantics=("arbitrary",)),
        name="attn_decode",
    )(q4, kn4, vn4, knt, vnt, ck, cv, sink_b)


def _post_decode_kernel(x_ref, ada_ref, za_ref, sgb_ref, att_ref, w_b_ref, w_o_ref, g_ref, w_up_ref,
                        fcw_ref, fcb_ref, w_down_ref, gf_ref, fst_ref,
                        w_b_bf_ref, w_o_bf_ref, w_up_bf_ref, w_down_bf_ref, y_ref, fstn_ref,
                        yb_scr, x1_scr, h_scr, up_scr, act_scr, acc_scr, *, phases):
    j = pl.program_id(0)
    (b0, nb, cb), (o0, no, co), (u0, nu, cu), (d0, nd, cd) = phases

    def mod(k):
        return ada_ref[:, k * D_MODEL:(k + 1) * D_MODEL]

    for c in range(nb):
        @pl.when(j == b0 + c)
        def _(c=c):
            w = w_b_ref[...].astype(bf16)
            w_b_bf_ref[...] = w
            yb_scr[:, c * cb:(c + 1) * cb] = _dot(att_ref[...].astype(bf16), w)

    for c in range(no):
        @pl.when(j == o0 + c)
        def _(c=c):
            w = w_o_ref[...].astype(bf16)
            w_o_bf_ref[...] = w
            mix = (za_ref[...] + sgb_ref[...] * yb_scr[...]).astype(bf16)
            cols = slice(c * co, (c + 1) * co)
            x1_scr[:, cols] = x_ref[:, 0, cols] + mod(2)[:, cols] * _dot(mix, w)

    for c in range(nu):
        @pl.when(j == u0 + c)
        def _(c=c):
            if c == 0:
                h_scr[...] = (_rms(x1_scr[...], g_ref[...]) * (1.0 + mod(4)) + mod(3)).astype(bf16)
            w = w_up_ref[...].astype(bf16)
            w_up_bf_ref[...] = w
            up_scr[:, c * cu:(c + 1) * cu] = _dot(h_scr[...], w)

    for c in range(nd):
        @pl.when(j == d0 + c)
        def _(c=c):
            if c == 0:
                up = up_scr[...]
                prev0 = fst_ref[:, 0, :]
                prev1 = fst_ref[:, 1, :]
                conv = fcw_ref[0:1, :] * prev0 + fcw_ref[1:2, :] * prev1 + fcw_ref[2:3, :] * up + fcb_ref[...]
                fstn_ref[:, 0, :] = prev1
                fstn_ref[:, 1, :] = up
                act_scr[...] = (_silu(conv[:, 0:D_FF]) * conv[:, D_FF:2 * D_FF]).astype(bf16)
            w = w_down_ref[...].astype(bf16)
            w_down_bf_ref[...] = w
            cols = slice(c * cd, (c + 1) * cd)
            acc_scr[:, cols] = x1_scr[:, cols] + mod(5)[:, cols] * _dot(act_scr[...], w)
            if c == nd - 1:
                y_ref[:, 0, :] = _rms(acc_scr[...], gf_ref[...])


def _post_decode(x, ada, za, sgb, att, w_b, w_o, g_ffn, w_up, fcw, fcb, w_down, g_final, fstate,
                 *, proj_chunk, up_chunk, down_chunk):
    n = x.shape[0]
    assert D_MODEL % proj_chunk == 0 and (2 * D_FF) % up_chunk == 0 and D_MODEL % down_chunk == 0
    assert proj_chunk % LANES == 0 and up_chunk % LANES == 0 and down_chunk % LANES == 0
    nb = no = D_MODEL // proj_chunk
    nu, nd = 2 * D_FF // up_chunk, D_MODEL // down_chunk
    b0, o0, u0, d0 = 0, nb, nb + no, nb + no + nu
    phases = ((b0, nb, proj_chunk), (o0, no, proj_chunk), (u0, nu, up_chunk), (d0, nd, down_chunk))

    def chunk_index(start, count):
        return lambda j: jnp.clip(j - start, 0, count - 1)

    ib, io, iu, idn = (chunk_index(s0, cnt) for s0, cnt, _ in phases)
    const2 = lambda j: (0, 0)
    const3 = lambda j: (0, 0, 0)
    rows = lambda cols: pl.BlockSpec((n, cols), const2, pipeline_mode=pl.Buffered(1))
    state_block = pl.BlockSpec((n, CONV_W - 1, 2 * D_FF), const3, pipeline_mode=pl.Buffered(1))
    token_block = pl.BlockSpec((n, 1, D_MODEL), const3, pipeline_mode=pl.Buffered(1))
    col_chunk = lambda k, width, idx: pl.BlockSpec((k, width), lambda j: (0, idx(j)))
    w_down_block = col_chunk(D_FF, down_chunk, idn)
    return pl.pallas_call(
        functools.partial(_post_decode_kernel, phases=phases),
        grid=(d0 + nd,),
        in_specs=[token_block, rows(N_MOD * D_MODEL), rows(D_MODEL), rows(D_MODEL), rows(N_HEADS * HEAD_DIM),
                  col_chunk(N_HEADS * HEAD_DIM, proj_chunk, ib), col_chunk(D_MODEL, proj_chunk, io),
                  pl.BlockSpec((1, D_MODEL), const2), col_chunk(D_MODEL, up_chunk, iu),
                  pl.BlockSpec((CONV_W, 2 * D_FF), const2), pl.BlockSpec((1, 2 * D_FF), const2),
                  w_down_block, pl.BlockSpec((1, D_MODEL), const2), state_block],
        out_specs=[col_chunk(N_HEADS * HEAD_DIM, proj_chunk, ib), col_chunk(D_MODEL, proj_chunk, io),
                   col_chunk(D_MODEL, up_chunk, iu), w_down_block, token_block, state_block],
        out_shape=[jax.ShapeDtypeStruct((N_HEADS * HEAD_DIM, D_MODEL), bf16),
                   jax.ShapeDtypeStruct((D_MODEL, D_MODEL), bf16),
                   jax.ShapeDtypeStruct((D_MODEL, 2 * D_FF), bf16),
                   jax.ShapeDtypeStruct((D_FF, D_MODEL), bf16),
                   jax.ShapeDtypeStruct((n, 1, D_MODEL), f32),
                   jax.ShapeDtypeStruct((n, CONV_W - 1, 2 * D_FF), f32)],
        scratch_shapes=[pltpu.VMEM((n, D_MODEL), f32), pltpu.VMEM((n, D_MODEL), f32), pltpu.VMEM((n, D_MODEL), bf16),
                        pltpu.VMEM((n, 2 * D_FF), f32), pltpu.VMEM((n, D_FF), bf16), pltpu.VMEM((n, D_MODEL), f32)],
        compiler_params=pltpu.CompilerParams(
            dimension_semantics=("arbitrary",),
            vmem_limit_bytes=VMEM_BYTES_V7X * 7 // 8),
        name="post_decode",
    )(x, ada, za, sgb, att, w_b, w_o, g_ffn, w_up, fcw, fcb, w_down, g_final, fstate)


PROMPT_MIXER_ROWS = 512
PROMPT_FFN_ROWS = 1024
PROMPT_FFN_COLS = 256
PROMPT_FFN_DOWN_COLS = 512
DECODE_ATTN_BATCH = 32
DECODE_IN_CHUNK = 1664
DECODE_PROJ_CHUNK = 512
DECODE_UP_CHUNK = 512
DECODE_DOWN_CHUNK = 256


def kernel(x_prompt, x_sample, c_prompt, c_sample, state_conv_a, cache_k_win, cache_v_win, state_ffn_conv, w_ada, b_ada, g_mix, w_in, conv_a_w, attn_sinks, w_a_out, w_b_out, w_o, g_ffn, w_up, ffn_conv_w, ffn_conv_b, w_down, g_final):
    depth = w_in.shape[0]
    n_p, seq, _ = x_prompt.shape
    n_s, t_s, _ = x_sample.shape
    assert n_p == 1 and t_s == 1, "one prompt sequence and single-token decode only"
    xp = x_prompt.reshape(seq, D_MODEL)
    xs = x_sample
    gf = g_final.reshape(1, D_MODEL)
    outs = [[] for _ in range(8)]
    for l in range(depth):
        ada_s, ada_p = _ada(c_sample, c_prompt, w_ada[l], b_ada[l])
        gm, gn = g_mix[l].reshape(1, D_MODEL), g_ffn[l].reshape(1, D_MODEL)
        fcb = ffn_conv_b[l].reshape(1, 2 * D_FF)

        w_in_b, w_a_b, q, k_n, v_n, za, sgb, conv_s = _mixer_decode_pre(
            xs, ada_s, gm, w_in[l], conv_a_w[l], state_conv_a[l], w_a_out[l], chunk=DECODE_IN_CHUNK)
        to_native = lambda c: c.transpose(0, 2, 3, 1)
        from_native = lambda c: c.transpose(0, 3, 1, 2)
        sink_b = jnp.broadcast_to(attn_sinks[l].reshape(N_KV, GROUP, 1), (N_KV, GROUP, LANES))
        att4, k_s, v_s = _attn_decode(
            q.reshape(n_s, N_KV, GROUP, HEAD_DIM),
            k_n.reshape(n_s, N_KV, 1, HEAD_DIM), v_n.reshape(n_s, N_KV, 1, HEAD_DIM),
            k_n.T.reshape(N_KV, HEAD_DIM, n_s), v_n.T.reshape(N_KV, HEAD_DIM, n_s),
            to_native(cache_k_win[l]), to_native(cache_v_win[l]), sink_b, bb=DECODE_ATTN_BATCH)
        k_s, v_s = from_native(k_s), from_native(v_s)
        w_b_b, w_o_b, w_up_b, w_down_b, xs, ffn_s = _post_decode(
            xs, ada_s, za, sgb, att4.reshape(n_s, N_HEADS * HEAD_DIM), w_b_out[l], w_o[l], gn, w_up[l],
            ffn_conv_w[l], fcb, w_down[l], gf, state_ffn_conv[l],
            proj_chunk=DECODE_PROJ_CHUNK, up_chunk=DECODE_UP_CHUNK, down_chunk=DECODE_DOWN_CHUNK)

        x1, conv_p, k_p, v_p = _mixer_prompt(xp, ada_p, gm, w_in_b, conv_a_w[l], attn_sinks[l],
                                             w_a_b, w_b_b, w_o_b, tb=PROMPT_MIXER_ROWS)
        xp, ffn_p = _ffn_prompt(x1, ada_p, gn, w_up_b, ffn_conv_w[l], fcb, w_down_b, gf,
                                tb=PROMPT_FFN_ROWS, ch=PROMPT_FFN_COLS, nw=PROMPT_FFN_DOWN_COLS)

        for lst, val in zip(outs, (
                conv_p.reshape(n_p, CONV_W - 1, D_CONV), conv_s,
                k_p.reshape(n_p, WINDOW, N_KV, HEAD_DIM), k_s,
                v_p.reshape(n_p, WINDOW, N_KV, HEAD_DIM), v_s,
                ffn_p.reshape(n_p, CONV_W - 1, 2 * D_FF), ffn_s)):
            lst.append(val)
    assert depth == 1, "final RMSNorm is fused into the single layer's FFN kernels"
    return (xp.reshape(n_p, seq, D_MODEL), xs) + tuple(jnp.stack(o) for o in outs)
```

```python
import functools

import jax
import jax.numpy as jnp
from jax import lax
from jax.experimental import pallas as pl
from jax.experimental.pallas import tpu as pltpu

f32 = jnp.float32
bf16 = jnp.bfloat16

D_MODEL = 1024
D_CONV = D_MODEL
CONV_W = 3
N_HEADS = 16
N_KV = 4
GROUP = N_HEADS // N_KV
HEAD_DIM = 64
WINDOW = 128
D_FF = 2816
EPS = 1e-6
N_MOD = 6
ATTN_SCALE = HEAD_DIM ** -0.5
KV_COLS = N_KV * HEAD_DIM
C_XIN, C_B, C_C = 0, D_CONV, 2 * D_CONV
C_Q = 3 * D_CONV
C_K = C_Q + N_HEADS * HEAD_DIM
C_V = C_K + KV_COLS
C_GA = C_V + KV_COLS
C_GB = C_GA + D_MODEL
IN_COLS = C_GB + D_MODEL

LANES = 128
SUBLANES = 8
Q_SUB = 128
ATTN_LOOKAHEAD = 3
FFN_LOOKAHEAD = 2
MXU_COLS = 256
VMEM_BYTES_V7X = 64 * 1024 * 1024


def _rms(x, g):
    ms = jnp.mean(x * x, axis=-1, keepdims=True)
    return x * lax.rsqrt(ms + EPS) * g


def _silu(x):
    return x * jax.nn.sigmoid(x)


def _dot(a, b):
    return jnp.dot(a, b, preferred_element_type=f32)


def _const_spec(shape):
    nd = len(shape)
    return pl.BlockSpec(shape, lambda i: (0,) * nd, pipeline_mode=pl.Buffered(1))


def _ada_kernel(cs_ref, cp_ref, w_ref, b_ref, os_ref, op_ref, *, per_step):
    w = w_ref[...].astype(bf16)
    os_ref[...] = _dot(_silu(cs_ref[...]).astype(bf16), w) + b_ref[...]
    cp = jnp.broadcast_to(_silu(cp_ref[...]), (SUBLANES, D_MODEL)).astype(bf16)
    mod_p = _dot(cp, w)[0:1, :] + b_ref[...]
    for m in range(per_step):
        op_ref[m] = mod_p[:, m * D_MODEL:(m + 1) * D_MODEL]


def _ada(c_sample, c_prompt, w_ada, b_ada, *, per_step):
    n_s = c_sample.shape[0]
    assert c_prompt.shape[0] == 1 and N_MOD % per_step == 0
    cols = per_step * D_MODEL
    return pl.pallas_call(
        functools.partial(_ada_kernel, per_step=per_step),
        grid=(N_MOD // per_step,),
        in_specs=[
            pl.BlockSpec((n_s, D_MODEL), lambda j: (0, 0)),
            pl.BlockSpec((1, D_MODEL), lambda j: (0, 0)),
            pl.BlockSpec((D_MODEL, cols), lambda j: (0, j)),
            pl.BlockSpec((1, cols), lambda j: (0, j)),
        ],
        out_specs=[pl.BlockSpec((n_s, cols), lambda j: (0, j)),
                   pl.BlockSpec((per_step, 1, D_MODEL), lambda j: (j, 0, 0))],
        out_shape=[jax.ShapeDtypeStruct((n_s, N_MOD * D_MODEL), f32),
                   jax.ShapeDtypeStruct((N_MOD, 1, D_MODEL), f32)],
        compiler_params=pltpu.CompilerParams(
            dimension_semantics=("arbitrary",),
            vmem_limit_bytes=VMEM_BYTES_V7X // 2),
        name="ada",
    )(c_sample, c_prompt, w_ada, b_ada.reshape(1, -1))


def _mixer_prompt_kernel(sinks_ref, x_ref, ada_ref, g_ref, w_in_ref, cw_ref, w_a_ref, w_b_ref, w_o_ref,
                         x1_ref, conv_ref, knew_ref, vnew_ref,
                         ubuf, klo, khi, vt, attbuf, *, tb):
    i = pl.program_id(0)

    @pl.when(i == 0)
    def _():
        ubuf[0:SUBLANES, :] = jnp.zeros((SUBLANES, D_CONV), f32)
        for r in (klo, khi):
            r[:, 0:WINDOW, :] = jnp.zeros((N_KV, WINDOW, LANES), bf16)
        vt[:, 0:WINDOW] = jnp.zeros((KV_COLS, WINDOW), bf16)

    x = x_ref[...]
    sh1, sc1, gt1 = ada_ref[0], ada_ref[1], ada_ref[2]
    h = (_rms(x, g_ref[...]) * (1.0 + sc1) + sh1).astype(bf16)

    def proj(c0, n):
        return _dot(h, w_in_ref[:, c0:c0 + n])

    k = proj(C_K, KV_COLS)
    v = proj(C_V, KV_COLS)
    u = proj(C_C, D_CONV) * proj(C_XIN, D_CONV)
    ubuf[SUBLANES:SUBLANES + tb, :] = u
    b_gate = proj(C_B, D_CONV)
    q = (proj(C_Q, N_HEADS * HEAD_DIM) * ATTN_SCALE).astype(bf16)
    conv = (cw_ref[0:1, :] * ubuf[SUBLANES - 2:SUBLANES - 2 + tb, :]
            + cw_ref[1:2, :] * ubuf[SUBLANES - 1:SUBLANES - 1 + tb, :]
            + cw_ref[2:3, :] * u)
    conv_in = (b_gate * conv).astype(bf16)
    knew_ref[...] = k[tb - WINDOW:, :].T
    vnew_ref[...] = v[tb - WINDOW:, :].T

    lo = lax.broadcasted_iota(jnp.int32, (tb, LANES), 1) < HEAD_DIM
    for j in range(N_KV // 2):
        pair = k[:, LANES * j:LANES * (j + 1)]
        rolled = pltpu.roll(pair, HEAD_DIM, axis=1)
        zero = jnp.zeros_like(pair)
        klo[2 * j, WINDOW:WINDOW + tb, :] = jnp.where(lo, pair, zero).astype(bf16)
        khi[2 * j, WINDOW:WINDOW + tb, :] = jnp.where(lo, zero, rolled).astype(bf16)
        klo[2 * j + 1, WINDOW:WINDOW + tb, :] = jnp.where(lo, rolled, zero).astype(bf16)
        khi[2 * j + 1, WINDOW:WINDOW + tb, :] = jnp.where(lo, zero, pair).astype(bf16)
    vt[:, WINDOW:WINDOW + tb] = v.T.astype(bf16)

    cc = lax.broadcasted_iota(jnp.int32, (Q_SUB + WINDOW, 2 * Q_SUB), 0)
    col = lax.broadcasted_iota(jnp.int32, (Q_SUB + WINDOW, 2 * Q_SUB), 1)
    rr = col & (Q_SUB - 1)
    band = (cc >= rr) & (cc <= rr + WINDOW)
    first_head = lax.broadcasted_iota(jnp.int32, (1, 2 * Q_SUB), 1) < Q_SUB
    mask0 = band & (cc >= jnp.where(i == 0, WINDOW, 0))

    def scores(j, g, t):
        r0 = Q_SUB * j
        qs = jnp.concatenate([q[r0:r0 + Q_SUB, 2 * LANES * g:2 * LANES * g + LANES],
                              q[r0:r0 + Q_SUB, 2 * LANES * g + LANES:2 * LANES * (g + 1)]], axis=0)
        kr = (klo, khi)[t]
        return lax.dot_general(kr[g, r0:r0 + Q_SUB + WINDOW, :], qs, (((1,), (1,)), ((), ())),
                               preferred_element_type=f32)

    def finish(j, g, t, st):
        r0 = Q_SUB * j
        h0, h1 = GROUP * g + t, GROUP * g + 2 + t
        sink = jnp.where(first_head, sinks_ref[h0], sinks_ref[h1])
        st = jnp.where(mask0 if j == 0 else band, st, -jnp.inf)
        m = jnp.maximum(jnp.max(st, axis=0, keepdims=True), sink)
        e = jnp.exp(st - m)
        linv = 1.0 / (jnp.sum(e, axis=0, keepdims=True) + jnp.exp(sink - m))
        vtg = vt[HEAD_DIM * g:HEAD_DIM * (g + 1), r0:r0 + Q_SUB + WINDOW]
        ot = _dot(vtg, e.astype(bf16)) * linv
        attbuf[HEAD_DIM * h0:HEAD_DIM * (h0 + 1), r0:r0 + Q_SUB] = ot[:, 0:Q_SUB]
        attbuf[HEAD_DIM * h1:HEAD_DIM * (h1 + 1), r0:r0 + Q_SUB] = ot[:, Q_SUB:2 * Q_SUB]

    chains = [(j, g, t) for j in range(tb // Q_SUB) for g in range(N_KV) for t in range(2)]
    fillers = ([functools.partial(proj, c, MXU_COLS) for c in range(C_GA, C_GB + D_MODEL, MXU_COLS)]
               + [(lambda c=c: _dot(conv_in, w_a_ref[:, c:c + MXU_COLS])) for c in range(0, D_MODEL, MXU_COLS)])
    fill_every = -(-len(chains) // len(fillers))
    filled = []
    pending = [scores(*c) for c in chains[:ATTN_LOOKAHEAD]]
    for n, chain in enumerate(chains):
        if n + ATTN_LOOKAHEAD < len(chains):
            pending.append(scores(*chains[n + ATTN_LOOKAHEAD]))
        if n % fill_every == 0:
            filled.extend(f() for f in fillers[len(filled):len(filled) + 1])
        finish(*chain, pending.pop(0))
    filled.extend(f() for f in fillers[len(filled):])
    ga, gb, ya = (jnp.concatenate(filled[j:j + D_MODEL // MXU_COLS], axis=1)
                  for j in range(0, len(filled), D_MODEL // MXU_COLS))

    yb = _dot(attbuf[...].T.astype(bf16), w_b_ref[...])
    mix = (jax.nn.sigmoid(ga) * ya + jax.nn.sigmoid(gb) * yb).astype(bf16)
    for r0 in range(0, tb, tb // 2):
        x1_ref[r0:r0 + tb // 2, :] = x[r0:r0 + tb // 2, :] + gt1 * _dot(mix[r0:r0 + tb // 2, :], w_o_ref[...])

    conv_ref[...] = ubuf[SUBLANES + tb - (CONV_W - 1):SUBLANES + tb, :]
    ubuf[0:SUBLANES, :] = ubuf[tb:tb + SUBLANES, :]
    for r in (klo, khi):
        r[:, 0:WINDOW, :] = r[:, tb:tb + WINDOW, :]
    vt[:, 0:WINDOW] = vt[:, tb:tb + WINDOW]


def _mixer_prompt(x, ada, g_mix, w_in, conv_w, sinks, w_a, w_b, w_o, *, tb):
    s = x.shape[0]
    assert s % tb == 0 and tb % Q_SUB == 0 and tb >= WINDOW
    kv_scratch = pltpu.VMEM((N_KV, WINDOW + tb, LANES), bf16)
    grid_spec = pltpu.PrefetchScalarGridSpec(
        num_scalar_prefetch=1,
        grid=(s // tb,),
        in_specs=[
            pl.BlockSpec((tb, D_MODEL), lambda i, sk: (i, 0)),
            pl.BlockSpec((N_MOD, 1, D_MODEL), lambda i, sk: (0, 0, 0)),
            pl.BlockSpec((1, D_MODEL), lambda i, sk: (0, 0)),
            pl.BlockSpec((D_MODEL, IN_COLS), lambda i, sk: (0, 0), pipeline_mode=pl.Buffered(1)),
            pl.BlockSpec((CONV_W, D_CONV), lambda i, sk: (0, 0)),
            pl.BlockSpec((D_CONV, D_MODEL), lambda i, sk: (0, 0), pipeline_mode=pl.Buffered(1)),
            pl.BlockSpec((N_HEADS * HEAD_DIM, D_MODEL), lambda i, sk: (0, 0), pipeline_mode=pl.Buffered(1)),
            pl.BlockSpec((D_MODEL, D_MODEL), lambda i, sk: (0, 0), pipeline_mode=pl.Buffered(1)),
        ],
        out_specs=[
            pl.BlockSpec((tb, D_MODEL), lambda i, sk: (i, 0)),
            pl.BlockSpec((CONV_W - 1, D_CONV), lambda i, sk: (0, 0)),
            pl.BlockSpec((KV_COLS, WINDOW), lambda i, sk: (0, 0)),
            pl.BlockSpec((KV_COLS, WINDOW), lambda i, sk: (0, 0)),
        ],
        scratch_shapes=[
            pltpu.VMEM((SUBLANES + tb, D_CONV), f32),
            kv_scratch, kv_scratch,
            pltpu.VMEM((KV_COLS, WINDOW + tb), bf16),
            pltpu.VMEM((N_HEADS * HEAD_DIM, tb), f32),
        ],
    )
    return pl.pallas_call(
        functools.partial(_mixer_prompt_kernel, tb=tb),
        grid_spec=grid_spec,
        out_shape=[
            jax.ShapeDtypeStruct((s, D_MODEL), f32),
            jax.ShapeDtypeStruct((CONV_W - 1, D_CONV), f32),
            jax.ShapeDtypeStruct((KV_COLS, WINDOW), f32),
            jax.ShapeDtypeStruct((KV_COLS, WINDOW), f32),
        ],
        compiler_params=pltpu.CompilerParams(
            dimension_semantics=("arbitrary",),
            vmem_limit_bytes=VMEM_BYTES_V7X * 7 // 8),
        name="mixer_prompt",
    )(sinks, x, ada, g_mix, w_in, conv_w, w_a, w_b, w_o)


def _ffn_prompt_kernel(x_ref, ada_ref, g_ref, w_up_ref, fcw_ref, fcb_ref, w_down_ref, gf_ref,
                       y_ref, fst_ref, upbuf, actbuf, *, tb, ch, nw):
    i = pl.program_id(0)

    @pl.when(i == 0)
    def _():
        upbuf[:, 0:SUBLANES, :] = jnp.zeros((2 * D_FF // LANES, SUBLANES, LANES), f32)

    hb = tb // 2
    sh2, sc2, gt2 = ada_ref[3], ada_ref[4], ada_ref[5]
    h = (_rms(x_ref[...], g_ref[...]) * (1.0 + sc2) + sh2).astype(bf16)

    def up_cols(half, c0):
        up = _dot(h[half * hb:(half + 1) * hb, :], w_up_ref[:, c0:c0 + ch])
        for s in range(ch // LANES):
            upbuf[c0 // LANES + s, SUBLANES:SUBLANES + hb, :] = up[:, s * LANES:(s + 1) * LANES]
        return up

    def conv_cols(c0, up):
        pieces = []
        for s in range(ch // LANES):
            slab = c0 // LANES + s
            cols = slice(c0 + s * LANES, c0 + (s + 1) * LANES)
            pieces.append(fcw_ref[0:1, cols] * upbuf[slab, SUBLANES - 2:SUBLANES - 2 + hb, :]
                          + fcw_ref[1:2, cols] * upbuf[slab, SUBLANES - 1:SUBLANES - 1 + hb, :]
                          + fcw_ref[2:3, cols] * up[:, s * LANES:(s + 1) * LANES]
                          + fcb_ref[0:1, cols])
            upbuf[slab, 0:SUBLANES, :] = upbuf[slab, hb:hb + SUBLANES, :]
        return jnp.concatenate(pieces, axis=1)

    def down_cols(half, n0):
        return _dot(actbuf[half], w_down_ref[:, n0:n0 + nw])

    def finish(half, parts):
        rows = slice(half * hb, (half + 1) * hb)
        x2 = x_ref[rows, :] + gt2 * jnp.concatenate(parts, axis=1)
        y_ref[rows, :] = _rms(x2, gf_ref[...])

    chunks = list(range(0, D_FF, ch))
    down_starts = list(range(0, D_MODEL, nw))
    down_at = {len(chunks) * (k + 1) // (len(down_starts) + 1): n0 for k, n0 in enumerate(down_starts)}
    assert len(down_at) == len(down_starts)
    for half in range(2):
        parts = []
        pending = [(up_cols(half, c0), up_cols(half, D_FF + c0)) for c0 in chunks[:FFN_LOOKAHEAD]]
        for n, c0 in enumerate(chunks):
            if n + FFN_LOOKAHEAD < len(chunks):
                c1 = chunks[n + FFN_LOOKAHEAD]
                pending.append((up_cols(half, c1), up_cols(half, D_FF + c1)))
            if half == 1 and n in down_at:
                parts.append(down_cols(0, down_at[n]))
            up_g, up_v = pending.pop(0)
            actbuf[half, :, c0:c0 + ch] = (_silu(conv_cols(c0, up_g))
                                           * conv_cols(D_FF + c0, up_v)).astype(bf16)
        if half == 1:
            finish(0, parts)
    finish(1, [down_cols(1, n0) for n0 in down_starts])

    for slab in range(2 * D_FF // LANES):
        fst_ref[:, slab * LANES:(slab + 1) * LANES] = upbuf[slab, SUBLANES - (CONV_W - 1):SUBLANES, :]


def _ffn_prompt(x1, ada, g_ffn, w_up, fcw, fcb, w_down, g_final, *, tb, ch, nw):
    s = x1.shape[0]
    assert s % tb == 0 and tb % (4 * SUBLANES) == 0
    assert D_FF % ch == 0 and ch % LANES == 0 and D_MODEL % nw == 0 and nw % LANES == 0
    return pl.pallas_call(
        functools.partial(_ffn_prompt_kernel, tb=tb, ch=ch, nw=nw),
        grid=(s // tb,),
        in_specs=[
            pl.BlockSpec((tb, D_MODEL), lambda i: (i, 0)),
            pl.BlockSpec((N_MOD, 1, D_MODEL), lambda i: (0, 0, 0)),
            pl.BlockSpec((1, D_MODEL), lambda i: (0, 0)),
            _const_spec((D_MODEL, 2 * D_FF)),
            pl.BlockSpec((CONV_W, 2 * D_FF), lambda i: (0, 0)),
            pl.BlockSpec((1, 2 * D_FF), lambda i: (0, 0)),
            _const_spec((D_FF, D_MODEL)),
            pl.BlockSpec((1, D_MODEL), lambda i: (0, 0)),
        ],
        out_specs=[
            pl.BlockSpec((tb, D_MODEL), lambda i: (i, 0)),
            pl.BlockSpec((CONV_W - 1, 2 * D_FF), lambda i: (0, 0)),
        ],
        out_shape=[
            jax.ShapeDtypeStruct((s, D_MODEL), f32),
            jax.ShapeDtypeStruct((CONV_W - 1, 2 * D_FF), f32),
        ],
        scratch_shapes=[
            pltpu.VMEM((2 * D_FF // LANES, SUBLANES + tb // 2, LANES), f32),
            pltpu.VMEM((2, tb // 2, D_FF), bf16),
        ],
        compiler_params=pltpu.CompilerParams(
            dimension_semantics=("arbitrary",),
            vmem_limit_bytes=VMEM_BYTES_V7X * 7 // 8),
        name="ffn_prompt",
    )(x1, ada, g_ffn, w_up, fcw, fcb, w_down, g_final)


def _mixer_decode_pre_kernel(x_ref, ada_ref, g_ref, w_in_ref, cw_ref, st_ref, w_a_ref,
                             w_in_bf_ref, w_a_bf_ref, q_ref, k_ref, v_ref, knt_ref, vnt_ref, za_ref, sgb_ref, stn_ref,
                             h_scr, proj_scr, *, chunk):
    j = pl.program_id(0)
    n_chunks = IN_COLS // chunk

    @pl.when(j == 0)
    def _():
        sh1 = ada_ref[:, 0:D_MODEL]
        sc1 = ada_ref[:, D_MODEL:2 * D_MODEL]
        h_scr[...] = (_rms(x_ref[:, 0, :], g_ref[...]) * (1.0 + sc1) + sh1).astype(bf16)

    w_chunk = w_in_ref[...].astype(bf16)
    w_in_bf_ref[...] = w_chunk
    part = _dot(h_scr[...], w_chunk)
    for c in range(n_chunks):
        @pl.when(j == c)
        def _(c=c):
            proj_scr[:, c * chunk:(c + 1) * chunk] = part

    @pl.when(j == n_chunks - 1)
    def _():
        def proj(c0, n):
            return proj_scr[:, c0:c0 + n]

        w_a = w_a_ref[...].astype(bf16)
        w_a_bf_ref[...] = w_a
        u = proj(C_C, D_CONV) * proj(C_XIN, D_CONV)
        prev0 = st_ref[:, 0, :]
        prev1 = st_ref[:, 1, :]
        conv = cw_ref[0:1, :] * prev0 + cw_ref[1:2, :] * prev1 + cw_ref[2:3, :] * u
        stn_ref[:, 0, :] = prev1
        stn_ref[:, 1, :] = u
        ya = _dot((proj(C_B, D_CONV) * conv).astype(bf16), w_a)
        q_ref[...] = proj(C_Q, N_HEADS * HEAD_DIM) * ATTN_SCALE
        k = proj(C_K, KV_COLS)
        v = proj(C_V, KV_COLS)
        k_ref[...] = k
        v_ref[...] = v
        knt_ref[...] = k.T
        vnt_ref[...] = v.T
        za_ref[...] = jax.nn.sigmoid(proj(C_GA, D_MODEL)) * ya
        sgb_ref[...] = jax.nn.sigmoid(proj(C_GB, D_MODEL))


def _mixer_decode_pre(x, ada, g_mix, w_in, conv_w, state, w_a, *, chunk):
    n = x.shape[0]
    assert IN_COLS % chunk == 0 and chunk % LANES == 0
    const2 = lambda j: (0, 0)
    const3 = lambda j: (0, 0, 0)
    row_block = lambda cols: pl.BlockSpec((n, cols), const2)
    state_block = pl.BlockSpec((n, CONV_W - 1, D_CONV), const3)
    shapes = [((D_MODEL, IN_COLS), bf16), ((D_CONV, D_MODEL), bf16),
              ((n, N_HEADS * HEAD_DIM), f32), ((n, KV_COLS), f32), ((n, KV_COLS), f32),
              ((KV_COLS, n), f32), ((KV_COLS, n), f32),
              ((n, D_MODEL), f32), ((n, D_MODEL), f32), ((n, CONV_W - 1, D_CONV), f32)]
    return pl.pallas_call(
        functools.partial(_mixer_decode_pre_kernel, chunk=chunk),
        grid=(IN_COLS // chunk,),
        in_specs=[pl.BlockSpec((n, 1, D_MODEL), const3), row_block(2 * D_MODEL), pl.BlockSpec((1, D_MODEL), const2),
                  pl.BlockSpec((D_MODEL, chunk), lambda j: (0, j)),
                  pl.BlockSpec((CONV_W, D_CONV), const2), state_block,
                  pl.BlockSpec((D_CONV, D_MODEL), const2)],
        out_specs=[pl.BlockSpec((D_MODEL, chunk), lambda j: (0, j)), pl.BlockSpec((D_CONV, D_MODEL), const2),
                   row_block(N_HEADS * HEAD_DIM), row_block(KV_COLS), row_block(KV_COLS),
                   pl.BlockSpec((KV_COLS, n), const2), pl.BlockSpec((KV_COLS, n), const2),
                   row_block(D_MODEL), row_block(D_MODEL), state_block],
        out_shape=[jax.ShapeDtypeStruct(shp, dt) for shp, dt in shapes],
        scratch_shapes=[pltpu.VMEM((n, D_MODEL), bf16), pltpu.VMEM((n, IN_COLS), f32)],
        compiler_params=pltpu.CompilerParams(
            dimension_semantics=("arbitrary",),
            vmem_limit_bytes=VMEM_BYTES_V7X * 5 // 8),
        name="mixer_decode_pre",
    )(x, ada, g_mix, w_in, conv_w, state, w_a)


def _attn_decode_kernel(q_ref, kn_ref, vn_ref, knt_ref, vnt_ref, ck_ref, cv_ref, sink_ref,
                        att_ref, ok_ref, ov_ref, *, bb):
    step = pl.program_id(0)
    last = lax.broadcasted_iota(jnp.int32, (HEAD_DIM, WINDOW), 1) == WINDOW - 1
    to_front = (LANES - step * bb) % LANES
    kstep = [pltpu.roll(knt_ref[g], to_front, axis=1) for g in range(N_KV)]
    vstep = [pltpu.roll(vnt_ref[g], to_front, axis=1) for g in range(N_KV)]

    pairs = [(b, g) for b in range(bb) for g in range(N_KV)]
    n_stage = 7
    cuts = [len(pairs) * s // n_stage for s in range(n_stage + 1)]

    def shift_caches(stage):
        for b, g in pairs[cuts[stage]:cuts[stage + 1]]:
            ok_ref[b, g] = jnp.where(last, pltpu.roll(kstep[g], WINDOW - 1 - b, axis=1),
                                     pltpu.roll(ck_ref[b, g], WINDOW - 1, axis=1))
            ov_ref[b, g] = jnp.where(last, pltpu.roll(vstep[g], WINDOW - 1 - b, axis=1),
                                     pltpu.roll(cv_ref[b, g], WINDOW - 1, axis=1))

    sinks = [sink_ref[g][:, 0:1] for g in range(N_KV)]
    shift_caches(0)
    s_old = [_dot(q_ref[b, g].astype(bf16), ck_ref[b, g].astype(bf16)) for b, g in pairs]
    shift_caches(1)
    s_new = [jnp.sum(q_ref[b, g] * kn_ref[b, g], axis=-1, keepdims=True) for b, g in pairs]
    shift_caches(2)
    m = [jnp.maximum(jnp.maximum(jnp.max(so, axis=-1, keepdims=True), sn), sinks[g])
         for (b, g), so, sn in zip(pairs, s_old, s_new)]
    shift_caches(3)
    e_old = [jnp.exp(so - mm) for so, mm in zip(s_old, m)]
    e_new = [jnp.exp(sn - mm) for sn, mm in zip(s_new, m)]
    den = [jnp.sum(eo, axis=-1, keepdims=True) + en + jnp.exp(sinks[g] - mm)
           for (b, g), eo, en, mm in zip(pairs, e_old, e_new, m)]
    shift_caches(4)
    o_old = [lax.dot_general(eo.astype(bf16), cv_ref[b, g].astype(bf16), (((1,), (1,)), ((), ())),
                             preferred_element_type=f32) for (b, g), eo in zip(pairs, e_old)]
    shift_caches(5)
    for (b, g), oo, en, dd in zip(pairs, o_old, e_new, den):
        att_ref[b, g] = (oo + en * vn_ref[b, g]) / dd
    shift_caches(6)


def _attn_decode(q4, kn4, vn4, knt, vnt, ck, cv, sink_b, *, bb):
    n = q4.shape[0]
    assert n % bb == 0 and n == LANES and WINDOW == LANES
    cache_spec = pl.BlockSpec((bb, N_KV, HEAD_DIM, WINDOW), lambda b: (b, 0, 0, 0))
    row_spec = pl.BlockSpec((bb, N_KV, 1, HEAD_DIM), lambda b: (b, 0, 0, 0))
    q_spec = pl.BlockSpec((bb, N_KV, GROUP, HEAD_DIM), lambda b: (b, 0, 0, 0))
    new_t_spec = pl.BlockSpec((N_KV, HEAD_DIM, n), lambda b: (0, 0, 0))
    return pl.pallas_call(
        functools.partial(_attn_decode_kernel, bb=bb),
        grid=(n // bb,),
        in_specs=[q_spec, row_spec, row_spec, new_t_spec, new_t_spec, cache_spec, cache_spec,
                  pl.BlockSpec((N_KV, GROUP, LANES), lambda b: (0, 0, 0))],
        out_specs=[q_spec, cache_spec, cache_spec],
        out_shape=[jax.ShapeDtypeStruct((n, N_KV, GROUP, HEAD_DIM), f32),
                   jax.ShapeDtypeStruct((n, N_KV, HEAD_DIM, WINDOW), f32),
                   jax.ShapeDtypeStruct((n, N_KV, HEAD_DIM, WINDOW), f32)],
        compiler_params=pltpu.CompilerParams(dimension_semantics=("arbitrary",)),
        name="attn_decode",
    )(q4, kn4, vn4, knt, vnt, ck, cv, sink_b)


def _post_decode_kernel(x_ref, ada_ref, za_ref, sgb_ref, att_ref, w_b_ref, w_o_ref, g_ref, w_up_ref,
                        fcw_ref, fcb_ref, w_down_ref, gf_ref, fst_ref,
                        w_b_bf_ref, w_o_bf_ref, w_up_bf_ref, w_down_bf_ref, y_ref, fstn_ref,
                        yb_scr, x1_scr, h_scr, up_scr, act_scr, acc_scr, *, phases):
    j = pl.program_id(0)
    (b0, nb, cb), (o0, no, co), (u0, nu, cu), (d0, nd, cd) = phases

    def mod(k):
        return ada_ref[:, k * D_MODEL:(k + 1) * D_MODEL]

    for c in range(nb):
        @pl.when(j == b0 + c)
        def _(c=c):
            w = w_b_ref[...].astype(bf16)
            w_b_bf_ref[...] = w
            yb_scr[:, c * cb:(c + 1) * cb] = _dot(att_ref[...].astype(bf16), w)

    for c in range(no):
        @pl.when(j == o0 + c)
        def _(c=c):
            w = w_o_ref[...].astype(bf16)
            w_o_bf_ref[...] = w
            mix = (za_ref[...] + sgb_ref[...] * yb_scr[...]).astype(bf16)
            cols = slice(c * co, (c + 1) * co)
            x1_scr[:, cols] = x_ref[:, 0, cols] + mod(2)[:, cols] * _dot(mix, w)

    for c in range(nu):
        @pl.when(j == u0 + c)
        def _(c=c):
            if c == 0:
                h_scr[...] = (_rms(x1_scr[...], g_ref[...]) * (1.0 + mod(4)) + mod(3)).astype(bf16)
            w = w_up_ref[...].astype(bf16)
            w_up_bf_ref[...] = w
            up_scr[:, c * cu:(c + 1) * cu] = _dot(h_scr[...], w)

    for c in range(nd):
        @pl.when(j == d0 + c)
        def _(c=c):
            if c == 0:
                up = up_scr[...]
                prev0 = fst_ref[:, 0, :]
                prev1 = fst_ref[:, 1, :]
                conv = fcw_ref[0:1, :] * prev0 + fcw_ref[1:2, :] * prev1 + fcw_ref[2:3, :] * up + fcb_ref[...]
                fstn_ref[:, 0, :] = prev1
                fstn_ref[:, 1, :] = up
                act_scr[...] = (_silu(conv[:, 0:D_FF]) * conv[:, D_FF:2 * D_FF]).astype(bf16)
            w = w_down_ref[...].astype(bf16)
            w_down_bf_ref[...] = w
            cols = slice(c * cd, (c + 1) * cd)
            acc_scr[:, cols] = x1_scr[:, cols] + mod(5)[:, cols] * _dot(act_scr[...], w)
            if c == nd - 1:
                y_ref[:, 0, :] = _rms(acc_scr[...], gf_ref[...])


def _post_decode(x, ada, za, sgb, att, w_b, w_o, g_ffn, w_up, fcw, fcb, w_down, g_final, fstate,
                 *, proj_chunk, up_chunk, down_chunk):
    n = x.shape[0]
    assert D_MODEL % proj_chunk == 0 and (2 * D_FF) % up_chunk == 0 and D_MODEL % down_chunk == 0
    assert proj_chunk % LANES == 0 and up_chunk % LANES == 0 and down_chunk % LANES == 0
    nb = no = D_MODEL // proj_chunk
    nu, nd = 2 * D_FF // up_chunk, D_MODEL // down_chunk
    b0, o0, u0, d0 = 0, nb, nb + no, nb + no + nu
    phases = ((b0, nb, proj_chunk), (o0, no, proj_chunk), (u0, nu, up_chunk), (d0, nd, down_chunk))

    def chunk_index(start, count):
        return lambda j: jnp.clip(j - start, 0, count - 1)

    ib, io, iu, idn = (chunk_index(s0, cnt) for s0, cnt, _ in phases)
    const2 = lambda j: (0, 0)
    const3 = lambda j: (0, 0, 0)
    rows = lambda cols: pl.BlockSpec((n, cols), const2, pipeline_mode=pl.Buffered(1))
    state_block = pl.BlockSpec((n, CONV_W - 1, 2 * D_FF), const3, pipeline_mode=pl.Buffered(1))
    token_block = pl.BlockSpec((n, 1, D_MODEL), const3, pipeline_mode=pl.Buffered(1))
    col_chunk = lambda k, width, idx: pl.BlockSpec((k, width), lambda j: (0, idx(j)))
    w_down_block = col_chunk(D_FF, down_chunk, idn)
    return pl.pallas_call(
        functools.partial(_post_decode_kernel, phases=phases),
        grid=(d0 + nd,),
        in_specs=[token_block, rows(N_MOD * D_MODEL), rows(D_MODEL), rows(D_MODEL), rows(N_HEADS * HEAD_DIM),
                  col_chunk(N_HEADS * HEAD_DIM, proj_chunk, ib), col_chunk(D_MODEL, proj_chunk, io),
                  pl.BlockSpec((1, D_MODEL), const2), col_chunk(D_MODEL, up_chunk, iu),
                  pl.BlockSpec((CONV_W, 2 * D_FF), const2), pl.BlockSpec((1, 2 * D_FF), const2),
                  w_down_block, pl.BlockSpec((1, D_MODEL), const2), state_block],
        out_specs=[col_chunk(N_HEADS * HEAD_DIM, proj_chunk, ib), col_chunk(D_MODEL, proj_chunk, io),
                   col_chunk(D_MODEL, up_chunk, iu), w_down_block, token_block, state_block],
        out_shape=[jax.ShapeDtypeStruct((N_HEADS * HEAD_DIM, D_MODEL), bf16),
                   jax.ShapeDtypeStruct((D_MODEL, D_MODEL), bf16),
                   jax.ShapeDtypeStruct((D_MODEL, 2 * D_FF), bf16),
                   jax.ShapeDtypeStruct((D_FF, D_MODEL), bf16),
                   jax.ShapeDtypeStruct((n, 1, D_MODEL), f32),
                   jax.ShapeDtypeStruct((n, CONV_W - 1, 2 * D_FF), f32)],
        scratch_shapes=[pltpu.VMEM((n, D_MODEL), f32), pltpu.VMEM((n, D_MODEL), f32), pltpu.VMEM((n, D_MODEL), bf16),
                        pltpu.VMEM((n, 2 * D_FF), f32), pltpu.VMEM((n, D_FF), bf16), pltpu.VMEM((n, D_MODEL), f32)],
        compiler_params=pltpu.CompilerParams(
            dimension_semantics=("arbitrary",),
            vmem_limit_bytes=VMEM_BYTES_V7X * 7 // 8),
        name="post_decode",
    )(x, ada, za, sgb, att, w_b, w_o, g_ffn, w_up, fcw, fcb, w_down, g_final, fstate)


PROMPT_MIXER_ROWS = 512
PROMPT_FFN_ROWS = 1024
PROMPT_FFN_COLS = 256
PROMPT_FFN_DOWN_COLS = 512
DECODE_ATTN_BATCH = 32
ADA_PER_STEP = 2
DECODE_IN_CHUNK = 1664
DECODE_PROJ_CHUNK = 512
DECODE_UP_CHUNK = 512
DECODE_DOWN_CHUNK = 256


def kernel(x_prompt, x_sample, c_prompt, c_sample, state_conv_a, cache_k_win, cache_v_win, state_ffn_conv, w_ada, b_ada, g_mix, w_in, conv_a_w, attn_sinks, w_a_out, w_b_out, w_o, g_ffn, w_up, ffn_conv_w, ffn_conv_b, w_down, g_final):
    depth = w_in.shape[0]
    n_p, seq, _ = x_prompt.shape
    n_s, t_s, _ = x_sample.shape
    assert n_p == 1 and t_s == 1, "one prompt sequence and single-token decode only"
    xp = x_prompt.reshape(seq, D_MODEL)
    xs = x_sample
    gf = g_final.reshape(1, D_MODEL)
    outs = [[] for _ in range(8)]
    for l in range(depth):
        ada_s, ada_p = _ada(c_sample, c_prompt, w_ada[l], b_ada[l], per_step=ADA_PER_STEP)
        gm, gn = g_mix[l].reshape(1, D_MODEL), g_ffn[l].reshape(1, D_MODEL)
        fcb = ffn_conv_b[l].reshape(1, 2 * D_FF)

        w_in_b, w_a_b, q, k_n, v_n, k_nt, v_nt, za, sgb, conv_s = _mixer_decode_pre(
            xs, ada_s, gm, w_in[l], conv_a_w[l], state_conv_a[l], w_a_out[l], chunk=DECODE_IN_CHUNK)
        to_native = lambda c: c.transpose(0, 2, 3, 1)
        from_native = lambda c: c.transpose(0, 3, 1, 2)
        sink_b = jnp.broadcast_to(attn_sinks[l].reshape(N_KV, GROUP, 1), (N_KV, GROUP, LANES))
        att4, k_s, v_s = _attn_decode(
            q.reshape(n_s, N_KV, GROUP, HEAD_DIM),
            k_n.reshape(n_s, N_KV, 1, HEAD_DIM), v_n.reshape(n_s, N_KV, 1, HEAD_DIM),
            k_nt.reshape(N_KV, HEAD_DIM, n_s), v_nt.reshape(N_KV, HEAD_DIM, n_s),
            to_native(cache_k_win[l]), to_native(cache_v_win[l]), sink_b, bb=DECODE_ATTN_BATCH)
        k_s, v_s = from_native(k_s), from_native(v_s)
        w_b_b, w_o_b, w_up_b, w_down_b, xs, ffn_s = _post_decode(
            xs, ada_s, za, sgb, att4.reshape(n_s, N_HEADS * HEAD_DIM), w_b_out[l], w_o[l], gn, w_up[l],
            ffn_conv_w[l], fcb, w_down[l], gf, state_ffn_conv[l],
            proj_chunk=DECODE_PROJ_CHUNK, up_chunk=DECODE_UP_CHUNK, down_chunk=DECODE_DOWN_CHUNK)

        x1, conv_p, k_p, v_p = _mixer_prompt(xp, ada_p, gm, w_in_b, conv_a_w[l], attn_sinks[l],
                                             w_a_b, w_b_b, w_o_b, tb=PROMPT_MIXER_ROWS)
        xp, ffn_p = _ffn_prompt(x1, ada_p, gn, w_up_b, ffn_conv_w[l], fcb, w_down_b, gf,
                                tb=PROMPT_FFN_ROWS, ch=PROMPT_FFN_COLS, nw=PROMPT_FFN_DOWN_COLS)

        for lst, val in zip(outs, (
                conv_p.reshape(n_p, CONV_W - 1, D_CONV), conv_s,
                from_native(k_p.reshape(n_p, N_KV, HEAD_DIM, WINDOW)), k_s,
                from_native(v_p.reshape(n_p, N_KV, HEAD_DIM, WINDOW)), v_s,
                ffn_p.reshape(n_p, CONV_W - 1, 2 * D_FF), ffn_s)):
            lst.append(val)
    assert depth == 1, "final RMSNorm is fused into the single layer's FFN kernels"
    return (xp.reshape(n_p, seq, D_MODEL), xs) + tuple(jnp.stack(o) for o in outs)
```

```python
import functools

import jax
import jax.numpy as jnp
from jax import lax
from jax.experimental import pallas as pl
from jax.experimental.pallas import tpu as pltpu

f32 = jnp.float32
bf16 = jnp.bfloat16

D_MODEL = 1024
D_CONV = D_MODEL
CONV_W = 3
N_HEADS = 16
N_KV = 4
GROUP = N_HEADS // N_KV
HEAD_DIM = 64
WINDOW = 128
D_FF = 2816
EPS = 1e-6
N_MOD = 6
ATTN_SCALE = HEAD_DIM ** -0.5
KV_COLS = N_KV * HEAD_DIM
C_XIN, C_B, C_C = 0, D_CONV, 2 * D_CONV
C_Q = 3 * D_CONV
C_K = C_Q + N_HEADS * HEAD_DIM
C_V = C_K + KV_COLS
C_GA = C_V + KV_COLS
C_GB = C_GA + D_MODEL
IN_COLS = C_GB + D_MODEL

LANES = 128
SUBLANES = 8
Q_SUB = 128
ATTN_LOOKAHEAD = 3
FFN_LOOKAHEAD = 2
MXU_COLS = 256
VMEM_BYTES_V7X = 64 * 1024 * 1024
VMEM_LIMIT_LARGE = VMEM_BYTES_V7X * 7 // 8
VMEM_LIMIT_MEDIUM = VMEM_BYTES_V7X * 5 // 8
VMEM_LIMIT_SMALL = VMEM_BYTES_V7X // 2


def _rms(x, g):
    ms = jnp.mean(x * x, axis=-1, keepdims=True)
    return x * lax.rsqrt(ms + EPS) * g


def _silu(x):
    return x * jax.nn.sigmoid(x)


def _dot(a, b):
    return jnp.dot(a, b, preferred_element_type=f32)


def _const_spec(shape):
    nd = len(shape)
    return pl.BlockSpec(shape, lambda i: (0,) * nd, pipeline_mode=pl.Buffered(1))


def _ada_kernel(cs_ref, cp_ref, w_ref, b_ref, os_ref, op_ref, *, per_step):
    w = w_ref[...].astype(bf16)
    os_ref[...] = _dot(_silu(cs_ref[...]).astype(bf16), w) + b_ref[...]
    cp = jnp.broadcast_to(_silu(cp_ref[...]), (SUBLANES, D_MODEL)).astype(bf16)
    mod_p = _dot(cp, w)[0:1, :] + b_ref[...]
    for m in range(per_step):
        op_ref[m] = mod_p[:, m * D_MODEL:(m + 1) * D_MODEL]


def _ada(c_sample, c_prompt, w_ada, b_ada, *, per_step):
    n_s = c_sample.shape[0]
    assert c_prompt.shape[0] == 1 and N_MOD % per_step == 0
    cols = per_step * D_MODEL
    return pl.pallas_call(
        functools.partial(_ada_kernel, per_step=per_step),
        grid=(N_MOD // per_step,),
        in_specs=[
            pl.BlockSpec((n_s, D_MODEL), lambda j: (0, 0)),
            pl.BlockSpec((1, D_MODEL), lambda j: (0, 0)),
            pl.BlockSpec((D_MODEL, cols), lambda j: (0, j)),
            pl.BlockSpec((1, cols), lambda j: (0, j)),
        ],
        out_specs=[pl.BlockSpec((n_s, cols), lambda j: (0, j)),
                   pl.BlockSpec((per_step, 1, D_MODEL), lambda j: (j, 0, 0))],
        out_shape=[jax.ShapeDtypeStruct((n_s, N_MOD * D_MODEL), f32),
                   jax.ShapeDtypeStruct((N_MOD, 1, D_MODEL), f32)],
        compiler_params=pltpu.CompilerParams(
            dimension_semantics=("arbitrary",),
            vmem_limit_bytes=VMEM_LIMIT_SMALL),
        name="ada",
    )(c_sample, c_prompt, w_ada, b_ada.reshape(1, -1))


def _mixer_prompt_kernel(sinks_ref, x_ref, ada_ref, g_ref, w_in_ref, cw_ref, w_a_ref, w_b_ref, w_o_ref,
                         x1_ref, conv_ref, knew_ref, vnew_ref,
                         ubuf, klo, khi, vt, attbuf, *, tb):
    i = pl.program_id(0)

    @pl.when(i == 0)
    def _():
        ubuf[0:SUBLANES, :] = jnp.zeros((SUBLANES, D_CONV), f32)
        for r in (klo, khi):
            r[:, 0:WINDOW, :] = jnp.zeros((N_KV, WINDOW, LANES), bf16)
        vt[:, 0:WINDOW] = jnp.zeros((KV_COLS, WINDOW), bf16)

    x = x_ref[...]
    sh1, sc1, gt1 = ada_ref[0], ada_ref[1], ada_ref[2]
    h = (_rms(x, g_ref[...]) * (1.0 + sc1) + sh1).astype(bf16)

    def proj(c0, n):
        return _dot(h, w_in_ref[:, c0:c0 + n])

    k = proj(C_K, KV_COLS)
    v = proj(C_V, KV_COLS)
    u = proj(C_C, D_CONV) * proj(C_XIN, D_CONV)
    ubuf[SUBLANES:SUBLANES + tb, :] = u
    b_gate = proj(C_B, D_CONV)
    q = (proj(C_Q, N_HEADS * HEAD_DIM) * ATTN_SCALE).astype(bf16)
    conv = (cw_ref[0:1, :] * ubuf[SUBLANES - 2:SUBLANES - 2 + tb, :]
            + cw_ref[1:2, :] * ubuf[SUBLANES - 1:SUBLANES - 1 + tb, :]
            + cw_ref[2:3, :] * u)
    conv_in = (b_gate * conv).astype(bf16)
    knew_ref[...] = k[tb - WINDOW:, :].T
    vnew_ref[...] = v[tb - WINDOW:, :].T

    lo = lax.broadcasted_iota(jnp.int32, (tb, LANES), 1) < HEAD_DIM
    for j in range(N_KV // 2):
        pair = k[:, LANES * j:LANES * (j + 1)]
        rolled = pltpu.roll(pair, HEAD_DIM, axis=1)
        zero = jnp.zeros_like(pair)
        klo[2 * j, WINDOW:WINDOW + tb, :] = jnp.where(lo, pair, zero).astype(bf16)
        khi[2 * j, WINDOW:WINDOW + tb, :] = jnp.where(lo, zero, rolled).astype(bf16)
        klo[2 * j + 1, WINDOW:WINDOW + tb, :] = jnp.where(lo, rolled, zero).astype(bf16)
        khi[2 * j + 1, WINDOW:WINDOW + tb, :] = jnp.where(lo, zero, pair).astype(bf16)
    vt[:, WINDOW:WINDOW + tb] = v.T.astype(bf16)

    cc = lax.broadcasted_iota(jnp.int32, (Q_SUB + WINDOW, 2 * Q_SUB), 0)
    col = lax.broadcasted_iota(jnp.int32, (Q_SUB + WINDOW, 2 * Q_SUB), 1)
    rr = col & (Q_SUB - 1)
    band = (cc >= rr) & (cc <= rr + WINDOW)
    first_head = lax.broadcasted_iota(jnp.int32, (1, 2 * Q_SUB), 1) < Q_SUB
    mask0 = band & (cc >= jnp.where(i == 0, WINDOW, 0))

    def scores(j, g, t):
        r0 = Q_SUB * j
        qs = jnp.concatenate([q[r0:r0 + Q_SUB, 2 * LANES * g:2 * LANES * g + LANES],
                              q[r0:r0 + Q_SUB, 2 * LANES * g + LANES:2 * LANES * (g + 1)]], axis=0)
        kr = (klo, khi)[t]
        return lax.dot_general(kr[g, r0:r0 + Q_SUB + WINDOW, :], qs, (((1,), (1,)), ((), ())),
                               preferred_element_type=f32)

    def finish(j, g, t, st):
        r0 = Q_SUB * j
        h0, h1 = GROUP * g + t, GROUP * g + 2 + t
        sink = jnp.where(first_head, sinks_ref[h0], sinks_ref[h1])
        st = jnp.where(mask0 if j == 0 else band, st, -jnp.inf)
        m = jnp.maximum(jnp.max(st, axis=0, keepdims=True), sink)
        e = jnp.exp(st - m)
        linv = 1.0 / (jnp.sum(e, axis=0, keepdims=True) + jnp.exp(sink - m))
        vtg = vt[HEAD_DIM * g:HEAD_DIM * (g + 1), r0:r0 + Q_SUB + WINDOW]
        ot = _dot(vtg, e.astype(bf16)) * linv
        attbuf[HEAD_DIM * h0:HEAD_DIM * (h0 + 1), r0:r0 + Q_SUB] = ot[:, 0:Q_SUB]
        attbuf[HEAD_DIM * h1:HEAD_DIM * (h1 + 1), r0:r0 + Q_SUB] = ot[:, Q_SUB:2 * Q_SUB]

    chains = [(j, g, t) for j in range(tb // Q_SUB) for g in range(N_KV) for t in range(2)]
    fillers = ([functools.partial(proj, c, MXU_COLS) for c in range(C_GA, C_GB + D_MODEL, MXU_COLS)]
               + [(lambda c=c: _dot(conv_in, w_a_ref[:, c:c + MXU_COLS])) for c in range(0, D_MODEL, MXU_COLS)])
    fill_every = -(-len(chains) // len(fillers))
    filled = []
    pending = [scores(*c) for c in chains[:ATTN_LOOKAHEAD]]
    for n, chain in enumerate(chains):
        if n + ATTN_LOOKAHEAD < len(chains):
            pending.append(scores(*chains[n + ATTN_LOOKAHEAD]))
        if n % fill_every == 0:
            filled.extend(f() for f in fillers[len(filled):len(filled) + 1])
        finish(*chain, pending.pop(0))
    filled.extend(f() for f in fillers[len(filled):])
    ga, gb, ya = (jnp.concatenate(filled[j:j + D_MODEL // MXU_COLS], axis=1)
                  for j in range(0, len(filled), D_MODEL // MXU_COLS))

    yb = _dot(attbuf[...].T.astype(bf16), w_b_ref[...])
    mix = (jax.nn.sigmoid(ga) * ya + jax.nn.sigmoid(gb) * yb).astype(bf16)
    for r0 in range(0, tb, tb // 2):
        x1_ref[r0:r0 + tb // 2, :] = x[r0:r0 + tb // 2, :] + gt1 * _dot(mix[r0:r0 + tb // 2, :], w_o_ref[...])

    conv_ref[...] = ubuf[SUBLANES + tb - (CONV_W - 1):SUBLANES + tb, :]
    ubuf[0:SUBLANES, :] = ubuf[tb:tb + SUBLANES, :]
    for r in (klo, khi):
        r[:, 0:WINDOW, :] = r[:, tb:tb + WINDOW, :]
    vt[:, 0:WINDOW] = vt[:, tb:tb + WINDOW]


def _mixer_prompt(x, ada, g_mix, w_in, conv_w, sinks, w_a, w_b, w_o, *, tb):
    s = x.shape[0]
    assert s % tb == 0 and tb % Q_SUB == 0 and tb >= WINDOW
    kv_scratch = pltpu.VMEM((N_KV, WINDOW + tb, LANES), bf16)
    grid_spec = pltpu.PrefetchScalarGridSpec(
        num_scalar_prefetch=1,
        grid=(s // tb,),
        in_specs=[
            pl.BlockSpec((tb, D_MODEL), lambda i, sk: (i, 0)),
            pl.BlockSpec((N_MOD, 1, D_MODEL), lambda i, sk: (0, 0, 0)),
            pl.BlockSpec((1, D_MODEL), lambda i, sk: (0, 0)),
            pl.BlockSpec((D_MODEL, IN_COLS), lambda i, sk: (0, 0), pipeline_mode=pl.Buffered(1)),
            pl.BlockSpec((CONV_W, D_CONV), lambda i, sk: (0, 0)),
            pl.BlockSpec((D_CONV, D_MODEL), lambda i, sk: (0, 0), pipeline_mode=pl.Buffered(1)),
            pl.BlockSpec((N_HEADS * HEAD_DIM, D_MODEL), lambda i, sk: (0, 0), pipeline_mode=pl.Buffered(1)),
            pl.BlockSpec((D_MODEL, D_MODEL), lambda i, sk: (0, 0), pipeline_mode=pl.Buffered(1)),
        ],
        out_specs=[
            pl.BlockSpec((tb, D_MODEL), lambda i, sk: (i, 0)),
            pl.BlockSpec((CONV_W - 1, D_CONV), lambda i, sk: (0, 0)),
            pl.BlockSpec((KV_COLS, WINDOW), lambda i, sk: (0, 0)),
            pl.BlockSpec((KV_COLS, WINDOW), lambda i, sk: (0, 0)),
        ],
        scratch_shapes=[
            pltpu.VMEM((SUBLANES + tb, D_CONV), f32),
            kv_scratch, kv_scratch,
            pltpu.VMEM((KV_COLS, WINDOW + tb), bf16),
            pltpu.VMEM((N_HEADS * HEAD_DIM, tb), f32),
        ],
    )
    return pl.pallas_call(
        functools.partial(_mixer_prompt_kernel, tb=tb),
        grid_spec=grid_spec,
        out_shape=[
            jax.ShapeDtypeStruct((s, D_MODEL), f32),
            jax.ShapeDtypeStruct((CONV_W - 1, D_CONV), f32),
            jax.ShapeDtypeStruct((KV_COLS, WINDOW), f32),
            jax.ShapeDtypeStruct((KV_COLS, WINDOW), f32),
        ],
        compiler_params=pltpu.CompilerParams(
            dimension_semantics=("arbitrary",),
            vmem_limit_bytes=VMEM_LIMIT_LARGE),
        name="mixer_prompt",
    )(sinks, x, ada, g_mix, w_in, conv_w, w_a, w_b, w_o)


def _ffn_prompt_kernel(x_ref, ada_ref, g_ref, w_up_ref, fcw_ref, fcb_ref, w_down_ref, gf_ref,
                       y_ref, fst_ref, upbuf, actbuf, *, tb, ch, nw):
    i = pl.program_id(0)

    @pl.when(i == 0)
    def _():
        upbuf[:, 0:SUBLANES, :] = jnp.zeros((2 * D_FF // LANES, SUBLANES, LANES), f32)

    hb = tb // 2
    sh2, sc2, gt2 = ada_ref[3], ada_ref[4], ada_ref[5]
    h = (_rms(x_ref[...], g_ref[...]) * (1.0 + sc2) + sh2).astype(bf16)

    def up_cols(half, c0):
        up = _dot(h[half * hb:(half + 1) * hb, :], w_up_ref[:, c0:c0 + ch])
        for s in range(ch // LANES):
            upbuf[c0 // LANES + s, SUBLANES:SUBLANES + hb, :] = up[:, s * LANES:(s + 1) * LANES]
        return up

    def conv_cols(c0, up):
        pieces = []
        for s in range(ch // LANES):
            slab = c0 // LANES + s
            cols = slice(c0 + s * LANES, c0 + (s + 1) * LANES)
            pieces.append(fcw_ref[0:1, cols] * upbuf[slab, SUBLANES - 2:SUBLANES - 2 + hb, :]
                          + fcw_ref[1:2, cols] * upbuf[slab, SUBLANES - 1:SUBLANES - 1 + hb, :]
                          + fcw_ref[2:3, cols] * up[:, s * LANES:(s + 1) * LANES]
                          + fcb_ref[0:1, cols])
            upbuf[slab, 0:SUBLANES, :] = upbuf[slab, hb:hb + SUBLANES, :]
        return jnp.concatenate(pieces, axis=1)

    def down_cols(half, n0):
        return _dot(actbuf[half], w_down_ref[:, n0:n0 + nw])

    def finish(half, parts):
        rows = slice(half * hb, (half + 1) * hb)
        x2 = x_ref[rows, :] + gt2 * jnp.concatenate(parts, axis=1)
        y_ref[rows, :] = _rms(x2, gf_ref[...])

    chunks = list(range(0, D_FF, ch))
    down_starts = list(range(0, D_MODEL, nw))
    down_at = {len(chunks) * (k + 1) // (len(down_starts) + 1): n0 for k, n0 in enumerate(down_starts)}
    assert len(down_at) == len(down_starts)
    for half in range(2):
        parts = []
        pending = [(up_cols(half, c0), up_cols(half, D_FF + c0)) for c0 in chunks[:FFN_LOOKAHEAD]]
        for n, c0 in enumerate(chunks):
            if n + FFN_LOOKAHEAD < len(chunks):
                c1 = chunks[n + FFN_LOOKAHEAD]
                pending.append((up_cols(half, c1), up_cols(half, D_FF + c1)))
            if half == 1 and n in down_at:
                parts.append(down_cols(0, down_at[n]))
            up_g, up_v = pending.pop(0)
            actbuf[half, :, c0:c0 + ch] = (_silu(conv_cols(c0, up_g))
                                           * conv_cols(D_FF + c0, up_v)).astype(bf16)
        if half == 1:
            finish(0, parts)
    finish(1, [down_cols(1, n0) for n0 in down_starts])

    for slab in range(2 * D_FF // LANES):
        fst_ref[:, slab * LANES:(slab + 1) * LANES] = upbuf[slab, SUBLANES - (CONV_W - 1):SUBLANES, :]


def _ffn_prompt(x1, ada, g_ffn, w_up, fcw, fcb, w_down, g_final, *, tb, ch, nw):
    s = x1.shape[0]
    assert s % tb == 0 and tb % (4 * SUBLANES) == 0
    assert D_FF % ch == 0 and ch % LANES == 0 and D_MODEL % nw == 0 and nw % LANES == 0
    return pl.pallas_call(
        functools.partial(_ffn_prompt_kernel, tb=tb, ch=ch, nw=nw),
        grid=(s // tb,),
        in_specs=[
            pl.BlockSpec((tb, D_MODEL), lambda i: (i, 0)),
            pl.BlockSpec((N_MOD, 1, D_MODEL), lambda i: (0, 0, 0)),
            pl.BlockSpec((1, D_MODEL), lambda i: (0, 0)),
            _const_spec((D_MODEL, 2 * D_FF)),
            pl.BlockSpec((CONV_W, 2 * D_FF), lambda i: (0, 0)),
            pl.BlockSpec((1, 2 * D_FF), lambda i: (0, 0)),
            _const_spec((D_FF, D_MODEL)),
            pl.BlockSpec((1, D_MODEL), lambda i: (0, 0)),
        ],
        out_specs=[
            pl.BlockSpec((tb, D_MODEL), lambda i: (i, 0)),
            pl.BlockSpec((CONV_W - 1, 2 * D_FF), lambda i: (0, 0)),
        ],
        out_shape=[
            jax.ShapeDtypeStruct((s, D_MODEL), f32),
            jax.ShapeDtypeStruct((CONV_W - 1, 2 * D_FF), f32),
        ],
        scratch_shapes=[
            pltpu.VMEM((2 * D_FF // LANES, SUBLANES + tb // 2, LANES), f32),
            pltpu.VMEM((2, tb // 2, D_FF), bf16),
        ],
        compiler_params=pltpu.CompilerParams(
            dimension_semantics=("arbitrary",),
            vmem_limit_bytes=VMEM_LIMIT_LARGE),
        name="ffn_prompt",
    )(x1, ada, g_ffn, w_up, fcw, fcb, w_down, g_final)


def _mixer_decode_pre_kernel(x_ref, ada_ref, g_ref, w_in_ref, cw_ref, st_ref, w_a_ref,
                             w_in_bf_ref, w_a_bf_ref, q_ref, k_ref, v_ref, knt_ref, vnt_ref, za_ref, sgb_ref, stn_ref,
                             h_scr, proj_scr, *, chunk):
    j = pl.program_id(0)
    n_chunks = IN_COLS // chunk

    @pl.when(j == 0)
    def _():
        sh1 = ada_ref[:, 0:D_MODEL]
        sc1 = ada_ref[:, D_MODEL:2 * D_MODEL]
        h_scr[...] = (_rms(x_ref[:, 0, :], g_ref[...]) * (1.0 + sc1) + sh1).astype(bf16)

    w_chunk = w_in_ref[...].astype(bf16)
    w_in_bf_ref[...] = w_chunk
    part = _dot(h_scr[...], w_chunk)
    for c in range(n_chunks):
        @pl.when(j == c)
        def _(c=c):
            proj_scr[:, c * chunk:(c + 1) * chunk] = part

    @pl.when(j == n_chunks - 1)
    def _():
        def proj(c0, n):
            return proj_scr[:, c0:c0 + n]

        w_a = w_a_ref[...].astype(bf16)
        w_a_bf_ref[...] = w_a
        u = proj(C_C, D_CONV) * proj(C_XIN, D_CONV)
        prev0 = st_ref[:, 0, :]
        prev1 = st_ref[:, 1, :]
        conv = cw_ref[0:1, :] * prev0 + cw_ref[1:2, :] * prev1 + cw_ref[2:3, :] * u
        stn_ref[:, 0, :] = prev1
        stn_ref[:, 1, :] = u
        ya = _dot((proj(C_B, D_CONV) * conv).astype(bf16), w_a)
        q_ref[...] = proj(C_Q, N_HEADS * HEAD_DIM) * ATTN_SCALE
        k = proj(C_K, KV_COLS)
        v = proj(C_V, KV_COLS)
        k_ref[...] = k
        v_ref[...] = v
        knt_ref[...] = k.T
        vnt_ref[...] = v.T
        za_ref[...] = jax.nn.sigmoid(proj(C_GA, D_MODEL)) * ya
        sgb_ref[...] = jax.nn.sigmoid(proj(C_GB, D_MODEL))


def _mixer_decode_pre(x, ada, g_mix, w_in, conv_w, state, w_a, *, chunk):
    n = x.shape[0]
    assert IN_COLS % chunk == 0 and chunk % LANES == 0
    const2 = lambda j: (0, 0)
    const3 = lambda j: (0, 0, 0)
    row_block = lambda cols: pl.BlockSpec((n, cols), const2)
    state_block = pl.BlockSpec((n, CONV_W - 1, D_CONV), const3)
    shapes = [((D_MODEL, IN_COLS), bf16), ((D_CONV, D_MODEL), bf16),
              ((n, N_HEADS * HEAD_DIM), f32), ((n, KV_COLS), f32), ((n, KV_COLS), f32),
              ((KV_COLS, n), f32), ((KV_COLS, n), f32),
              ((n, D_MODEL), f32), ((n, D_MODEL), f32), ((n, CONV_W - 1, D_CONV), f32)]
    return pl.pallas_call(
        functools.partial(_mixer_decode_pre_kernel, chunk=chunk),
        grid=(IN_COLS // chunk,),
        in_specs=[pl.BlockSpec((n, 1, D_MODEL), const3), row_block(2 * D_MODEL), pl.BlockSpec((1, D_MODEL), const2),
                  pl.BlockSpec((D_MODEL, chunk), lambda j: (0, j)),
                  pl.BlockSpec((CONV_W, D_CONV), const2), state_block,
                  pl.BlockSpec((D_CONV, D_MODEL), const2)],
        out_specs=[pl.BlockSpec((D_MODEL, chunk), lambda j: (0, j)), pl.BlockSpec((D_CONV, D_MODEL), const2),
                   row_block(N_HEADS * HEAD_DIM), row_block(KV_COLS), row_block(KV_COLS),
                   pl.BlockSpec((KV_COLS, n), const2), pl.BlockSpec((KV_COLS, n), const2),
                   row_block(D_MODEL), row_block(D_MODEL), state_block],
        out_shape=[jax.ShapeDtypeStruct(shp, dt) for shp, dt in shapes],
        scratch_shapes=[pltpu.VMEM((n, D_MODEL), bf16), pltpu.VMEM((n, IN_COLS), f32)],
        compiler_params=pltpu.CompilerParams(
            dimension_semantics=("arbitrary",),
            vmem_limit_bytes=VMEM_LIMIT_MEDIUM),
        name="mixer_decode_pre",
    )(x, ada, g_mix, w_in, conv_w, state, w_a)


def _attn_decode_kernel(q_ref, kn_ref, vn_ref, knt_ref, vnt_ref, ck_ref, cv_ref, sink_ref,
                        att_ref, ok_ref, ov_ref, *, bb):
    step = pl.program_id(0)
    last = lax.broadcasted_iota(jnp.int32, (HEAD_DIM, WINDOW), 1) == WINDOW - 1
    to_front = (LANES - step * bb) % LANES
    kstep = [pltpu.roll(knt_ref[g], to_front, axis=1) for g in range(N_KV)]
    vstep = [pltpu.roll(vnt_ref[g], to_front, axis=1) for g in range(N_KV)]

    pairs = [(b, g) for b in range(bb) for g in range(N_KV)]
    n_stage = 7
    cuts = [len(pairs) * s // n_stage for s in range(n_stage + 1)]

    def shift_caches(stage):
        for b, g in pairs[cuts[stage]:cuts[stage + 1]]:
            ok_ref[b, g] = jnp.where(last, pltpu.roll(kstep[g], WINDOW - 1 - b, axis=1),
                                     pltpu.roll(ck_ref[b, g], WINDOW - 1, axis=1))
            ov_ref[b, g] = jnp.where(last, pltpu.roll(vstep[g], WINDOW - 1 - b, axis=1),
                                     pltpu.roll(cv_ref[b, g], WINDOW - 1, axis=1))

    sinks = [sink_ref[g][:, 0:1] for g in range(N_KV)]
    shift_caches(0)
    s_old = [_dot(q_ref[b, g].astype(bf16), ck_ref[b, g].astype(bf16)) for b, g in pairs]
    shift_caches(1)
    s_new = [jnp.sum(q_ref[b, g] * kn_ref[b, g], axis=-1, keepdims=True) for b, g in pairs]
    shift_caches(2)
    m = [jnp.maximum(jnp.maximum(jnp.max(so, axis=-1, keepdims=True), sn), sinks[g])
         for (b, g), so, sn in zip(pairs, s_old, s_new)]
    shift_caches(3)
    e_old = [jnp.exp(so - mm) for so, mm in zip(s_old, m)]
    e_new = [jnp.exp(sn - mm) for sn, mm in zip(s_new, m)]
    den = [jnp.sum(eo, axis=-1, keepdims=True) + en + jnp.exp(sinks[g] - mm)
           for (b, g), eo, en, mm in zip(pairs, e_old, e_new, m)]
    shift_caches(4)
    o_old = [lax.dot_general(eo.astype(bf16), cv_ref[b, g].astype(bf16), (((1,), (1,)), ((), ())),
                             preferred_element_type=f32) for (b, g), eo in zip(pairs, e_old)]
    shift_caches(5)
    for (b, g), oo, en, dd in zip(pairs, o_old, e_new, den):
        att_ref[b, g] = (oo + en * vn_ref[b, g]) / dd
    shift_caches(6)


def _attn_decode(q4, kn4, vn4, knt, vnt, ck, cv, sink_b, *, bb):
    n = q4.shape[0]
    assert n % bb == 0 and n == LANES and WINDOW == LANES
    cache_spec = pl.BlockSpec((bb, N_KV, HEAD_DIM, WINDOW), lambda b: (b, 0, 0, 0))
    row_spec = pl.BlockSpec((bb, N_KV, 1, HEAD_DIM), lambda b: (b, 0, 0, 0))
    q_spec = pl.BlockSpec((bb, N_KV, GROUP, HEAD_DIM), lambda b: (b, 0, 0, 0))
    new_t_spec = pl.BlockSpec((N_KV, HEAD_DIM, n), lambda b: (0, 0, 0))
    return pl.pallas_call(
        functools.partial(_attn_decode_kernel, bb=bb),
        grid=(n // bb,),
        in_specs=[q_spec, row_spec, row_spec, new_t_spec, new_t_spec, cache_spec, cache_spec,
                  pl.BlockSpec((N_KV, GROUP, LANES), lambda b: (0, 0, 0))],
        out_specs=[q_spec, cache_spec, cache_spec],
        out_shape=[jax.ShapeDtypeStruct((n, N_KV, GROUP, HEAD_DIM), f32),
                   jax.ShapeDtypeStruct((n, N_KV, HEAD_DIM, WINDOW), f32),
                   jax.ShapeDtypeStruct((n, N_KV, HEAD_DIM, WINDOW), f32)],
        compiler_params=pltpu.CompilerParams(dimension_semantics=("arbitrary",)),
        name="attn_decode",
    )(q4, kn4, vn4, knt, vnt, ck, cv, sink_b)


def _post_decode_kernel(x_ref, ada_ref, za_ref, sgb_ref, att_ref, w_b_ref, w_o_ref, g_ref, w_up_ref,
                        fcw_ref, fcb_ref, w_down_ref, gf_ref, fst_ref,
                        w_b_bf_ref, w_o_bf_ref, w_up_bf_ref, w_down_bf_ref, y_ref, fstn_ref,
                        yb_scr, x1_scr, h_scr, up_scr, act_scr, acc_scr, *, phases):
    j = pl.program_id(0)
    (b0, nb, cb), (o0, no, co), (u0, nu, cu), (d0, nd, cd) = phases

    def mod(k):
        return ada_ref[:, k * D_MODEL:(k + 1) * D_MODEL]

    for c in range(nb):
        @pl.when(j == b0 + c)
        def _(c=c):
            w = w_b_ref[...].astype(bf16)
            w_b_bf_ref[...] = w
            yb_scr[:, c * cb:(c + 1) * cb] = _dot(att_ref[...].astype(bf16), w)

    for c in range(no):
        @pl.when(j == o0 + c)
        def _(c=c):
            w = w_o_ref[...].astype(bf16)
            w_o_bf_ref[...] = w
            mix = (za_ref[...] + sgb_ref[...] * yb_scr[...]).astype(bf16)
            cols = slice(c * co, (c + 1) * co)
            x1_scr[:, cols] = x_ref[:, 0, cols] + mod(2)[:, cols] * _dot(mix, w)

    for c in range(nu):
        @pl.when(j == u0 + c)
        def _(c=c):
            if c == 0:
                h_scr[...] = (_rms(x1_scr[...], g_ref[...]) * (1.0 + mod(4)) + mod(3)).astype(bf16)
            w = w_up_ref[...].astype(bf16)
            w_up_bf_ref[...] = w
            up_scr[:, c * cu:(c + 1) * cu] = _dot(h_scr[...], w)

    for c in range(nd):
        @pl.when(j == d0 + c)
        def _(c=c):
            if c == 0:
                up = up_scr[...]
                prev0 = fst_ref[:, 0, :]
                prev1 = fst_ref[:, 1, :]
                conv = fcw_ref[0:1, :] * prev0 + fcw_ref[1:2, :] * prev1 + fcw_ref[2:3, :] * up + fcb_ref[...]
                fstn_ref[:, 0, :] = prev1
                fstn_ref[:, 1, :] = up
                act_scr[...] = (_silu(conv[:, 0:D_FF]) * conv[:, D_FF:2 * D_FF]).astype(bf16)
            w = w_down_ref[...].astype(bf16)
            w_down_bf_ref[...] = w
            cols = slice(c * cd, (c + 1) * cd)
            acc_scr[:, cols] = x1_scr[:, cols] + mod(5)[:, cols] * _dot(act_scr[...], w)
            if c == nd - 1:
                y_ref[:, 0, :] = _rms(acc_scr[...], gf_ref[...])


def _post_decode(x, ada, za, sgb, att, w_b, w_o, g_ffn, w_up, fcw, fcb, w_down, g_final, fstate,
                 *, proj_chunk, up_chunk, down_chunk):
    n = x.shape[0]
    assert D_MODEL % proj_chunk == 0 and (2 * D_FF) % up_chunk == 0 and D_MODEL % down_chunk == 0
    assert proj_chunk % LANES == 0 and up_chunk % LANES == 0 and down_chunk % LANES == 0
    nb = no = D_MODEL // proj_chunk
    nu, nd = 2 * D_FF // up_chunk, D_MODEL // down_chunk
    b0, o0, u0, d0 = 0, nb, nb + no, nb + no + nu
    phases = ((b0, nb, proj_chunk), (o0, no, proj_chunk), (u0, nu, up_chunk), (d0, nd, down_chunk))

    def chunk_index(start, count):
        return lambda j: jnp.clip(j - start, 0, count - 1)

    ib, io, iu, idn = (chunk_index(s0, cnt) for s0, cnt, _ in phases)
    const2 = lambda j: (0, 0)
    const3 = lambda j: (0, 0, 0)
    rows = lambda cols: pl.BlockSpec((n, cols), const2, pipeline_mode=pl.Buffered(1))
    state_block = pl.BlockSpec((n, CONV_W - 1, 2 * D_FF), const3, pipeline_mode=pl.Buffered(1))
    token_block = pl.BlockSpec((n, 1, D_MODEL), const3, pipeline_mode=pl.Buffered(1))
    col_chunk = lambda k, width, idx: pl.BlockSpec((k, width), lambda j: (0, idx(j)))
    w_down_block = col_chunk(D_FF, down_chunk, idn)
    return pl.pallas_call(
        functools.partial(_post_decode_kernel, phases=phases),
        grid=(d0 + nd,),
        in_specs=[token_block, rows(N_MOD * D_MODEL), rows(D_MODEL), rows(D_MODEL), rows(N_HEADS * HEAD_DIM),
                  col_chunk(N_HEADS * HEAD_DIM, proj_chunk, ib), col_chunk(D_MODEL, proj_chunk, io),
                  pl.BlockSpec((1, D_MODEL), const2), col_chunk(D_MODEL, up_chunk, iu),
                  pl.BlockSpec((CONV_W, 2 * D_FF), const2), pl.BlockSpec((1, 2 * D_FF), const2),
                  w_down_block, pl.BlockSpec((1, D_MODEL), const2), state_block],
        out_specs=[col_chunk(N_HEADS * HEAD_DIM, proj_chunk, ib), col_chunk(D_MODEL, proj_chunk, io),
                   col_chunk(D_MODEL, up_chunk, iu), w_down_block, token_block, state_block],
        out_shape=[jax.ShapeDtypeStruct((N_HEADS * HEAD_DIM, D_MODEL), bf16),
                   jax.ShapeDtypeStruct((D_MODEL, D_MODEL), bf16),
                   jax.ShapeDtypeStruct((D_MODEL, 2 * D_FF), bf16),
                   jax.ShapeDtypeStruct((D_FF, D_MODEL), bf16),
                   jax.ShapeDtypeStruct((n, 1, D_MODEL), f32),
                   jax.ShapeDtypeStruct((n, CONV_W - 1, 2 * D_FF), f32)],
        scratch_shapes=[pltpu.VMEM((n, D_MODEL), f32), pltpu.VMEM((n, D_MODEL), f32), pltpu.VMEM((n, D_MODEL), bf16),
                        pltpu.VMEM((n, 2 * D_FF), f32), pltpu.VMEM((n, D_FF), bf16), pltpu.VMEM((n, D_MODEL), f32)],
        compiler_params=pltpu.CompilerParams(
            dimension_semantics=("arbitrary",),
            vmem_limit_bytes=VMEM_LIMIT_LARGE),
        name="post_decode",
    )(x, ada, za, sgb, att, w_b, w_o, g_ffn, w_up, fcw, fcb, w_down, g_final, fstate)


PROMPT_MIXER_ROWS = 512
PROMPT_FFN_ROWS = 1024
PROMPT_FFN_COLS = 256
PROMPT_FFN_DOWN_COLS = 512
DECODE_ATTN_BATCH = 32
ADA_PER_STEP = 2
DECODE_IN_CHUNK = 1664
DECODE_PROJ_CHUNK = 512
DECODE_UP_CHUNK = 512
DECODE_DOWN_CHUNK = 256


def kernel(x_prompt, x_sample, c_prompt, c_sample, state_conv_a, cache_k_win, cache_v_win, state_ffn_conv, w_ada, b_ada, g_mix, w_in, conv_a_w, attn_sinks, w_a_out, w_b_out, w_o, g_ffn, w_up, ffn_conv_w, ffn_conv_b, w_down, g_final):
    depth = w_in.shape[0]
    n_p, seq, _ = x_prompt.shape
    n_s, t_s, _ = x_sample.shape
    assert n_p == 1 and t_s == 1, "one prompt sequence and single-token decode only"
    xp = x_prompt.reshape(seq, D_MODEL)
    xs = x_sample
    gf = g_final.reshape(1, D_MODEL)
    outs = [[] for _ in range(8)]
    for l in range(depth):
        ada_s, ada_p = _ada(c_sample, c_prompt, w_ada[l], b_ada[l], per_step=ADA_PER_STEP)
        gm, gn = g_mix[l].reshape(1, D_MODEL), g_ffn[l].reshape(1, D_MODEL)
        fcb = ffn_conv_b[l].reshape(1, 2 * D_FF)

        w_in_b, w_a_b, q, k_n, v_n, k_nt, v_nt, za, sgb, conv_s = _mixer_decode_pre(
            xs, ada_s, gm, w_in[l], conv_a_w[l], state_conv_a[l], w_a_out[l], chunk=DECODE_IN_CHUNK)
        to_native = lambda c: c.transpose(0, 2, 3, 1)
        from_native = lambda c: c.transpose(0, 3, 1, 2)
        sink_b = jnp.broadcast_to(attn_sinks[l].reshape(N_KV, GROUP, 1), (N_KV, GROUP, LANES))
        att4, k_s, v_s = _attn_decode(
            q.reshape(n_s, N_KV, GROUP, HEAD_DIM),
            k_n.reshape(n_s, N_KV, 1, HEAD_DIM), v_n.reshape(n_s, N_KV, 1, HEAD_DIM),
            k_nt.reshape(N_KV, HEAD_DIM, n_s), v_nt.reshape(N_KV, HEAD_DIM, n_s),
            to_native(cache_k_win[l]), to_native(cache_v_win[l]), sink_b, bb=DECODE_ATTN_BATCH)
        k_s, v_s = from_native(k_s), from_native(v_s)
        w_b_b, w_o_b, w_up_b, w_down_b, xs, ffn_s = _post_decode(
            xs, ada_s, za, sgb, att4.reshape(n_s, N_HEADS * HEAD_DIM), w_b_out[l], w_o[l], gn, w_up[l],
            ffn_conv_w[l], fcb, w_down[l], gf, state_ffn_conv[l],
            proj_chunk=DECODE_PROJ_CHUNK, up_chunk=DECODE_UP_CHUNK, down_chunk=DECODE_DOWN_CHUNK)

        x1, conv_p, k_p, v_p = _mixer_prompt(xp, ada_p, gm, w_in_b, conv_a_w[l], attn_sinks[l],
                                             w_a_b, w_b_b, w_o_b, tb=PROMPT_MIXER_ROWS)
        xp, ffn_p = _ffn_prompt(x1, ada_p, gn, w_up_b, ffn_conv_w[l], fcb, w_down_b, gf,
                                tb=PROMPT_FFN_ROWS, ch=PROMPT_FFN_COLS, nw=PROMPT_FFN_DOWN_COLS)

        for lst, val in zip(outs, (
                conv_p.reshape(n_p, CONV_W - 1, D_CONV), conv_s,
                from_native(k_p.reshape(n_p, N_KV, HEAD_DIM, WINDOW)), k_s,
                from_native(v_p.reshape(n_p, N_KV, HEAD_DIM, WINDOW)), v_s,
                ffn_p.reshape(n_p, CONV_W - 1, 2 * D_FF), ffn_s)):
            lst.append(val)
    assert depth == 1, "final RMSNorm is fused into the single layer's FFN kernels"
    return (xp.reshape(n_p, seq, D_MODEL), xs) + tuple(jnp.stack(o) for o in outs)
```

```python
import functools

import jax
import jax.numpy as jnp
from jax import lax
from jax.experimental import pallas as pl
from jax.experimental.pallas import tpu as pltpu

f32 = jnp.float32
bf16 = jnp.bfloat16

D_MODEL = 1024
D_CONV = D_MODEL
CONV_W = 3
N_HEADS = 16
N_KV = 4
GROUP = N_HEADS // N_KV
HEAD_DIM = 64
WINDOW = 128
D_FF = 2816
EPS = 1e-6
N_MOD = 6
ATTN_SCALE = HEAD_DIM ** -0.5
KV_COLS = N_KV * HEAD_DIM
C_XIN, C_B, C_C = 0, D_CONV, 2 * D_CONV
C_Q = 3 * D_CONV
C_K = C_Q + N_HEADS * HEAD_DIM
C_V = C_K + KV_COLS
C_GA = C_V + KV_COLS
C_GB = C_GA + D_MODEL
IN_COLS = C_GB + D_MODEL

LANES = 128
SUBLANES = 8
Q_SUB = 128
ATTN_LOOKAHEAD = 3
FFN_LOOKAHEAD = 2
MXU_COLS = 256
VMEM_BYTES_V7X = 64 * 1024 * 1024
VMEM_LIMIT_LARGE = VMEM_BYTES_V7X * 7 // 8
VMEM_LIMIT_MEDIUM = VMEM_BYTES_V7X * 5 // 8
VMEM_LIMIT_SMALL = VMEM_BYTES_V7X // 2


def _rms(x, g):
    ms = jnp.mean(x * x, axis=-1, keepdims=True)
    return x * lax.rsqrt(ms + EPS) * g


def _silu(x):
    return x * jax.nn.sigmoid(x)


def _dot(a, b):
    return jnp.dot(a, b, preferred_element_type=f32)


def _const_spec(shape):
    nd = len(shape)
    return pl.BlockSpec(shape, lambda i: (0,) * nd, pipeline_mode=pl.Buffered(1))


def _ada_kernel(cs_ref, cp_ref, w_ref, b_ref, os_ref, op_ref, *, per_step):
    w = w_ref[...].astype(bf16)
    os_ref[...] = _dot(_silu(cs_ref[...]).astype(bf16), w) + b_ref[...]
    cp = jnp.broadcast_to(_silu(cp_ref[...]), (SUBLANES, D_MODEL)).astype(bf16)
    mod_p = _dot(cp, w)[0:1, :] + b_ref[...]
    for m in range(per_step):
        op_ref[m] = mod_p[:, m * D_MODEL:(m + 1) * D_MODEL]


def _ada(c_sample, c_prompt, w_ada, b_ada, *, per_step):
    n_s = c_sample.shape[0]
    assert c_prompt.shape[0] == 1 and N_MOD % per_step == 0
    cols = per_step * D_MODEL
    return pl.pallas_call(
        functools.partial(_ada_kernel, per_step=per_step),
        grid=(N_MOD // per_step,),
        in_specs=[
            pl.BlockSpec((n_s, D_MODEL), lambda j: (0, 0)),
            pl.BlockSpec((1, D_MODEL), lambda j: (0, 0)),
            pl.BlockSpec((D_MODEL, cols), lambda j: (0, j)),
            pl.BlockSpec((1, cols), lambda j: (0, j)),
        ],
        out_specs=[pl.BlockSpec((n_s, cols), lambda j: (0, j)),
                   pl.BlockSpec((per_step, 1, D_MODEL), lambda j: (j, 0, 0))],
        out_shape=[jax.ShapeDtypeStruct((n_s, N_MOD * D_MODEL), f32),
                   jax.ShapeDtypeStruct((N_MOD, 1, D_MODEL), f32)],
        compiler_params=pltpu.CompilerParams(
            dimension_semantics=("arbitrary",),
            vmem_limit_bytes=VMEM_LIMIT_SMALL),
        name="ada",
    )(c_sample, c_prompt, w_ada, b_ada.reshape(1, -1))


def _mixer_prompt_kernel(sinks_ref, x_ref, xn_ref, ada_ref, g_ref, w_in_ref, cw_ref, w_a_ref, w_b_ref, w_o_ref,
                         x1_ref, conv_ref, knew_ref, vnew_ref,
                         ubuf, klo, khi, vt, attbuf, h_scr, kv_scr, *, tb):
    i = pl.program_id(0)

    @pl.when(i == 0)
    def _():
        ubuf[0:SUBLANES, :] = jnp.zeros((SUBLANES, D_CONV), f32)
        for r in (klo, khi):
            r[:, 0:WINDOW, :] = jnp.zeros((N_KV, WINDOW, LANES), bf16)
        vt[:, 0:WINDOW] = jnp.zeros((KV_COLS, WINDOW), bf16)

    x = x_ref[...]
    sh1, sc1, gt1 = ada_ref[0], ada_ref[1], ada_ref[2]

    def norm_mod(xb):
        return (_rms(xb, g_ref[...]) * (1.0 + sc1) + sh1).astype(bf16)

    def kv_proj(hb):
        return _dot(hb, w_in_ref[:, C_K:C_K + 2 * KV_COLS])

    @pl.when(i == 0)
    def _():
        h_scr[0] = norm_mod(x)
        kv_scr[0] = kv_proj(h_scr[0])

    h = h_scr[i % 2]
    kv = kv_scr[i % 2]

    def look_ahead():
        h_next = norm_mod(xn_ref[...])
        h_scr[(i + 1) % 2] = h_next
        kv_scr[(i + 1) % 2] = kv_proj(h_next)

    def proj(c0, n):
        return _dot(h, w_in_ref[:, c0:c0 + n])

    k = kv[:, 0:KV_COLS]
    v = kv[:, KV_COLS:2 * KV_COLS]
    u = proj(C_C, D_CONV) * proj(C_XIN, D_CONV)
    ubuf[SUBLANES:SUBLANES + tb, :] = u
    b_gate = proj(C_B, D_CONV)
    q = (proj(C_Q, N_HEADS * HEAD_DIM) * ATTN_SCALE).astype(bf16)
    conv = (cw_ref[0:1, :] * ubuf[SUBLANES - 2:SUBLANES - 2 + tb, :]
            + cw_ref[1:2, :] * ubuf[SUBLANES - 1:SUBLANES - 1 + tb, :]
            + cw_ref[2:3, :] * u)
    conv_in = (b_gate * conv).astype(bf16)
    knew_ref[...] = k[tb - WINDOW:, :].T
    vnew_ref[...] = v[tb - WINDOW:, :].T

    lo = lax.broadcasted_iota(jnp.int32, (tb, LANES), 1) < HEAD_DIM
    for j in range(N_KV // 2):
        pair = k[:, LANES * j:LANES * (j + 1)]
        rolled = pltpu.roll(pair, HEAD_DIM, axis=1)
        zero = jnp.zeros_like(pair)
        klo[2 * j, WINDOW:WINDOW + tb, :] = jnp.where(lo, pair, zero).astype(bf16)
        khi[2 * j, WINDOW:WINDOW + tb, :] = jnp.where(lo, zero, rolled).astype(bf16)
        klo[2 * j + 1, WINDOW:WINDOW + tb, :] = jnp.where(lo, rolled, zero).astype(bf16)
        khi[2 * j + 1, WINDOW:WINDOW + tb, :] = jnp.where(lo, zero, pair).astype(bf16)
    vt[:, WINDOW:WINDOW + tb] = v.T.astype(bf16)

    cc = lax.broadcasted_iota(jnp.int32, (Q_SUB + WINDOW, 2 * Q_SUB), 0)
    col = lax.broadcasted_iota(jnp.int32, (Q_SUB + WINDOW, 2 * Q_SUB), 1)
    rr = col & (Q_SUB - 1)
    band = (cc >= rr) & (cc <= rr + WINDOW)
    first_head = lax.broadcasted_iota(jnp.int32, (1, 2 * Q_SUB), 1) < Q_SUB
    mask0 = band & (cc >= jnp.where(i == 0, WINDOW, 0))

    def scores(j, g, t):
        r0 = Q_SUB * j
        qs = jnp.concatenate([q[r0:r0 + Q_SUB, 2 * LANES * g:2 * LANES * g + LANES],
                              q[r0:r0 + Q_SUB, 2 * LANES * g + LANES:2 * LANES * (g + 1)]], axis=0)
        kr = (klo, khi)[t]
        return lax.dot_general(kr[g, r0:r0 + Q_SUB + WINDOW, :], qs, (((1,), (1,)), ((), ())),
                               preferred_element_type=f32)

    def finish(j, g, t, st):
        r0 = Q_SUB * j
        h0, h1 = GROUP * g + t, GROUP * g + 2 + t
        sink = jnp.where(first_head, sinks_ref[h0], sinks_ref[h1])
        st = jnp.where(mask0 if j == 0 else band, st, -jnp.inf)
        m = jnp.maximum(jnp.max(st, axis=0, keepdims=True), sink)
        e = jnp.exp(st - m)
        linv = 1.0 / (jnp.sum(e, axis=0, keepdims=True) + jnp.exp(sink - m))
        vtg = vt[HEAD_DIM * g:HEAD_DIM * (g + 1), r0:r0 + Q_SUB + WINDOW]
        ot = _dot(vtg, e.astype(bf16)) * linv
        attbuf[HEAD_DIM * h0:HEAD_DIM * (h0 + 1), r0:r0 + Q_SUB] = ot[:, 0:Q_SUB]
        attbuf[HEAD_DIM * h1:HEAD_DIM * (h1 + 1), r0:r0 + Q_SUB] = ot[:, Q_SUB:2 * Q_SUB]

    chains = [(j, g, t) for j in range(tb // Q_SUB) for g in range(N_KV) for t in range(2)]
    fillers = ([functools.partial(proj, c, MXU_COLS) for c in range(C_GA, C_GB + D_MODEL, MXU_COLS)]
               + [(lambda c=c: _dot(conv_in, w_a_ref[:, c:c + MXU_COLS])) for c in range(0, D_MODEL, MXU_COLS)])
    fill_every = -(-len(chains) // len(fillers))
    filled = []
    pending = [scores(*c) for c in chains[:ATTN_LOOKAHEAD]]
    for n, chain in enumerate(chains):
        if n + ATTN_LOOKAHEAD < len(chains):
            pending.append(scores(*chains[n + ATTN_LOOKAHEAD]))
        if n % fill_every == 0:
            filled.extend(f() for f in fillers[len(filled):len(filled) + 1])
        finish(*chain, pending.pop(0))
    filled.extend(f() for f in fillers[len(filled):])
    ga, gb, ya = (jnp.concatenate(filled[j:j + D_MODEL // MXU_COLS], axis=1)
                  for j in range(0, len(filled), D_MODEL // MXU_COLS))

    yb = _dot(attbuf[...].T.astype(bf16), w_b_ref[...])
    look_ahead()
    mix = (jax.nn.sigmoid(ga) * ya + jax.nn.sigmoid(gb) * yb).astype(bf16)
    for r0 in range(0, tb, tb // 2):
        x1_ref[r0:r0 + tb // 2, :] = x[r0:r0 + tb // 2, :] + gt1 * _dot(mix[r0:r0 + tb // 2, :], w_o_ref[...])

    conv_ref[...] = ubuf[SUBLANES + tb - (CONV_W - 1):SUBLANES + tb, :]
    ubuf[0:SUBLANES, :] = ubuf[tb:tb + SUBLANES, :]
    for r in (klo, khi):
        r[:, 0:WINDOW, :] = r[:, tb:tb + WINDOW, :]
    vt[:, 0:WINDOW] = vt[:, tb:tb + WINDOW]


def _mixer_prompt(x, ada, g_mix, w_in, conv_w, sinks, w_a, w_b, w_o, *, tb):
    s = x.shape[0]
    assert s % tb == 0 and tb % Q_SUB == 0 and tb >= WINDOW
    kv_scratch = pltpu.VMEM((N_KV, WINDOW + tb, LANES), bf16)
    grid_spec = pltpu.PrefetchScalarGridSpec(
        num_scalar_prefetch=1,
        grid=(s // tb,),
        in_specs=[
            pl.BlockSpec((tb, D_MODEL), lambda i, sk: (i, 0)),
            pl.BlockSpec((tb, D_MODEL), lambda i, sk: (jnp.minimum(i + 1, s // tb - 1), 0)),
            pl.BlockSpec((N_MOD, 1, D_MODEL), lambda i, sk: (0, 0, 0)),
            pl.BlockSpec((1, D_MODEL), lambda i, sk: (0, 0)),
            pl.BlockSpec((D_MODEL, IN_COLS), lambda i, sk: (0, 0), pipeline_mode=pl.Buffered(1)),
            pl.BlockSpec((CONV_W, D_CONV), lambda i, sk: (0, 0)),
            pl.BlockSpec((D_CONV, D_MODEL), lambda i, sk: (0, 0), pipeline_mode=pl.Buffered(1)),
            pl.BlockSpec((N_HEADS * HEAD_DIM, D_MODEL), lambda i, sk: (0, 0), pipeline_mode=pl.Buffered(1)),
            pl.BlockSpec((D_MODEL, D_MODEL), lambda i, sk: (0, 0), pipeline_mode=pl.Buffered(1)),
        ],
        out_specs=[
            pl.BlockSpec((tb, D_MODEL), lambda i, sk: (i, 0)),
            pl.BlockSpec((CONV_W - 1, D_CONV), lambda i, sk: (0, 0)),
            pl.BlockSpec((KV_COLS, WINDOW), lambda i, sk: (0, 0)),
            pl.BlockSpec((KV_COLS, WINDOW), lambda i, sk: (0, 0)),
        ],
        scratch_shapes=[
            pltpu.VMEM((SUBLANES + tb, D_CONV), f32),
            kv_scratch, kv_scratch,
            pltpu.VMEM((KV_COLS, WINDOW + tb), bf16),
            pltpu.VMEM((N_HEADS * HEAD_DIM, tb), f32),
            pltpu.VMEM((2, tb, D_MODEL), bf16),
            pltpu.VMEM((2, tb, 2 * KV_COLS), f32),
        ],
    )
    return pl.pallas_call(
        functools.partial(_mixer_prompt_kernel, tb=tb),
        grid_spec=grid_spec,
        out_shape=[
            jax.ShapeDtypeStruct((s, D_MODEL), f32),
            jax.ShapeDtypeStruct((CONV_W - 1, D_CONV), f32),
            jax.ShapeDtypeStruct((KV_COLS, WINDOW), f32),
            jax.ShapeDtypeStruct((KV_COLS, WINDOW), f32),
        ],
        compiler_params=pltpu.CompilerParams(
            dimension_semantics=("arbitrary",),
            vmem_limit_bytes=VMEM_LIMIT_LARGE),
        name="mixer_prompt",
    )(sinks, x, x, ada, g_mix, w_in, conv_w, w_a, w_b, w_o)


def _ffn_prompt_kernel(x_ref, ada_ref, g_ref, w_up_ref, fcw_ref, fcb_ref, w_down_ref, gf_ref,
                       y_ref, fst_ref, upbuf, actbuf, *, tb, ch, nw):
    i = pl.program_id(0)

    @pl.when(i == 0)
    def _():
        upbuf[:, 0:SUBLANES, :] = jnp.zeros((2 * D_FF // LANES, SUBLANES, LANES), f32)

    hb = tb // 2
    sh2, sc2, gt2 = ada_ref[3], ada_ref[4], ada_ref[5]
    h = (_rms(x_ref[...], g_ref[...]) * (1.0 + sc2) + sh2).astype(bf16)

    def up_cols(half, c0):
        up = _dot(h[half * hb:(half + 1) * hb, :], w_up_ref[:, c0:c0 + ch])
        for s in range(ch // LANES):
            upbuf[c0 // LANES + s, SUBLANES:SUBLANES + hb, :] = up[:, s * LANES:(s + 1) * LANES]
        return up

    def conv_cols(c0, up):
        pieces = []
        for s in range(ch // LANES):
            slab = c0 // LANES + s
            cols = slice(c0 + s * LANES, c0 + (s + 1) * LANES)
            pieces.append(fcw_ref[0:1, cols] * upbuf[slab, SUBLANES - 2:SUBLANES - 2 + hb, :]
                          + fcw_ref[1:2, cols] * upbuf[slab, SUBLANES - 1:SUBLANES - 1 + hb, :]
                          + fcw_ref[2:3, cols] * up[:, s * LANES:(s + 1) * LANES]
                          + fcb_ref[0:1, cols])
            upbuf[slab, 0:SUBLANES, :] = upbuf[slab, hb:hb + SUBLANES, :]
        return jnp.concatenate(pieces, axis=1)

    def down_cols(half, n0):
        return _dot(actbuf[half], w_down_ref[:, n0:n0 + nw])

    def finish(half, parts):
        rows = slice(half * hb, (half + 1) * hb)
        x2 = x_ref[rows, :] + gt2 * jnp.concatenate(parts, axis=1)
        y_ref[rows, :] = _rms(x2, gf_ref[...])

    chunks = list(range(0, D_FF, ch))
    down_starts = list(range(0, D_MODEL, nw))
    down_at = {len(chunks) * (k + 1) // (len(down_starts) + 1): n0 for k, n0 in enumerate(down_starts)}
    assert len(down_at) == len(down_starts)
    for half in range(2):
        parts = []
        pending = [(up_cols(half, c0), up_cols(half, D_FF + c0)) for c0 in chunks[:FFN_LOOKAHEAD]]
        for n, c0 in enumerate(chunks):
            if n + FFN_LOOKAHEAD < len(chunks):
                c1 = chunks[n + FFN_LOOKAHEAD]
                pending.append((up_cols(half, c1), up_cols(half, D_FF + c1)))
            if half == 1 and n in down_at:
                parts.append(down_cols(0, down_at[n]))
            up_g, up_v = pending.pop(0)
            actbuf[half, :, c0:c0 + ch] = (_silu(conv_cols(c0, up_g))
                                           * conv_cols(D_FF + c0, up_v)).astype(bf16)
        if half == 1:
            finish(0, parts)
    finish(1, [down_cols(1, n0) for n0 in down_starts])

    for slab in range(2 * D_FF // LANES):
        fst_ref[:, slab * LANES:(slab + 1) * LANES] = upbuf[slab, SUBLANES - (CONV_W - 1):SUBLANES, :]


def _ffn_prompt(x1, ada, g_ffn, w_up, fcw, fcb, w_down, g_final, *, tb, ch, nw):
    s = x1.shape[0]
    assert s % tb == 0 and tb % (4 * SUBLANES) == 0
    assert D_FF % ch == 0 and ch % LANES == 0 and D_MODEL % nw == 0 and nw % LANES == 0
    return pl.pallas_call(
        functools.partial(_ffn_prompt_kernel, tb=tb, ch=ch, nw=nw),
        grid=(s // tb,),
        in_specs=[
            pl.BlockSpec((tb, D_MODEL), lambda i: (i, 0)),
            pl.BlockSpec((N_MOD, 1, D_MODEL), lambda i: (0, 0, 0)),
            pl.BlockSpec((1, D_MODEL), lambda i: (0, 0)),
            _const_spec((D_MODEL, 2 * D_FF)),
            pl.BlockSpec((CONV_W, 2 * D_FF), lambda i: (0, 0)),
            pl.BlockSpec((1, 2 * D_FF), lambda i: (0, 0)),
            _const_spec((D_FF, D_MODEL)),
            pl.BlockSpec((1, D_MODEL), lambda i: (0, 0)),
        ],
        out_specs=[
            pl.BlockSpec((tb, D_MODEL), lambda i: (i, 0)),
            pl.BlockSpec((CONV_W - 1, 2 * D_FF), lambda i: (0, 0)),
        ],
        out_shape=[
            jax.ShapeDtypeStruct((s, D_MODEL), f32),
            jax.ShapeDtypeStruct((CONV_W - 1, 2 * D_FF), f32),
        ],
        scratch_shapes=[
            pltpu.VMEM((2 * D_FF // LANES, SUBLANES + tb // 2, LANES), f32),
            pltpu.VMEM((2, tb // 2, D_FF), bf16),
        ],
        compiler_params=pltpu.CompilerParams(
            dimension_semantics=("arbitrary",),
            vmem_limit_bytes=VMEM_LIMIT_LARGE),
        name="ffn_prompt",
    )(x1, ada, g_ffn, w_up, fcw, fcb, w_down, g_final)


def _mixer_decode_pre_kernel(x_ref, ada_ref, g_ref, w_in_ref, cw_ref, st_ref, w_a_ref,
                             w_in_bf_ref, w_a_bf_ref, q_ref, k_ref, v_ref, knt_ref, vnt_ref, za_ref, sgb_ref, stn_ref,
                             h_scr, proj_scr, *, chunk):
    j = pl.program_id(0)
    n_chunks = IN_COLS // chunk

    @pl.when(j == 0)
    def _():
        sh1 = ada_ref[:, 0:D_MODEL]
        sc1 = ada_ref[:, D_MODEL:2 * D_MODEL]
        h_scr[...] = (_rms(x_ref[:, 0, :], g_ref[...]) * (1.0 + sc1) + sh1).astype(bf16)

    w_chunk = w_in_ref[...].astype(bf16)
    w_in_bf_ref[...] = w_chunk
    part = _dot(h_scr[...], w_chunk)
    for c in range(n_chunks):
        @pl.when(j == c)
        def _(c=c):
            proj_scr[:, c * chunk:(c + 1) * chunk] = part

    @pl.when(j == n_chunks - 1)
    def _():
        def proj(c0, n):
            return proj_scr[:, c0:c0 + n]

        w_a = w_a_ref[...].astype(bf16)
        w_a_bf_ref[...] = w_a
        u = proj(C_C, D_CONV) * proj(C_XIN, D_CONV)
        prev0 = st_ref[:, 0, :]
        prev1 = st_ref[:, 1, :]
        conv = cw_ref[0:1, :] * prev0 + cw_ref[1:2, :] * prev1 + cw_ref[2:3, :] * u
        stn_ref[:, 0, :] = prev1
        stn_ref[:, 1, :] = u
        ya = _dot((proj(C_B, D_CONV) * conv).astype(bf16), w_a)
        q_ref[...] = proj(C_Q, N_HEADS * HEAD_DIM) * ATTN_SCALE
        k = proj(C_K, KV_COLS)
        v = proj(C_V, KV_COLS)
        k_ref[...] = k
        v_ref[...] = v
        knt_ref[...] = k.T
        vnt_ref[...] = v.T
        za_ref[...] = jax.nn.sigmoid(proj(C_GA, D_MODEL)) * ya
        sgb_ref[...] = jax.nn.sigmoid(proj(C_GB, D_MODEL))


def _mixer_decode_pre(x, ada, g_mix, w_in, conv_w, state, w_a, *, chunk):
    n = x.shape[0]
    assert IN_COLS % chunk == 0 and chunk % LANES == 0
    const2 = lambda j: (0, 0)
    const3 = lambda j: (0, 0, 0)
    row_block = lambda cols: pl.BlockSpec((n, cols), const2)
    state_block = pl.BlockSpec((n, CONV_W - 1, D_CONV), const3)
    shapes = [((D_MODEL, IN_COLS), bf16), ((D_CONV, D_MODEL), bf16),
              ((n, N_HEADS * HEAD_DIM), f32), ((n, KV_COLS), f32), ((n, KV_COLS), f32),
              ((KV_COLS, n), f32), ((KV_COLS, n), f32),
              ((n, D_MODEL), f32), ((n, D_MODEL), f32), ((n, CONV_W - 1, D_CONV), f32)]
    return pl.pallas_call(
        functools.partial(_mixer_decode_pre_kernel, chunk=chunk),
        grid=(IN_COLS // chunk,),
        in_specs=[pl.BlockSpec((n, 1, D_MODEL), const3), row_block(2 * D_MODEL), pl.BlockSpec((1, D_MODEL), const2),
                  pl.BlockSpec((D_MODEL, chunk), lambda j: (0, j)),
                  pl.BlockSpec((CONV_W, D_CONV), const2), state_block,
                  pl.BlockSpec((D_CONV, D_MODEL), const2)],
        out_specs=[pl.BlockSpec((D_MODEL, chunk), lambda j: (0, j)), pl.BlockSpec((D_CONV, D_MODEL), const2),
                   row_block(N_HEADS * HEAD_DIM), row_block(KV_COLS), row_block(KV_COLS),
                   pl.BlockSpec((KV_COLS, n), const2), pl.BlockSpec((KV_COLS, n), const2),
                   row_block(D_MODEL), row_block(D_MODEL), state_block],
        out_shape=[jax.ShapeDtypeStruct(shp, dt) for shp, dt in shapes],
        scratch_shapes=[pltpu.VMEM((n, D_MODEL), bf16), pltpu.VMEM((n, IN_COLS), f32)],
        compiler_params=pltpu.CompilerParams(
            dimension_semantics=("arbitrary",),
            vmem_limit_bytes=VMEM_LIMIT_MEDIUM),
        name="mixer_decode_pre",
    )(x, ada, g_mix, w_in, conv_w, state, w_a)


def _attn_decode_kernel(q_ref, kn_ref, vn_ref, knt_ref, vnt_ref, ck_ref, cv_ref, sink_ref,
                        att_ref, ok_ref, ov_ref, *, bb):
    step = pl.program_id(0)
    last = lax.broadcasted_iota(jnp.int32, (HEAD_DIM, WINDOW), 1) == WINDOW - 1
    to_front = (LANES - step * bb) % LANES
    kstep = [pltpu.roll(knt_ref[g], to_front, axis=1) for g in range(N_KV)]
    vstep = [pltpu.roll(vnt_ref[g], to_front, axis=1) for g in range(N_KV)]

    pairs = [(b, g) for b in range(bb) for g in range(N_KV)]
    n_stage = 7
    cuts = [len(pairs) * s // n_stage for s in range(n_stage + 1)]

    def shift_caches(stage):
        for b, g in pairs[cuts[stage]:cuts[stage + 1]]:
            ok_ref[b, g] = jnp.where(last, pltpu.roll(kstep[g], WINDOW - 1 - b, axis=1),
                                     pltpu.roll(ck_ref[b, g], WINDOW - 1, axis=1))
            ov_ref[b, g] = jnp.where(last, pltpu.roll(vstep[g], WINDOW - 1 - b, axis=1),
                                     pltpu.roll(cv_ref[b, g], WINDOW - 1, axis=1))

    sinks = [sink_ref[g][:, 0:1] for g in range(N_KV)]
    shift_caches(0)
    s_old = [_dot(q_ref[b, g].astype(bf16), ck_ref[b, g].astype(bf16)) for b, g in pairs]
    shift_caches(1)
    s_new = [jnp.sum(q_ref[b, g] * kn_ref[b, g], axis=-1, keepdims=True) for b, g in pairs]
    shift_caches(2)
    m = [jnp.maximum(jnp.maximum(jnp.max(so, axis=-1, keepdims=True), sn), sinks[g])
         for (b, g), so, sn in zip(pairs, s_old, s_new)]
    shift_caches(3)
    e_old = [jnp.exp(so - mm) for so, mm in zip(s_old, m)]
    e_new = [jnp.exp(sn - mm) for sn, mm in zip(s_new, m)]
    den = [jnp.sum(eo, axis=-1, keepdims=True) + en + jnp.exp(sinks[g] - mm)
           for (b, g), eo, en, mm in zip(pairs, e_old, e_new, m)]
    shift_caches(4)
    o_old = [lax.dot_general(eo.astype(bf16), cv_ref[b, g].astype(bf16), (((1,), (1,)), ((), ())),
                             preferred_element_type=f32) for (b, g), eo in zip(pairs, e_old)]
    shift_caches(5)
    for (b, g), oo, en, dd in zip(pairs, o_old, e_new, den):
        att_ref[b, g] = (oo + en * vn_ref[b, g]) / dd
    shift_caches(6)


def _attn_decode(q4, kn4, vn4, knt, vnt, ck, cv, sink_b, *, bb):
    n = q4.shape[0]
    assert n % bb == 0 and n == LANES and WINDOW == LANES
    cache_spec = pl.BlockSpec((bb, N_KV, HEAD_DIM, WINDOW), lambda b: (b, 0, 0, 0))
    row_spec = pl.BlockSpec((bb, N_KV, 1, HEAD_DIM), lambda b: (b, 0, 0, 0))
    q_spec = pl.BlockSpec((bb, N_KV, GROUP, HEAD_DIM), lambda b: (b, 0, 0, 0))
    new_t_spec = pl.BlockSpec((N_KV, HEAD_DIM, n), lambda b: (0, 0, 0))
    return pl.pallas_call(
        functools.partial(_attn_decode_kernel, bb=bb),
        grid=(n // bb,),
        in_specs=[q_spec, row_spec, row_spec, new_t_spec, new_t_spec, cache_spec, cache_spec,
                  pl.BlockSpec((N_KV, GROUP, LANES), lambda b: (0, 0, 0))],
        out_specs=[q_spec, cache_spec, cache_spec],
        out_shape=[jax.ShapeDtypeStruct((n, N_KV, GROUP, HEAD_DIM), f32),
                   jax.ShapeDtypeStruct((n, N_KV, HEAD_DIM, WINDOW), f32),
                   jax.ShapeDtypeStruct((n, N_KV, HEAD_DIM, WINDOW), f32)],
        compiler_params=pltpu.CompilerParams(dimension_semantics=("arbitrary",)),
        name="attn_decode",
    )(q4, kn4, vn4, knt, vnt, ck, cv, sink_b)


def _post_decode_kernel(x_ref, ada_ref, za_ref, sgb_ref, att_ref, w_b_ref, w_o_ref, g_ref, w_up_ref,
                        fcw_ref, fcb_ref, w_down_ref, gf_ref, fst_ref,
                        w_b_bf_ref, w_o_bf_ref, w_up_bf_ref, w_down_bf_ref, y_ref, fstn_ref,
                        yb_scr, x1_scr, h_scr, up_scr, act_scr, acc_scr, *, phases):
    j = pl.program_id(0)
    (b0, nb, cb), (o0, no, co), (u0, nu, cu), (d0, nd, cd) = phases

    def mod(k):
        return ada_ref[:, k * D_MODEL:(k + 1) * D_MODEL]

    for c in range(nb):
        @pl.when(j == b0 + c)
        def _(c=c):
            w = w_b_ref[...].astype(bf16)
            w_b_bf_ref[...] = w
            yb_scr[:, c * cb:(c + 1) * cb] = _dot(att_ref[...].astype(bf16), w)

    for c in range(no):
        @pl.when(j == o0 + c)
        def _(c=c):
            w = w_o_ref[...].astype(bf16)
            w_o_bf_ref[...] = w
            mix = (za_ref[...] + sgb_ref[...] * yb_scr[...]).astype(bf16)
            cols = slice(c * co, (c + 1) * co)
            x1_scr[:, cols] = x_ref[:, 0, cols] + mod(2)[:, cols] * _dot(mix, w)

    for c in range(nu):
        @pl.when(j == u0 + c)
        def _(c=c):
            if c == 0:
                h_scr[...] = (_rms(x1_scr[...], g_ref[...]) * (1.0 + mod(4)) + mod(3)).astype(bf16)
            w = w_up_ref[...].astype(bf16)
            w_up_bf_ref[...] = w
            up_scr[:, c * cu:(c + 1) * cu] = _dot(h_scr[...], w)

    for c in range(nd):
        @pl.when(j == d0 + c)
        def _(c=c):
            if c == 0:
                up = up_scr[...]
                prev0 = fst_ref[:, 0, :]
                prev1 = fst_ref[:, 1, :]
                conv = fcw_ref[0:1, :] * prev0 + fcw_ref[1:2, :] * prev1 + fcw_ref[2:3, :] * up + fcb_ref[...]
                fstn_ref[:, 0, :] = prev1
                fstn_ref[:, 1, :] = up
                act_scr[...] = (_silu(conv[:, 0:D_FF]) * conv[:, D_FF:2 * D_FF]).astype(bf16)
            w = w_down_ref[...].astype(bf16)
            w_down_bf_ref[...] = w
            cols = slice(c * cd, (c + 1) * cd)
            acc_scr[:, cols] = x1_scr[:, cols] + mod(5)[:, cols] * _dot(act_scr[...], w)
            if c == nd - 1:
                y_ref[:, 0, :] = _rms(acc_scr[...], gf_ref[...])


def _post_decode(x, ada, za, sgb, att, w_b, w_o, g_ffn, w_up, fcw, fcb, w_down, g_final, fstate,
                 *, proj_chunk, up_chunk, down_chunk):
    n = x.shape[0]
    assert D_MODEL % proj_chunk == 0 and (2 * D_FF) % up_chunk == 0 and D_MODEL % down_chunk == 0
    assert proj_chunk % LANES == 0 and up_chunk % LANES == 0 and down_chunk % LANES == 0
    nb = no = D_MODEL // proj_chunk
    nu, nd = 2 * D_FF // up_chunk, D_MODEL // down_chunk
    b0, o0, u0, d0 = 0, nb, nb + no, nb + no + nu
    phases = ((b0, nb, proj_chunk), (o0, no, proj_chunk), (u0, nu, up_chunk), (d0, nd, down_chunk))

    def chunk_index(start, count):
        return lambda j: jnp.clip(j - start, 0, count - 1)

    ib, io, iu, idn = (chunk_index(s0, cnt) for s0, cnt, _ in phases)
    const2 = lambda j: (0, 0)
    const3 = lambda j: (0, 0, 0)
    rows = lambda cols: pl.BlockSpec((n, cols), const2, pipeline_mode=pl.Buffered(1))
    state_block = pl.BlockSpec((n, CONV_W - 1, 2 * D_FF), const3, pipeline_mode=pl.Buffered(1))
    token_block = pl.BlockSpec((n, 1, D_MODEL), const3, pipeline_mode=pl.Buffered(1))
    col_chunk = lambda k, width, idx: pl.BlockSpec((k, width), lambda j: (0, idx(j)))
    w_down_block = col_chunk(D_FF, down_chunk, idn)
    return pl.pallas_call(
        functools.partial(_post_decode_kernel, phases=phases),
        grid=(d0 + nd,),
        in_specs=[token_block, rows(N_MOD * D_MODEL), rows(D_MODEL), rows(D_MODEL), rows(N_HEADS * HEAD_DIM),
                  col_chunk(N_HEADS * HEAD_DIM, proj_chunk, ib), col_chunk(D_MODEL, proj_chunk, io),
                  pl.BlockSpec((1, D_MODEL), const2), col_chunk(D_MODEL, up_chunk, iu),
                  pl.BlockSpec((CONV_W, 2 * D_FF), const2), pl.BlockSpec((1, 2 * D_FF), const2),
                  w_down_block, pl.BlockSpec((1, D_MODEL), const2), state_block],
        out_specs=[col_chunk(N_HEADS * HEAD_DIM, proj_chunk, ib), col_chunk(D_MODEL, proj_chunk, io),
                   col_chunk(D_MODEL, up_chunk, iu), w_down_block, token_block, state_block],
        out_shape=[jax.ShapeDtypeStruct((N_HEADS * HEAD_DIM, D_MODEL), bf16),
                   jax.ShapeDtypeStruct((D_MODEL, D_MODEL), bf16),
                   jax.ShapeDtypeStruct((D_MODEL, 2 * D_FF), bf16),
                   jax.ShapeDtypeStruct((D_FF, D_MODEL), bf16),
                   jax.ShapeDtypeStruct((n, 1, D_MODEL), f32),
                   jax.ShapeDtypeStruct((n, CONV_W - 1, 2 * D_FF), f32)],
        scratch_shapes=[pltpu.VMEM((n, D_MODEL), f32), pltpu.VMEM((n, D_MODEL), f32), pltpu.VMEM((n, D_MODEL), bf16),
                        pltpu.VMEM((n, 2 * D_FF), f32), pltpu.VMEM((n, D_FF), bf16), pltpu.VMEM((n, D_MODEL), f32)],
        compiler_params=pltpu.CompilerParams(
            dimension_semantics=("arbitrary",),
            vmem_limit_bytes=VMEM_LIMIT_LARGE),
        name="post_decode",
    )(x, ada, za, sgb, att, w_b, w_o, g_ffn, w_up, fcw, fcb, w_down, g_final, fstate)


PROMPT_MIXER_ROWS = 512
PROMPT_FFN_ROWS = 1024
PROMPT_FFN_COLS = 256
PROMPT_FFN_DOWN_COLS = 512
DECODE_ATTN_BATCH = 32
ADA_PER_STEP = 2
DECODE_IN_CHUNK = 1664
DECODE_PROJ_CHUNK = 512
DECODE_UP_CHUNK = 512
DECODE_DOWN_CHUNK = 256


def kernel(x_prompt, x_sample, c_prompt, c_sample, state_conv_a, cache_k_win, cache_v_win, state_ffn_conv, w_ada, b_ada, g_mix, w_in, conv_a_w, attn_sinks, w_a_out, w_b_out, w_o, g_ffn, w_up, ffn_conv_w, ffn_conv_b, w_down, g_final):
    depth = w_in.shape[0]
    n_p, seq, _ = x_prompt.shape
    n_s, t_s, _ = x_sample.shape
    assert n_p == 1 and t_s == 1, "one prompt sequence and single-token decode only"
    xp = x_prompt.reshape(seq, D_MODEL)
    xs = x_sample
    gf = g_final.reshape(1, D_MODEL)
    outs = [[] for _ in range(8)]
    for l in range(depth):
        ada_s, ada_p = _ada(c_sample, c_prompt, w_ada[l], b_ada[l], per_step=ADA_PER_STEP)
        gm, gn = g_mix[l].reshape(1, D_MODEL), g_ffn[l].reshape(1, D_MODEL)
        fcb = ffn_conv_b[l].reshape(1, 2 * D_FF)

        w_in_b, w_a_b, q, k_n, v_n, k_nt, v_nt, za, sgb, conv_s = _mixer_decode_pre(
            xs, ada_s, gm, w_in[l], conv_a_w[l], state_conv_a[l], w_a_out[l], chunk=DECODE_IN_CHUNK)
        to_native = lambda c: c.transpose(0, 2, 3, 1)
        from_native = lambda c: c.transpose(0, 3, 1, 2)
        sink_b = jnp.broadcast_to(attn_sinks[l].reshape(N_KV, GROUP, 1), (N_KV, GROUP, LANES))
        att4, k_s, v_s = _attn_decode(
            q.reshape(n_s, N_KV, GROUP, HEAD_DIM),
            k_n.reshape(n_s, N_KV, 1, HEAD_DIM), v_n.reshape(n_s, N_KV, 1, HEAD_DIM),
            k_nt.reshape(N_KV, HEAD_DIM, n_s), v_nt.reshape(N_KV, HEAD_DIM, n_s),
            to_native(cache_k_win[l]), to_native(cache_v_win[l]), sink_b, bb=DECODE_ATTN_BATCH)
        k_s, v_s = from_native(k_s), from_native(v_s)
        w_b_b, w_o_b, w_up_b, w_down_b, xs, ffn_s = _post_decode(
            xs, ada_s, za, sgb, att4.reshape(n_s, N_HEADS * HEAD_DIM), w_b_out[l], w_o[l], gn, w_up[l],
            ffn_conv_w[l], fcb, w_down[l], gf, state_ffn_conv[l],
            proj_chunk=DECODE_PROJ_CHUNK, up_chunk=DECODE_UP_CHUNK, down_chunk=DECODE_DOWN_CHUNK)

        x1, conv_p, k_p, v_p = _mixer_prompt(xp, ada_p, gm, w_in_b, conv_a_w[l], attn_sinks[l],
                                             w_a_b, w_b_b, w_o_b, tb=PROMPT_MIXER_ROWS)
        xp, ffn_p = _ffn_prompt(x1, ada_p, gn, w_up_b, ffn_conv_w[l], fcb, w_down_b, gf,
                                tb=PROMPT_FFN_ROWS, ch=PROMPT_FFN_COLS, nw=PROMPT_FFN_DOWN_COLS)

        for lst, val in zip(outs, (
                conv_p.reshape(n_p, CONV_W - 1, D_CONV), conv_s,
                from_native(k_p.reshape(n_p, N_KV, HEAD_DIM, WINDOW)), k_s,
                from_native(v_p.reshape(n_p, N_KV, HEAD_DIM, WINDOW)), v_s,
                ffn_p.reshape(n_p, CONV_W - 1, 2 * D_FF), ffn_s)):
            lst.append(val)
    assert depth == 1, "final RMSNorm is fused into the single layer's FFN kernels"
    return (xp.reshape(n_p, seq, D_MODEL), xs) + tuple(jnp.stack(o) for o in outs)
```

```python
import functools

import jax
import jax.numpy as jnp
from jax import lax
from jax.experimental import pallas as pl
from jax.experimental.pallas import tpu as pltpu

f32 = jnp.float32
bf16 = jnp.bfloat16

D_MODEL = 1024
D_CONV = D_MODEL
CONV_W = 3
N_HEADS = 16
N_KV = 4
GROUP = N_HEADS // N_KV
HEAD_DIM = 64
WINDOW = 128
D_FF = 2816
EPS = 1e-6
N_MOD = 6
ATTN_SCALE = HEAD_DIM ** -0.5
KV_COLS = N_KV * HEAD_DIM
C_XIN, C_B, C_C = 0, D_CONV, 2 * D_CONV
C_Q = 3 * D_CONV
C_K = C_Q + N_HEADS * HEAD_DIM
C_V = C_K + KV_COLS
C_GA = C_V + KV_COLS
C_GB = C_GA + D_MODEL
IN_COLS = C_GB + D_MODEL

LANES = 128
SUBLANES = 8
Q_SUB = 128
ATTN_LOOKAHEAD = 3
FFN_LOOKAHEAD = 2
MXU_COLS = 256
VMEM_BYTES_V7X = 64 * 1024 * 1024
VMEM_LIMIT_LARGE = VMEM_BYTES_V7X * 7 // 8
VMEM_LIMIT_MIXER = VMEM_BYTES_V7X * 15 // 16
VMEM_LIMIT_MEDIUM = VMEM_BYTES_V7X * 5 // 8
VMEM_LIMIT_SMALL = VMEM_BYTES_V7X // 2


def _rms(x, g):
    ms = jnp.mean(x * x, axis=-1, keepdims=True)
    return x * lax.rsqrt(ms + EPS) * g


def _silu(x):
    return x * jax.nn.sigmoid(x)


def _dot(a, b):
    return jnp.dot(a, b, preferred_element_type=f32)


def _const_spec(shape):
    nd = len(shape)
    return pl.BlockSpec(shape, lambda i: (0,) * nd, pipeline_mode=pl.Buffered(1))


def _ada_kernel(cs_ref, cp_ref, w_ref, b_ref, os_ref, op_ref, *, per_step):
    w = w_ref[...].astype(bf16)
    os_ref[...] = _dot(_silu(cs_ref[...]).astype(bf16), w) + b_ref[...]
    cp = jnp.broadcast_to(_silu(cp_ref[...]), (SUBLANES, D_MODEL)).astype(bf16)
    mod_p = _dot(cp, w)[0:1, :] + b_ref[...]
    for m in range(per_step):
        op_ref[m] = mod_p[:, m * D_MODEL:(m + 1) * D_MODEL]


def _ada(c_sample, c_prompt, w_ada, b_ada, *, per_step):
    n_s = c_sample.shape[0]
    assert c_prompt.shape[0] == 1 and N_MOD % per_step == 0
    cols = per_step * D_MODEL
    return pl.pallas_call(
        functools.partial(_ada_kernel, per_step=per_step),
        grid=(N_MOD // per_step,),
        in_specs=[
            pl.BlockSpec((n_s, D_MODEL), lambda j: (0, 0)),
            pl.BlockSpec((1, D_MODEL), lambda j: (0, 0)),
            pl.BlockSpec((D_MODEL, cols), lambda j: (0, j)),
            pl.BlockSpec((1, cols), lambda j: (0, j)),
        ],
        out_specs=[pl.BlockSpec((n_s, cols), lambda j: (0, j)),
                   pl.BlockSpec((per_step, 1, D_MODEL), lambda j: (j, 0, 0))],
        out_shape=[jax.ShapeDtypeStruct((n_s, N_MOD * D_MODEL), f32),
                   jax.ShapeDtypeStruct((N_MOD, 1, D_MODEL), f32)],
        compiler_params=pltpu.CompilerParams(
            dimension_semantics=("arbitrary",),
            vmem_limit_bytes=VMEM_LIMIT_SMALL),
        name="ada",
    )(c_sample, c_prompt, w_ada, b_ada.reshape(1, -1))


def _mixer_prompt_kernel(sinks_ref, x_ref, ada_ref, g_ref, w_in_ref, cw_ref, w_a_ref, w_b_ref, w_o_ref,
                         x1_ref, conv_ref, knew_ref, vnew_ref,
                         ubuf, klo, khi, vt, attbuf, *, tb):
    i = pl.program_id(0)

    @pl.when(i == 0)
    def _():
        ubuf[0:SUBLANES, :] = jnp.zeros((SUBLANES, D_CONV), f32)
        for r in (klo, khi):
            r[:, 0:WINDOW, :] = jnp.zeros((N_KV, WINDOW, LANES), bf16)
        vt[:, 0:WINDOW] = jnp.zeros((KV_COLS, WINDOW), bf16)

    x = x_ref[...]
    sh1, sc1, gt1 = ada_ref[0], ada_ref[1], ada_ref[2]
    h = (_rms(x, g_ref[...]) * (1.0 + sc1) + sh1).astype(bf16)

    def proj(c0, n):
        return _dot(h, w_in_ref[:, c0:c0 + n])

    k = proj(C_K, KV_COLS)
    v = proj(C_V, KV_COLS)
    u = proj(C_C, D_CONV) * proj(C_XIN, D_CONV)
    ubuf[SUBLANES:SUBLANES + tb, :] = u
    b_gate = proj(C_B, D_CONV)
    q = (proj(C_Q, N_HEADS * HEAD_DIM) * ATTN_SCALE).astype(bf16)
    conv = (cw_ref[0:1, :] * ubuf[SUBLANES - 2:SUBLANES - 2 + tb, :]
            + cw_ref[1:2, :] * ubuf[SUBLANES - 1:SUBLANES - 1 + tb, :]
            + cw_ref[2:3, :] * u)
    conv_in = (b_gate * conv).astype(bf16)
    knew_ref[...] = k[tb - WINDOW:, :].T
    vnew_ref[...] = v[tb - WINDOW:, :].T

    lo = lax.broadcasted_iota(jnp.int32, (tb, LANES), 1) < HEAD_DIM
    for j in range(N_KV // 2):
        pair = k[:, LANES * j:LANES * (j + 1)]
        rolled = pltpu.roll(pair, HEAD_DIM, axis=1)
        zero = jnp.zeros_like(pair)
        klo[2 * j, WINDOW:WINDOW + tb, :] = jnp.where(lo, pair, zero).astype(bf16)
        khi[2 * j, WINDOW:WINDOW + tb, :] = jnp.where(lo, zero, rolled).astype(bf16)
        klo[2 * j + 1, WINDOW:WINDOW + tb, :] = jnp.where(lo, rolled, zero).astype(bf16)
        khi[2 * j + 1, WINDOW:WINDOW + tb, :] = jnp.where(lo, zero, pair).astype(bf16)
    vt[:, WINDOW:WINDOW + tb] = v.T.astype(bf16)

    cc = lax.broadcasted_iota(jnp.int32, (Q_SUB + WINDOW, 2 * Q_SUB), 0)
    col = lax.broadcasted_iota(jnp.int32, (Q_SUB + WINDOW, 2 * Q_SUB), 1)
    rr = col & (Q_SUB - 1)
    band = (cc >= rr) & (cc <= rr + WINDOW)
    first_head = lax.broadcasted_iota(jnp.int32, (1, 2 * Q_SUB), 1) < Q_SUB
    mask0 = band & (cc >= jnp.where(i == 0, WINDOW, 0))

    def scores(j, g, t):
        r0 = Q_SUB * j
        qs = jnp.concatenate([q[r0:r0 + Q_SUB, 2 * LANES * g:2 * LANES * g + LANES],
                              q[r0:r0 + Q_SUB, 2 * LANES * g + LANES:2 * LANES * (g + 1)]], axis=0)
        kr = (klo, khi)[t]
        return lax.dot_general(kr[g, r0:r0 + Q_SUB + WINDOW, :], qs, (((1,), (1,)), ((), ())),
                               preferred_element_type=f32)

    def finish(j, g, t, st):
        r0 = Q_SUB * j
        h0, h1 = GROUP * g + t, GROUP * g + 2 + t
        sink = jnp.where(first_head, sinks_ref[h0], sinks_ref[h1])
        st = jnp.where(mask0 if j == 0 else band, st, -jnp.inf)
        m = jnp.maximum(jnp.max(st, axis=0, keepdims=True), sink)
        e = jnp.exp(st - m)
        linv = 1.0 / (jnp.sum(e, axis=0, keepdims=True) + jnp.exp(sink - m))
        vtg = vt[HEAD_DIM * g:HEAD_DIM * (g + 1), r0:r0 + Q_SUB + WINDOW]
        ot = _dot(vtg, e.astype(bf16)) * linv
        attbuf[HEAD_DIM * h0:HEAD_DIM * (h0 + 1), r0:r0 + Q_SUB] = ot[:, 0:Q_SUB]
        attbuf[HEAD_DIM * h1:HEAD_DIM * (h1 + 1), r0:r0 + Q_SUB] = ot[:, Q_SUB:2 * Q_SUB]

    chains = [(j, g, t) for j in range(tb // Q_SUB) for g in range(N_KV) for t in range(2)]
    fillers = ([functools.partial(proj, c, MXU_COLS) for c in range(C_GA, C_GB + D_MODEL, MXU_COLS)]
               + [(lambda c=c: _dot(conv_in, w_a_ref[:, c:c + MXU_COLS])) for c in range(0, D_MODEL, MXU_COLS)])
    fill_every = -(-len(chains) // len(fillers))
    filled = []
    pending = [scores(*c) for c in chains[:ATTN_LOOKAHEAD]]
    for n, chain in enumerate(chains):
        if n + ATTN_LOOKAHEAD < len(chains):
            pending.append(scores(*chains[n + ATTN_LOOKAHEAD]))
        if n % fill_every == 0:
            filled.extend(f() for f in fillers[len(filled):len(filled) + 1])
        finish(*chain, pending.pop(0))
    filled.extend(f() for f in fillers[len(filled):])
    ga, gb, ya = (jnp.concatenate(filled[j:j + D_MODEL // MXU_COLS], axis=1)
                  for j in range(0, len(filled), D_MODEL // MXU_COLS))

    yb = _dot(attbuf[...].T.astype(bf16), w_b_ref[...])
    mix = (jax.nn.sigmoid(ga) * ya + jax.nn.sigmoid(gb) * yb).astype(bf16)
    for r0 in range(0, tb, tb // 2):
        x1_ref[r0:r0 + tb // 2, :] = x[r0:r0 + tb // 2, :] + gt1 * _dot(mix[r0:r0 + tb // 2, :], w_o_ref[...])

    conv_ref[...] = ubuf[SUBLANES + tb - (CONV_W - 1):SUBLANES + tb, :]
    ubuf[0:SUBLANES, :] = ubuf[tb:tb + SUBLANES, :]
    for r in (klo, khi):
        r[:, 0:WINDOW, :] = r[:, tb:tb + WINDOW, :]
    vt[:, 0:WINDOW] = vt[:, tb:tb + WINDOW]


def _mixer_prompt(x, ada, g_mix, w_in, conv_w, sinks, w_a, w_b, w_o, *, tb):
    s = x.shape[0]
    assert s % tb == 0 and tb % Q_SUB == 0 and tb >= WINDOW
    kv_scratch = pltpu.VMEM((N_KV, WINDOW + tb, LANES), bf16)
    grid_spec = pltpu.PrefetchScalarGridSpec(
        num_scalar_prefetch=1,
        grid=(s // tb,),
        in_specs=[
            pl.BlockSpec((tb, D_MODEL), lambda i, sk: (i, 0)),
            pl.BlockSpec((N_MOD, 1, D_MODEL), lambda i, sk: (0, 0, 0)),
            pl.BlockSpec((1, D_MODEL), lambda i, sk: (0, 0)),
            pl.BlockSpec((D_MODEL, IN_COLS), lambda i, sk: (0, 0), pipeline_mode=pl.Buffered(1)),
            pl.BlockSpec((CONV_W, D_CONV), lambda i, sk: (0, 0)),
            pl.BlockSpec((D_CONV, D_MODEL), lambda i, sk: (0, 0), pipeline_mode=pl.Buffered(1)),
            pl.BlockSpec((N_HEADS * HEAD_DIM, D_MODEL), lambda i, sk: (0, 0), pipeline_mode=pl.Buffered(1)),
            pl.BlockSpec((D_MODEL, D_MODEL), lambda i, sk: (0, 0), pipeline_mode=pl.Buffered(1)),
        ],
        out_specs=[
            pl.BlockSpec((tb, D_MODEL), lambda i, sk: (i, 0)),
            pl.BlockSpec((CONV_W - 1, D_CONV), lambda i, sk: (0, 0)),
            pl.BlockSpec((KV_COLS, WINDOW), lambda i, sk: (0, 0)),
            pl.BlockSpec((KV_COLS, WINDOW), lambda i, sk: (0, 0)),
        ],
        scratch_shapes=[
            pltpu.VMEM((SUBLANES + tb, D_CONV), f32),
            kv_scratch, kv_scratch,
            pltpu.VMEM((KV_COLS, WINDOW + tb), bf16),
            pltpu.VMEM((N_HEADS * HEAD_DIM, tb), f32),
        ],
    )
    return pl.pallas_call(
        functools.partial(_mixer_prompt_kernel, tb=tb),
        grid_spec=grid_spec,
        out_shape=[
            jax.ShapeDtypeStruct((s, D_MODEL), f32),
            jax.ShapeDtypeStruct((CONV_W - 1, D_CONV), f32),
            jax.ShapeDtypeStruct((KV_COLS, WINDOW), f32),
            jax.ShapeDtypeStruct((KV_COLS, WINDOW), f32),
        ],
        compiler_params=pltpu.CompilerParams(
            dimension_semantics=("arbitrary",),
            vmem_limit_bytes=VMEM_LIMIT_MIXER),
        name="mixer_prompt",
    )(sinks, x, ada, g_mix, w_in, conv_w, w_a, w_b, w_o)


def _ffn_prompt_kernel(x_ref, ada_ref, g_ref, w_up_ref, fcw_ref, fcb_ref, w_down_ref, gf_ref,
                       y_ref, fst_ref, upbuf, actbuf, *, tb, ch, nw):
    i = pl.program_id(0)

    @pl.when(i == 0)
    def _():
        upbuf[:, 0:SUBLANES, :] = jnp.zeros((2 * D_FF // LANES, SUBLANES, LANES), f32)

    hb = tb // 2
    sh2, sc2, gt2 = ada_ref[3], ada_ref[4], ada_ref[5]
    h = (_rms(x_ref[...], g_ref[...]) * (1.0 + sc2) + sh2).astype(bf16)

    def up_cols(half, c0):
        up = _dot(h[half * hb:(half + 1) * hb, :], w_up_ref[:, c0:c0 + ch])
        for s in range(ch // LANES):
            upbuf[c0 // LANES + s, SUBLANES:SUBLANES + hb, :] = up[:, s * LANES:(s + 1) * LANES]
        return up

    def conv_cols(c0, up):
        pieces = []
        for s in range(ch // LANES):
            slab = c0 // LANES + s
            cols = slice(c0 + s * LANES, c0 + (s + 1) * LANES)
            pieces.append(fcw_ref[0:1, cols] * upbuf[slab, SUBLANES - 2:SUBLANES - 2 + hb, :]
                          + fcw_ref[1:2, cols] * upbuf[slab, SUBLANES - 1:SUBLANES - 1 + hb, :]
                          + fcw_ref[2:3, cols] * up[:, s * LANES:(s + 1) * LANES]
                          + fcb_ref[0:1, cols])
            upbuf[slab, 0:SUBLANES, :] = upbuf[slab, hb:hb + SUBLANES, :]
        return jnp.concatenate(pieces, axis=1)

    def down_cols(half, n0):
        return _dot(actbuf[half], w_down_ref[:, n0:n0 + nw])

    def finish(half, parts):
        rows = slice(half * hb, (half + 1) * hb)
        x2 = x_ref[rows, :] + gt2 * jnp.concatenate(parts, axis=1)
        y_ref[rows, :] = _rms(x2, gf_ref[...])

    chunks = list(range(0, D_FF, ch))
    down_starts = list(range(0, D_MODEL, nw))
    down_at = {len(chunks) * (k + 1) // (len(down_starts) + 1): n0 for k, n0 in enumerate(down_starts)}
    assert len(down_at) == len(down_starts)
    for half in range(2):
        parts = []
        pending = [(up_cols(half, c0), up_cols(half, D_FF + c0)) for c0 in chunks[:FFN_LOOKAHEAD]]
        for n, c0 in enumerate(chunks):
            if n + FFN_LOOKAHEAD < len(chunks):
                c1 = chunks[n + FFN_LOOKAHEAD]
                pending.append((up_cols(half, c1), up_cols(half, D_FF + c1)))
            if half == 1 and n in down_at:
                parts.append(down_cols(0, down_at[n]))
            up_g, up_v = pending.pop(0)
            actbuf[half, :, c0:c0 + ch] = (_silu(conv_cols(c0, up_g))
                                           * conv_cols(D_FF + c0, up_v)).astype(bf16)
        if half == 1:
            finish(0, parts)
    finish(1, [down_cols(1, n0) for n0 in down_starts])

    for slab in range(2 * D_FF // LANES):
        fst_ref[:, slab * LANES:(slab + 1) * LANES] = upbuf[slab, SUBLANES - (CONV_W - 1):SUBLANES, :]


def _ffn_prompt(x1, ada, g_ffn, w_up, fcw, fcb, w_down, g_final, *, tb, ch, nw):
    s = x1.shape[0]
    assert s % tb == 0 and tb % (4 * SUBLANES) == 0
    assert D_FF % ch == 0 and ch % LANES == 0 and D_MODEL % nw == 0 and nw % LANES == 0
    return pl.pallas_call(
        functools.partial(_ffn_prompt_kernel, tb=tb, ch=ch, nw=nw),
        grid=(s // tb,),
        in_specs=[
            pl.BlockSpec((tb, D_MODEL), lambda i: (i, 0)),
            pl.BlockSpec((N_MOD, 1, D_MODEL), lambda i: (0, 0, 0)),
            pl.BlockSpec((1, D_MODEL), lambda i: (0, 0)),
            _const_spec((D_MODEL, 2 * D_FF)),
            pl.BlockSpec((CONV_W, 2 * D_FF), lambda i: (0, 0)),
            pl.BlockSpec((1, 2 * D_FF), lambda i: (0, 0)),
            _const_spec((D_FF, D_MODEL)),
            pl.BlockSpec((1, D_MODEL), lambda i: (0, 0)),
        ],
        out_specs=[
            pl.BlockSpec((tb, D_MODEL), lambda i: (i, 0)),
            pl.BlockSpec((CONV_W - 1, 2 * D_FF), lambda i: (0, 0)),
        ],
        out_shape=[
            jax.ShapeDtypeStruct((s, D_MODEL), f32),
            jax.ShapeDtypeStruct((CONV_W - 1, 2 * D_FF), f32),
        ],
        scratch_shapes=[
            pltpu.VMEM((2 * D_FF // LANES, SUBLANES + tb // 2, LANES), f32),
            pltpu.VMEM((2, tb // 2, D_FF), bf16),
        ],
        compiler_params=pltpu.CompilerParams(
            dimension_semantics=("arbitrary",),
            vmem_limit_bytes=VMEM_LIMIT_LARGE),
        name="ffn_prompt",
    )(x1, ada, g_ffn, w_up, fcw, fcb, w_down, g_final)


def _mixer_decode_pre_kernel(x_ref, ada_ref, g_ref, w_in_ref, cw_ref, st_ref, w_a_ref,
                             w_in_bf_ref, w_a_bf_ref, q_ref, k_ref, v_ref, knt_ref, vnt_ref, za_ref, sgb_ref, stn_ref,
                             h_scr, proj_scr, *, chunk):
    j = pl.program_id(0)
    n_chunks = IN_COLS // chunk

    @pl.when(j == 0)
    def _():
        sh1 = ada_ref[:, 0:D_MODEL]
        sc1 = ada_ref[:, D_MODEL:2 * D_MODEL]
        h_scr[...] = (_rms(x_ref[:, 0, :], g_ref[...]) * (1.0 + sc1) + sh1).astype(bf16)

    w_chunk = w_in_ref[...].astype(bf16)
    w_in_bf_ref[...] = w_chunk
    part = _dot(h_scr[...], w_chunk)
    for c in range(n_chunks):
        @pl.when(j == c)
        def _(c=c):
            proj_scr[:, c * chunk:(c + 1) * chunk] = part

    @pl.when(j == n_chunks - 1)
    def _():
        def proj(c0, n):
            return proj_scr[:, c0:c0 + n]

        w_a = w_a_ref[...].astype(bf16)
        w_a_bf_ref[...] = w_a
        u = proj(C_C, D_CONV) * proj(C_XIN, D_CONV)
        prev0 = st_ref[:, 0, :]
        prev1 = st_ref[:, 1, :]
        conv = cw_ref[0:1, :] * prev0 + cw_ref[1:2, :] * prev1 + cw_ref[2:3, :] * u
        stn_ref[:, 0, :] = prev1
        stn_ref[:, 1, :] = u
        ya = _dot((proj(C_B, D_CONV) * conv).astype(bf16), w_a)
        q_ref[...] = proj(C_Q, N_HEADS * HEAD_DIM) * ATTN_SCALE
        k = proj(C_K, KV_COLS)
        v = proj(C_V, KV_COLS)
        k_ref[...] = k
        v_ref[...] = v
        knt_ref[...] = k.T
        vnt_ref[...] = v.T
        za_ref[...] = jax.nn.sigmoid(proj(C_GA, D_MODEL)) * ya
        sgb_ref[...] = jax.nn.sigmoid(proj(C_GB, D_MODEL))


def _mixer_decode_pre(x, ada, g_mix, w_in, conv_w, state, w_a, *, chunk):
    n = x.shape[0]
    assert IN_COLS % chunk == 0 and chunk % LANES == 0
    const2 = lambda j: (0, 0)
    const3 = lambda j: (0, 0, 0)
    row_block = lambda cols: pl.BlockSpec((n, cols), const2)
    state_block = pl.BlockSpec((n, CONV_W - 1, D_CONV), const3)
    shapes = [((D_MODEL, IN_COLS), bf16), ((D_CONV, D_MODEL), bf16),
              ((n, N_HEADS * HEAD_DIM), f32), ((n, KV_COLS), f32), ((n, KV_COLS), f32),
              ((KV_COLS, n), f32), ((KV_COLS, n), f32),
              ((n, D_MODEL), f32), ((n, D_MODEL), f32), ((n, CONV_W - 1, D_CONV), f32)]
    return pl.pallas_call(
        functools.partial(_mixer_decode_pre_kernel, chunk=chunk),
        grid=(IN_COLS // chunk,),
        in_specs=[pl.BlockSpec((n, 1, D_MODEL), const3), row_block(2 * D_MODEL), pl.BlockSpec((1, D_MODEL), const2),
                  pl.BlockSpec((D_MODEL, chunk), lambda j: (0, j)),
                  pl.BlockSpec((CONV_W, D_CONV), const2), state_block,
                  pl.BlockSpec((D_CONV, D_MODEL), const2)],
        out_specs=[pl.BlockSpec((D_MODEL, chunk), lambda j: (0, j)), pl.BlockSpec((D_CONV, D_MODEL), const2),
                   row_block(N_HEADS * HEAD_DIM), row_block(KV_COLS), row_block(KV_COLS),
                   pl.BlockSpec((KV_COLS, n), const2), pl.BlockSpec((KV_COLS, n), const2),
                   row_block(D_MODEL), row_block(D_MODEL), state_block],
        out_shape=[jax.ShapeDtypeStruct(shp, dt) for shp, dt in shapes],
        scratch_shapes=[pltpu.VMEM((n, D_MODEL), bf16), pltpu.VMEM((n, IN_COLS), f32)],
        compiler_params=pltpu.CompilerParams(
            dimension_semantics=("arbitrary",),
            vmem_limit_bytes=VMEM_LIMIT_MEDIUM),
        name="mixer_decode_pre",
    )(x, ada, g_mix, w_in, conv_w, state, w_a)


def _attn_decode_kernel(q_ref, kn_ref, vn_ref, knt_ref, vnt_ref, ck_ref, cv_ref, sink_ref,
                        att_ref, ok_ref, ov_ref, *, bb):
    step = pl.program_id(0)
    last = lax.broadcasted_iota(jnp.int32, (HEAD_DIM, WINDOW), 1) == WINDOW - 1
    to_front = (LANES - step * bb) % LANES
    kstep = [pltpu.roll(knt_ref[g], to_front, axis=1) for g in range(N_KV)]
    vstep = [pltpu.roll(vnt_ref[g], to_front, axis=1) for g in range(N_KV)]

    pairs = [(b, g) for b in range(bb) for g in range(N_KV)]
    n_stage = 7
    cuts = [len(pairs) * s // n_stage for s in range(n_stage + 1)]

    def shift_caches(stage):
        for b, g in pairs[cuts[stage]:cuts[stage + 1]]:
            ok_ref[b, g] = jnp.where(last, pltpu.roll(kstep[g], WINDOW - 1 - b, axis=1),
                                     pltpu.roll(ck_ref[b, g], WINDOW - 1, axis=1))
            ov_ref[b, g] = jnp.where(last, pltpu.roll(vstep[g], WINDOW - 1 - b, axis=1),
                                     pltpu.roll(cv_ref[b, g], WINDOW - 1, axis=1))

    sinks = [sink_ref[g][:, 0:1] for g in range(N_KV)]
    shift_caches(0)
    s_old = [_dot(q_ref[b, g].astype(bf16), ck_ref[b, g].astype(bf16)) for b, g in pairs]
    shift_caches(1)
    s_new = [jnp.sum(q_ref[b, g] * kn_ref[b, g], axis=-1, keepdims=True) for b, g in pairs]
    shift_caches(2)
    m = [jnp.maximum(jnp.maximum(jnp.max(so, axis=-1, keepdims=True), sn), sinks[g])
         for (b, g), so, sn in zip(pairs, s_old, s_new)]
    shift_caches(3)
    e_old = [jnp.exp(so - mm) for so, mm in zip(s_old, m)]
    e_new = [jnp.exp(sn - mm) for sn, mm in zip(s_new, m)]
    den = [jnp.sum(eo, axis=-1, keepdims=True) + en + jnp.exp(sinks[g] - mm)
           for (b, g), eo, en, mm in zip(pairs, e_old, e_new, m)]
    shift_caches(4)
    o_old = [lax.dot_general(eo.astype(bf16), cv_ref[b, g].astype(bf16), (((1,), (1,)), ((), ())),
                             preferred_element_type=f32) for (b, g), eo in zip(pairs, e_old)]
    shift_caches(5)
    for (b, g), oo, en, dd in zip(pairs, o_old, e_new, den):
        att_ref[b, g] = (oo + en * vn_ref[b, g]) / dd
    shift_caches(6)


def _attn_decode(q4, kn4, vn4, knt, vnt, ck, cv, sink_b, *, bb):
    n = q4.shape[0]
    assert n % bb == 0 and n == LANES and WINDOW == LANES
    cache_spec = pl.BlockSpec((bb, N_KV, HEAD_DIM, WINDOW), lambda b: (b, 0, 0, 0))
    row_spec = pl.BlockSpec((bb, N_KV, 1, HEAD_DIM), lambda b: (b, 0, 0, 0))
    q_spec = pl.BlockSpec((bb, N_KV, GROUP, HEAD_DIM), lambda b: (b, 0, 0, 0))
    new_t_spec = pl.BlockSpec((N_KV, HEAD_DIM, n), lambda b: (0, 0, 0))
    return pl.pallas_call(
        functools.partial(_attn_decode_kernel, bb=bb),
        grid=(n // bb,),
        in_specs=[q_spec, row_spec, row_spec, new_t_spec, new_t_spec, cache_spec, cache_spec,
                  pl.BlockSpec((N_KV, GROUP, LANES), lambda b: (0, 0, 0))],
        out_specs=[q_spec, cache_spec, cache_spec],
        out_shape=[jax.ShapeDtypeStruct((n, N_KV, GROUP, HEAD_DIM), f32),
                   jax.ShapeDtypeStruct((n, N_KV, HEAD_DIM, WINDOW), f32),
                   jax.ShapeDtypeStruct((n, N_KV, HEAD_DIM, WINDOW), f32)],
        compiler_params=pltpu.CompilerParams(dimension_semantics=("arbitrary",)),
        name="attn_decode",
    )(q4, kn4, vn4, knt, vnt, ck, cv, sink_b)


def _post_decode_kernel(x_ref, ada_ref, za_ref, sgb_ref, att_ref, w_b_ref, w_o_ref, g_ref, w_up_ref,
                        fcw_ref, fcb_ref, w_down_ref, gf_ref, fst_ref,
                        w_b_bf_ref, w_o_bf_ref, w_up_bf_ref, w_down_bf_ref, y_ref, fstn_ref,
                        yb_scr, x1_scr, h_scr, up_scr, act_scr, acc_scr, *, phases):
    j = pl.program_id(0)
    (b0, nb, cb), (o0, no, co), (u0, nu, cu), (d0, nd, cd) = phases

    def mod(k):
        return ada_ref[:, k * D_MODEL:(k + 1) * D_MODEL]

    for c in range(nb):
        @pl.when(j == b0 + c)
        def _(c=c):
            w = w_b_ref[...].astype(bf16)
            w_b_bf_ref[...] = w
            yb_scr[:, c * cb:(c + 1) * cb] = _dot(att_ref[...].astype(bf16), w)

    for c in range(no):
        @pl.when(j == o0 + c)
        def _(c=c):
            w = w_o_ref[...].astype(bf16)
            w_o_bf_ref[...] = w
            mix = (za_ref[...] + sgb_ref[...] * yb_scr[...]).astype(bf16)
            cols = slice(c * co, (c + 1) * co)
            x1_scr[:, cols] = x_ref[:, 0, cols] + mod(2)[:, cols] * _dot(mix, w)

    for c in range(nu):
        @pl.when(j == u0 + c)
        def _(c=c):
            if c == 0:
                h_scr[...] = (_rms(x1_scr[...], g_ref[...]) * (1.0 + mod(4)) + mod(3)).astype(bf16)
            w = w_up_ref[...].astype(bf16)
            w_up_bf_ref[...] = w
            up_scr[:, c * cu:(c + 1) * cu] = _dot(h_scr[...], w)

    for c in range(nd):
        @pl.when(j == d0 + c)
        def _(c=c):
            if c == 0:
                up = up_scr[...]
                prev0 = fst_ref[:, 0, :]
                prev1 = fst_ref[:, 1, :]
                conv = fcw_ref[0:1, :] * prev0 + fcw_ref[1:2, :] * prev1 + fcw_ref[2:3, :] * up + fcb_ref[...]
                fstn_ref[:, 0, :] = prev1
                fstn_ref[:, 1, :] = up
                act_scr[...] = (_silu(conv[:, 0:D_FF]) * conv[:, D_FF:2 * D_FF]).astype(bf16)
            w = w_down_ref[...].astype(bf16)
            w_down_bf_ref[...] = w
            cols = slice(c * cd, (c + 1) * cd)
            acc_scr[:, cols] = x1_scr[:, cols] + mod(5)[:, cols] * _dot(act_scr[...], w)
            if c == nd - 1:
                y_ref[:, 0, :] = _rms(acc_scr[...], gf_ref[...])


def _post_decode(x, ada, za, sgb, att, w_b, w_o, g_ffn, w_up, fcw, fcb, w_down, g_final, fstate,
                 *, proj_chunk, up_chunk, down_chunk):
    n = x.shape[0]
    assert D_MODEL % proj_chunk == 0 and (2 * D_FF) % up_chunk == 0 and D_MODEL % down_chunk == 0
    assert proj_chunk % LANES == 0 and up_chunk % LANES == 0 and down_chunk % LANES == 0
    nb = no = D_MODEL // proj_chunk
    nu, nd = 2 * D_FF // up_chunk, D_MODEL // down_chunk
    b0, o0, u0, d0 = 0, nb, nb + no, nb + no + nu
    phases = ((b0, nb, proj_chunk), (o0, no, proj_chunk), (u0, nu, up_chunk), (d0, nd, down_chunk))

    def chunk_index(start, count):
        return lambda j: jnp.clip(j - start, 0, count - 1)

    ib, io, iu, idn = (chunk_index(s0, cnt) for s0, cnt, _ in phases)
    const2 = lambda j: (0, 0)
    const3 = lambda j: (0, 0, 0)
    rows = lambda cols: pl.BlockSpec((n, cols), const2, pipeline_mode=pl.Buffered(1))
    state_block = pl.BlockSpec((n, CONV_W - 1, 2 * D_FF), const3, pipeline_mode=pl.Buffered(1))
    token_block = pl.BlockSpec((n, 1, D_MODEL), const3, pipeline_mode=pl.Buffered(1))
    col_chunk = lambda k, width, idx: pl.BlockSpec((k, width), lambda j: (0, idx(j)))
    w_down_block = col_chunk(D_FF, down_chunk, idn)
    return pl.pallas_call(
        functools.partial(_post_decode_kernel, phases=phases),
        grid=(d0 + nd,),
        in_specs=[token_block, rows(N_MOD * D_MODEL), rows(D_MODEL), rows(D_MODEL), rows(N_HEADS * HEAD_DIM),
                  col_chunk(N_HEADS * HEAD_DIM, proj_chunk, ib), col_chunk(D_MODEL, proj_chunk, io),
                  pl.BlockSpec((1, D_MODEL), const2), col_chunk(D_MODEL, up_chunk, iu),
                  pl.BlockSpec((CONV_W, 2 * D_FF), const2), pl.BlockSpec((1, 2 * D_FF), const2),
                  w_down_block, pl.BlockSpec((1, D_MODEL), const2), state_block],
        out_specs=[col_chunk(N_HEADS * HEAD_DIM, proj_chunk, ib), col_chunk(D_MODEL, proj_chunk, io),
                   col_chunk(D_MODEL, up_chunk, iu), w_down_block, token_block, state_block],
        out_shape=[jax.ShapeDtypeStruct((N_HEADS * HEAD_DIM, D_MODEL), bf16),
                   jax.ShapeDtypeStruct((D_MODEL, D_MODEL), bf16),
                   jax.ShapeDtypeStruct((D_MODEL, 2 * D_FF), bf16),
                   jax.ShapeDtypeStruct((D_FF, D_MODEL), bf16),
                   jax.ShapeDtypeStruct((n, 1, D_MODEL), f32),
                   jax.ShapeDtypeStruct((n, CONV_W - 1, 2 * D_FF), f32)],
        scratch_shapes=[pltpu.VMEM((n, D_MODEL), f32), pltpu.VMEM((n, D_MODEL), f32), pltpu.VMEM((n, D_MODEL), bf16),
                        pltpu.VMEM((n, 2 * D_FF), f32), pltpu.VMEM((n, D_FF), bf16), pltpu.VMEM((n, D_MODEL), f32)],
        compiler_params=pltpu.CompilerParams(
            dimension_semantics=("arbitrary",),
            vmem_limit_bytes=VMEM_LIMIT_LARGE),
        name="post_decode",
    )(x, ada, za, sgb, att, w_b, w_o, g_ffn, w_up, fcw, fcb, w_down, g_final, fstate)


PROMPT_MIXER_ROWS = 1024
PROMPT_FFN_ROWS = 1024
PROMPT_FFN_COLS = 256
PROMPT_FFN_DOWN_COLS = 512
DECODE_ATTN_BATCH = 32
ADA_PER_STEP = 2
DECODE_IN_CHUNK = 1664
DECODE_PROJ_CHUNK = 512
DECODE_UP_CHUNK = 512
DECODE_DOWN_CHUNK = 256


def kernel(x_prompt, x_sample, c_prompt, c_sample, state_conv_a, cache_k_win, cache_v_win, state_ffn_conv, w_ada, b_ada, g_mix, w_in, conv_a_w, attn_sinks, w_a_out, w_b_out, w_o, g_ffn, w_up, ffn_conv_w, ffn_conv_b, w_down, g_final):
    depth = w_in.shape[0]
    n_p, seq, _ = x_prompt.shape
    n_s, t_s, _ = x_sample.shape
    assert n_p == 1 and t_s == 1, "one prompt sequence and single-token decode only"
    xp = x_prompt.reshape(seq, D_MODEL)
    xs = x_sample
    gf = g_final.reshape(1, D_MODEL)
    outs = [[] for _ in range(8)]
    for l in range(depth):
        ada_s, ada_p = _ada(c_sample, c_prompt, w_ada[l], b_ada[l], per_step=ADA_PER_STEP)
        gm, gn = g_mix[l].reshape(1, D_MODEL), g_ffn[l].reshape(1, D_MODEL)
        fcb = ffn_conv_b[l].reshape(1, 2 * D_FF)

        w_in_b, w_a_b, q, k_n, v_n, k_nt, v_nt, za, sgb, conv_s = _mixer_decode_pre(
            xs, ada_s, gm, w_in[l], conv_a_w[l], state_conv_a[l], w_a_out[l], chunk=DECODE_IN_CHUNK)
        to_native = lambda c: c.transpose(0, 2, 3, 1)
        from_native = lambda c: c.transpose(0, 3, 1, 2)
        sink_b = jnp.broadcast_to(attn_sinks[l].reshape(N_KV, GROUP, 1), (N_KV, GROUP, LANES))
        att4, k_s, v_s = _attn_decode(
            q.reshape(n_s, N_KV, GROUP, HEAD_DIM),
            k_n.reshape(n_s, N_KV, 1, HEAD_DIM), v_n.reshape(n_s, N_KV, 1, HEAD_DIM),
            k_nt.reshape(N_KV, HEAD_DIM, n_s), v_nt.reshape(N_KV, HEAD_DIM, n_s),
            to_native(cache_k_win[l]), to_native(cache_v_win[l]), sink_b, bb=DECODE_ATTN_BATCH)
        k_s, v_s = from_native(k_s), from_native(v_s)
        w_b_b, w_o_b, w_up_b, w_down_b, xs, ffn_s = _post_decode(
            xs, ada_s, za, sgb, att4.reshape(n_s, N_HEADS * HEAD_DIM), w_b_out[l], w_o[l], gn, w_up[l],
            ffn_conv_w[l], fcb, w_down[l], gf, state_ffn_conv[l],
            proj_chunk=DECODE_PROJ_CHUNK, up_chunk=DECODE_UP_CHUNK, down_chunk=DECODE_DOWN_CHUNK)

        x1, conv_p, k_p, v_p = _mixer_prompt(xp, ada_p, gm, w_in_b, conv_a_w[l], attn_sinks[l],
                                             w_a_b, w_b_b, w_o_b, tb=PROMPT_MIXER_ROWS)
        xp, ffn_p = _ffn_prompt(x1, ada_p, gn, w_up_b, ffn_conv_w[l], fcb, w_down_b, gf,
                                tb=PROMPT_FFN_ROWS, ch=PROMPT_FFN_COLS, nw=PROMPT_FFN_DOWN_COLS)

        for lst, val in zip(outs, (
                conv_p.reshape(n_p, CONV_W - 1, D_CONV), conv_s,
                from_native(k_p.reshape(n_p, N_KV, HEAD_DIM, WINDOW)), k_s,
                from_native(v_p.reshape(n_p, N_KV, HEAD_DIM, WINDOW)), v_s,
                ffn_p.reshape(n_p, CONV_W - 1, 2 * D_FF), ffn_s)):
            lst.append(val)
    assert depth == 1, "final RMSNorm is fused into the single layer's FFN kernels"
    return (xp.reshape(n_p, seq, D_MODEL), xs) + tuple(jnp.stack(o) for o in outs)
```

```python
import functools

import jax
import jax.numpy as jnp
from jax import lax
from jax.experimental import pallas as pl
from jax.experimental.pallas import tpu as pltpu

f32 = jnp.float32
bf16 = jnp.bfloat16

D_MODEL = 1024
D_CONV = D_MODEL
CONV_W = 3
N_HEADS = 16
N_KV = 4
GROUP = N_HEADS // N_KV
HEAD_DIM = 64
WINDOW = 128
D_FF = 2816
EPS = 1e-6
N_MOD = 6
ATTN_SCALE = HEAD_DIM ** -0.5
KV_COLS = N_KV * HEAD_DIM
C_XIN, C_B, C_C = 0, D_CONV, 2 * D_CONV
C_Q = 3 * D_CONV
C_K = C_Q + N_HEADS * HEAD_DIM
C_V = C_K + KV_COLS
C_GA = C_V + KV_COLS
C_GB = C_GA + D_MODEL
IN_COLS = C_GB + D_MODEL

LANES = 128
SUBLANES = 8
Q_SUB = 128
ATTN_LOOKAHEAD = 2
FFN_LOOKAHEAD = 2
MXU_COLS = 256
VMEM_BYTES_V7X = 64 * 1024 * 1024
VMEM_LIMIT_LARGE = VMEM_BYTES_V7X * 7 // 8
VMEM_LIMIT_MEDIUM = VMEM_BYTES_V7X * 5 // 8
VMEM_LIMIT_SMALL = VMEM_BYTES_V7X // 2


def _rms(x, g):
    ms = jnp.mean(x * x, axis=-1, keepdims=True)
    return x * lax.rsqrt(ms + EPS) * g


def _silu(x):
    return x * jax.nn.sigmoid(x)


def _dot(a, b):
    return jnp.dot(a, b, preferred_element_type=f32)


def _const_spec(shape):
    nd = len(shape)
    return pl.BlockSpec(shape, lambda i: (0,) * nd, pipeline_mode=pl.Buffered(1))


def _ada_kernel(cs_ref, cp_ref, w_ref, b_ref, os_ref, op_ref, *, per_step):
    w = w_ref[...].astype(bf16)
    os_ref[...] = _dot(_silu(cs_ref[...]).astype(bf16), w) + b_ref[...]
    cp = jnp.broadcast_to(_silu(cp_ref[...]), (SUBLANES, D_MODEL)).astype(bf16)
    mod_p = _dot(cp, w)[0:1, :] + b_ref[...]
    for m in range(per_step):
        op_ref[m] = mod_p[:, m * D_MODEL:(m + 1) * D_MODEL]


def _ada(c_sample, c_prompt, w_ada, b_ada, *, per_step):
    n_s = c_sample.shape[0]
    assert c_prompt.shape[0] == 1 and N_MOD % per_step == 0
    cols = per_step * D_MODEL
    return pl.pallas_call(
        functools.partial(_ada_kernel, per_step=per_step),
        grid=(N_MOD // per_step,),
        in_specs=[
            pl.BlockSpec((n_s, D_MODEL), lambda j: (0, 0)),
            pl.BlockSpec((1, D_MODEL), lambda j: (0, 0)),
            pl.BlockSpec((D_MODEL, cols), lambda j: (0, j)),
            pl.BlockSpec((1, cols), lambda j: (0, j)),
        ],
        out_specs=[pl.BlockSpec((n_s, cols), lambda j: (0, j)),
                   pl.BlockSpec((per_step, 1, D_MODEL), lambda j: (j, 0, 0))],
        out_shape=[jax.ShapeDtypeStruct((n_s, N_MOD * D_MODEL), f32),
                   jax.ShapeDtypeStruct((N_MOD, 1, D_MODEL), f32)],
        compiler_params=pltpu.CompilerParams(
            dimension_semantics=("arbitrary",),
            vmem_limit_bytes=VMEM_LIMIT_SMALL),
        name="ada",
    )(c_sample, c_prompt, w_ada, b_ada.reshape(1, -1))


def _mixer_prompt_kernel(sinks_ref, x_ref, ada_ref, g_ref, w_in_ref, cw_ref, w_a_ref, w_b_ref, w_o_ref,
                         x1_ref, conv_ref, knew_ref, vnew_ref,
                         ubuf, klo, khi, vt, attbuf, *, tb):
    i = pl.program_id(0)

    @pl.when(i == 0)
    def _():
        ubuf[0:SUBLANES, :] = jnp.zeros((SUBLANES, D_CONV), f32)
        for r in (klo, khi):
            r[:, 0:WINDOW, :] = jnp.zeros((N_KV, WINDOW, LANES), bf16)
        vt[:, 0:WINDOW] = jnp.zeros((KV_COLS, WINDOW), bf16)

    x = x_ref[...]
    sh1, sc1, gt1 = ada_ref[0], ada_ref[1], ada_ref[2]
    h = (_rms(x, g_ref[...]) * (1.0 + sc1) + sh1).astype(bf16)

    def proj(c0, n):
        return _dot(h, w_in_ref[:, c0:c0 + n])

    k = proj(C_K, KV_COLS)
    v = proj(C_V, KV_COLS)
    u = proj(C_C, D_CONV) * proj(C_XIN, D_CONV)
    ubuf[SUBLANES:SUBLANES + tb, :] = u
    b_gate = proj(C_B, D_CONV)
    q = (proj(C_Q, N_HEADS * HEAD_DIM) * ATTN_SCALE).astype(bf16)
    conv = (cw_ref[0:1, :] * ubuf[SUBLANES - 2:SUBLANES - 2 + tb, :]
            + cw_ref[1:2, :] * ubuf[SUBLANES - 1:SUBLANES - 1 + tb, :]
            + cw_ref[2:3, :] * u)
    conv_in = (b_gate * conv).astype(bf16)
    knew_ref[...] = k[tb - WINDOW:, :].T
    vnew_ref[...] = v[tb - WINDOW:, :].T

    lo = lax.broadcasted_iota(jnp.int32, (tb, LANES), 1) < HEAD_DIM
    for j in range(N_KV // 2):
        pair = k[:, LANES * j:LANES * (j + 1)]
        rolled = pltpu.roll(pair, HEAD_DIM, axis=1)
        zero = jnp.zeros_like(pair)
        klo[2 * j, WINDOW:WINDOW + tb, :] = jnp.where(lo, pair, zero).astype(bf16)
        khi[2 * j, WINDOW:WINDOW + tb, :] = jnp.where(lo, zero, rolled).astype(bf16)
        klo[2 * j + 1, WINDOW:WINDOW + tb, :] = jnp.where(lo, rolled, zero).astype(bf16)
        khi[2 * j + 1, WINDOW:WINDOW + tb, :] = jnp.where(lo, zero, pair).astype(bf16)
    vt[:, WINDOW:WINDOW + tb] = v.T.astype(bf16)

    cc = lax.broadcasted_iota(jnp.int32, (Q_SUB + WINDOW, 2 * Q_SUB), 0)
    col = lax.broadcasted_iota(jnp.int32, (Q_SUB + WINDOW, 2 * Q_SUB), 1)
    rr = col & (Q_SUB - 1)
    band = (cc >= rr) & (cc <= rr + WINDOW)
    first_head = lax.broadcasted_iota(jnp.int32, (1, 2 * Q_SUB), 1) < Q_SUB
    mask0 = band & (cc >= jnp.where(i == 0, WINDOW, 0))

    def scores(j, g, t):
        r0 = Q_SUB * j
        qs = jnp.concatenate([q[r0:r0 + Q_SUB, 2 * LANES * g:2 * LANES * g + LANES],
                              q[r0:r0 + Q_SUB, 2 * LANES * g + LANES:2 * LANES * (g + 1)]], axis=0)
        kr = (klo, khi)[t]
        return lax.dot_general(kr[g, r0:r0 + Q_SUB + WINDOW, :], qs, (((1,), (1,)), ((), ())),
                               preferred_element_type=f32)

    def finish(j, g, t, st):
        r0 = Q_SUB * j
        h0, h1 = GROUP * g + t, GROUP * g + 2 + t
        sink = jnp.where(first_head, sinks_ref[h0], sinks_ref[h1])
        st = jnp.where(mask0 if j == 0 else band, st, -jnp.inf)
        m = jnp.maximum(jnp.max(st, axis=0, keepdims=True), sink)
        e = jnp.exp(st - m)
        linv = 1.0 / (jnp.sum(e, axis=0, keepdims=True) + jnp.exp(sink - m))
        vtg = vt[HEAD_DIM * g:HEAD_DIM * (g + 1), r0:r0 + Q_SUB + WINDOW]
        ot = _dot(vtg, e.astype(bf16)) * linv
        attbuf[HEAD_DIM * h0:HEAD_DIM * (h0 + 1), r0:r0 + Q_SUB] = ot[:, 0:Q_SUB]
        attbuf[HEAD_DIM * h1:HEAD_DIM * (h1 + 1), r0:r0 + Q_SUB] = ot[:, Q_SUB:2 * Q_SUB]

    chains = [(j, g, t) for j in range(tb // Q_SUB) for g in range(N_KV) for t in range(2)]
    fillers = ([functools.partial(proj, c, MXU_COLS) for c in range(C_GA, C_GB + D_MODEL, MXU_COLS)]
               + [(lambda c=c: _dot(conv_in, w_a_ref[:, c:c + MXU_COLS])) for c in range(0, D_MODEL, MXU_COLS)])
    fill_every = -(-len(chains) // len(fillers))
    filled = []
    pending = [scores(*c) for c in chains[:ATTN_LOOKAHEAD]]
    for n, chain in enumerate(chains):
        if n + ATTN_LOOKAHEAD < len(chains):
            pending.append(scores(*chains[n + ATTN_LOOKAHEAD]))
        if n % fill_every == 0:
            filled.extend(f() for f in fillers[len(filled):len(filled) + 1])
        finish(*chain, pending.pop(0))
    filled.extend(f() for f in fillers[len(filled):])
    ga, gb, ya = (jnp.concatenate(filled[j:j + D_MODEL // MXU_COLS], axis=1)
                  for j in range(0, len(filled), D_MODEL // MXU_COLS))

    yb = _dot(attbuf[...].T.astype(bf16), w_b_ref[...])
    mix = (jax.nn.sigmoid(ga) * ya + jax.nn.sigmoid(gb) * yb).astype(bf16)
    for r0 in range(0, tb, tb // 2):
        x1_ref[r0:r0 + tb // 2, :] = x[r0:r0 + tb // 2, :] + gt1 * _dot(mix[r0:r0 + tb // 2, :], w_o_ref[...])

    conv_ref[...] = ubuf[SUBLANES + tb - (CONV_W - 1):SUBLANES + tb, :]
    ubuf[0:SUBLANES, :] = ubuf[tb:tb + SUBLANES, :]
    for r in (klo, khi):
        r[:, 0:WINDOW, :] = r[:, tb:tb + WINDOW, :]
    vt[:, 0:WINDOW] = vt[:, tb:tb + WINDOW]


def _mixer_prompt(x, ada, g_mix, w_in, conv_w, sinks, w_a, w_b, w_o, *, tb):
    s = x.shape[0]
    assert s % tb == 0 and tb % Q_SUB == 0 and tb >= WINDOW
    kv_scratch = pltpu.VMEM((N_KV, WINDOW + tb, LANES), bf16)
    grid_spec = pltpu.PrefetchScalarGridSpec(
        num_scalar_prefetch=1,
        grid=(s // tb,),
        in_specs=[
            pl.BlockSpec((tb, D_MODEL), lambda i, sk: (i, 0)),
            pl.BlockSpec((N_MOD, 1, D_MODEL), lambda i, sk: (0, 0, 0)),
            pl.BlockSpec((1, D_MODEL), lambda i, sk: (0, 0)),
            pl.BlockSpec((D_MODEL, IN_COLS), lambda i, sk: (0, 0), pipeline_mode=pl.Buffered(1)),
            pl.BlockSpec((CONV_W, D_CONV), lambda i, sk: (0, 0)),
            pl.BlockSpec((D_CONV, D_MODEL), lambda i, sk: (0, 0), pipeline_mode=pl.Buffered(1)),
            pl.BlockSpec((N_HEADS * HEAD_DIM, D_MODEL), lambda i, sk: (0, 0), pipeline_mode=pl.Buffered(1)),
            pl.BlockSpec((D_MODEL, D_MODEL), lambda i, sk: (0, 0), pipeline_mode=pl.Buffered(1)),
        ],
        out_specs=[
            pl.BlockSpec((tb, D_MODEL), lambda i, sk: (i, 0)),
            pl.BlockSpec((CONV_W - 1, D_CONV), lambda i, sk: (0, 0)),
            pl.BlockSpec((KV_COLS, WINDOW), lambda i, sk: (0, 0)),
            pl.BlockSpec((KV_COLS, WINDOW), lambda i, sk: (0, 0)),
        ],
        scratch_shapes=[
            pltpu.VMEM((SUBLANES + tb, D_CONV), f32),
            kv_scratch, kv_scratch,
            pltpu.VMEM((KV_COLS, WINDOW + tb), bf16),
            pltpu.VMEM((N_HEADS * HEAD_DIM, tb), f32),
        ],
    )
    return pl.pallas_call(
        functools.partial(_mixer_prompt_kernel, tb=tb),
        grid_spec=grid_spec,
        out_shape=[
            jax.ShapeDtypeStruct((s, D_MODEL), f32),
            jax.ShapeDtypeStruct((CONV_W - 1, D_CONV), f32),
            jax.ShapeDtypeStruct((KV_COLS, WINDOW), f32),
            jax.ShapeDtypeStruct((KV_COLS, WINDOW), f32),
        ],
        compiler_params=pltpu.CompilerParams(
            dimension_semantics=("arbitrary",),
            vmem_limit_bytes=VMEM_LIMIT_LARGE),
        name="mixer_prompt",
    )(sinks, x, ada, g_mix, w_in, conv_w, w_a, w_b, w_o)


def _ffn_prompt_kernel(x_ref, ada_ref, g_ref, w_up_ref, fcw_ref, fcb_ref, w_down_ref, gf_ref,
                       y_ref, fst_ref, upbuf, actbuf, *, tb, ch, nw):
    i = pl.program_id(0)

    @pl.when(i == 0)
    def _():
        upbuf[:, 0:SUBLANES, :] = jnp.zeros((2 * D_FF // LANES, SUBLANES, LANES), f32)

    hb = tb // 2
    sh2, sc2, gt2 = ada_ref[3], ada_ref[4], ada_ref[5]
    h = (_rms(x_ref[...], g_ref[...]) * (1.0 + sc2) + sh2).astype(bf16)

    def up_cols(half, c0):
        up = _dot(h[half * hb:(half + 1) * hb, :], w_up_ref[:, c0:c0 + ch])
        for s in range(ch // LANES):
            upbuf[c0 // LANES + s, SUBLANES:SUBLANES + hb, :] = up[:, s * LANES:(s + 1) * LANES]
        return up

    def conv_cols(c0, up):
        pieces = []
        for s in range(ch // LANES):
            slab = c0 // LANES + s
            cols = slice(c0 + s * LANES, c0 + (s + 1) * LANES)
            pieces.append(fcw_ref[0:1, cols] * upbuf[slab, SUBLANES - 2:SUBLANES - 2 + hb, :]
                          + fcw_ref[1:2, cols] * upbuf[slab, SUBLANES - 1:SUBLANES - 1 + hb, :]
                          + fcw_ref[2:3, cols] * up[:, s * LANES:(s + 1) * LANES]
                          + fcb_ref[0:1, cols])
            upbuf[slab, 0:SUBLANES, :] = upbuf[slab, hb:hb + SUBLANES, :]
        return jnp.concatenate(pieces, axis=1)

    def down_cols(half, n0):
        return _dot(actbuf[half], w_down_ref[:, n0:n0 + nw])

    def finish(half, parts):
        rows = slice(half * hb, (half + 1) * hb)
        x2 = x_ref[rows, :] + gt2 * jnp.concatenate(parts, axis=1)
        y_ref[rows, :] = _rms(x2, gf_ref[...])

    chunks = list(range(0, D_FF, ch))
    down_starts = list(range(0, D_MODEL, nw))
    down_at = {len(chunks) * (k + 1) // (len(down_starts) + 1): n0 for k, n0 in enumerate(down_starts)}
    assert len(down_at) == len(down_starts)
    for half in range(2):
        parts = []
        pending = [(up_cols(half, c0), up_cols(half, D_FF + c0)) for c0 in chunks[:FFN_LOOKAHEAD]]
        for n, c0 in enumerate(chunks):
            if n + FFN_LOOKAHEAD < len(chunks):
                c1 = chunks[n + FFN_LOOKAHEAD]
                pending.append((up_cols(half, c1), up_cols(half, D_FF + c1)))
            if half == 1 and n in down_at:
                parts.append(down_cols(0, down_at[n]))
            up_g, up_v = pending.pop(0)
            actbuf[half, :, c0:c0 + ch] = (_silu(conv_cols(c0, up_g))
                                           * conv_cols(D_FF + c0, up_v)).astype(bf16)
        if half == 1:
            finish(0, parts)
    finish(1, [down_cols(1, n0) for n0 in down_starts])

    for slab in range(2 * D_FF // LANES):
        fst_ref[:, slab * LANES:(slab + 1) * LANES] = upbuf[slab, SUBLANES - (CONV_W - 1):SUBLANES, :]


def _ffn_prompt(x1, ada, g_ffn, w_up, fcw, fcb, w_down, g_final, *, tb, ch, nw):
    s = x1.shape[0]
    assert s % tb == 0 and tb % (4 * SUBLANES) == 0
    assert D_FF % ch == 0 and ch % LANES == 0 and D_MODEL % nw == 0 and nw % LANES == 0
    return pl.pallas_call(
        functools.partial(_ffn_prompt_kernel, tb=tb, ch=ch, nw=nw),
        grid=(s // tb,),
        in_specs=[
            pl.BlockSpec((tb, D_MODEL), lambda i: (i, 0)),
            pl.BlockSpec((N_MOD, 1, D_MODEL), lambda i: (0, 0, 0)),
            pl.BlockSpec((1, D_MODEL), lambda i: (0, 0)),
            _const_spec((D_MODEL, 2 * D_FF)),
            pl.BlockSpec((CONV_W, 2 * D_FF), lambda i: (0, 0)),
            pl.BlockSpec((1, 2 * D_FF), lambda i: (0, 0)),
            _const_spec((D_FF, D_MODEL)),
            pl.BlockSpec((1, D_MODEL), lambda i: (0, 0)),
        ],
        out_specs=[
            pl.BlockSpec((tb, D_MODEL), lambda i: (i, 0)),
            pl.BlockSpec((CONV_W - 1, 2 * D_FF), lambda i: (0, 0)),
        ],
        out_shape=[
            jax.ShapeDtypeStruct((s, D_MODEL), f32),
            jax.ShapeDtypeStruct((CONV_W - 1, 2 * D_FF), f32),
        ],
        scratch_shapes=[
            pltpu.VMEM((2 * D_FF // LANES, SUBLANES + tb // 2, LANES), f32),
            pltpu.VMEM((2, tb // 2, D_FF), bf16),
        ],
        compiler_params=pltpu.CompilerParams(
            dimension_semantics=("arbitrary",),
            vmem_limit_bytes=VMEM_LIMIT_LARGE),
        name="ffn_prompt",
    )(x1, ada, g_ffn, w_up, fcw, fcb, w_down, g_final)


def _mixer_decode_pre_kernel(x_ref, ada_ref, g_ref, w_in_ref, cw_ref, st_ref, w_a_ref,
                             w_in_bf_ref, w_a_bf_ref, q_ref, k_ref, v_ref, knt_ref, vnt_ref, za_ref, sgb_ref, stn_ref,
                             h_scr, proj_scr, *, chunk):
    j = pl.program_id(0)
    n_chunks = IN_COLS // chunk

    @pl.when(j == 0)
    def _():
        sh1 = ada_ref[:, 0:D_MODEL]
        sc1 = ada_ref[:, D_MODEL:2 * D_MODEL]
        h_scr[...] = (_rms(x_ref[:, 0, :], g_ref[...]) * (1.0 + sc1) + sh1).astype(bf16)

    w_chunk = w_in_ref[...].astype(bf16)
    w_in_bf_ref[...] = w_chunk
    part = _dot(h_scr[...], w_chunk)
    for c in range(n_chunks):
        @pl.when(j == c)
        def _(c=c):
            proj_scr[:, c * chunk:(c + 1) * chunk] = part

    @pl.when(j == n_chunks - 1)
    def _():
        def proj(c0, n):
            return proj_scr[:, c0:c0 + n]

        w_a = w_a_ref[...].astype(bf16)
        w_a_bf_ref[...] = w_a
        u = proj(C_C, D_CONV) * proj(C_XIN, D_CONV)
        prev0 = st_ref[:, 0, :]
        prev1 = st_ref[:, 1, :]
        conv = cw_ref[0:1, :] * prev0 + cw_ref[1:2, :] * prev1 + cw_ref[2:3, :] * u
        stn_ref[:, 0, :] = prev1
        stn_ref[:, 1, :] = u
        ya = _dot((proj(C_B, D_CONV) * conv).astype(bf16), w_a)
        q_ref[...] = proj(C_Q, N_HEADS * HEAD_DIM) * ATTN_SCALE
        k = proj(C_K, KV_COLS)
        v = proj(C_V, KV_COLS)
        k_ref[...] = k
        v_ref[...] = v
        knt_ref[...] = k.T
        vnt_ref[...] = v.T
        za_ref[...] = jax.nn.sigmoid(proj(C_GA, D_MODEL)) * ya
        sgb_ref[...] = jax.nn.sigmoid(proj(C_GB, D_MODEL))


def _mixer_decode_pre(x, ada, g_mix, w_in, conv_w, state, w_a, *, chunk):
    n = x.shape[0]
    assert IN_COLS % chunk == 0 and chunk % LANES == 0
    const2 = lambda j: (0, 0)
    const3 = lambda j: (0, 0, 0)
    row_block = lambda cols: pl.BlockSpec((n, cols), const2)
    state_block = pl.BlockSpec((n, CONV_W - 1, D_CONV), const3)
    shapes = [((D_MODEL, IN_COLS), bf16), ((D_CONV, D_MODEL), bf16),
              ((n, N_HEADS * HEAD_DIM), f32), ((n, KV_COLS), f32), ((n, KV_COLS), f32),
              ((KV_COLS, n), f32), ((KV_COLS, n), f32),
              ((n, D_MODEL), f32), ((n, D_MODEL), f32), ((n, CONV_W - 1, D_CONV), f32)]
    return pl.pallas_call(
        functools.partial(_mixer_decode_pre_kernel, chunk=chunk),
        grid=(IN_COLS // chunk,),
        in_specs=[pl.BlockSpec((n, 1, D_MODEL), const3), row_block(2 * D_MODEL), pl.BlockSpec((1, D_MODEL), const2),
                  pl.BlockSpec((D_MODEL, chunk), lambda j: (0, j)),
                  pl.BlockSpec((CONV_W, D_CONV), const2), state_block,
                  pl.BlockSpec((D_CONV, D_MODEL), const2)],
        out_specs=[pl.BlockSpec((D_MODEL, chunk), lambda j: (0, j)), pl.BlockSpec((D_CONV, D_MODEL), const2),
                   row_block(N_HEADS * HEAD_DIM), row_block(KV_COLS), row_block(KV_COLS),
                   pl.BlockSpec((KV_COLS, n), const2), pl.BlockSpec((KV_COLS, n), const2),
                   row_block(D_MODEL), row_block(D_MODEL), state_block],
        out_shape=[jax.ShapeDtypeStruct(shp, dt) for shp, dt in shapes],
        scratch_shapes=[pltpu.VMEM((n, D_MODEL), bf16), pltpu.VMEM((n, IN_COLS), f32)],
        compiler_params=pltpu.CompilerParams(
            dimension_semantics=("arbitrary",),
            vmem_limit_bytes=VMEM_LIMIT_MEDIUM),
        name="mixer_decode_pre",
    )(x, ada, g_mix, w_in, conv_w, state, w_a)


def _attn_decode_kernel(q_ref, kn_ref, vn_ref, knt_ref, vnt_ref, ck_ref, cv_ref, sink_ref,
                        att_ref, ok_ref, ov_ref, *, bb):
    step = pl.program_id(0)
    last = lax.broadcasted_iota(jnp.int32, (HEAD_DIM, WINDOW), 1) == WINDOW - 1
    to_front = (LANES - step * bb) % LANES
    kstep = [pltpu.roll(knt_ref[g], to_front, axis=1) for g in range(N_KV)]
    vstep = [pltpu.roll(vnt_ref[g], to_front, axis=1) for g in range(N_KV)]

    pairs = [(b, g) for b in range(bb) for g in range(N_KV)]
    n_stage = 7
    cuts = [len(pairs) * s // n_stage for s in range(n_stage + 1)]

    def shift_caches(stage):
        for b, g in pairs[cuts[stage]:cuts[stage + 1]]:
            ok_ref[b, g] = jnp.where(last, pltpu.roll(kstep[g], WINDOW - 1 - b, axis=1),
                                     pltpu.roll(ck_ref[b, g], WINDOW - 1, axis=1))
            ov_ref[b, g] = jnp.where(last, pltpu.roll(vstep[g], WINDOW - 1 - b, axis=1),
                                     pltpu.roll(cv_ref[b, g], WINDOW - 1, axis=1))

    sinks = [sink_ref[g][:, 0:1] for g in range(N_KV)]
    shift_caches(0)
    s_old = [_dot(q_ref[b, g].astype(bf16), ck_ref[b, g].astype(bf16)) for b, g in pairs]
    shift_caches(1)
    s_new = [jnp.sum(q_ref[b, g] * kn_ref[b, g], axis=-1, keepdims=True) for b, g in pairs]
    shift_caches(2)
    m = [jnp.maximum(jnp.maximum(jnp.max(so, axis=-1, keepdims=True), sn), sinks[g])
         for (b, g), so, sn in zip(pairs, s_old, s_new)]
    shift_caches(3)
    e_old = [jnp.exp(so - mm) for so, mm in zip(s_old, m)]
    e_new = [jnp.exp(sn - mm) for sn, mm in zip(s_new, m)]
    den = [jnp.sum(eo, axis=-1, keepdims=True) + en + jnp.exp(sinks[g] - mm)
           for (b, g), eo, en, mm in zip(pairs, e_old, e_new, m)]
    shift_caches(4)
    o_old = [lax.dot_general(eo.astype(bf16), cv_ref[b, g].astype(bf16), (((1,), (1,)), ((), ())),
                             preferred_element_type=f32) for (b, g), eo in zip(pairs, e_old)]
    shift_caches(5)
    for (b, g), oo, en, dd in zip(pairs, o_old, e_new, den):
        att_ref[b, g] = (oo + en * vn_ref[b, g]) / dd
    shift_caches(6)


def _attn_decode(q4, kn4, vn4, knt, vnt, ck, cv, sink_b, *, bb):
    n = q4.shape[0]
    assert n % bb == 0 and n == LANES and WINDOW == LANES
    cache_spec = pl.BlockSpec((bb, N_KV, HEAD_DIM, WINDOW), lambda b: (b, 0, 0, 0))
    row_spec = pl.BlockSpec((bb, N_KV, 1, HEAD_DIM), lambda b: (b, 0, 0, 0))
    q_spec = pl.BlockSpec((bb, N_KV, GROUP, HEAD_DIM), lambda b: (b, 0, 0, 0))
    new_t_spec = pl.BlockSpec((N_KV, HEAD_DIM, n), lambda b: (0, 0, 0))
    return pl.pallas_call(
        functools.partial(_attn_decode_kernel, bb=bb),
        grid=(n // bb,),
        in_specs=[q_spec, row_spec, row_spec, new_t_spec, new_t_spec, cache_spec, cache_spec,
                  pl.BlockSpec((N_KV, GROUP, LANES), lambda b: (0, 0, 0))],
        out_specs=[q_spec, cache_spec, cache_spec],
        out_shape=[jax.ShapeDtypeStruct((n, N_KV, GROUP, HEAD_DIM), f32),
                   jax.ShapeDtypeStruct((n, N_KV, HEAD_DIM, WINDOW), f32),
                   jax.ShapeDtypeStruct((n, N_KV, HEAD_DIM, WINDOW), f32)],
        compiler_params=pltpu.CompilerParams(dimension_semantics=("arbitrary",)),
        name="attn_decode",
    )(q4, kn4, vn4, knt, vnt, ck, cv, sink_b)


def _post_decode_kernel(x_ref, ada_ref, za_ref, sgb_ref, att_ref, w_b_ref, w_o_ref, g_ref, w_up_ref,
                        fcw_ref, fcb_ref, w_down_ref, gf_ref, fst_ref,
                        w_b_bf_ref, w_o_bf_ref, w_up_bf_ref, w_down_bf_ref, y_ref, fstn_ref,
                        yb_scr, x1_scr, h_scr, up_scr, act_scr, acc_scr, *, phases):
    j = pl.program_id(0)
    (b0, nb, cb), (o0, no, co), (u0, nu, cu), (d0, nd, cd) = phases

    def mod(k):
        return ada_ref[:, k * D_MODEL:(k + 1) * D_MODEL]

    for c in range(nb):
        @pl.when(j == b0 + c)
        def _(c=c):
            w = w_b_ref[...].astype(bf16)
            w_b_bf_ref[...] = w
            yb_scr[:, c * cb:(c + 1) * cb] = _dot(att_ref[...].astype(bf16), w)

    for c in range(no):
        @pl.when(j == o0 + c)
        def _(c=c):
            w = w_o_ref[...].astype(bf16)
            w_o_bf_ref[...] = w
            mix = (za_ref[...] + sgb_ref[...] * yb_scr[...]).astype(bf16)
            cols = slice(c * co, (c + 1) * co)
            x1_scr[:, cols] = x_ref[:, 0, cols] + mod(2)[:, cols] * _dot(mix, w)

    for c in range(nu):
        @pl.when(j == u0 + c)
        def _(c=c):
            if c == 0:
                h_scr[...] = (_rms(x1_scr[...], g_ref[...]) * (1.0 + mod(4)) + mod(3)).astype(bf16)
            w = w_up_ref[...].astype(bf16)
            w_up_bf_ref[...] = w
            up_scr[:, c * cu:(c + 1) * cu] = _dot(h_scr[...], w)

    for c in range(nd):
        @pl.when(j == d0 + c)
        def _(c=c):
            if c == 0:
                up = up_scr[...]
                prev0 = fst_ref[:, 0, :]
                prev1 = fst_ref[:, 1, :]
                conv = fcw_ref[0:1, :] * prev0 + fcw_ref[1:2, :] * prev1 + fcw_ref[2:3, :] * up + fcb_ref[...]
                fstn_ref[:, 0, :] = prev1
                fstn_ref[:, 1, :] = up
                act_scr[...] = (_silu(conv[:, 0:D_FF]) * conv[:, D_FF:2 * D_FF]).astype(bf16)
            w = w_down_ref[...].astype(bf16)
            w_down_bf_ref[...] = w
            cols = slice(c * cd, (c + 1) * cd)
            acc_scr[:, cols] = x1_scr[:, cols] + mod(5)[:, cols] * _dot(act_scr[...], w)
            if c == nd - 1:
                y_ref[:, 0, :] = _rms(acc_scr[...], gf_ref[...])


def _post_decode(x, ada, za, sgb, att, w_b, w_o, g_ffn, w_up, fcw, fcb, w_down, g_final, fstate,
                 *, proj_chunk, up_chunk, down_chunk):
    n = x.shape[0]
    assert D_MODEL % proj_chunk == 0 and (2 * D_FF) % up_chunk == 0 and D_MODEL % down_chunk == 0
    assert proj_chunk % LANES == 0 and up_chunk % LANES == 0 and down_chunk % LANES == 0
    nb = no = D_MODEL // proj_chunk
    nu, nd = 2 * D_FF // up_chunk, D_MODEL // down_chunk
    b0, o0, u0, d0 = 0, nb, nb + no, nb + no + nu
    phases = ((b0, nb, proj_chunk), (o0, no, proj_chunk), (u0, nu, up_chunk), (d0, nd, down_chunk))

    def chunk_index(start, count):
        return lambda j: jnp.clip(j - start, 0, count - 1)

    ib, io, iu, idn = (chunk_index(s0, cnt) for s0, cnt, _ in phases)
    const2 = lambda j: (0, 0)
    const3 = lambda j: (0, 0, 0)
    rows = lambda cols: pl.BlockSpec((n, cols), const2, pipeline_mode=pl.Buffered(1))
    state_block = pl.BlockSpec((n, CONV_W - 1, 2 * D_FF), const3, pipeline_mode=pl.Buffered(1))
    token_block = pl.BlockSpec((n, 1, D_MODEL), const3, pipeline_mode=pl.Buffered(1))
    col_chunk = lambda k, width, idx: pl.BlockSpec((k, width), lambda j: (0, idx(j)))
    w_down_block = col_chunk(D_FF, down_chunk, idn)
    return pl.pallas_call(
        functools.partial(_post_decode_kernel, phases=phases),
        grid=(d0 + nd,),
        in_specs=[token_block, rows(N_MOD * D_MODEL), rows(D_MODEL), rows(D_MODEL), rows(N_HEADS * HEAD_DIM),
                  col_chunk(N_HEADS * HEAD_DIM, proj_chunk, ib), col_chunk(D_MODEL, proj_chunk, io),
                  pl.BlockSpec((1, D_MODEL), const2), col_chunk(D_MODEL, up_chunk, iu),
                  pl.BlockSpec((CONV_W, 2 * D_FF), const2), pl.BlockSpec((1, 2 * D_FF), const2),
                  w_down_block, pl.BlockSpec((1, D_MODEL), const2), state_block],
        out_specs=[col_chunk(N_HEADS * HEAD_DIM, proj_chunk, ib), col_chunk(D_MODEL, proj_chunk, io),
                   col_chunk(D_MODEL, up_chunk, iu), w_down_block, token_block, state_block],
        out_shape=[jax.ShapeDtypeStruct((N_HEADS * HEAD_DIM, D_MODEL), bf16),
                   jax.ShapeDtypeStruct((D_MODEL, D_MODEL), bf16),
                   jax.ShapeDtypeStruct((D_MODEL, 2 * D_FF), bf16),
                   jax.ShapeDtypeStruct((D_FF, D_MODEL), bf16),
                   jax.ShapeDtypeStruct((n, 1, D_MODEL), f32),
                   jax.ShapeDtypeStruct((n, CONV_W - 1, 2 * D_FF), f32)],
        scratch_shapes=[pltpu.VMEM((n, D_MODEL), f32), pltpu.VMEM((n, D_MODEL), f32), pltpu.VMEM((n, D_MODEL), bf16),
                        pltpu.VMEM((n, 2 * D_FF), f32), pltpu.VMEM((n, D_FF), bf16), pltpu.VMEM((n, D_MODEL), f32)],
        compiler_params=pltpu.CompilerParams(
            dimension_semantics=("arbitrary",),
            vmem_limit_bytes=VMEM_LIMIT_LARGE),
        name="post_decode",
    )(x, ada, za, sgb, att, w_b, w_o, g_ffn, w_up, fcw, fcb, w_down, g_final, fstate)


PROMPT_MIXER_ROWS = 512
PROMPT_FFN_ROWS = 1024
PROMPT_FFN_COLS = 256
PROMPT_FFN_DOWN_COLS = 512
DECODE_ATTN_BATCH = 32
ADA_PER_STEP = 2
DECODE_IN_CHUNK = 1664
DECODE_PROJ_CHUNK = 512
DECODE_UP_CHUNK = 512
DECODE_DOWN_CHUNK = 256


def kernel(x_prompt, x_sample, c_prompt, c_sample, state_conv_a, cache_k_win, cache_v_win, state_ffn_conv, w_ada, b_ada, g_mix, w_in, conv_a_w, attn_sinks, w_a_out, w_b_out, w_o, g_ffn, w_up, ffn_conv_w, ffn_conv_b, w_down, g_final):
    depth = w_in.shape[0]
    n_p, seq, _ = x_prompt.shape
    n_s, t_s, _ = x_sample.shape
    assert n_p == 1 and t_s == 1, "one prompt sequence and single-token decode only"
    xp = x_prompt.reshape(seq, D_MODEL)
    xs = x_sample
    gf = g_final.reshape(1, D_MODEL)
    outs = [[] for _ in range(8)]
    for l in range(depth):
        ada_s, ada_p = _ada(c_sample, c_prompt, w_ada[l], b_ada[l], per_step=ADA_PER_STEP)
        gm, gn = g_mix[l].reshape(1, D_MODEL), g_ffn[l].reshape(1, D_MODEL)
        fcb = ffn_conv_b[l].reshape(1, 2 * D_FF)

        w_in_b, w_a_b, q, k_n, v_n, k_nt, v_nt, za, sgb, conv_s = _mixer_decode_pre(
            xs, ada_s, gm, w_in[l], conv_a_w[l], state_conv_a[l], w_a_out[l], chunk=DECODE_IN_CHUNK)
        to_native = lambda c: c.transpose(0, 2, 3, 1)
        from_native = lambda c: c.transpose(0, 3, 1, 2)
        sink_b = jnp.broadcast_to(attn_sinks[l].reshape(N_KV, GROUP, 1), (N_KV, GROUP, LANES))
        att4, k_s, v_s = _attn_decode(
            q.reshape(n_s, N_KV, GROUP, HEAD_DIM),
            k_n.reshape(n_s, N_KV, 1, HEAD_DIM), v_n.reshape(n_s, N_KV, 1, HEAD_DIM),
            k_nt.reshape(N_KV, HEAD_DIM, n_s), v_nt.reshape(N_KV, HEAD_DIM, n_s),
            to_native(cache_k_win[l]), to_native(cache_v_win[l]), sink_b, bb=DECODE_ATTN_BATCH)
        k_s, v_s = from_native(k_s), from_native(v_s)
        w_b_b, w_o_b, w_up_b, w_down_b, xs, ffn_s = _post_decode(
            xs, ada_s, za, sgb, att4.reshape(n_s, N_HEADS * HEAD_DIM), w_b_out[l], w_o[l], gn, w_up[l],
            ffn_conv_w[l], fcb, w_down[l], gf, state_ffn_conv[l],
            proj_chunk=DECODE_PROJ_CHUNK, up_chunk=DECODE_UP_CHUNK, down_chunk=DECODE_DOWN_CHUNK)

        x1, conv_p, k_p, v_p = _mixer_prompt(xp, ada_p, gm, w_in_b, conv_a_w[l], attn_sinks[l],
                                             w_a_b, w_b_b, w_o_b, tb=PROMPT_MIXER_ROWS)
        xp, ffn_p = _ffn_prompt(x1, ada_p, gn, w_up_b, ffn_conv_w[l], fcb, w_down_b, gf,
                                tb=PROMPT_FFN_ROWS, ch=PROMPT_FFN_COLS, nw=PROMPT_FFN_DOWN_COLS)

        for lst, val in zip(outs, (
                conv_p.reshape(n_p, CONV_W - 1, D_CONV), conv_s,
                from_native(k_p.reshape(n_p, N_KV, HEAD_DIM, WINDOW)), k_s,
                from_native(v_p.reshape(n_p, N_KV, HEAD_DIM, WINDOW)), v_s,
                ffn_p.reshape(n_p, CONV_W - 1, 2 * D_FF), ffn_s)):
            lst.append(val)
    assert depth == 1, "final RMSNorm is fused into the single layer's FFN kernels"
    return (xp.reshape(n_p, seq, D_MODEL), xs) + tuple(jnp.stack(o) for o in outs)
```

```python
import functools

import jax
import jax.numpy as jnp
from jax import lax
from jax.experimental import pallas as pl
from jax.experimental.pallas import tpu as pltpu

f32 = jnp.float32
bf16 = jnp.bfloat16

D_MODEL = 1024
D_CONV = D_MODEL
CONV_W = 3
N_HEADS = 16
N_KV = 4
GROUP = N_HEADS // N_KV
HEAD_DIM = 64
WINDOW = 128
D_FF = 2816
EPS = 1e-6
N_MOD = 6
ATTN_SCALE = HEAD_DIM ** -0.5
KV_COLS = N_KV * HEAD_DIM
C_XIN, C_B, C_C = 0, D_CONV, 2 * D_CONV
C_Q = 3 * D_CONV
C_K = C_Q + N_HEADS * HEAD_DIM
C_V = C_K + KV_COLS
C_GA = C_V + KV_COLS
C_GB = C_GA + D_MODEL
IN_COLS = C_GB + D_MODEL

LANES = 128
SUBLANES = 8
Q_SUB = 128
ATTN_LOOKAHEAD = 3
FFN_LOOKAHEAD = 2
MXU_COLS = 256
VMEM_BYTES_V7X = 64 * 1024 * 1024
VMEM_LIMIT_LARGE = VMEM_BYTES_V7X * 7 // 8
VMEM_LIMIT_MEDIUM = VMEM_BYTES_V7X * 5 // 8
VMEM_LIMIT_SMALL = VMEM_BYTES_V7X // 2


def _rms(x, g):
    ms = jnp.mean(x * x, axis=-1, keepdims=True)
    return x * lax.rsqrt(ms + EPS) * g


def _silu(x):
    return x * jax.nn.sigmoid(x)


def _dot(a, b):
    return jnp.dot(a, b, preferred_element_type=f32)


def _const_spec(shape):
    nd = len(shape)
    return pl.BlockSpec(shape, lambda i: (0,) * nd, pipeline_mode=pl.Buffered(1))


def _ada_kernel(cs_ref, cp_ref, w_ref, b_ref, os_ref, op_ref, *, per_step):
    w = w_ref[...].astype(bf16)
    os_ref[...] = _dot(_silu(cs_ref[...]).astype(bf16), w) + b_ref[...]
    cp = jnp.broadcast_to(_silu(cp_ref[...]), (SUBLANES, D_MODEL)).astype(bf16)
    mod_p = _dot(cp, w)[0:1, :] + b_ref[...]
    for m in range(per_step):
        op_ref[m] = mod_p[:, m * D_MODEL:(m + 1) * D_MODEL]


def _ada(c_sample, c_prompt, w_ada, b_ada, *, per_step):
    n_s = c_sample.shape[0]
    assert c_prompt.shape[0] == 1 and N_MOD % per_step == 0
    cols = per_step * D_MODEL
    return pl.pallas_call(
        functools.partial(_ada_kernel, per_step=per_step),
        grid=(N_MOD // per_step,),
        in_specs=[
            pl.BlockSpec((n_s, D_MODEL), lambda j: (0, 0)),
            pl.BlockSpec((1, D_MODEL), lambda j: (0, 0)),
            pl.BlockSpec((D_MODEL, cols), lambda j: (0, j)),
            pl.BlockSpec((1, cols), lambda j: (0, j)),
        ],
        out_specs=[pl.BlockSpec((n_s, cols), lambda j: (0, j)),
                   pl.BlockSpec((per_step, 1, D_MODEL), lambda j: (j, 0, 0))],
        out_shape=[jax.ShapeDtypeStruct((n_s, N_MOD * D_MODEL), f32),
                   jax.ShapeDtypeStruct((N_MOD, 1, D_MODEL), f32)],
        compiler_params=pltpu.CompilerParams(
            dimension_semantics=("arbitrary",),
            vmem_limit_bytes=VMEM_LIMIT_SMALL),
        name="ada",
    )(c_sample, c_prompt, w_ada, b_ada.reshape(1, -1))


def _mixer_prompt_kernel(sinks_ref, x_ref, ada_ref, g_ref, w_in_ref, cw_ref, w_a_ref, w_b_ref, w_o_ref,
                         x1_ref, conv_ref, knew_ref, vnew_ref,
                         ubuf, klo, khi, vt, attbuf, *, tb):
    i = pl.program_id(0)

    @pl.when(i == 0)
    def _():
        ubuf[0:SUBLANES, :] = jnp.zeros((SUBLANES, D_CONV), f32)
        for r in (klo, khi):
            r[:, 0:WINDOW, :] = jnp.zeros((N_KV, WINDOW, LANES), bf16)
        vt[:, 0:WINDOW] = jnp.zeros((KV_COLS, WINDOW), bf16)

    x = x_ref[...]
    sh1, sc1, gt1 = ada_ref[0], ada_ref[1], ada_ref[2]
    h = (_rms(x, g_ref[...]) * (1.0 + sc1) + sh1).astype(bf16)

    def proj(c0, n):
        return _dot(h, w_in_ref[:, c0:c0 + n])

    k = proj(C_K, KV_COLS)
    v = proj(C_V, KV_COLS)
    u = proj(C_C, D_CONV) * proj(C_XIN, D_CONV)
    ubuf[SUBLANES:SUBLANES + tb, :] = u
    b_gate = proj(C_B, D_CONV)
    q = (proj(C_Q, N_HEADS * HEAD_DIM) * ATTN_SCALE).astype(bf16)
    conv = (cw_ref[0:1, :] * ubuf[SUBLANES - 2:SUBLANES - 2 + tb, :]
            + cw_ref[1:2, :] * ubuf[SUBLANES - 1:SUBLANES - 1 + tb, :]
            + cw_ref[2:3, :] * u)
    conv_in = (b_gate * conv).astype(bf16)
    knew_ref[...] = k[tb - WINDOW:, :].T
    vnew_ref[...] = v[tb - WINDOW:, :].T

    lo = lax.broadcasted_iota(jnp.int32, (tb, LANES), 1) < HEAD_DIM
    for j in range(N_KV // 2):
        pair = k[:, LANES * j:LANES * (j + 1)]
        rolled = pltpu.roll(pair, HEAD_DIM, axis=1)
        zero = jnp.zeros_like(pair)
        klo[2 * j, WINDOW:WINDOW + tb, :] = jnp.where(lo, pair, zero).astype(bf16)
        khi[2 * j, WINDOW:WINDOW + tb, :] = jnp.where(lo, zero, rolled).astype(bf16)
        klo[2 * j + 1, WINDOW:WINDOW + tb, :] = jnp.where(lo, rolled, zero).astype(bf16)
        khi[2 * j + 1, WINDOW:WINDOW + tb, :] = jnp.where(lo, zero, pair).astype(bf16)
    vt[:, WINDOW:WINDOW + tb] = v.T.astype(bf16)

    cc = lax.broadcasted_iota(jnp.int32, (Q_SUB + WINDOW, 2 * Q_SUB), 0)
    col = lax.broadcasted_iota(jnp.int32, (Q_SUB + WINDOW, 2 * Q_SUB), 1)
    rr = col & (Q_SUB - 1)
    band = (cc >= rr) & (cc <= rr + WINDOW)
    first_head = lax.broadcasted_iota(jnp.int32, (1, 2 * Q_SUB), 1) < Q_SUB
    mask0 = band & (cc >= jnp.where(i == 0, WINDOW, 0))

    def scores(j, g, t):
        r0 = Q_SUB * j
        qs = jnp.concatenate([q[r0:r0 + Q_SUB, 2 * LANES * g:2 * LANES * g + LANES],
                              q[r0:r0 + Q_SUB, 2 * LANES * g + LANES:2 * LANES * (g + 1)]], axis=0)
        kr = (klo, khi)[t]
        return lax.dot_general(kr[g, r0:r0 + Q_SUB + WINDOW, :], qs, (((1,), (1,)), ((), ())),
                               preferred_element_type=f32)

    def finish(j, g, t, st):
        r0 = Q_SUB * j
        h0, h1 = GROUP * g + t, GROUP * g + 2 + t
        sink = jnp.where(first_head, sinks_ref[h0], sinks_ref[h1])
        st = jnp.where(mask0 if j == 0 else band, st, -jnp.inf)
        m = jnp.maximum(jnp.max(st, axis=0, keepdims=True), sink)
        e = jnp.exp(st - m)
        linv = 1.0 / (jnp.sum(e, axis=0, keepdims=True) + jnp.exp(sink - m))
        vtg = vt[HEAD_DIM * g:HEAD_DIM * (g + 1), r0:r0 + Q_SUB + WINDOW]
        ot = _dot(vtg, e.astype(bf16)) * linv
        attbuf[HEAD_DIM * h0:HEAD_DIM * (h0 + 1), r0:r0 + Q_SUB] = ot[:, 0:Q_SUB]
        attbuf[HEAD_DIM * h1:HEAD_DIM * (h1 + 1), r0:r0 + Q_SUB] = ot[:, Q_SUB:2 * Q_SUB]

    chains = [(j, g, t) for j in range(tb // Q_SUB) for g in range(N_KV) for t in range(2)]
    fillers = ([functools.partial(proj, c, MXU_COLS) for c in range(C_GA, C_GB + D_MODEL, MXU_COLS)]
               + [(lambda c=c: _dot(conv_in, w_a_ref[:, c:c + MXU_COLS])) for c in range(0, D_MODEL, MXU_COLS)])
    fill_every = -(-len(chains) // len(fillers))
    filled = []
    pending = [scores(*c) for c in chains[:ATTN_LOOKAHEAD]]
    for n, chain in enumerate(chains):
        if n + ATTN_LOOKAHEAD < len(chains):
            pending.append(scores(*chains[n + ATTN_LOOKAHEAD]))
        if n % fill_every == 0:
            filled.extend(f() for f in fillers[len(filled):len(filled) + 1])
        finish(*chain, pending.pop(0))
    filled.extend(f() for f in fillers[len(filled):])
    ga, gb, ya = (jnp.concatenate(filled[j:j + D_MODEL // MXU_COLS], axis=1)
                  for j in range(0, len(filled), D_MODEL // MXU_COLS))

    yb = _dot(attbuf[...].T.astype(bf16), w_b_ref[...])
    mix = (jax.nn.sigmoid(ga) * ya + jax.nn.sigmoid(gb) * yb).astype(bf16)
    for r0 in range(0, tb, tb // 2):
        x1_ref[r0:r0 + tb // 2, :] = x[r0:r0 + tb // 2, :] + gt1 * _dot(mix[r0:r0 + tb // 2, :], w_o_ref[...])

    conv_ref[...] = ubuf[SUBLANES + tb - (CONV_W - 1):SUBLANES + tb, :]
    ubuf[0:SUBLANES, :] = ubuf[tb:tb + SUBLANES, :]
    for r in (klo, khi):
        r[:, 0:WINDOW, :] = r[:, tb:tb + WINDOW, :]
    vt[:, 0:WINDOW] = vt[:, tb:tb + WINDOW]


def _mixer_prompt(x, ada, g_mix, w_in, conv_w, sinks, w_a, w_b, w_o, *, tb):
    s = x.shape[0]
    assert s % tb == 0 and tb % Q_SUB == 0 and tb >= WINDOW
    kv_scratch = pltpu.VMEM((N_KV, WINDOW + tb, LANES), bf16)
    grid_spec = pltpu.PrefetchScalarGridSpec(
        num_scalar_prefetch=1,
        grid=(s // tb,),
        in_specs=[
            pl.BlockSpec((tb, D_MODEL), lambda i, sk: (i, 0)),
            pl.BlockSpec((N_MOD, 1, D_MODEL), lambda i, sk: (0, 0, 0)),
            pl.BlockSpec((1, D_MODEL), lambda i, sk: (0, 0)),
            pl.BlockSpec((D_MODEL, IN_COLS), lambda i, sk: (0, 0), pipeline_mode=pl.Buffered(1)),
            pl.BlockSpec((CONV_W, D_CONV), lambda i, sk: (0, 0)),
            pl.BlockSpec((D_CONV, D_MODEL), lambda i, sk: (0, 0), pipeline_mode=pl.Buffered(1)),
            pl.BlockSpec((N_HEADS * HEAD_DIM, D_MODEL), lambda i, sk: (0, 0), pipeline_mode=pl.Buffered(1)),
            pl.BlockSpec((D_MODEL, D_MODEL), lambda i, sk: (0, 0), pipeline_mode=pl.Buffered(1)),
        ],
        out_specs=[
            pl.BlockSpec((tb, D_MODEL), lambda i, sk: (i, 0)),
            pl.BlockSpec((CONV_W - 1, D_CONV), lambda i, sk: (0, 0)),
            pl.BlockSpec((KV_COLS, WINDOW), lambda i, sk: (0, 0)),
            pl.BlockSpec((KV_COLS, WINDOW), lambda i, sk: (0, 0)),
        ],
        scratch_shapes=[
            pltpu.VMEM((SUBLANES + tb, D_CONV), f32),
            kv_scratch, kv_scratch,
            pltpu.VMEM((KV_COLS, WINDOW + tb), bf16),
            pltpu.VMEM((N_HEADS * HEAD_DIM, tb), f32),
        ],
    )
    return pl.pallas_call(
        functools.partial(_mixer_prompt_kernel, tb=tb),
        grid_spec=grid_spec,
        out_shape=[
            jax.ShapeDtypeStruct((s, D_MODEL), f32),
            jax.ShapeDtypeStruct((CONV_W - 1, D_CONV), f32),
            jax.ShapeDtypeStruct((KV_COLS, WINDOW), f32),
            jax.ShapeDtypeStruct((KV_COLS, WINDOW), f32),
        ],
        compiler_params=pltpu.CompilerParams(
            dimension_semantics=("arbitrary",),
            vmem_limit_bytes=VMEM_LIMIT_LARGE),
        name="mixer_prompt",
    )(sinks, x, ada, g_mix, w_in, conv_w, w_a, w_b, w_o)


def _ffn_prompt_kernel(x_ref, ada_ref, g_ref, w_up_ref, fcw_ref, fcb_ref, w_down_ref, gf_ref,
                       y_ref, fst_ref, upbuf, actbuf, *, tb, ch, nw):
    i = pl.program_id(0)

    @pl.when(i == 0)
    def _():
        upbuf[:, 0:SUBLANES, :] = jnp.zeros((2 * D_FF // LANES, SUBLANES, LANES), f32)

    hb = tb // 2
    sh2, sc2, gt2 = ada_ref[3], ada_ref[4], ada_ref[5]
    h = (_rms(x_ref[...], g_ref[...]) * (1.0 + sc2) + sh2).astype(bf16)

    def up_cols(half, c0):
        up = _dot(h[half * hb:(half + 1) * hb, :], w_up_ref[:, c0:c0 + ch])
        for s in range(ch // LANES):
            upbuf[c0 // LANES + s, SUBLANES:SUBLANES + hb, :] = up[:, s * LANES:(s + 1) * LANES]
        return up

    def conv_cols(c0, up):
        pieces = []
        for s in range(ch // LANES):
            slab = c0 // LANES + s
            cols = slice(c0 + s * LANES, c0 + (s + 1) * LANES)
            pieces.append(fcw_ref[0:1, cols] * upbuf[slab, SUBLANES - 2:SUBLANES - 2 + hb, :]
                          + fcw_ref[1:2, cols] * upbuf[slab, SUBLANES - 1:SUBLANES - 1 + hb, :]
                          + fcw_ref[2:3, cols] * up[:, s * LANES:(s + 1) * LANES]
                          + fcb_ref[0:1, cols])
            upbuf[slab, 0:SUBLANES, :] = upbuf[slab, hb:hb + SUBLANES, :]
        return jnp.concatenate(pieces, axis=1)

    def down_cols(half, n0):
        return _dot(actbuf[half], w_down_ref[:, n0:n0 + nw])

    def finish(half, parts):
        rows = slice(half * hb, (half + 1) * hb)
        x2 = x_ref[rows, :] + gt2 * jnp.concatenate(parts, axis=1)
        y_ref[rows, :] = _rms(x2, gf_ref[...])

    chunks = list(range(0, D_FF, ch))
    down_starts = list(range(0, D_MODEL, nw))
    down_at = {len(chunks) * (k + 1) // (len(down_starts) + 1): n0 for k, n0 in enumerate(down_starts)}
    assert len(down_at) == len(down_starts)
    for half in range(2):
        parts = []
        pending = [(up_cols(half, c0), up_cols(half, D_FF + c0)) for c0 in chunks[:FFN_LOOKAHEAD]]
        for n, c0 in enumerate(chunks):
            if n + FFN_LOOKAHEAD < len(chunks):
                c1 = chunks[n + FFN_LOOKAHEAD]
                pending.append((up_cols(half, c1), up_cols(half, D_FF + c1)))
            if half == 1 and n in down_at:
                parts.append(down_cols(0, down_at[n]))
            up_g, up_v = pending.pop(0)
            actbuf[half, :, c0:c0 + ch] = (_silu(conv_cols(c0, up_g))
                                           * conv_cols(D_FF + c0, up_v)).astype(bf16)
        if half == 1:
            finish(0, parts)
    finish(1, [down_cols(1, n0) for n0 in down_starts])

    for slab in range(2 * D_FF // LANES):
        fst_ref[:, slab * LANES:(slab + 1) * LANES] = upbuf[slab, SUBLANES - (CONV_W - 1):SUBLANES, :]


def _ffn_prompt(x1, ada, g_ffn, w_up, fcw, fcb, w_down, g_final, *, tb, ch, nw):
    s = x1.shape[0]
    assert s % tb == 0 and tb % (4 * SUBLANES) == 0
    assert D_FF % ch == 0 and ch % LANES == 0 and D_MODEL % nw == 0 and nw % LANES == 0
    return pl.pallas_call(
        functools.partial(_ffn_prompt_kernel, tb=tb, ch=ch, nw=nw),
        grid=(s // tb,),
        in_specs=[
            pl.BlockSpec((tb, D_MODEL), lambda i: (i, 0)),
            pl.BlockSpec((N_MOD, 1, D_MODEL), lambda i: (0, 0, 0)),
            pl.BlockSpec((1, D_MODEL), lambda i: (0, 0)),
            _const_spec((D_MODEL, 2 * D_FF)),
            pl.BlockSpec((CONV_W, 2 * D_FF), lambda i: (0, 0)),
            pl.BlockSpec((1, 2 * D_FF), lambda i: (0, 0)),
            _const_spec((D_FF, D_MODEL)),
            pl.BlockSpec((1, D_MODEL), lambda i: (0, 0)),
        ],
        out_specs=[
            pl.BlockSpec((tb, D_MODEL), lambda i: (i, 0)),
            pl.BlockSpec((CONV_W - 1, 2 * D_FF), lambda i: (0, 0)),
        ],
        out_shape=[
            jax.ShapeDtypeStruct((s, D_MODEL), f32),
            jax.ShapeDtypeStruct((CONV_W - 1, 2 * D_FF), f32),
        ],
        scratch_shapes=[
            pltpu.VMEM((2 * D_FF // LANES, SUBLANES + tb // 2, LANES), f32),
            pltpu.VMEM((2, tb // 2, D_FF), bf16),
        ],
        compiler_params=pltpu.CompilerParams(
            dimension_semantics=("arbitrary",),
            vmem_limit_bytes=VMEM_LIMIT_LARGE),
        name="ffn_prompt",
    )(x1, ada, g_ffn, w_up, fcw, fcb, w_down, g_final)


def _mixer_decode_pre_kernel(x_ref, ada_ref, g_ref, w_in_ref, cw_ref, st_ref, w_a_ref,
                             w_in_bf_ref, w_a_bf_ref, q_ref, k_ref, v_ref, knt_ref, vnt_ref, za_ref, sgb_ref, stn_ref,
                             h_scr, proj_scr, *, chunk):
    j = pl.program_id(0)
    n_chunks = IN_COLS // chunk

    @pl.when(j == 0)
    def _():
        sh1 = ada_ref[:, 0:D_MODEL]
        sc1 = ada_ref[:, D_MODEL:2 * D_MODEL]
        h_scr[...] = (_rms(x_ref[:, 0, :], g_ref[...]) * (1.0 + sc1) + sh1).astype(bf16)

    w_chunk = w_in_ref[...].astype(bf16)
    w_in_bf_ref[...] = w_chunk
    part = _dot(h_scr[...], w_chunk)
    for c in range(n_chunks):
        @pl.when(j == c)
        def _(c=c):
            proj_scr[:, c * chunk:(c + 1) * chunk] = part

    @pl.when(j == n_chunks - 1)
    def _():
        def proj(c0, n):
            return proj_scr[:, c0:c0 + n]

        w_a = w_a_ref[...].astype(bf16)
        w_a_bf_ref[...] = w_a
        u = proj(C_C, D_CONV) * proj(C_XIN, D_CONV)
        prev0 = st_ref[:, 0, :]
        prev1 = st_ref[:, 1, :]
        conv = cw_ref[0:1, :] * prev0 + cw_ref[1:2, :] * prev1 + cw_ref[2:3, :] * u
        stn_ref[:, 0, :] = prev1
        stn_ref[:, 1, :] = u
        ya = _dot((proj(C_B, D_CONV) * conv).astype(bf16), w_a)
        q_ref[...] = proj(C_Q, N_HEADS * HEAD_DIM) * ATTN_SCALE
        k = proj(C_K, KV_COLS)
        v = proj(C_V, KV_COLS)
        k_ref[...] = k
        v_ref[...] = v
        knt_ref[...] = k.T
        vnt_ref[...] = v.T
        za_ref[...] = jax.nn.sigmoid(proj(C_GA, D_MODEL)) * ya
        sgb_ref[...] = jax.nn.sigmoid(proj(C_GB, D_MODEL))


def _mixer_decode_pre(x, ada, g_mix, w_in, conv_w, state, w_a, *, chunk):
    n = x.shape[0]
    assert IN_COLS % chunk == 0 and chunk % LANES == 0
    const2 = lambda j: (0, 0)
    const3 = lambda j: (0, 0, 0)
    row_block = lambda cols: pl.BlockSpec((n, cols), const2)
    state_block = pl.BlockSpec((n, CONV_W - 1, D_CONV), const3)
    shapes = [((D_MODEL, IN_COLS), bf16), ((D_CONV, D_MODEL), bf16),
              ((n, N_HEADS * HEAD_DIM), f32), ((n, KV_COLS), f32), ((n, KV_COLS), f32),
              ((KV_COLS, n), f32), ((KV_COLS, n), f32),
              ((n, D_MODEL), f32), ((n, D_MODEL), f32), ((n, CONV_W - 1, D_CONV), f32)]
    return pl.pallas_call(
        functools.partial(_mixer_decode_pre_kernel, chunk=chunk),
        grid=(IN_COLS // chunk,),
        in_specs=[pl.BlockSpec((n, 1, D_MODEL), const3), row_block(2 * D_MODEL), pl.BlockSpec((1, D_MODEL), const2),
                  pl.BlockSpec((D_MODEL, chunk), lambda j: (0, j)),
                  pl.BlockSpec((CONV_W, D_CONV), const2), state_block,
                  pl.BlockSpec((D_CONV, D_MODEL), const2)],
        out_specs=[pl.BlockSpec((D_MODEL, chunk), lambda j: (0, j)), pl.BlockSpec((D_CONV, D_MODEL), const2),
                   row_block(N_HEADS * HEAD_DIM), row_block(KV_COLS), row_block(KV_COLS),
                   pl.BlockSpec((KV_COLS, n), const2), pl.BlockSpec((KV_COLS, n), const2),
                   row_block(D_MODEL), row_block(D_MODEL), state_block],
        out_shape=[jax.ShapeDtypeStruct(shp, dt) for shp, dt in shapes],
        scratch_shapes=[pltpu.VMEM((n, D_MODEL), bf16), pltpu.VMEM((n, IN_COLS), f32)],
        compiler_params=pltpu.CompilerParams(
            dimension_semantics=("arbitrary",),
            vmem_limit_bytes=VMEM_LIMIT_MEDIUM),
        name="mixer_decode_pre",
    )(x, ada, g_mix, w_in, conv_w, state, w_a)


def _attn_decode_kernel(q_ref, kn_ref, vn_ref, knt_ref, vnt_ref, ck_ref, cv_ref, sink_ref,
                        att_ref, ok_ref, ov_ref, *, bb):
    step = pl.program_id(0)
    last = lax.broadcasted_iota(jnp.int32, (HEAD_DIM, WINDOW), 1) == WINDOW - 1
    to_front = (LANES - step * bb) % LANES
    kstep = [pltpu.roll(knt_ref[g], to_front, axis=1) for g in range(N_KV)]
    vstep = [pltpu.roll(vnt_ref[g], to_front, axis=1) for g in range(N_KV)]

    pairs = [(b, g) for b in range(bb) for g in range(N_KV)]
    n_stage = 7
    cuts = [len(pairs) * s // n_stage for s in range(n_stage + 1)]

    def shift_caches(stage):
        for b, g in pairs[cuts[stage]:cuts[stage + 1]]:
            ok_ref[b, g] = jnp.where(last, pltpu.roll(kstep[g], WINDOW - 1 - b, axis=1),
                                     pltpu.roll(ck_ref[b, g], WINDOW - 1, axis=1))
            ov_ref[b, g] = jnp.where(last, pltpu.roll(vstep[g], WINDOW - 1 - b, axis=1),
                                     pltpu.roll(cv_ref[b, g], WINDOW - 1, axis=1))

    sinks = [sink_ref[g][:, 0:1] for g in range(N_KV)]
    shift_caches(0)
    s_old = [_dot(q_ref[b, g].astype(bf16), ck_ref[b, g].astype(bf16)) for b, g in pairs]
    shift_caches(1)
    s_new = [jnp.sum(q_ref[b, g] * kn_ref[b, g], axis=-1, keepdims=True) for b, g in pairs]
    shift_caches(2)
    m = [jnp.maximum(jnp.maximum(jnp.max(so, axis=-1, keepdims=True), sn), sinks[g])
         for (b, g), so, sn in zip(pairs, s_old, s_new)]
    shift_caches(3)
    e_old = [jnp.exp(so - mm) for so, mm in zip(s_old, m)]
    e_new = [jnp.exp(sn - mm) for sn, mm in zip(s_new, m)]
    den = [jnp.sum(eo, axis=-1, keepdims=True) + en + jnp.exp(sinks[g] - mm)
           for (b, g), eo, en, mm in zip(pairs, e_old, e_new, m)]
    shift_caches(4)
    o_old = [lax.dot_general(eo.astype(bf16), cv_ref[b, g].astype(bf16), (((1,), (1,)), ((), ())),
                             preferred_element_type=f32) for (b, g), eo in zip(pairs, e_old)]
    shift_caches(5)
    for (b, g), oo, en, dd in zip(pairs, o_old, e_new, den):
        att_ref[b, g] = (oo + en * vn_ref[b, g]) / dd
    shift_caches(6)


def _attn_decode(q4, kn4, vn4, knt, vnt, ck, cv, sink_b, *, bb):
    n = q4.shape[0]
    assert n % bb == 0 and n == LANES and WINDOW == LANES
    cache_spec = pl.BlockSpec((bb, N_KV, HEAD_DIM, WINDOW), lambda b: (b, 0, 0, 0))
    row_spec = pl.BlockSpec((bb, N_KV, 1, HEAD_DIM), lambda b: (b, 0, 0, 0))
    q_spec = pl.BlockSpec((bb, N_KV, GROUP, HEAD_DIM), lambda b: (b, 0, 0, 0))
    new_t_spec = pl.BlockSpec((N_KV, HEAD_DIM, n), lambda b: (0, 0, 0))
    return pl.pallas_call(
        functools.partial(_attn_decode_kernel, bb=bb),
        grid=(n // bb,),
        in_specs=[q_spec, row_spec, row_spec, new_t_spec, new_t_spec, cache_spec, cache_spec,
                  pl.BlockSpec((N_KV, GROUP, LANES), lambda b: (0, 0, 0))],
        out_specs=[q_spec, cache_spec, cache_spec],
        out_shape=[jax.ShapeDtypeStruct((n, N_KV, GROUP, HEAD_DIM), f32),
                   jax.ShapeDtypeStruct((n, N_KV, HEAD_DIM, WINDOW), f32),
                   jax.ShapeDtypeStruct((n, N_KV, HEAD_DIM, WINDOW), f32)],
        compiler_params=pltpu.CompilerParams(dimension_semantics=("arbitrary",)),
        name="attn_decode",
    )(q4, kn4, vn4, knt, vnt, ck, cv, sink_b)


def _post_decode_kernel(x_ref, ada_ref, za_ref, sgb_ref, att_ref, w_b_ref, w_o_ref, g_ref, w_up_ref,
                        fcw_ref, fcb_ref, w_down_ref, gf_ref, fst_ref,
                        w_b_bf_ref, w_o_bf_ref, w_up_bf_ref, w_down_bf_ref, y_ref, fstn_ref,
                        yb_scr, x1_scr, h_scr, up_scr, act_scr, acc_scr, *, phases):
    j = pl.program_id(0)
    (b0, nb, cb), (o0, no, co), (u0, nu, cu), (d0, nd, cd) = phases

    def mod(k):
        return ada_ref[:, k * D_MODEL:(k + 1) * D_MODEL]

    for c in range(nb):
        @pl.when(j == b0 + c)
        def _(c=c):
            w = w_b_ref[...].astype(bf16)
            w_b_bf_ref[...] = w
            yb_scr[:, c * cb:(c + 1) * cb] = _dot(att_ref[...].astype(bf16), w)

    for c in range(no):
        @pl.when(j == o0 + c)
        def _(c=c):
            w = w_o_ref[...].astype(bf16)
            w_o_bf_ref[...] = w
            mix = (za_ref[...] + sgb_ref[...] * yb_scr[...]).astype(bf16)
            cols = slice(c * co, (c + 1) * co)
            x1_scr[:, cols] = x_ref[:, 0, cols] + mod(2)[:, cols] * _dot(mix, w)

    for c in range(nu):
        @pl.when(j == u0 + c)
        def _(c=c):
            if c == 0:
                h_scr[...] = (_rms(x1_scr[...], g_ref[...]) * (1.0 + mod(4)) + mod(3)).astype(bf16)
            w = w_up_ref[...].astype(bf16)
            w_up_bf_ref[...] = w
            up_scr[:, c * cu:(c + 1) * cu] = _dot(h_scr[...], w)

    for c in range(nd):
        @pl.when(j == d0 + c)
        def _(c=c):
            if c == 0:
                up = up_scr[...]
                prev0 = fst_ref[:, 0, :]
                prev1 = fst_ref[:, 1, :]
                conv = fcw_ref[0:1, :] * prev0 + fcw_ref[1:2, :] * prev1 + fcw_ref[2:3, :] * up + fcb_ref[...]
                fstn_ref[:, 0, :] = prev1
                fstn_ref[:, 1, :] = up
                act_scr[...] = (_silu(conv[:, 0:D_FF]) * conv[:, D_FF:2 * D_FF]).astype(bf16)
            w = w_down_ref[...].astype(bf16)
            w_down_bf_ref[...] = w
            cols = slice(c * cd, (c + 1) * cd)
            acc_scr[:, cols] = x1_scr[:, cols] + mod(5)[:, cols] * _dot(act_scr[...], w)
            if c == nd - 1:
                y_ref[:, 0, :] = _rms(acc_scr[...], gf_ref[...])


def _post_decode(x, ada, za, sgb, att, w_b, w_o, g_ffn, w_up, fcw, fcb, w_down, g_final, fstate,
                 *, proj_chunk, up_chunk, down_chunk):
    n = x.shape[0]
    assert D_MODEL % proj_chunk == 0 and (2 * D_FF) % up_chunk == 0 and D_MODEL % down_chunk == 0
    assert proj_chunk % LANES == 0 and up_chunk % LANES == 0 and down_chunk % LANES == 0
    nb = no = D_MODEL // proj_chunk
    nu, nd = 2 * D_FF // up_chunk, D_MODEL // down_chunk
    b0, o0, u0, d0 = 0, nb, nb + no, nb + no + nu
    phases = ((b0, nb, proj_chunk), (o0, no, proj_chunk), (u0, nu, up_chunk), (d0, nd, down_chunk))

    def chunk_index(start, count):
        return lambda j: jnp.clip(j - start, 0, count - 1)

    ib, io, iu, idn = (chunk_index(s0, cnt) for s0, cnt, _ in phases)
    const2 = lambda j: (0, 0)
    const3 = lambda j: (0, 0, 0)
    rows = lambda cols: pl.BlockSpec((n, cols), const2, pipeline_mode=pl.Buffered(1))
    state_block = pl.BlockSpec((n, CONV_W - 1, 2 * D_FF), const3, pipeline_mode=pl.Buffered(1))
    token_block = pl.BlockSpec((n, 1, D_MODEL), const3, pipeline_mode=pl.Buffered(1))
    col_chunk = lambda k, width, idx: pl.BlockSpec((k, width), lambda j: (0, idx(j)))
    w_down_block = col_chunk(D_FF, down_chunk, idn)
    return pl.pallas_call(
        functools.partial(_post_decode_kernel, phases=phases),
        grid=(d0 + nd,),
        in_specs=[token_block, rows(N_MOD * D_MODEL), rows(D_MODEL), rows(D_MODEL), rows(N_HEADS * HEAD_DIM),
                  col_chunk(N_HEADS * HEAD_DIM, proj_chunk, ib), col_chunk(D_MODEL, proj_chunk, io),
                  pl.BlockSpec((1, D_MODEL), const2), col_chunk(D_MODEL, up_chunk, iu),
                  pl.BlockSpec((CONV_W, 2 * D_FF), const2), pl.BlockSpec((1, 2 * D_FF), const2),
                  w_down_block, pl.BlockSpec((1, D_MODEL), const2), state_block],
        out_specs=[col_chunk(N_HEADS * HEAD_DIM, proj_chunk, ib), col_chunk(D_MODEL, proj_chunk, io),
                   col_chunk(D_MODEL, up_chunk, iu), w_down_block, token_block, state_block],
        out_shape=[jax.ShapeDtypeStruct((N_HEADS * HEAD_DIM, D_MODEL), bf16),
                   jax.ShapeDtypeStruct((D_MODEL, D_MODEL), bf16),
                   jax.ShapeDtypeStruct((D_MODEL, 2 * D_FF), bf16),
                   jax.ShapeDtypeStruct((D_FF, D_MODEL), bf16),
                   jax.ShapeDtypeStruct((n, 1, D_MODEL), f32),
                   jax.ShapeDtypeStruct((n, CONV_W - 1, 2 * D_FF), f32)],
        scratch_shapes=[pltpu.VMEM((n, D_MODEL), f32), pltpu.VMEM((n, D_MODEL), f32), pltpu.VMEM((n, D_MODEL), bf16),
                        pltpu.VMEM((n, 2 * D_FF), f32), pltpu.VMEM((n, D_FF), bf16), pltpu.VMEM((n, D_MODEL), f32)],
        compiler_params=pltpu.CompilerParams(
            dimension_semantics=("arbitrary",),
            vmem_limit_bytes=VMEM_LIMIT_LARGE),
        name="post_decode",
    )(x, ada, za, sgb, att, w_b, w_o, g_ffn, w_up, fcw, fcb, w_down, g_final, fstate)


PROMPT_MIXER_ROWS = 512
PROMPT_FFN_ROWS = 1024
PROMPT_FFN_COLS = 256
PROMPT_FFN_DOWN_COLS = 1024
DECODE_ATTN_BATCH = 32
ADA_PER_STEP = 2
DECODE_IN_CHUNK = 1664
DECODE_PROJ_CHUNK = 512
DECODE_UP_CHUNK = 512
DECODE_DOWN_CHUNK = 256


def kernel(x_prompt, x_sample, c_prompt, c_sample, state_conv_a, cache_k_win, cache_v_win, state_ffn_conv, w_ada, b_ada, g_mix, w_in, conv_a_w, attn_sinks, w_a_out, w_b_out, w_o, g_ffn, w_up, ffn_conv_w, ffn_conv_b, w_down, g_final):
    depth = w_in.shape[0]
    n_p, seq, _ = x_prompt.shape
    n_s, t_s, _ = x_sample.shape
    assert n_p == 1 and t_s == 1, "one prompt sequence and single-token decode only"
    xp = x_prompt.reshape(seq, D_MODEL)
    xs = x_sample
    gf = g_final.reshape(1, D_MODEL)
    outs = [[] for _ in range(8)]
    for l in range(depth):
        ada_s, ada_p = _ada(c_sample, c_prompt, w_ada[l], b_ada[l], per_step=ADA_PER_STEP)
        gm, gn = g_mix[l].reshape(1, D_MODEL), g_ffn[l].reshape(1, D_MODEL)
        fcb = ffn_conv_b[l].reshape(1, 2 * D_FF)

        w_in_b, w_a_b, q, k_n, v_n, k_nt, v_nt, za, sgb, conv_s = _mixer_decode_pre(
            xs, ada_s, gm, w_in[l], conv_a_w[l], state_conv_a[l], w_a_out[l], chunk=DECODE_IN_CHUNK)
        to_native = lambda c: c.transpose(0, 2, 3, 1)
        from_native = lambda c: c.transpose(0, 3, 1, 2)
        sink_b = jnp.broadcast_to(attn_sinks[l].reshape(N_KV, GROUP, 1), (N_KV, GROUP, LANES))
        att4, k_s, v_s = _attn_decode(
            q.reshape(n_s, N_KV, GROUP, HEAD_DIM),
            k_n.reshape(n_s, N_KV, 1, HEAD_DIM), v_n.reshape(n_s, N_KV, 1, HEAD_DIM),
            k_nt.reshape(N_KV, HEAD_DIM, n_s), v_nt.reshape(N_KV, HEAD_DIM, n_s),
            to_native(cache_k_win[l]), to_native(cache_v_win[l]), sink_b, bb=DECODE_ATTN_BATCH)
        k_s, v_s = from_native(k_s), from_native(v_s)
        w_b_b, w_o_b, w_up_b, w_down_b, xs, ffn_s = _post_decode(
            xs, ada_s, za, sgb, att4.reshape(n_s, N_HEADS * HEAD_DIM), w_b_out[l], w_o[l], gn, w_up[l],
            ffn_conv_w[l], fcb, w_down[l], gf, state_ffn_conv[l],
            proj_chunk=DECODE_PROJ_CHUNK, up_chunk=DECODE_UP_CHUNK, down_chunk=DECODE_DOWN_CHUNK)

        x1, conv_p, k_p, v_p = _mixer_prompt(xp, ada_p, gm, w_in_b, conv_a_w[l], attn_sinks[l],
                                             w_a_b, w_b_b, w_o_b, tb=PROMPT_MIXER_ROWS)
        xp, ffn_p = _ffn_prompt(x1, ada_p, gn, w_up_b, ffn_conv_w[l], fcb, w_down_b, gf,
                                tb=PROMPT_FFN_ROWS, ch=PROMPT_FFN_COLS, nw=PROMPT_FFN_DOWN_COLS)

        for lst, val in zip(outs, (
                conv_p.reshape(n_p, CONV_W - 1, D_CONV), conv_s,
                from_native(k_p.reshape(n_p, N_KV, HEAD_DIM, WINDOW)), k_s,
                from_native(v_p.reshape(n_p, N_KV, HEAD_DIM, WINDOW)), v_s,
                ffn_p.reshape(n_p, CONV_W - 1, 2 * D_FF), ffn_s)):
            lst.append(val)
    assert depth == 1, "final RMSNorm is fused into the single layer's FFN kernels"
    return (xp.reshape(n_p, seq, D_MODEL), xs) + tuple(jnp.stack(o) for o in outs)
```

```python
import functools

import jax
import jax.numpy as jnp
from jax import lax
from jax.experimental import pallas as pl
from jax.experimental.pallas import tpu as pltpu

f32 = jnp.float32
bf16 = jnp.bfloat16

D_MODEL = 1024
D_CONV = D_MODEL
CONV_W = 3
N_HEADS = 16
N_KV = 4
GROUP = N_HEADS // N_KV
HEAD_DIM = 64
WINDOW = 128
D_FF = 2816
EPS = 1e-6
N_MOD = 6
ATTN_SCALE = HEAD_DIM ** -0.5
KV_COLS = N_KV * HEAD_DIM
C_XIN, C_B, C_C = 0, D_CONV, 2 * D_CONV
C_Q = 3 * D_CONV
C_K = C_Q + N_HEADS * HEAD_DIM
C_V = C_K + KV_COLS
C_GA = C_V + KV_COLS
C_GB = C_GA + D_MODEL
IN_COLS = C_GB + D_MODEL

LANES = 128
SUBLANES = 8
Q_SUB = 128
ATTN_LOOKAHEAD = 3
FFN_LOOKAHEAD = 3
MXU_COLS = 256
VMEM_BYTES_V7X = 64 * 1024 * 1024
VMEM_LIMIT_LARGE = VMEM_BYTES_V7X * 7 // 8
VMEM_LIMIT_MEDIUM = VMEM_BYTES_V7X * 5 // 8
VMEM_LIMIT_SMALL = VMEM_BYTES_V7X // 2


def _rms(x, g):
    ms = jnp.mean(x * x, axis=-1, keepdims=True)
    return x * lax.rsqrt(ms + EPS) * g


def _silu(x):
    return x * jax.nn.sigmoid(x)


def _dot(a, b):
    return jnp.dot(a, b, preferred_element_type=f32)


def _const_spec(shape):
    nd = len(shape)
    return pl.BlockSpec(shape, lambda i: (0,) * nd, pipeline_mode=pl.Buffered(1))


def _ada_kernel(cs_ref, cp_ref, w_ref, b_ref, os_ref, op_ref, *, per_step):
    w = w_ref[...].astype(bf16)
    os_ref[...] = _dot(_silu(cs_ref[...]).astype(bf16), w) + b_ref[...]
    cp = jnp.broadcast_to(_silu(cp_ref[...]), (SUBLANES, D_MODEL)).astype(bf16)
    mod_p = _dot(cp, w)[0:1, :] + b_ref[...]
    for m in range(per_step):
        op_ref[m] = mod_p[:, m * D_MODEL:(m + 1) * D_MODEL]


def _ada(c_sample, c_prompt, w_ada, b_ada, *, per_step):
    n_s = c_sample.shape[0]
    assert c_prompt.shape[0] == 1 and N_MOD % per_step == 0
    cols = per_step * D_MODEL
    return pl.pallas_call(
        functools.partial(_ada_kernel, per_step=per_step),
        grid=(N_MOD // per_step,),
        in_specs=[
            pl.BlockSpec((n_s, D_MODEL), lambda j: (0, 0)),
            pl.BlockSpec((1, D_MODEL), lambda j: (0, 0)),
            pl.BlockSpec((D_MODEL, cols), lambda j: (0, j)),
            pl.BlockSpec((1, cols), lambda j: (0, j)),
        ],
        out_specs=[pl.BlockSpec((n_s, cols), lambda j: (0, j)),
                   pl.BlockSpec((per_step, 1, D_MODEL), lambda j: (j, 0, 0))],
        out_shape=[jax.ShapeDtypeStruct((n_s, N_MOD * D_MODEL), f32),
                   jax.ShapeDtypeStruct((N_MOD, 1, D_MODEL), f32)],
        compiler_params=pltpu.CompilerParams(
            dimension_semantics=("arbitrary",),
            vmem_limit_bytes=VMEM_LIMIT_SMALL),
        name="ada",
    )(c_sample, c_prompt, w_ada, b_ada.reshape(1, -1))


def _mixer_prompt_kernel(sinks_ref, x_ref, ada_ref, g_ref, w_in_ref, cw_ref, w_a_ref, w_b_ref, w_o_ref,
                         x1_ref, conv_ref, knew_ref, vnew_ref,
                         ubuf, klo, khi, vt, attbuf, *, tb):
    i = pl.program_id(0)

    @pl.when(i == 0)
    def _():
        ubuf[0:SUBLANES, :] = jnp.zeros((SUBLANES, D_CONV), f32)
        for r in (klo, khi):
            r[:, 0:WINDOW, :] = jnp.zeros((N_KV, WINDOW, LANES), bf16)
        vt[:, 0:WINDOW] = jnp.zeros((KV_COLS, WINDOW), bf16)

    x = x_ref[...]
    sh1, sc1, gt1 = ada_ref[0], ada_ref[1], ada_ref[2]
    h = (_rms(x, g_ref[...]) * (1.0 + sc1) + sh1).astype(bf16)

    def proj(c0, n):
        return _dot(h, w_in_ref[:, c0:c0 + n])

    k = proj(C_K, KV_COLS)
    v = proj(C_V, KV_COLS)
    u = proj(C_C, D_CONV) * proj(C_XIN, D_CONV)
    ubuf[SUBLANES:SUBLANES + tb, :] = u
    b_gate = proj(C_B, D_CONV)
    q = (proj(C_Q, N_HEADS * HEAD_DIM) * ATTN_SCALE).astype(bf16)
    conv = (cw_ref[0:1, :] * ubuf[SUBLANES - 2:SUBLANES - 2 + tb, :]
            + cw_ref[1:2, :] * ubuf[SUBLANES - 1:SUBLANES - 1 + tb, :]
            + cw_ref[2:3, :] * u)
    conv_in = (b_gate * conv).astype(bf16)
    knew_ref[...] = k[tb - WINDOW:, :].T
    vnew_ref[...] = v[tb - WINDOW:, :].T

    lo = lax.broadcasted_iota(jnp.int32, (tb, LANES), 1) < HEAD_DIM
    for j in range(N_KV // 2):
        pair = k[:, LANES * j:LANES * (j + 1)]
        rolled = pltpu.roll(pair, HEAD_DIM, axis=1)
        zero = jnp.zeros_like(pair)
        klo[2 * j, WINDOW:WINDOW + tb, :] = jnp.where(lo, pair, zero).astype(bf16)
        khi[2 * j, WINDOW:WINDOW + tb, :] = jnp.where(lo, zero, rolled).astype(bf16)
        klo[2 * j + 1, WINDOW:WINDOW + tb, :] = jnp.where(lo, rolled, zero).astype(bf16)
        khi[2 * j + 1, WINDOW:WINDOW + tb, :] = jnp.where(lo, zero, pair).astype(bf16)
    vt[:, WINDOW:WINDOW + tb] = v.T.astype(bf16)

    cc = lax.broadcasted_iota(jnp.int32, (Q_SUB + WINDOW, 2 * Q_SUB), 0)
    col = lax.broadcasted_iota(jnp.int32, (Q_SUB + WINDOW, 2 * Q_SUB), 1)
    rr = col & (Q_SUB - 1)
    band = (cc >= rr) & (cc <= rr + WINDOW)
    first_head = lax.broadcasted_iota(jnp.int32, (1, 2 * Q_SUB), 1) < Q_SUB
    mask0 = band & (cc >= jnp.where(i == 0, WINDOW, 0))

    def scores(j, g, t):
        r0 = Q_SUB * j
        qs = jnp.concatenate([q[r0:r0 + Q_SUB, 2 * LANES * g:2 * LANES * g + LANES],
                              q[r0:r0 + Q_SUB, 2 * LANES * g + LANES:2 * LANES * (g + 1)]], axis=0)
        kr = (klo, khi)[t]
        return lax.dot_general(kr[g, r0:r0 + Q_SUB + WINDOW, :], qs, (((1,), (1,)), ((), ())),
                               preferred_element_type=f32)

    def finish(j, g, t, st):
        r0 = Q_SUB * j
        h0, h1 = GROUP * g + t, GROUP * g + 2 + t
        sink = jnp.where(first_head, sinks_ref[h0], sinks_ref[h1])
        st = jnp.where(mask0 if j == 0 else band, st, -jnp.inf)
        m = jnp.maximum(jnp.max(st, axis=0, keepdims=True), sink)
        e = jnp.exp(st - m)
        linv = 1.0 / (jnp.sum(e, axis=0, keepdims=True) + jnp.exp(sink - m))
        vtg = vt[HEAD_DIM * g:HEAD_DIM * (g + 1), r0:r0 + Q_SUB + WINDOW]
        ot = _dot(vtg, e.astype(bf16)) * linv
        attbuf[HEAD_DIM * h0:HEAD_DIM * (h0 + 1), r0:r0 + Q_SUB] = ot[:, 0:Q_SUB]
        attbuf[HEAD_DIM * h1:HEAD_DIM * (h1 + 1), r0:r0 + Q_SUB] = ot[:, Q_SUB:2 * Q_SUB]

    chains = [(j, g, t) for j in range(tb // Q_SUB) for g in range(N_KV) for t in range(2)]
    fillers = ([functools.partial(proj, c, MXU_COLS) for c in range(C_GA, C_GB + D_MODEL, MXU_COLS)]
               + [(lambda c=c: _dot(conv_in, w_a_ref[:, c:c + MXU_COLS])) for c in range(0, D_MODEL, MXU_COLS)])
    fill_every = -(-len(chains) // len(fillers))
    filled = []
    pending = [scores(*c) for c in chains[:ATTN_LOOKAHEAD]]
    for n, chain in enumerate(chains):
        if n + ATTN_LOOKAHEAD < len(chains):
            pending.append(scores(*chains[n + ATTN_LOOKAHEAD]))
        if n % fill_every == 0:
            filled.extend(f() for f in fillers[len(filled):len(filled) + 1])
        finish(*chain, pending.pop(0))
    filled.extend(f() for f in fillers[len(filled):])
    ga, gb, ya = (jnp.concatenate(filled[j:j + D_MODEL // MXU_COLS], axis=1)
                  for j in range(0, len(filled), D_MODEL // MXU_COLS))

    yb = _dot(attbuf[...].T.astype(bf16), w_b_ref[...])
    mix = (jax.nn.sigmoid(ga) * ya + jax.nn.sigmoid(gb) * yb).astype(bf16)
    for r0 in range(0, tb, tb // 2):
        x1_ref[r0:r0 + tb // 2, :] = x[r0:r0 + tb // 2, :] + gt1 * _dot(mix[r0:r0 + tb // 2, :], w_o_ref[...])

    conv_ref[...] = ubuf[SUBLANES + tb - (CONV_W - 1):SUBLANES + tb, :]
    ubuf[0:SUBLANES, :] = ubuf[tb:tb + SUBLANES, :]
    for r in (klo, khi):
        r[:, 0:WINDOW, :] = r[:, tb:tb + WINDOW, :]
    vt[:, 0:WINDOW] = vt[:, tb:tb + WINDOW]


def _mixer_prompt(x, ada, g_mix, w_in, conv_w, sinks, w_a, w_b, w_o, *, tb):
    s = x.shape[0]
    assert s % tb == 0 and tb % Q_SUB == 0 and tb >= WINDOW
    kv_scratch = pltpu.VMEM((N_KV, WINDOW + tb, LANES), bf16)
    grid_spec = pltpu.PrefetchScalarGridSpec(
        num_scalar_prefetch=1,
        grid=(s // tb,),
        in_specs=[
            pl.BlockSpec((tb, D_MODEL), lambda i, sk: (i, 0)),
            pl.BlockSpec((N_MOD, 1, D_MODEL), lambda i, sk: (0, 0, 0)),
            pl.BlockSpec((1, D_MODEL), lambda i, sk: (0, 0)),
            pl.BlockSpec((D_MODEL, IN_COLS), lambda i, sk: (0, 0), pipeline_mode=pl.Buffered(1)),
            pl.BlockSpec((CONV_W, D_CONV), lambda i, sk: (0, 0)),
            pl.BlockSpec((D_CONV, D_MODEL), lambda i, sk: (0, 0), pipeline_mode=pl.Buffered(1)),
            pl.BlockSpec((N_HEADS * HEAD_DIM, D_MODEL), lambda i, sk: (0, 0), pipeline_mode=pl.Buffered(1)),
            pl.BlockSpec((D_MODEL, D_MODEL), lambda i, sk: (0, 0), pipeline_mode=pl.Buffered(1)),
        ],
        out_specs=[
            pl.BlockSpec((tb, D_MODEL), lambda i, sk: (i, 0)),
            pl.BlockSpec((CONV_W - 1, D_CONV), lambda i, sk: (0, 0)),
            pl.BlockSpec((KV_COLS, WINDOW), lambda i, sk: (0, 0)),
            pl.BlockSpec((KV_COLS, WINDOW), lambda i, sk: (0, 0)),
        ],
        scratch_shapes=[
            pltpu.VMEM((SUBLANES + tb, D_CONV), f32),
            kv_scratch, kv_scratch,
            pltpu.VMEM((KV_COLS, WINDOW + tb), bf16),
            pltpu.VMEM((N_HEADS * HEAD_DIM, tb), f32),
        ],
    )
    return pl.pallas_call(
        functools.partial(_mixer_prompt_kernel, tb=tb),
        grid_spec=grid_spec,
        out_shape=[
            jax.ShapeDtypeStruct((s, D_MODEL), f32),
            jax.ShapeDtypeStruct((CONV_W - 1, D_CONV), f32),
            jax.ShapeDtypeStruct((KV_COLS, WINDOW), f32),
            jax.ShapeDtypeStruct((KV_COLS, WINDOW), f32),
        ],
        compiler_params=pltpu.CompilerParams(
            dimension_semantics=("arbitrary",),
            vmem_limit_bytes=VMEM_LIMIT_LARGE),
        name="mixer_prompt",
    )(sinks, x, ada, g_mix, w_in, conv_w, w_a, w_b, w_o)


def _ffn_prompt_kernel(x_ref, ada_ref, g_ref, w_up_ref, fcw_ref, fcb_ref, w_down_ref, gf_ref,
                       y_ref, fst_ref, upbuf, actbuf, *, tb, ch, nw):
    i = pl.program_id(0)

    @pl.when(i == 0)
    def _():
        upbuf[:, 0:SUBLANES, :] = jnp.zeros((2 * D_FF // LANES, SUBLANES, LANES), f32)

    hb = tb // 2
    sh2, sc2, gt2 = ada_ref[3], ada_ref[4], ada_ref[5]
    h = (_rms(x_ref[...], g_ref[...]) * (1.0 + sc2) + sh2).astype(bf16)

    def up_cols(half, c0):
        up = _dot(h[half * hb:(half + 1) * hb, :], w_up_ref[:, c0:c0 + ch])
        for s in range(ch // LANES):
            upbuf[c0 // LANES + s, SUBLANES:SUBLANES + hb, :] = up[:, s * LANES:(s + 1) * LANES]
        return up

    def conv_cols(c0, up):
        pieces = []
        for s in range(ch // LANES):
            slab = c0 // LANES + s
            cols = slice(c0 + s * LANES, c0 + (s + 1) * LANES)
            pieces.append(fcw_ref[0:1, cols] * upbuf[slab, SUBLANES - 2:SUBLANES - 2 + hb, :]
                          + fcw_ref[1:2, cols] * upbuf[slab, SUBLANES - 1:SUBLANES - 1 + hb, :]
                          + fcw_ref[2:3, cols] * up[:, s * LANES:(s + 1) * LANES]
                          + fcb_ref[0:1, cols])
            upbuf[slab, 0:SUBLANES, :] = upbuf[slab, hb:hb + SUBLANES, :]
        return jnp.concatenate(pieces, axis=1)

    def down_cols(half, n0):
        return _dot(actbuf[half], w_down_ref[:, n0:n0 + nw])

    def finish(half, parts):
        rows = slice(half * hb, (half + 1) * hb)
        x2 = x_ref[rows, :] + gt2 * jnp.concatenate(parts, axis=1)
        y_ref[rows, :] = _rms(x2, gf_ref[...])

    chunks = list(range(0, D_FF, ch))
    down_starts = list(range(0, D_MODEL, nw))
    down_at = {len(chunks) * (k + 1) // (len(down_starts) + 1): n0 for k, n0 in enumerate(down_starts)}
    assert len(down_at) == len(down_starts)
    for half in range(2):
        parts = []
        pending = [(up_cols(half, c0), up_cols(half, D_FF + c0)) for c0 in chunks[:FFN_LOOKAHEAD]]
        for n, c0 in enumerate(chunks):
            if n + FFN_LOOKAHEAD < len(chunks):
                c1 = chunks[n + FFN_LOOKAHEAD]
                pending.append((up_cols(half, c1), up_cols(half, D_FF + c1)))
            if half == 1 and n in down_at:
                parts.append(down_cols(0, down_at[n]))
            up_g, up_v = pending.pop(0)
            actbuf[half, :, c0:c0 + ch] = (_silu(conv_cols(c0, up_g))
                                           * conv_cols(D_FF + c0, up_v)).astype(bf16)
        if half == 1:
            finish(0, parts)
    finish(1, [down_cols(1, n0) for n0 in down_starts])

    for slab in range(2 * D_FF // LANES):
        fst_ref[:, slab * LANES:(slab + 1) * LANES] = upbuf[slab, SUBLANES - (CONV_W - 1):SUBLANES, :]


def _ffn_prompt(x1, ada, g_ffn, w_up, fcw, fcb, w_down, g_final, *, tb, ch, nw):
    s = x1.shape[0]
    assert s % tb == 0 and tb % (4 * SUBLANES) == 0
    assert D_FF % ch == 0 and ch % LANES == 0 and D_MODEL % nw == 0 and nw % LANES == 0
    return pl.pallas_call(
        functools.partial(_ffn_prompt_kernel, tb=tb, ch=ch, nw=nw),
        grid=(s // tb,),
        in_specs=[
            pl.BlockSpec((tb, D_MODEL), lambda i: (i, 0)),
            pl.BlockSpec((N_MOD, 1, D_MODEL), lambda i: (0, 0, 0)),
            pl.BlockSpec((1, D_MODEL), lambda i: (0, 0)),
            _const_spec((D_MODEL, 2 * D_FF)),
            pl.BlockSpec((CONV_W, 2 * D_FF), lambda i: (0, 0)),
            pl.BlockSpec((1, 2 * D_FF), lambda i: (0, 0)),
            _const_spec((D_FF, D_MODEL)),
            pl.BlockSpec((1, D_MODEL), lambda i: (0, 0)),
        ],
        out_specs=[
            pl.BlockSpec((tb, D_MODEL), lambda i: (i, 0)),
            pl.BlockSpec((CONV_W - 1, 2 * D_FF), lambda i: (0, 0)),
        ],
        out_shape=[
            jax.ShapeDtypeStruct((s, D_MODEL), f32),
            jax.ShapeDtypeStruct((CONV_W - 1, 2 * D_FF), f32),
        ],
        scratch_shapes=[
            pltpu.VMEM((2 * D_FF // LANES, SUBLANES + tb // 2, LANES), f32),
            pltpu.VMEM((2, tb // 2, D_FF), bf16),
        ],
        compiler_params=pltpu.CompilerParams(
            dimension_semantics=("arbitrary",),
            vmem_limit_bytes=VMEM_LIMIT_LARGE),
        name="ffn_prompt",
    )(x1, ada, g_ffn, w_up, fcw, fcb, w_down, g_final)


def _mixer_decode_pre_kernel(x_ref, ada_ref, g_ref, w_in_ref, cw_ref, st_ref, w_a_ref,
                             w_in_bf_ref, w_a_bf_ref, q_ref, k_ref, v_ref, knt_ref, vnt_ref, za_ref, sgb_ref, stn_ref,
                             h_scr, proj_scr, *, chunk):
    j = pl.program_id(0)
    n_chunks = IN_COLS // chunk

    @pl.when(j == 0)
    def _():
        sh1 = ada_ref[:, 0:D_MODEL]
        sc1 = ada_ref[:, D_MODEL:2 * D_MODEL]
        h_scr[...] = (_rms(x_ref[:, 0, :], g_ref[...]) * (1.0 + sc1) + sh1).astype(bf16)

    w_chunk = w_in_ref[...].astype(bf16)
    w_in_bf_ref[...] = w_chunk
    part = _dot(h_scr[...], w_chunk)
    for c in range(n_chunks):
        @pl.when(j == c)
        def _(c=c):
            proj_scr[:, c * chunk:(c + 1) * chunk] = part

    @pl.when(j == n_chunks - 1)
    def _():
        def proj(c0, n):
            return proj_scr[:, c0:c0 + n]

        w_a = w_a_ref[...].astype(bf16)
        w_a_bf_ref[...] = w_a
        u = proj(C_C, D_CONV) * proj(C_XIN, D_CONV)
        prev0 = st_ref[:, 0, :]
        prev1 = st_ref[:, 1, :]
        conv = cw_ref[0:1, :] * prev0 + cw_ref[1:2, :] * prev1 + cw_ref[2:3, :] * u
        stn_ref[:, 0, :] = prev1
        stn_ref[:, 1, :] = u
        ya = _dot((proj(C_B, D_CONV) * conv).astype(bf16), w_a)
        q_ref[...] = proj(C_Q, N_HEADS * HEAD_DIM) * ATTN_SCALE
        k = proj(C_K, KV_COLS)
        v = proj(C_V, KV_COLS)
        k_ref[...] = k
        v_ref[...] = v
        knt_ref[...] = k.T
        vnt_ref[...] = v.T
        za_ref[...] = jax.nn.sigmoid(proj(C_GA, D_MODEL)) * ya
        sgb_ref[...] = jax.nn.sigmoid(proj(C_GB, D_MODEL))


def _mixer_decode_pre(x, ada, g_mix, w_in, conv_w, state, w_a, *, chunk):
    n = x.shape[0]
    assert IN_COLS % chunk == 0 and chunk % LANES == 0
    const2 = lambda j: (0, 0)
    const3 = lambda j: (0, 0, 0)
    row_block = lambda cols: pl.BlockSpec((n, cols), const2)
    state_block = pl.BlockSpec((n, CONV_W - 1, D_CONV), const3)
    shapes = [((D_MODEL, IN_COLS), bf16), ((D_CONV, D_MODEL), bf16),
              ((n, N_HEADS * HEAD_DIM), f32), ((n, KV_COLS), f32), ((n, KV_COLS), f32),
              ((KV_COLS, n), f32), ((KV_COLS, n), f32),
              ((n, D_MODEL), f32), ((n, D_MODEL), f32), ((n, CONV_W - 1, D_CONV), f32)]
    return pl.pallas_call(
        functools.partial(_mixer_decode_pre_kernel, chunk=chunk),
        grid=(IN_COLS // chunk,),
        in_specs=[pl.BlockSpec((n, 1, D_MODEL), const3), row_block(2 * D_MODEL), pl.BlockSpec((1, D_MODEL), const2),
                  pl.BlockSpec((D_MODEL, chunk), lambda j: (0, j)),
                  pl.BlockSpec((CONV_W, D_CONV), const2), state_block,
                  pl.BlockSpec((D_CONV, D_MODEL), const2)],
        out_specs=[pl.BlockSpec((D_MODEL, chunk), lambda j: (0, j)), pl.BlockSpec((D_CONV, D_MODEL), const2),
                   row_block(N_HEADS * HEAD_DIM), row_block(KV_COLS), row_block(KV_COLS),
                   pl.BlockSpec((KV_COLS, n), const2), pl.BlockSpec((KV_COLS, n), const2),
                   row_block(D_MODEL), row_block(D_MODEL), state_block],
        out_shape=[jax.ShapeDtypeStruct(shp, dt) for shp, dt in shapes],
        scratch_shapes=[pltpu.VMEM((n, D_MODEL), bf16), pltpu.VMEM((n, IN_COLS), f32)],
        compiler_params=pltpu.CompilerParams(
            dimension_semantics=("arbitrary",),
            vmem_limit_bytes=VMEM_LIMIT_MEDIUM),
        name="mixer_decode_pre",
    )(x, ada, g_mix, w_in, conv_w, state, w_a)


def _attn_decode_kernel(q_ref, kn_ref, vn_ref, knt_ref, vnt_ref, ck_ref, cv_ref, sink_ref,
                        att_ref, ok_ref, ov_ref, *, bb):
    step = pl.program_id(0)
    last = lax.broadcasted_iota(jnp.int32, (HEAD_DIM, WINDOW), 1) == WINDOW - 1
    to_front = (LANES - step * bb) % LANES
    kstep = [pltpu.roll(knt_ref[g], to_front, axis=1) for g in range(N_KV)]
    vstep = [pltpu.roll(vnt_ref[g], to_front, axis=1) for g in range(N_KV)]

    pairs = [(b, g) for b in range(bb) for g in range(N_KV)]
    n_stage = 7
    cuts = [len(pairs) * s // n_stage for s in range(n_stage + 1)]

    def shift_caches(stage):
        for b, g in pairs[cuts[stage]:cuts[stage + 1]]:
            ok_ref[b, g] = jnp.where(last, pltpu.roll(kstep[g], WINDOW - 1 - b, axis=1),
                                     pltpu.roll(ck_ref[b, g], WINDOW - 1, axis=1))
            ov_ref[b, g] = jnp.where(last, pltpu.roll(vstep[g], WINDOW - 1 - b, axis=1),
                                     pltpu.roll(cv_ref[b, g], WINDOW - 1, axis=1))

    sinks = [sink_ref[g][:, 0:1] for g in range(N_KV)]
    shift_caches(0)
    s_old = [_dot(q_ref[b, g].astype(bf16), ck_ref[b, g].astype(bf16)) for b, g in pairs]
    shift_caches(1)
    s_new = [jnp.sum(q_ref[b, g] * kn_ref[b, g], axis=-1, keepdims=True) for b, g in pairs]
    shift_caches(2)
    m = [jnp.maximum(jnp.maximum(jnp.max(so, axis=-1, keepdims=True), sn), sinks[g])
         for (b, g), so, sn in zip(pairs, s_old, s_new)]
    shift_caches(3)
    e_old = [jnp.exp(so - mm) for so, mm in zip(s_old, m)]
    e_new = [jnp.exp(sn - mm) for sn, mm in zip(s_new, m)]
    den = [jnp.sum(eo, axis=-1, keepdims=True) + en + jnp.exp(sinks[g] - mm)
           for (b, g), eo, en, mm in zip(pairs, e_old, e_new, m)]
    shift_caches(4)
    o_old = [lax.dot_general(eo.astype(bf16), cv_ref[b, g].astype(bf16), (((1,), (1,)), ((), ())),
                             preferred_element_type=f32) for (b, g), eo in zip(pairs, e_old)]
    shift_caches(5)
    for (b, g), oo, en, dd in zip(pairs, o_old, e_new, den):
        att_ref[b, g] = (oo + en * vn_ref[b, g]) / dd
    shift_caches(6)


def _attn_decode(q4, kn4, vn4, knt, vnt, ck, cv, sink_b, *, bb):
    n = q4.shape[0]
    assert n % bb == 0 and n == LANES and WINDOW == LANES
    cache_spec = pl.BlockSpec((bb, N_KV, HEAD_DIM, WINDOW), lambda b: (b, 0, 0, 0))
    row_spec = pl.BlockSpec((bb, N_KV, 1, HEAD_DIM), lambda b: (b, 0, 0, 0))
    q_spec = pl.BlockSpec((bb, N_KV, GROUP, HEAD_DIM), lambda b: (b, 0, 0, 0))
    new_t_spec = pl.BlockSpec((N_KV, HEAD_DIM, n), lambda b: (0, 0, 0))
    return pl.pallas_call(
        functools.partial(_attn_decode_kernel, bb=bb),
        grid=(n // bb,),
        in_specs=[q_spec, row_spec, row_spec, new_t_spec, new_t_spec, cache_spec, cache_spec,
                  pl.BlockSpec((N_KV, GROUP, LANES), lambda b: (0, 0, 0))],
        out_specs=[q_spec, cache_spec, cache_spec],
        out_shape=[jax.ShapeDtypeStruct((n, N_KV, GROUP, HEAD_DIM), f32),
                   jax.ShapeDtypeStruct((n, N_KV, HEAD_DIM, WINDOW), f32),
                   jax.ShapeDtypeStruct((n, N_KV, HEAD_DIM, WINDOW), f32)],
        compiler_params=pltpu.CompilerParams(dimension_semantics=("arbitrary",)),
        name="attn_decode",
    )(q4, kn4, vn4, knt, vnt, ck, cv, sink_b)


def _post_decode_kernel(x_ref, ada_ref, za_ref, sgb_ref, att_ref, w_b_ref, w_o_ref, g_ref, w_up_ref,
                        fcw_ref, fcb_ref, w_down_ref, gf_ref, fst_ref,
                        w_b_bf_ref, w_o_bf_ref, w_up_bf_ref, w_down_bf_ref, y_ref, fstn_ref,
                        yb_scr, x1_scr, h_scr, up_scr, act_scr, acc_scr, *, phases):
    j = pl.program_id(0)
    (b0, nb, cb), (o0, no, co), (u0, nu, cu), (d0, nd, cd) = phases

    def mod(k):
        return ada_ref[:, k * D_MODEL:(k + 1) * D_MODEL]

    for c in range(nb):
        @pl.when(j == b0 + c)
        def _(c=c):
            w = w_b_ref[...].astype(bf16)
            w_b_bf_ref[...] = w
            yb_scr[:, c * cb:(c + 1) * cb] = _dot(att_ref[...].astype(bf16), w)

    for c in range(no):
        @pl.when(j == o0 + c)
        def _(c=c):
            w = w_o_ref[...].astype(bf16)
            w_o_bf_ref[...] = w
            mix = (za_ref[...] + sgb_ref[...] * yb_scr[...]).astype(bf16)
            cols = slice(c * co, (c + 1) * co)
            x1_scr[:, cols] = x_ref[:, 0, cols] + mod(2)[:, cols] * _dot(mix, w)

    for c in range(nu):
        @pl.when(j == u0 + c)
        def _(c=c):
            if c == 0:
                h_scr[...] = (_rms(x1_scr[...], g_ref[...]) * (1.0 + mod(4)) + mod(3)).astype(bf16)
            w = w_up_ref[...].astype(bf16)
            w_up_bf_ref[...] = w
            up_scr[:, c * cu:(c + 1) * cu] = _dot(h_scr[...], w)

    for c in range(nd):
        @pl.when(j == d0 + c)
        def _(c=c):
            if c == 0:
                up = up_scr[...]
                prev0 = fst_ref[:, 0, :]
                prev1 = fst_ref[:, 1, :]
                conv = fcw_ref[0:1, :] * prev0 + fcw_ref[1:2, :] * prev1 + fcw_ref[2:3, :] * up + fcb_ref[...]
                fstn_ref[:, 0, :] = prev1
                fstn_ref[:, 1, :] = up
                act_scr[...] = (_silu(conv[:, 0:D_FF]) * conv[:, D_FF:2 * D_FF]).astype(bf16)
            w = w_down_ref[...].astype(bf16)
            w_down_bf_ref[...] = w
            cols = slice(c * cd, (c + 1) * cd)
            acc_scr[:, cols] = x1_scr[:, cols] + mod(5)[:, cols] * _dot(act_scr[...], w)
            if c == nd - 1:
                y_ref[:, 0, :] = _rms(acc_scr[...], gf_ref[...])


def _post_decode(x, ada, za, sgb, att, w_b, w_o, g_ffn, w_up, fcw, fcb, w_down, g_final, fstate,
                 *, proj_chunk, up_chunk, down_chunk):
    n = x.shape[0]
    assert D_MODEL % proj_chunk == 0 and (2 * D_FF) % up_chunk == 0 and D_MODEL % down_chunk == 0
    assert proj_chunk % LANES == 0 and up_chunk % LANES == 0 and down_chunk % LANES == 0
    nb = no = D_MODEL // proj_chunk
    nu, nd = 2 * D_FF // up_chunk, D_MODEL // down_chunk
    b0, o0, u0, d0 = 0, nb, nb + no, nb + no + nu
    phases = ((b0, nb, proj_chunk), (o0, no, proj_chunk), (u0, nu, up_chunk), (d0, nd, down_chunk))

    def chunk_index(start, count):
        return lambda j: jnp.clip(j - start, 0, count - 1)

    ib, io, iu, idn = (chunk_index(s0, cnt) for s0, cnt, _ in phases)
    const2 = lambda j: (0, 0)
    const3 = lambda j: (0, 0, 0)
    rows = lambda cols: pl.BlockSpec((n, cols), const2, pipeline_mode=pl.Buffered(1))
    state_block = pl.BlockSpec((n, CONV_W - 1, 2 * D_FF), const3, pipeline_mode=pl.Buffered(1))
    token_block = pl.BlockSpec((n, 1, D_MODEL), const3, pipeline_mode=pl.Buffered(1))
    col_chunk = lambda k, width, idx: pl.BlockSpec((k, width), lambda j: (0, idx(j)))
    w_down_block = col_chunk(D_FF, down_chunk, idn)
    return pl.pallas_call(
        functools.partial(_post_decode_kernel, phases=phases),
        grid=(d0 + nd,),
        in_specs=[token_block, rows(N_MOD * D_MODEL), rows(D_MODEL), rows(D_MODEL), rows(N_HEADS * HEAD_DIM),
                  col_chunk(N_HEADS * HEAD_DIM, proj_chunk, ib), col_chunk(D_MODEL, proj_chunk, io),
                  pl.BlockSpec((1, D_MODEL), const2), col_chunk(D_MODEL, up_chunk, iu),
                  pl.BlockSpec((CONV_W, 2 * D_FF), const2), pl.BlockSpec((1, 2 * D_FF), const2),
                  w_down_block, pl.BlockSpec((1, D_MODEL), const2), state_block],
        out_specs=[col_chunk(N_HEADS * HEAD_DIM, proj_chunk, ib), col_chunk(D_MODEL, proj_chunk, io),
                   col_chunk(D_MODEL, up_chunk, iu), w_down_block, token_block, state_block],
        out_shape=[jax.ShapeDtypeStruct((N_HEADS * HEAD_DIM, D_MODEL), bf16),
                   jax.ShapeDtypeStruct((D_MODEL, D_MODEL), bf16),
                   jax.ShapeDtypeStruct((D_MODEL, 2 * D_FF), bf16),
                   jax.ShapeDtypeStruct((D_FF, D_MODEL), bf16),
                   jax.ShapeDtypeStruct((n, 1, D_MODEL), f32),
                   jax.ShapeDtypeStruct((n, CONV_W - 1, 2 * D_FF), f32)],
        scratch_shapes=[pltpu.VMEM((n, D_MODEL), f32), pltpu.VMEM((n, D_MODEL), f32), pltpu.VMEM((n, D_MODEL), bf16),
                        pltpu.VMEM((n, 2 * D_FF), f32), pltpu.VMEM((n, D_FF), bf16), pltpu.VMEM((n, D_MODEL), f32)],
        compiler_params=pltpu.CompilerParams(
            dimension_semantics=("arbitrary",),
            vmem_limit_bytes=VMEM_LIMIT_LARGE),
        name="post_decode",
    )(x, ada, za, sgb, att, w_b, w_o, g_ffn, w_up, fcw, fcb, w_down, g_final, fstate)


PROMPT_MIXER_ROWS = 512
PROMPT_FFN_ROWS = 1024
PROMPT_FFN_COLS = 256
PROMPT_FFN_DOWN_COLS = 1024
DECODE_ATTN_BATCH = 32
ADA_PER_STEP = 2
DECODE_IN_CHUNK = 1664
DECODE_PROJ_CHUNK = 512
DECODE_UP_CHUNK = 512
DECODE_DOWN_CHUNK = 256


def kernel(x_prompt, x_sample, c_prompt, c_sample, state_conv_a, cache_k_win, cache_v_win, state_ffn_conv, w_ada, b_ada, g_mix, w_in, conv_a_w, attn_sinks, w_a_out, w_b_out, w_o, g_ffn, w_up, ffn_conv_w, ffn_conv_b, w_down, g_final):
    depth = w_in.shape[0]
    n_p, seq, _ = x_prompt.shape
    n_s, t_s, _ = x_sample.shape
    assert n_p == 1 and t_s == 1, "one prompt sequence and single-token decode only"
    xp = x_prompt.reshape(seq, D_MODEL)
    xs = x_sample
    gf = g_final.reshape(1, D_MODEL)
    outs = [[] for _ in range(8)]
    for l in range(depth):
        ada_s, ada_p = _ada(c_sample, c_prompt, w_ada[l], b_ada[l], per_step=ADA_PER_STEP)
        gm, gn = g_mix[l].reshape(1, D_MODEL), g_ffn[l].reshape(1, D_MODEL)
        fcb = ffn_conv_b[l].reshape(1, 2 * D_FF)

        w_in_b, w_a_b, q, k_n, v_n, k_nt, v_nt, za, sgb, conv_s = _mixer_decode_pre(
            xs, ada_s, gm, w_in[l], conv_a_w[l], state_conv_a[l], w_a_out[l], chunk=DECODE_IN_CHUNK)
        to_native = lambda c: c.transpose(0, 2, 3, 1)
        from_native = lambda c: c.transpose(0, 3, 1, 2)
        sink_b = jnp.broadcast_to(attn_sinks[l].reshape(N_KV, GROUP, 1), (N_KV, GROUP, LANES))
        att4, k_s, v_s = _attn_decode(
            q.reshape(n_s, N_KV, GROUP, HEAD_DIM),
            k_n.reshape(n_s, N_KV, 1, HEAD_DIM), v_n.reshape(n_s, N_KV, 1, HEAD_DIM),
            k_nt.reshape(N_KV, HEAD_DIM, n_s), v_nt.reshape(N_KV, HEAD_DIM, n_s),
            to_native(cache_k_win[l]), to_native(cache_v_win[l]), sink_b, bb=DECODE_ATTN_BATCH)
        k_s, v_s = from_native(k_s), from_native(v_s)
        w_b_b, w_o_b, w_up_b, w_down_b, xs, ffn_s = _post_decode(
            xs, ada_s, za, sgb, att4.reshape(n_s, N_HEADS * HEAD_DIM), w_b_out[l], w_o[l], gn, w_up[l],
            ffn_conv_w[l], fcb, w_down[l], gf, state_ffn_conv[l],
            proj_chunk=DECODE_PROJ_CHUNK, up_chunk=DECODE_UP_CHUNK, down_chunk=DECODE_DOWN_CHUNK)

        x1, conv_p, k_p, v_p = _mixer_prompt(xp, ada_p, gm, w_in_b, conv_a_w[l], attn_sinks[l],
                                             w_a_b, w_b_b, w_o_b, tb=PROMPT_MIXER_ROWS)
        xp, ffn_p = _ffn_prompt(x1, ada_p, gn, w_up_b, ffn_conv_w[l], fcb, w_down_b, gf,
                                tb=PROMPT_FFN_ROWS, ch=PROMPT_FFN_COLS, nw=PROMPT_FFN_DOWN_COLS)

        for lst, val in zip(outs, (
                conv_p.reshape(n_p, CONV_W - 1, D_CONV), conv_s,
                from_native(k_p.reshape(n_p, N_KV, HEAD_DIM, WINDOW)), k_s,
                from_native(v_p.reshape(n_p, N_KV, HEAD_DIM, WINDOW)), v_s,
                ffn_p.reshape(n_p, CONV_W - 1, 2 * D_FF), ffn_s)):
            lst.append(val)
    assert depth == 1, "final RMSNorm is fused into the single layer's FFN kernels"
    return (xp.reshape(n_p, seq, D_MODEL), xs) + tuple(jnp.stack(o) for o in outs)
```

```python
import functools

import jax
import jax.numpy as jnp
from jax import lax
from jax.experimental import pallas as pl
from jax.experimental.pallas import tpu as pltpu

f32 = jnp.float32
bf16 = jnp.bfloat16

D_MODEL = 1024
D_CONV = D_MODEL
CONV_W = 3
N_HEADS = 16
N_KV = 4
GROUP = N_HEADS // N_KV
HEAD_DIM = 64
WINDOW = 128
D_FF = 2816
EPS = 1e-6
N_MOD = 6
ATTN_SCALE = HEAD_DIM ** -0.5
KV_COLS = N_KV * HEAD_DIM
C_XIN, C_B, C_C = 0, D_CONV, 2 * D_CONV
C_Q = 3 * D_CONV
C_K = C_Q + N_HEADS * HEAD_DIM
C_V = C_K + KV_COLS
C_GA = C_V + KV_COLS
C_GB = C_GA + D_MODEL
IN_COLS = C_GB + D_MODEL

LANES = 128
SUBLANES = 8
Q_SUB = 128
ATTN_LOOKAHEAD = 3
FFN_LOOKAHEAD = 2
MXU_COLS = 256
VMEM_BYTES_V7X = 64 * 1024 * 1024
VMEM_LIMIT_LARGE = VMEM_BYTES_V7X * 7 // 8
VMEM_LIMIT_MEDIUM = VMEM_BYTES_V7X * 5 // 8
VMEM_LIMIT_SMALL = VMEM_BYTES_V7X // 2


def _rms(x, g):
    ms = jnp.mean(x * x, axis=-1, keepdims=True)
    return x * lax.rsqrt(ms + EPS) * g


def _silu(x):
    return x * jax.nn.sigmoid(x)


def _dot(a, b):
    return jnp.dot(a, b, preferred_element_type=f32)


def _const_spec(shape):
    nd = len(shape)
    return pl.BlockSpec(shape, lambda i: (0,) * nd, pipeline_mode=pl.Buffered(1))


def _ada_kernel(cs_ref, cp_ref, w_ref, b_ref, os_ref, op_ref, *, per_step):
    w = w_ref[...].astype(bf16)
    os_ref[...] = _dot(_silu(cs_ref[...]).astype(bf16), w) + b_ref[...]
    cp = jnp.broadcast_to(_silu(cp_ref[...]), (SUBLANES, D_MODEL)).astype(bf16)
    mod_p = _dot(cp, w)[0:1, :] + b_ref[...]
    for m in range(per_step):
        op_ref[m] = mod_p[:, m * D_MODEL:(m + 1) * D_MODEL]


def _ada(c_sample, c_prompt, w_ada, b_ada, *, per_step):
    n_s = c_sample.shape[0]
    assert c_prompt.shape[0] == 1 and N_MOD % per_step == 0
    cols = per_step * D_MODEL
    return pl.pallas_call(
        functools.partial(_ada_kernel, per_step=per_step),
        grid=(N_MOD // per_step,),
        in_specs=[
            pl.BlockSpec((n_s, D_MODEL), lambda j: (0, 0)),
            pl.BlockSpec((1, D_MODEL), lambda j: (0, 0)),
            pl.BlockSpec((D_MODEL, cols), lambda j: (0, j)),
            pl.BlockSpec((1, cols), lambda j: (0, j)),
        ],
        out_specs=[pl.BlockSpec((n_s, cols), lambda j: (0, j)),
                   pl.BlockSpec((per_step, 1, D_MODEL), lambda j: (j, 0, 0))],
        out_shape=[jax.ShapeDtypeStruct((n_s, N_MOD * D_MODEL), f32),
                   jax.ShapeDtypeStruct((N_MOD, 1, D_MODEL), f32)],
        compiler_params=pltpu.CompilerParams(
            dimension_semantics=("arbitrary",),
            vmem_limit_bytes=VMEM_LIMIT_SMALL),
        name="ada",
    )(c_sample, c_prompt, w_ada, b_ada.reshape(1, -1))


def _mixer_prompt_kernel(sinks_ref, x_ref, ada_ref, g_ref, w_in_ref, cw_ref, w_a_ref, w_b_ref, w_o_ref,
                         x1_ref, conv_ref, knew_ref, vnew_ref,
                         ubuf, klo, khi, vt, attbuf, *, tb):
    i = pl.program_id(0)

    @pl.when(i == 0)
    def _():
        ubuf[0:SUBLANES, :] = jnp.zeros((SUBLANES, D_CONV), f32)
        for r in (klo, khi):
            r[:, 0:WINDOW, :] = jnp.zeros((N_KV, WINDOW, LANES), bf16)
        vt[:, 0:WINDOW] = jnp.zeros((KV_COLS, WINDOW), bf16)

    x = x_ref[...]
    sh1, sc1, gt1 = ada_ref[0], ada_ref[1], ada_ref[2]
    h = (_rms(x, g_ref[...]) * (1.0 + sc1) + sh1).astype(bf16)

    def proj(c0, n):
        return _dot(h, w_in_ref[:, c0:c0 + n])

    k = proj(C_K, KV_COLS)
    v = proj(C_V, KV_COLS)
    u = proj(C_C, D_CONV) * proj(C_XIN, D_CONV)
    ubuf[SUBLANES:SUBLANES + tb, :] = u
    b_gate = proj(C_B, D_CONV)
    q = (proj(C_Q, N_HEADS * HEAD_DIM) * ATTN_SCALE).astype(bf16)
    conv = (cw_ref[0:1, :] * ubuf[SUBLANES - 2:SUBLANES - 2 + tb, :]
            + cw_ref[1:2, :] * ubuf[SUBLANES - 1:SUBLANES - 1 + tb, :]
            + cw_ref[2:3, :] * u)
    conv_in = (b_gate * conv).astype(bf16)
    knew_ref[...] = k[tb - WINDOW:, :].T
    vnew_ref[...] = v[tb - WINDOW:, :].T

    lo = lax.broadcasted_iota(jnp.int32, (tb, LANES), 1) < HEAD_DIM
    for j in range(N_KV // 2):
        pair = k[:, LANES * j:LANES * (j + 1)]
        rolled = pltpu.roll(pair, HEAD_DIM, axis=1)
        zero = jnp.zeros_like(pair)
        klo[2 * j, WINDOW:WINDOW + tb, :] = jnp.where(lo, pair, zero).astype(bf16)
        khi[2 * j, WINDOW:WINDOW + tb, :] = jnp.where(lo, zero, rolled).astype(bf16)
        klo[2 * j + 1, WINDOW:WINDOW + tb, :] = jnp.where(lo, rolled, zero).astype(bf16)
        khi[2 * j + 1, WINDOW:WINDOW + tb, :] = jnp.where(lo, zero, pair).astype(bf16)
    vt[:, WINDOW:WINDOW + tb] = v.T.astype(bf16)

    cc = lax.broadcasted_iota(jnp.int32, (Q_SUB + WINDOW, 2 * Q_SUB), 0)
    col = lax.broadcasted_iota(jnp.int32, (Q_SUB + WINDOW, 2 * Q_SUB), 1)
    rr = col & (Q_SUB - 1)
    band = (cc >= rr) & (cc <= rr + WINDOW)
    first_head = lax.broadcasted_iota(jnp.int32, (1, 2 * Q_SUB), 1) < Q_SUB
    mask0 = band & (cc >= jnp.where(i == 0, WINDOW, 0))

    def scores(j, g, t):
        r0 = Q_SUB * j
        qs = jnp.concatenate([q[r0:r0 + Q_SUB, 2 * LANES * g:2 * LANES * g + LANES],
                              q[r0:r0 + Q_SUB, 2 * LANES * g + LANES:2 * LANES * (g + 1)]], axis=0)
        kr = (klo, khi)[t]
        return lax.dot_general(kr[g, r0:r0 + Q_SUB + WINDOW, :], qs, (((1,), (1,)), ((), ())),
                               preferred_element_type=f32)

    def finish(j, g, t, st):
        r0 = Q_SUB * j
        h0, h1 = GROUP * g + t, GROUP * g + 2 + t
        sink = jnp.where(first_head, sinks_ref[h0], sinks_ref[h1])
        st = jnp.where(mask0 if j == 0 else band, st, -jnp.inf)
        m = jnp.maximum(jnp.max(st, axis=0, keepdims=True), sink)
        e = jnp.exp(st - m)
        linv = 1.0 / (jnp.sum(e, axis=0, keepdims=True) + jnp.exp(sink - m))
        vtg = vt[HEAD_DIM * g:HEAD_DIM * (g + 1), r0:r0 + Q_SUB + WINDOW]
        ot = _dot(vtg, e.astype(bf16)) * linv
        attbuf[HEAD_DIM * h0:HEAD_DIM * (h0 + 1), r0:r0 + Q_SUB] = ot[:, 0:Q_SUB]
        attbuf[HEAD_DIM * h1:HEAD_DIM * (h1 + 1), r0:r0 + Q_SUB] = ot[:, Q_SUB:2 * Q_SUB]

    chains = [(j, g, t) for j in range(tb // Q_SUB) for g in range(N_KV) for t in range(2)]
    fillers = ([functools.partial(proj, c, MXU_COLS) for c in range(C_GA, C_GB + D_MODEL, MXU_COLS)]
               + [(lambda c=c: _dot(conv_in, w_a_ref[:, c:c + MXU_COLS])) for c in range(0, D_MODEL, MXU_COLS)])
    fill_every = -(-len(chains) // len(fillers))
    filled = []
    pending = [scores(*c) for c in chains[:ATTN_LOOKAHEAD]]
    for n, chain in enumerate(chains):
        if n + ATTN_LOOKAHEAD < len(chains):
            pending.append(scores(*chains[n + ATTN_LOOKAHEAD]))
        if n % fill_every == 0:
            filled.extend(f() for f in fillers[len(filled):len(filled) + 1])
        finish(*chain, pending.pop(0))
    filled.extend(f() for f in fillers[len(filled):])
    ga, gb, ya = (jnp.concatenate(filled[j:j + D_MODEL // MXU_COLS], axis=1)
                  for j in range(0, len(filled), D_MODEL // MXU_COLS))

    yb = _dot(attbuf[...].T.astype(bf16), w_b_ref[...])
    mix = (jax.nn.sigmoid(ga) * ya + jax.nn.sigmoid(gb) * yb).astype(bf16)
    for r0 in range(0, tb, tb // 2):
        x1_ref[r0:r0 + tb // 2, :] = x[r0:r0 + tb // 2, :] + gt1 * _dot(mix[r0:r0 + tb // 2, :], w_o_ref[...])

    conv_ref[...] = ubuf[SUBLANES + tb - (CONV_W - 1):SUBLANES + tb, :]
    ubuf[0:SUBLANES, :] = ubuf[tb:tb + SUBLANES, :]
    for r in (klo, khi):
        r[:, 0:WINDOW, :] = r[:, tb:tb + WINDOW, :]
    vt[:, 0:WINDOW] = vt[:, tb:tb + WINDOW]


def _mixer_prompt(x, ada, g_mix, w_in, conv_w, sinks, w_a, w_b, w_o, *, tb):
    s = x.shape[0]
    assert s % tb == 0 and tb % Q_SUB == 0 and tb >= WINDOW
    kv_scratch = pltpu.VMEM((N_KV, WINDOW + tb, LANES), bf16)
    grid_spec = pltpu.PrefetchScalarGridSpec(
        num_scalar_prefetch=1,
        grid=(s // tb,),
        in_specs=[
            pl.BlockSpec((tb, D_MODEL), lambda i, sk: (i, 0)),
            pl.BlockSpec((N_MOD, 1, D_MODEL), lambda i, sk: (0, 0, 0)),
            pl.BlockSpec((1, D_MODEL), lambda i, sk: (0, 0)),
            pl.BlockSpec((D_MODEL, IN_COLS), lambda i, sk: (0, 0), pipeline_mode=pl.Buffered(1)),
            pl.BlockSpec((CONV_W, D_CONV), lambda i, sk: (0, 0)),
            pl.BlockSpec((D_CONV, D_MODEL), lambda i, sk: (0, 0), pipeline_mode=pl.Buffered(1)),
            pl.BlockSpec((N_HEADS * HEAD_DIM, D_MODEL), lambda i, sk: (0, 0), pipeline_mode=pl.Buffered(1)),
            pl.BlockSpec((D_MODEL, D_MODEL), lambda i, sk: (0, 0), pipeline_mode=pl.Buffered(1)),
        ],
        out_specs=[
            pl.BlockSpec((tb, D_MODEL), lambda i, sk: (i, 0)),
            pl.BlockSpec((CONV_W - 1, D_CONV), lambda i, sk: (0, 0)),
            pl.BlockSpec((KV_COLS, WINDOW), lambda i, sk: (0, 0)),
            pl.BlockSpec((KV_COLS, WINDOW), lambda i, sk: (0, 0)),
        ],
        scratch_shapes=[
            pltpu.VMEM((SUBLANES + tb, D_CONV), f32),
            kv_scratch, kv_scratch,
            pltpu.VMEM((KV_COLS, WINDOW + tb), bf16),
            pltpu.VMEM((N_HEADS * HEAD_DIM, tb), f32),
        ],
    )
    return pl.pallas_call(
        functools.partial(_mixer_prompt_kernel, tb=tb),
        grid_spec=grid_spec,
        out_shape=[
            jax.ShapeDtypeStruct((s, D_MODEL), f32),
            jax.ShapeDtypeStruct((CONV_W - 1, D_CONV), f32),
            jax.ShapeDtypeStruct((KV_COLS, WINDOW), f32),
            jax.ShapeDtypeStruct((KV_COLS, WINDOW), f32),
        ],
        compiler_params=pltpu.CompilerParams(
            dimension_semantics=("arbitrary",),
            vmem_limit_bytes=VMEM_LIMIT_LARGE),
        name="mixer_prompt",
    )(sinks, x, ada, g_mix, w_in, conv_w, w_a, w_b, w_o)


def _ffn_prompt_kernel(x_ref, ada_ref, g_ref, w_up_ref, fcw_ref, fcb_ref, w_down_ref, gf_ref,
                       y_ref, fst_ref, upbuf, actbuf, *, tb, ch, nw):
    i = pl.program_id(0)

    @pl.when(i == 0)
    def _():
        upbuf[:, 0:SUBLANES, :] = jnp.zeros((2 * D_FF // LANES, SUBLANES, LANES), f32)

    hb = tb // 2
    sh2, sc2, gt2 = ada_ref[3], ada_ref[4], ada_ref[5]
    h = (_rms(x_ref[...], g_ref[...]) * (1.0 + sc2) + sh2).astype(bf16)

    def up_cols(half, c0):
        up = _dot(h[half * hb:(half + 1) * hb, :], w_up_ref[:, c0:c0 + ch])
        for s in range(ch // LANES):
            upbuf[c0 // LANES + s, SUBLANES:SUBLANES + hb, :] = up[:, s * LANES:(s + 1) * LANES]
        return up

    def conv_cols(c0, up):
        pieces = []
        for s in range(ch // LANES):
            slab = c0 // LANES + s
            cols = slice(c0 + s * LANES, c0 + (s + 1) * LANES)
            pieces.append(fcw_ref[0:1, cols] * upbuf[slab, SUBLANES - 2:SUBLANES - 2 + hb, :]
                          + fcw_ref[1:2, cols] * upbuf[slab, SUBLANES - 1:SUBLANES - 1 + hb, :]
                          + fcw_ref[2:3, cols] * up[:, s * LANES:(s + 1) * LANES]
                          + fcb_ref[0:1, cols])
            upbuf[slab, 0:SUBLANES, :] = upbuf[slab, hb:hb + SUBLANES, :]
        return jnp.concatenate(pieces, axis=1)

    def down_cols(half, n0):
        return _dot(actbuf[half], w_down_ref[:, n0:n0 + nw])

    def finish(half, parts):
        rows = slice(half * hb, (half + 1) * hb)
        x2 = x_ref[rows, :] + gt2 * jnp.concatenate(parts, axis=1)
        y_ref[rows, :] = _rms(x2, gf_ref[...])

    chunks = list(range(0, D_FF, ch))
    down_starts = list(range(0, D_MODEL, nw))
    down_first = len(chunks) * 3 // 4
    down_at = {down_first + k: n0 for k, n0 in enumerate(down_starts)}
    assert down_first + len(down_starts) <= len(chunks)
    for half in range(2):
        parts = []
        pending = [(up_cols(half, c0), up_cols(half, D_FF + c0)) for c0 in chunks[:FFN_LOOKAHEAD]]
        for n, c0 in enumerate(chunks):
            if n + FFN_LOOKAHEAD < len(chunks):
                c1 = chunks[n + FFN_LOOKAHEAD]
                pending.append((up_cols(half, c1), up_cols(half, D_FF + c1)))
            if half == 1 and n in down_at:
                parts.append(down_cols(0, down_at[n]))
            up_g, up_v = pending.pop(0)
            actbuf[half, :, c0:c0 + ch] = (_silu(conv_cols(c0, up_g))
                                           * conv_cols(D_FF + c0, up_v)).astype(bf16)
        if half == 1:
            finish(0, parts)
    finish(1, [down_cols(1, n0) for n0 in down_starts])

    for slab in range(2 * D_FF // LANES):
        fst_ref[:, slab * LANES:(slab + 1) * LANES] = upbuf[slab, SUBLANES - (CONV_W - 1):SUBLANES, :]


def _ffn_prompt(x1, ada, g_ffn, w_up, fcw, fcb, w_down, g_final, *, tb, ch, nw):
    s = x1.shape[0]
    assert s % tb == 0 and tb % (4 * SUBLANES) == 0
    assert D_FF % ch == 0 and ch % LANES == 0 and D_MODEL % nw == 0 and nw % LANES == 0
    return pl.pallas_call(
        functools.partial(_ffn_prompt_kernel, tb=tb, ch=ch, nw=nw),
        grid=(s // tb,),
        in_specs=[
            pl.BlockSpec((tb, D_MODEL), lambda i: (i, 0)),
            pl.BlockSpec((N_MOD, 1, D_MODEL), lambda i: (0, 0, 0)),
            pl.BlockSpec((1, D_MODEL), lambda i: (0, 0)),
            _const_spec((D_MODEL, 2 * D_FF)),
            pl.BlockSpec((CONV_W, 2 * D_FF), lambda i: (0, 0)),
            pl.BlockSpec((1, 2 * D_FF), lambda i: (0, 0)),
            _const_spec((D_FF, D_MODEL)),
            pl.BlockSpec((1, D_MODEL), lambda i: (0, 0)),
        ],
        out_specs=[
            pl.BlockSpec((tb, D_MODEL), lambda i: (i, 0)),
            pl.BlockSpec((CONV_W - 1, 2 * D_FF), lambda i: (0, 0)),
        ],
        out_shape=[
            jax.ShapeDtypeStruct((s, D_MODEL), f32),
            jax.ShapeDtypeStruct((CONV_W - 1, 2 * D_FF), f32),
        ],
        scratch_shapes=[
            pltpu.VMEM((2 * D_FF // LANES, SUBLANES + tb // 2, LANES), f32),
            pltpu.VMEM((2, tb // 2, D_FF), bf16),
        ],
        compiler_params=pltpu.CompilerParams(
            dimension_semantics=("arbitrary",),
            vmem_limit_bytes=VMEM_LIMIT_LARGE),
        name="ffn_prompt",
    )(x1, ada, g_ffn, w_up, fcw, fcb, w_down, g_final)


def _mixer_decode_pre_kernel(x_ref, ada_ref, g_ref, w_in_ref, cw_ref, st_ref, w_a_ref,
                             w_in_bf_ref, w_a_bf_ref, q_ref, k_ref, v_ref, knt_ref, vnt_ref, za_ref, sgb_ref, stn_ref,
                             h_scr, proj_scr, *, chunk):
    j = pl.program_id(0)
    n_chunks = IN_COLS // chunk

    @pl.when(j == 0)
    def _():
        sh1 = ada_ref[:, 0:D_MODEL]
        sc1 = ada_ref[:, D_MODEL:2 * D_MODEL]
        h_scr[...] = (_rms(x_ref[:, 0, :], g_ref[...]) * (1.0 + sc1) + sh1).astype(bf16)

    w_chunk = w_in_ref[...].astype(bf16)
    w_in_bf_ref[...] = w_chunk
    part = _dot(h_scr[...], w_chunk)
    for c in range(n_chunks):
        @pl.when(j == c)
        def _(c=c):
            proj_scr[:, c * chunk:(c + 1) * chunk] = part

    @pl.when(j == n_chunks - 1)
    def _():
        def proj(c0, n):
            return proj_scr[:, c0:c0 + n]

        w_a = w_a_ref[...].astype(bf16)
        w_a_bf_ref[...] = w_a
        u = proj(C_C, D_CONV) * proj(C_XIN, D_CONV)
        prev0 = st_ref[:, 0, :]
        prev1 = st_ref[:, 1, :]
        conv = cw_ref[0:1, :] * prev0 + cw_ref[1:2, :] * prev1 + cw_ref[2:3, :] * u
        stn_ref[:, 0, :] = prev1
        stn_ref[:, 1, :] = u
        ya = _dot((proj(C_B, D_CONV) * conv).astype(bf16), w_a)
        q_ref[...] = proj(C_Q, N_HEADS * HEAD_DIM) * ATTN_SCALE
        k = proj(C_K, KV_COLS)
        v = proj(C_V, KV_COLS)
        k_ref[...] = k
        v_ref[...] = v
        knt_ref[...] = k.T
        vnt_ref[...] = v.T
        za_ref[...] = jax.nn.sigmoid(proj(C_GA, D_MODEL)) * ya
        sgb_ref[...] = jax.nn.sigmoid(proj(C_GB, D_MODEL))


def _mixer_decode_pre(x, ada, g_mix, w_in, conv_w, state, w_a, *, chunk):
    n = x.shape[0]
    assert IN_COLS % chunk == 0 and chunk % LANES == 0
    const2 = lambda j: (0, 0)
    const3 = lambda j: (0, 0, 0)
    row_block = lambda cols: pl.BlockSpec((n, cols), const2)
    state_block = pl.BlockSpec((n, CONV_W - 1, D_CONV), const3)
    shapes = [((D_MODEL, IN_COLS), bf16), ((D_CONV, D_MODEL), bf16),
              ((n, N_HEADS * HEAD_DIM), f32), ((n, KV_COLS), f32), ((n, KV_COLS), f32),
              ((KV_COLS, n), f32), ((KV_COLS, n), f32),
              ((n, D_MODEL), f32), ((n, D_MODEL), f32), ((n, CONV_W - 1, D_CONV), f32)]
    return pl.pallas_call(
        functools.partial(_mixer_decode_pre_kernel, chunk=chunk),
        grid=(IN_COLS // chunk,),
        in_specs=[pl.BlockSpec((n, 1, D_MODEL), const3), row_block(2 * D_MODEL), pl.BlockSpec((1, D_MODEL), const2),
                  pl.BlockSpec((D_MODEL, chunk), lambda j: (0, j)),
                  pl.BlockSpec((CONV_W, D_CONV), const2), state_block,
                  pl.BlockSpec((D_CONV, D_MODEL), const2)],
        out_specs=[pl.BlockSpec((D_MODEL, chunk), lambda j: (0, j)), pl.BlockSpec((D_CONV, D_MODEL), const2),
                   row_block(N_HEADS * HEAD_DIM), row_block(KV_COLS), row_block(KV_COLS),
                   pl.BlockSpec((KV_COLS, n), const2), pl.BlockSpec((KV_COLS, n), const2),
                   row_block(D_MODEL), row_block(D_MODEL), state_block],
        out_shape=[jax.ShapeDtypeStruct(shp, dt) for shp, dt in shapes],
        scratch_shapes=[pltpu.VMEM((n, D_MODEL), bf16), pltpu.VMEM((n, IN_COLS), f32)],
        compiler_params=pltpu.CompilerParams(
            dimension_semantics=("arbitrary",),
            vmem_limit_bytes=VMEM_LIMIT_MEDIUM),
        name="mixer_decode_pre",
    )(x, ada, g_mix, w_in, conv_w, state, w_a)


def _attn_decode_kernel(q_ref, kn_ref, vn_ref, knt_ref, vnt_ref, ck_ref, cv_ref, sink_ref,
                        att_ref, ok_ref, ov_ref, *, bb):
    step = pl.program_id(0)
    last = lax.broadcasted_iota(jnp.int32, (HEAD_DIM, WINDOW), 1) == WINDOW - 1
    to_front = (LANES - step * bb) % LANES
    kstep = [pltpu.roll(knt_ref[g], to_front, axis=1) for g in range(N_KV)]
    vstep = [pltpu.roll(vnt_ref[g], to_front, axis=1) for g in range(N_KV)]

    pairs = [(b, g) for b in range(bb) for g in range(N_KV)]
    n_stage = 7
    cuts = [len(pairs) * s // n_stage for s in range(n_stage + 1)]

    def shift_caches(stage):
        for b, g in pairs[cuts[stage]:cuts[stage + 1]]:
            ok_ref[b, g] = jnp.where(last, pltpu.roll(kstep[g], WINDOW - 1 - b, axis=1),
                                     pltpu.roll(ck_ref[b, g], WINDOW - 1, axis=1))
            ov_ref[b, g] = jnp.where(last, pltpu.roll(vstep[g], WINDOW - 1 - b, axis=1),
                                     pltpu.roll(cv_ref[b, g], WINDOW - 1, axis=1))

    sinks = [sink_ref[g][:, 0:1] for g in range(N_KV)]
    shift_caches(0)
    s_old = [_dot(q_ref[b, g].astype(bf16), ck_ref[b, g].astype(bf16)) for b, g in pairs]
    shift_caches(1)
    s_new = [jnp.sum(q_ref[b, g] * kn_ref[b, g], axis=-1, keepdims=True) for b, g in pairs]
    shift_caches(2)
    m = [jnp.maximum(jnp.maximum(jnp.max(so, axis=-1, keepdims=True), sn), sinks[g])
         for (b, g), so, sn in zip(pairs, s_old, s_new)]
    shift_caches(3)
    e_old = [jnp.exp(so - mm) for so, mm in zip(s_old, m)]
    e_new = [jnp.exp(sn - mm) for sn, mm in zip(s_new, m)]
    den = [jnp.sum(eo, axis=-1, keepdims=True) + en + jnp.exp(sinks[g] - mm)
           for (b, g), eo, en, mm in zip(pairs, e_old, e_new, m)]
    shift_caches(4)
    o_old = [lax.dot_general(eo.astype(bf16), cv_ref[b, g].astype(bf16), (((1,), (1,)), ((), ())),
                             preferred_element_type=f32) for (b, g), eo in zip(pairs, e_old)]
    shift_caches(5)
    for (b, g), oo, en, dd in zip(pairs, o_old, e_new, den):
        att_ref[b, g] = (oo + en * vn_ref[b, g]) / dd
    shift_caches(6)


def _attn_decode(q4, kn4, vn4, knt, vnt, ck, cv, sink_b, *, bb):
    n = q4.shape[0]
    assert n % bb == 0 and n == LANES and WINDOW == LANES
    cache_spec = pl.BlockSpec((bb, N_KV, HEAD_DIM, WINDOW), lambda b: (b, 0, 0, 0))
    row_spec = pl.BlockSpec((bb, N_KV, 1, HEAD_DIM), lambda b: (b, 0, 0, 0))
    q_spec = pl.BlockSpec((bb, N_KV, GROUP, HEAD_DIM), lambda b: (b, 0, 0, 0))
    new_t_spec = pl.BlockSpec((N_KV, HEAD_DIM, n), lambda b: (0, 0, 0))
    return pl.pallas_call(
        functools.partial(_attn_decode_kernel, bb=bb),
        grid=(n // bb,),
        in_specs=[q_spec, row_spec, row_spec, new_t_spec, new_t_spec, cache_spec, cache_spec,
                  pl.BlockSpec((N_KV, GROUP, LANES), lambda b: (0, 0, 0))],
        out_specs=[q_spec, cache_spec, cache_spec],
        out_shape=[jax.ShapeDtypeStruct((n, N_KV, GROUP, HEAD_DIM), f32),
                   jax.ShapeDtypeStruct((n, N_KV, HEAD_DIM, WINDOW), f32),
                   jax.ShapeDtypeStruct((n, N_KV, HEAD_DIM, WINDOW), f32)],
        compiler_params=pltpu.CompilerParams(dimension_semantics=("arbitrary",)),
        name="attn_decode",
    )(q4, kn4, vn4, knt, vnt, ck, cv, sink_b)


def _post_decode_kernel(x_ref, ada_ref, za_ref, sgb_ref, att_ref, w_b_ref, w_o_ref, g_ref, w_up_ref,
                        fcw_ref, fcb_ref, w_down_ref, gf_ref, fst_ref,
                        w_b_bf_ref, w_o_bf_ref, w_up_bf_ref, w_down_bf_ref, y_ref, fstn_ref,
                        yb_scr, x1_scr, h_scr, up_scr, act_scr, acc_scr, *, phases):
    j = pl.program_id(0)
    (b0, nb, cb), (o0, no, co), (u0, nu, cu), (d0, nd, cd) = phases

    def mod(k):
        return ada_ref[:, k * D_MODEL:(k + 1) * D_MODEL]

    for c in range(nb):
        @pl.when(j == b0 + c)
        def _(c=c):
            w = w_b_ref[...].astype(bf16)
            w_b_bf_ref[...] = w
            yb_scr[:, c * cb:(c + 1) * cb] = _dot(att_ref[...].astype(bf16), w)

    for c in range(no):
        @pl.when(j == o0 + c)
        def _(c=c):
            w = w_o_ref[...].astype(bf16)
            w_o_bf_ref[...] = w
            mix = (za_ref[...] + sgb_ref[...] * yb_scr[...]).astype(bf16)
            cols = slice(c * co, (c + 1) * co)
            x1_scr[:, cols] = x_ref[:, 0, cols] + mod(2)[:, cols] * _dot(mix, w)

    for c in range(nu):
        @pl.when(j == u0 + c)
        def _(c=c):
            if c == 0:
                h_scr[...] = (_rms(x1_scr[...], g_ref[...]) * (1.0 + mod(4)) + mod(3)).astype(bf16)
            w = w_up_ref[...].astype(bf16)
            w_up_bf_ref[...] = w
            up_scr[:, c * cu:(c + 1) * cu] = _dot(h_scr[...], w)

    for c in range(nd):
        @pl.when(j == d0 + c)
        def _(c=c):
            if c == 0:
                up = up_scr[...]
                prev0 = fst_ref[:, 0, :]
                prev1 = fst_ref[:, 1, :]
                conv = fcw_ref[0:1, :] * prev0 + fcw_ref[1:2, :] * prev1 + fcw_ref[2:3, :] * up + fcb_ref[...]
                fstn_ref[:, 0, :] = prev1
                fstn_ref[:, 1, :] = up
                act_scr[...] = (_silu(conv[:, 0:D_FF]) * conv[:, D_FF:2 * D_FF]).astype(bf16)
            w = w_down_ref[...].astype(bf16)
            w_down_bf_ref[...] = w
            cols = slice(c * cd, (c + 1) * cd)
            acc_scr[:, cols] = x1_scr[:, cols] + mod(5)[:, cols] * _dot(act_scr[...], w)
            if c == nd - 1:
                y_ref[:, 0, :] = _rms(acc_scr[...], gf_ref[...])


def _post_decode(x, ada, za, sgb, att, w_b, w_o, g_ffn, w_up, fcw, fcb, w_down, g_final, fstate,
                 *, proj_chunk, up_chunk, down_chunk):
    n = x.shape[0]
    assert D_MODEL % proj_chunk == 0 and (2 * D_FF) % up_chunk == 0 and D_MODEL % down_chunk == 0
    assert proj_chunk % LANES == 0 and up_chunk % LANES == 0 and down_chunk % LANES == 0
    nb = no = D_MODEL // proj_chunk
    nu, nd = 2 * D_FF // up_chunk, D_MODEL // down_chunk
    b0, o0, u0, d0 = 0, nb, nb + no, nb + no + nu
    phases = ((b0, nb, proj_chunk), (o0, no, proj_chunk), (u0, nu, up_chunk), (d0, nd, down_chunk))

    def chunk_index(start, count):
        return lambda j: jnp.clip(j - start, 0, count - 1)

    ib, io, iu, idn = (chunk_index(s0, cnt) for s0, cnt, _ in phases)
    const2 = lambda j: (0, 0)
    const3 = lambda j: (0, 0, 0)
    rows = lambda cols: pl.BlockSpec((n, cols), const2, pipeline_mode=pl.Buffered(1))
    state_block = pl.BlockSpec((n, CONV_W - 1, 2 * D_FF), const3, pipeline_mode=pl.Buffered(1))
    token_block = pl.BlockSpec((n, 1, D_MODEL), const3, pipeline_mode=pl.Buffered(1))
    col_chunk = lambda k, width, idx: pl.BlockSpec((k, width), lambda j: (0, idx(j)))
    w_down_block = col_chunk(D_FF, down_chunk, idn)
    return pl.pallas_call(
        functools.partial(_post_decode_kernel, phases=phases),
        grid=(d0 + nd,),
        in_specs=[token_block, rows(N_MOD * D_MODEL), rows(D_MODEL), rows(D_MODEL), rows(N_HEADS * HEAD_DIM),
                  col_chunk(N_HEADS * HEAD_DIM, proj_chunk, ib), col_chunk(D_MODEL, proj_chunk, io),
                  pl.BlockSpec((1, D_MODEL), const2), col_chunk(D_MODEL, up_chunk, iu),
                  pl.BlockSpec((CONV_W, 2 * D_FF), const2), pl.BlockSpec((1, 2 * D_FF), const2),
                  w_down_block, pl.BlockSpec((1, D_MODEL), const2), state_block],
        out_specs=[col_chunk(N_HEADS * HEAD_DIM, proj_chunk, ib), col_chunk(D_MODEL, proj_chunk, io),
                   col_chunk(D_MODEL, up_chunk, iu), w_down_block, token_block, state_block],
        out_shape=[jax.ShapeDtypeStruct((N_HEADS * HEAD_DIM, D_MODEL), bf16),
                   jax.ShapeDtypeStruct((D_MODEL, D_MODEL), bf16),
                   jax.ShapeDtypeStruct((D_MODEL, 2 * D_FF), bf16),
                   jax.ShapeDtypeStruct((D_FF, D_MODEL), bf16),
                   jax.ShapeDtypeStruct((n, 1, D_MODEL), f32),
                   jax.ShapeDtypeStruct((n, CONV_W - 1, 2 * D_FF), f32)],
        scratch_shapes=[pltpu.VMEM((n, D_MODEL), f32), pltpu.VMEM((n, D_MODEL), f32), pltpu.VMEM((n, D_MODEL), bf16),
                        pltpu.VMEM((n, 2 * D_FF), f32), pltpu.VMEM((n, D_FF), bf16), pltpu.VMEM((n, D_MODEL), f32)],
        compiler_params=pltpu.CompilerParams(
            dimension_semantics=("arbitrary",),
            vmem_limit_bytes=VMEM_LIMIT_LARGE),
        name="post_decode",
    )(x, ada, za, sgb, att, w_b, w_o, g_ffn, w_up, fcw, fcb, w_down, g_final, fstate)


PROMPT_MIXER_ROWS = 512
PROMPT_FFN_ROWS = 1024
PROMPT_FFN_COLS = 256
PROMPT_FFN_DOWN_COLS = 1024
DECODE_ATTN_BATCH = 32
ADA_PER_STEP = 2
DECODE_IN_CHUNK = 1664
DECODE_PROJ_CHUNK = 512
DECODE_UP_CHUNK = 512
DECODE_DOWN_CHUNK = 256


def kernel(x_prompt, x_sample, c_prompt, c_sample, state_conv_a, cache_k_win, cache_v_win, state_ffn_conv, w_ada, b_ada, g_mix, w_in, conv_a_w, attn_sinks, w_a_out, w_b_out, w_o, g_ffn, w_up, ffn_conv_w, ffn_conv_b, w_down, g_final):
    depth = w_in.shape[0]
    n_p, seq, _ = x_prompt.shape
    n_s, t_s, _ = x_sample.shape
    assert n_p == 1 and t_s == 1, "one prompt sequence and single-token decode only"
    xp = x_prompt.reshape(seq, D_MODEL)
    xs = x_sample
    gf = g_final.reshape(1, D_MODEL)
    outs = [[] for _ in range(8)]
    for l in range(depth):
        ada_s, ada_p = _ada(c_sample, c_prompt, w_ada[l], b_ada[l], per_step=ADA_PER_STEP)
        gm, gn = g_mix[l].reshape(1, D_MODEL), g_ffn[l].reshape(1, D_MODEL)
        fcb = ffn_conv_b[l].reshape(1, 2 * D_FF)

        w_in_b, w_a_b, q, k_n, v_n, k_nt, v_nt, za, sgb, conv_s = _mixer_decode_pre(
            xs, ada_s, gm, w_in[l], conv_a_w[l], state_conv_a[l], w_a_out[l], chunk=DECODE_IN_CHUNK)
        to_native = lambda c: c.transpose(0, 2, 3, 1)
        from_native = lambda c: c.transpose(0, 3, 1, 2)
        sink_b = jnp.broadcast_to(attn_sinks[l].reshape(N_KV, GROUP, 1), (N_KV, GROUP, LANES))
        att4, k_s, v_s = _attn_decode(
            q.reshape(n_s, N_KV, GROUP, HEAD_DIM),
            k_n.reshape(n_s, N_KV, 1, HEAD_DIM), v_n.reshape(n_s, N_KV, 1, HEAD_DIM),
            k_nt.reshape(N_KV, HEAD_DIM, n_s), v_nt.reshape(N_KV, HEAD_DIM, n_s),
            to_native(cache_k_win[l]), to_native(cache_v_win[l]), sink_b, bb=DECODE_ATTN_BATCH)
        k_s, v_s = from_native(k_s), from_native(v_s)
        w_b_b, w_o_b, w_up_b, w_down_b, xs, ffn_s = _post_decode(
            xs, ada_s, za, sgb, att4.reshape(n_s, N_HEADS * HEAD_DIM), w_b_out[l], w_o[l], gn, w_up[l],
            ffn_conv_w[l], fcb, w_down[l], gf, state_ffn_conv[l],
            proj_chunk=DECODE_PROJ_CHUNK, up_chunk=DECODE_UP_CHUNK, down_chunk=DECODE_DOWN_CHUNK)

        x1, conv_p, k_p, v_p = _mixer_prompt(xp, ada_p, gm, w_in_b, conv_a_w[l], attn_sinks[l],
                                             w_a_b, w_b_b, w_o_b, tb=PROMPT_MIXER_ROWS)
        xp, ffn_p = _ffn_prompt(x1, ada_p, gn, w_up_b, ffn_conv_w[l], fcb, w_down_b, gf,
                                tb=PROMPT_FFN_ROWS, ch=PROMPT_FFN_COLS, nw=PROMPT_FFN_DOWN_COLS)

        for lst, val in zip(outs, (
                conv_p.reshape(n_p, CONV_W - 1, D_CONV), conv_s,
                from_native(k_p.reshape(n_p, N_KV, HEAD_DIM, WINDOW)), k_s,
                from_native(v_p.reshape(n_p, N_KV, HEAD_DIM, WINDOW)), v_s,
                ffn_p.reshape(n_p, CONV_W - 1, 2 * D_FF), ffn_s)):
            lst.append(val)
    assert depth == 1, "final RMSNorm is fused into the single layer's FFN kernels"
    return (xp.reshape(n_p, seq, D_MODEL), xs) + tuple(jnp.stack(o) for o in outs)
```

```python
import functools

import jax
import jax.numpy as jnp
from jax import lax
from jax.experimental import pallas as pl
from jax.experimental.pallas import tpu as pltpu

f32 = jnp.float32
bf16 = jnp.bfloat16

D_MODEL = 1024
D_CONV = D_MODEL
CONV_W = 3
N_HEADS = 16
N_KV = 4
GROUP = N_HEADS // N_KV
HEAD_DIM = 64
WINDOW = 128
D_FF = 2816
EPS = 1e-6
N_MOD = 6
ATTN_SCALE = HEAD_DIM ** -0.5
KV_COLS = N_KV * HEAD_DIM
C_XIN, C_B, C_C = 0, D_CONV, 2 * D_CONV
C_Q = 3 * D_CONV
C_K = C_Q + N_HEADS * HEAD_DIM
C_V = C_K + KV_COLS
C_GA = C_V + KV_COLS
C_GB = C_GA + D_MODEL
IN_COLS = C_GB + D_MODEL

LANES = 128
SUBLANES = 8
Q_SUB = 128
ATTN_LOOKAHEAD = 3
FFN_LOOKAHEAD = 2
MXU_COLS = 256
VMEM_BYTES_V7X = 64 * 1024 * 1024
VMEM_LIMIT_LARGE = VMEM_BYTES_V7X * 7 // 8
VMEM_LIMIT_MEDIUM = VMEM_BYTES_V7X * 5 // 8
VMEM_LIMIT_SMALL = VMEM_BYTES_V7X // 2


def _rms(x, g):
    ms = jnp.mean(x * x, axis=-1, keepdims=True)
    return x * lax.rsqrt(ms + EPS) * g


def _silu(x):
    return x * jax.nn.sigmoid(x)


def _dot(a, b):
    return jnp.dot(a, b, preferred_element_type=f32)


def _const_spec(shape):
    nd = len(shape)
    return pl.BlockSpec(shape, lambda i: (0,) * nd, pipeline_mode=pl.Buffered(1))


def _ada_kernel(cs_ref, cp_ref, w_ref, b_ref, os_ref, op_ref, *, per_step):
    w = w_ref[...].astype(bf16)
    os_ref[...] = _dot(_silu(cs_ref[...]).astype(bf16), w) + b_ref[...]
    cp = jnp.broadcast_to(_silu(cp_ref[...]), (SUBLANES, D_MODEL)).astype(bf16)
    mod_p = _dot(cp, w)[0:1, :] + b_ref[...]
    for m in range(per_step):
        op_ref[m] = mod_p[:, m * D_MODEL:(m + 1) * D_MODEL]


def _ada(c_sample, c_prompt, w_ada, b_ada, *, per_step):
    n_s = c_sample.shape[0]
    assert c_prompt.shape[0] == 1 and N_MOD % per_step == 0
    cols = per_step * D_MODEL
    return pl.pallas_call(
        functools.partial(_ada_kernel, per_step=per_step),
        grid=(N_MOD // per_step,),
        in_specs=[
            pl.BlockSpec((n_s, D_MODEL), lambda j: (0, 0)),
            pl.BlockSpec((1, D_MODEL), lambda j: (0, 0)),
            pl.BlockSpec((D_MODEL, cols), lambda j: (0, j)),
            pl.BlockSpec((1, cols), lambda j: (0, j)),
        ],
        out_specs=[pl.BlockSpec((n_s, cols), lambda j: (0, j)),
                   pl.BlockSpec((per_step, 1, D_MODEL), lambda j: (j, 0, 0))],
        out_shape=[jax.ShapeDtypeStruct((n_s, N_MOD * D_MODEL), f32),
                   jax.ShapeDtypeStruct((N_MOD, 1, D_MODEL), f32)],
        compiler_params=pltpu.CompilerParams(
            dimension_semantics=("arbitrary",),
            vmem_limit_bytes=VMEM_LIMIT_SMALL),
        name="ada",
    )(c_sample, c_prompt, w_ada, b_ada.reshape(1, -1))


def _mixer_prompt_kernel(sinks_ref, x_ref, ada_ref, g_ref, w_in_ref, cw_ref, w_a_ref, w_b_ref, w_o_ref,
                         x1_ref, conv_ref, knew_ref, vnew_ref,
                         ubuf, klo, khi, vt, attbuf, *, tb):
    i = pl.program_id(0)

    @pl.when(i == 0)
    def _():
        ubuf[0:SUBLANES, :] = jnp.zeros((SUBLANES, D_CONV), f32)
        for r in (klo, khi):
            r[:, 0:WINDOW, :] = jnp.zeros((N_KV, WINDOW, LANES), bf16)
        vt[:, 0:WINDOW] = jnp.zeros((KV_COLS, WINDOW), bf16)

    x = x_ref[...]
    sh1, sc1, gt1 = ada_ref[0], ada_ref[1], ada_ref[2]
    h = (_rms(x, g_ref[...]) * (1.0 + sc1) + sh1).astype(bf16)

    def proj(c0, n):
        return _dot(h, w_in_ref[:, c0:c0 + n])

    k = proj(C_K, KV_COLS)
    v = proj(C_V, KV_COLS)
    u = proj(C_C, D_CONV) * proj(C_XIN, D_CONV)
    ubuf[SUBLANES:SUBLANES + tb, :] = u
    b_gate = proj(C_B, D_CONV)
    q = (proj(C_Q, N_HEADS * HEAD_DIM) * ATTN_SCALE).astype(bf16)
    conv = (cw_ref[0:1, :] * ubuf[SUBLANES - 2:SUBLANES - 2 + tb, :]
            + cw_ref[1:2, :] * ubuf[SUBLANES - 1:SUBLANES - 1 + tb, :]
            + cw_ref[2:3, :] * u)
    conv_in = (b_gate * conv).astype(bf16)
    knew_ref[...] = k[tb - WINDOW:, :].T
    vnew_ref[...] = v[tb - WINDOW:, :].T

    lo = lax.broadcasted_iota(jnp.int32, (tb, LANES), 1) < HEAD_DIM
    for j in range(N_KV // 2):
        pair = k[:, LANES * j:LANES * (j + 1)]
        rolled = pltpu.roll(pair, HEAD_DIM, axis=1)
        zero = jnp.zeros_like(pair)
        klo[2 * j, WINDOW:WINDOW + tb, :] = jnp.where(lo, pair, zero).astype(bf16)
        khi[2 * j, WINDOW:WINDOW + tb, :] = jnp.where(lo, zero, rolled).astype(bf16)
        klo[2 * j + 1, WINDOW:WINDOW + tb, :] = jnp.where(lo, rolled, zero).astype(bf16)
        khi[2 * j + 1, WINDOW:WINDOW + tb, :] = jnp.where(lo, zero, pair).astype(bf16)
    vt[:, WINDOW:WINDOW + tb] = v.T.astype(bf16)

    cc = lax.broadcasted_iota(jnp.int32, (Q_SUB + WINDOW, 2 * Q_SUB), 0)
    col = lax.broadcasted_iota(jnp.int32, (Q_SUB + WINDOW, 2 * Q_SUB), 1)
    rr = col & (Q_SUB - 1)
    band = (cc >= rr) & (cc <= rr + WINDOW)
    first_head = lax.broadcasted_iota(jnp.int32, (1, 2 * Q_SUB), 1) < Q_SUB
    mask0 = band & (cc >= jnp.where(i == 0, WINDOW, 0))

    def scores(j, g, t):
        r0 = Q_SUB * j
        qs = jnp.concatenate([q[r0:r0 + Q_SUB, 2 * LANES * g:2 * LANES * g + LANES],
                              q[r0:r0 + Q_SUB, 2 * LANES * g + LANES:2 * LANES * (g + 1)]], axis=0)
        kr = (klo, khi)[t]
        return lax.dot_general(kr[g, r0:r0 + Q_SUB + WINDOW, :], qs, (((1,), (1,)), ((), ())),
                               preferred_element_type=f32)

    def finish(j, g, t, st):
        r0 = Q_SUB * j
        h0, h1 = GROUP * g + t, GROUP * g + 2 + t
        sink = jnp.where(first_head, sinks_ref[h0], sinks_ref[h1])
        st = jnp.where(mask0 if j == 0 else band, st, -jnp.inf)
        m = jnp.maximum(jnp.max(st, axis=0, keepdims=True), sink)
        e = jnp.exp(st - m)
        linv = 1.0 / (jnp.sum(e, axis=0, keepdims=True) + jnp.exp(sink - m))
        vtg = vt[HEAD_DIM * g:HEAD_DIM * (g + 1), r0:r0 + Q_SUB + WINDOW]
        ot = _dot(vtg, e.astype(bf16)) * linv
        attbuf[HEAD_DIM * h0:HEAD_DIM * (h0 + 1), r0:r0 + Q_SUB] = ot[:, 0:Q_SUB]
        attbuf[HEAD_DIM * h1:HEAD_DIM * (h1 + 1), r0:r0 + Q_SUB] = ot[:, Q_SUB:2 * Q_SUB]

    chains = [(j, g, t) for j in range(tb // Q_SUB) for g in range(N_KV) for t in range(2)]
    fillers = ([functools.partial(proj, c, MXU_COLS) for c in range(C_GA, C_GB + D_MODEL, MXU_COLS)]
               + [(lambda c=c: _dot(conv_in, w_a_ref[:, c:c + MXU_COLS])) for c in range(0, D_MODEL, MXU_COLS)])
    fill_every = -(-len(chains) // len(fillers))
    filled = []
    pending = [scores(*c) for c in chains[:ATTN_LOOKAHEAD]]
    for n, chain in enumerate(chains):
        if n + ATTN_LOOKAHEAD < len(chains):
            pending.append(scores(*chains[n + ATTN_LOOKAHEAD]))
        if n % fill_every == 0:
            filled.extend(f() for f in fillers[len(filled):len(filled) + 1])
        finish(*chain, pending.pop(0))
    filled.extend(f() for f in fillers[len(filled):])
    ga, gb, ya = (jnp.concatenate(filled[j:j + D_MODEL // MXU_COLS], axis=1)
                  for j in range(0, len(filled), D_MODEL // MXU_COLS))

    yb = _dot(attbuf[...].T.astype(bf16), w_b_ref[...])
    mix = (jax.nn.sigmoid(ga) * ya + jax.nn.sigmoid(gb) * yb).astype(bf16)
    x1_ref[...] = x + gt1 * _dot(mix, w_o_ref[...])

    conv_ref[...] = ubuf[SUBLANES + tb - (CONV_W - 1):SUBLANES + tb, :]
    ubuf[0:SUBLANES, :] = ubuf[tb:tb + SUBLANES, :]
    for r in (klo, khi):
        r[:, 0:WINDOW, :] = r[:, tb:tb + WINDOW, :]
    vt[:, 0:WINDOW] = vt[:, tb:tb + WINDOW]


def _mixer_prompt(x, ada, g_mix, w_in, conv_w, sinks, w_a, w_b, w_o, *, tb):
    s = x.shape[0]
    assert s % tb == 0 and tb % Q_SUB == 0 and tb >= WINDOW
    kv_scratch = pltpu.VMEM((N_KV, WINDOW + tb, LANES), bf16)
    grid_spec = pltpu.PrefetchScalarGridSpec(
        num_scalar_prefetch=1,
        grid=(s // tb,),
        in_specs=[
            pl.BlockSpec((tb, D_MODEL), lambda i, sk: (i, 0)),
            pl.BlockSpec((N_MOD, 1, D_MODEL), lambda i, sk: (0, 0, 0)),
            pl.BlockSpec((1, D_MODEL), lambda i, sk: (0, 0)),
            pl.BlockSpec((D_MODEL, IN_COLS), lambda i, sk: (0, 0), pipeline_mode=pl.Buffered(1)),
            pl.BlockSpec((CONV_W, D_CONV), lambda i, sk: (0, 0)),
            pl.BlockSpec((D_CONV, D_MODEL), lambda i, sk: (0, 0), pipeline_mode=pl.Buffered(1)),
            pl.BlockSpec((N_HEADS * HEAD_DIM, D_MODEL), lambda i, sk: (0, 0), pipeline_mode=pl.Buffered(1)),
            pl.BlockSpec((D_MODEL, D_MODEL), lambda i, sk: (0, 0), pipeline_mode=pl.Buffered(1)),
        ],
        out_specs=[
            pl.BlockSpec((tb, D_MODEL), lambda i, sk: (i, 0)),
            pl.BlockSpec((CONV_W - 1, D_CONV), lambda i, sk: (0, 0)),
            pl.BlockSpec((KV_COLS, WINDOW), lambda i, sk: (0, 0)),
            pl.BlockSpec((KV_COLS, WINDOW), lambda i, sk: (0, 0)),
        ],
        scratch_shapes=[
            pltpu.VMEM((SUBLANES + tb, D_CONV), f32),
            kv_scratch, kv_scratch,
            pltpu.VMEM((KV_COLS, WINDOW + tb), bf16),
            pltpu.VMEM((N_HEADS * HEAD_DIM, tb), f32),
        ],
    )
    return pl.pallas_call(
        functools.partial(_mixer_prompt_kernel, tb=tb),
        grid_spec=grid_spec,
        out_shape=[
            jax.ShapeDtypeStruct((s, D_MODEL), f32),
            jax.ShapeDtypeStruct((CONV_W - 1, D_CONV), f32),
            jax.ShapeDtypeStruct((KV_COLS, WINDOW), f32),
            jax.ShapeDtypeStruct((KV_COLS, WINDOW), f32),
        ],
        compiler_params=pltpu.CompilerParams(
            dimension_semantics=("arbitrary",),
            vmem_limit_bytes=VMEM_LIMIT_LARGE),
        name="mixer_prompt",
    )(sinks, x, ada, g_mix, w_in, conv_w, w_a, w_b, w_o)


def _ffn_prompt_kernel(x_ref, ada_ref, g_ref, w_up_ref, fcw_ref, fcb_ref, w_down_ref, gf_ref,
                       y_ref, fst_ref, upbuf, actbuf, *, tb, ch, nw):
    i = pl.program_id(0)

    @pl.when(i == 0)
    def _():
        upbuf[:, 0:SUBLANES, :] = jnp.zeros((2 * D_FF // LANES, SUBLANES, LANES), f32)

    hb = tb // 2
    sh2, sc2, gt2 = ada_ref[3], ada_ref[4], ada_ref[5]
    h = (_rms(x_ref[...], g_ref[...]) * (1.0 + sc2) + sh2).astype(bf16)

    def up_cols(half, c0):
        up = _dot(h[half * hb:(half + 1) * hb, :], w_up_ref[:, c0:c0 + ch])
        for s in range(ch // LANES):
            upbuf[c0 // LANES + s, SUBLANES:SUBLANES + hb, :] = up[:, s * LANES:(s + 1) * LANES]
        return up

    def conv_cols(c0, up):
        pieces = []
        for s in range(ch // LANES):
            slab = c0 // LANES + s
            cols = slice(c0 + s * LANES, c0 + (s + 1) * LANES)
            pieces.append(fcw_ref[0:1, cols] * upbuf[slab, SUBLANES - 2:SUBLANES - 2 + hb, :]
                          + fcw_ref[1:2, cols] * upbuf[slab, SUBLANES - 1:SUBLANES - 1 + hb, :]
                          + fcw_ref[2:3, cols] * up[:, s * LANES:(s + 1) * LANES]
                          + fcb_ref[0:1, cols])
            upbuf[slab, 0:SUBLANES, :] = upbuf[slab, hb:hb + SUBLANES, :]
        return jnp.concatenate(pieces, axis=1)

    def down_cols(half, n0):
        return _dot(actbuf[half], w_down_ref[:, n0:n0 + nw])

    def finish(half, parts):
        rows = slice(half * hb, (half + 1) * hb)
        x2 = x_ref[rows, :] + gt2 * jnp.concatenate(parts, axis=1)
        y_ref[rows, :] = _rms(x2, gf_ref[...])

    chunks = list(range(0, D_FF, ch))
    down_starts = list(range(0, D_MODEL, nw))
    down_at = {len(chunks) * (k + 1) // (len(down_starts) + 1): n0 for k, n0 in enumerate(down_starts)}
    assert len(down_at) == len(down_starts)
    for half in range(2):
        parts = []
        pending = [(up_cols(half, c0), up_cols(half, D_FF + c0)) for c0 in chunks[:FFN_LOOKAHEAD]]
        for n, c0 in enumerate(chunks):
            if n + FFN_LOOKAHEAD < len(chunks):
                c1 = chunks[n + FFN_LOOKAHEAD]
                pending.append((up_cols(half, c1), up_cols(half, D_FF + c1)))
            if half == 1 and n in down_at:
                parts.append(down_cols(0, down_at[n]))
            up_g, up_v = pending.pop(0)
            actbuf[half, :, c0:c0 + ch] = (_silu(conv_cols(c0, up_g))
                                           * conv_cols(D_FF + c0, up_v)).astype(bf16)
        if half == 1:
            finish(0, parts)
    finish(1, [down_cols(1, n0) for n0 in down_starts])

    for slab in range(2 * D_FF // LANES):
        fst_ref[:, slab * LANES:(slab + 1) * LANES] = upbuf[slab, SUBLANES - (CONV_W - 1):SUBLANES, :]


def _ffn_prompt(x1, ada, g_ffn, w_up, fcw, fcb, w_down, g_final, *, tb, ch, nw):
    s = x1.shape[0]
    assert s % tb == 0 and tb % (4 * SUBLANES) == 0
    assert D_FF % ch == 0 and ch % LANES == 0 and D_MODEL % nw == 0 and nw % LANES == 0
    return pl.pallas_call(
        functools.partial(_ffn_prompt_kernel, tb=tb, ch=ch, nw=nw),
        grid=(s // tb,),
        in_specs=[
            pl.BlockSpec((tb, D_MODEL), lambda i: (i, 0)),
            pl.BlockSpec((N_MOD, 1, D_MODEL), lambda i: (0, 0, 0)),
            pl.BlockSpec((1, D_MODEL), lambda i: (0, 0)),
            _const_spec((D_MODEL, 2 * D_FF)),
            pl.BlockSpec((CONV_W, 2 * D_FF), lambda i: (0, 0)),
            pl.BlockSpec((1, 2 * D_FF), lambda i: (0, 0)),
            _const_spec((D_FF, D_MODEL)),
            pl.BlockSpec((1, D_MODEL), lambda i: (0, 0)),
        ],
        out_specs=[
            pl.BlockSpec((tb, D_MODEL), lambda i: (i, 0)),
            pl.BlockSpec((CONV_W - 1, 2 * D_FF), lambda i: (0, 0)),
        ],
        out_shape=[
            jax.ShapeDtypeStruct((s, D_MODEL), f32),
            jax.ShapeDtypeStruct((CONV_W - 1, 2 * D_FF), f32),
        ],
        scratch_shapes=[
            pltpu.VMEM((2 * D_FF // LANES, SUBLANES + tb // 2, LANES), f32),
            pltpu.VMEM((2, tb // 2, D_FF), bf16),
        ],
        compiler_params=pltpu.CompilerParams(
            dimension_semantics=("arbitrary",),
            vmem_limit_bytes=VMEM_LIMIT_LARGE),
        name="ffn_prompt",
    )(x1, ada, g_ffn, w_up, fcw, fcb, w_down, g_final)


def _mixer_decode_pre_kernel(x_ref, ada_ref, g_ref, w_in_ref, cw_ref, st_ref, w_a_ref,
                             w_in_bf_ref, w_a_bf_ref, q_ref, k_ref, v_ref, knt_ref, vnt_ref, za_ref, sgb_ref, stn_ref,
                             h_scr, proj_scr, *, chunk):
    j = pl.program_id(0)
    n_chunks = IN_COLS // chunk

    @pl.when(j == 0)
    def _():
        sh1 = ada_ref[:, 0:D_MODEL]
        sc1 = ada_ref[:, D_MODEL:2 * D_MODEL]
        h_scr[...] = (_rms(x_ref[:, 0, :], g_ref[...]) * (1.0 + sc1) + sh1).astype(bf16)

    w_chunk = w_in_ref[...].astype(bf16)
    w_in_bf_ref[...] = w_chunk
    part = _dot(h_scr[...], w_chunk)
    for c in range(n_chunks):
        @pl.when(j == c)
        def _(c=c):
            proj_scr[:, c * chunk:(c + 1) * chunk] = part

    @pl.when(j == n_chunks - 1)
    def _():
        def proj(c0, n):
            return proj_scr[:, c0:c0 + n]

        w_a = w_a_ref[...].astype(bf16)
        w_a_bf_ref[...] = w_a
        u = proj(C_C, D_CONV) * proj(C_XIN, D_CONV)
        prev0 = st_ref[:, 0, :]
        prev1 = st_ref[:, 1, :]
        conv = cw_ref[0:1, :] * prev0 + cw_ref[1:2, :] * prev1 + cw_ref[2:3, :] * u
        stn_ref[:, 0, :] = prev1
        stn_ref[:, 1, :] = u
        ya = _dot((proj(C_B, D_CONV) * conv).astype(bf16), w_a)
        q_ref[...] = proj(C_Q, N_HEADS * HEAD_DIM) * ATTN_SCALE
        k = proj(C_K, KV_COLS)
        v = proj(C_V, KV_COLS)
        k_ref[...] = k
        v_ref[...] = v
        knt_ref[...] = k.T
        vnt_ref[...] = v.T
        za_ref[...] = jax.nn.sigmoid(proj(C_GA, D_MODEL)) * ya
        sgb_ref[...] = jax.nn.sigmoid(proj(C_GB, D_MODEL))


def _mixer_decode_pre(x, ada, g_mix, w_in, conv_w, state, w_a, *, chunk):
    n = x.shape[0]
    assert IN_COLS % chunk == 0 and chunk % LANES == 0
    const2 = lambda j: (0, 0)
    const3 = lambda j: (0, 0, 0)
    row_block = lambda cols: pl.BlockSpec((n, cols), const2)
    state_block = pl.BlockSpec((n, CONV_W - 1, D_CONV), const3)
    shapes = [((D_MODEL, IN_COLS), bf16), ((D_CONV, D_MODEL), bf16),
              ((n, N_HEADS * HEAD_DIM), f32), ((n, KV_COLS), f32), ((n, KV_COLS), f32),
              ((KV_COLS, n), f32), ((KV_COLS, n), f32),
              ((n, D_MODEL), f32), ((n, D_MODEL), f32), ((n, CONV_W - 1, D_CONV), f32)]
    return pl.pallas_call(
        functools.partial(_mixer_decode_pre_kernel, chunk=chunk),
        grid=(IN_COLS // chunk,),
        in_specs=[pl.BlockSpec((n, 1, D_MODEL), const3), row_block(2 * D_MODEL), pl.BlockSpec((1, D_MODEL), const2),
                  pl.BlockSpec((D_MODEL, chunk), lambda j: (0, j)),
                  pl.BlockSpec((CONV_W, D_CONV), const2), state_block,
                  pl.BlockSpec((D_CONV, D_MODEL), const2)],
        out_specs=[pl.BlockSpec((D_MODEL, chunk), lambda j: (0, j)), pl.BlockSpec((D_CONV, D_MODEL), const2),
                   row_block(N_HEADS * HEAD_DIM), row_block(KV_COLS), row_block(KV_COLS),
                   pl.BlockSpec((KV_COLS, n), const2), pl.BlockSpec((KV_COLS, n), const2),
                   row_block(D_MODEL), row_block(D_MODEL), state_block],
        out_shape=[jax.ShapeDtypeStruct(shp, dt) for shp, dt in shapes],
        scratch_shapes=[pltpu.VMEM((n, D_MODEL), bf16), pltpu.VMEM((n, IN_COLS), f32)],
        compiler_params=pltpu.CompilerParams(
            dimension_semantics=("arbitrary",),
            vmem_limit_bytes=VMEM_LIMIT_MEDIUM),
        name="mixer_decode_pre",
    )(x, ada, g_mix, w_in, conv_w, state, w_a)


def _attn_decode_kernel(q_ref, kn_ref, vn_ref, knt_ref, vnt_ref, ck_ref, cv_ref, sink_ref,
                        att_ref, ok_ref, ov_ref, *, bb):
    step = pl.program_id(0)
    last = lax.broadcasted_iota(jnp.int32, (HEAD_DIM, WINDOW), 1) == WINDOW - 1
    to_front = (LANES - step * bb) % LANES
    kstep = [pltpu.roll(knt_ref[g], to_front, axis=1) for g in range(N_KV)]
    vstep = [pltpu.roll(vnt_ref[g], to_front, axis=1) for g in range(N_KV)]

    pairs = [(b, g) for b in range(bb) for g in range(N_KV)]
    n_stage = 7
    cuts = [len(pairs) * s // n_stage for s in range(n_stage + 1)]

    def shift_caches(stage):
        for b, g in pairs[cuts[stage]:cuts[stage + 1]]:
            ok_ref[b, g] = jnp.where(last, pltpu.roll(kstep[g], WINDOW - 1 - b, axis=1),
                                     pltpu.roll(ck_ref[b, g], WINDOW - 1, axis=1))
            ov_ref[b, g] = jnp.where(last, pltpu.roll(vstep[g], WINDOW - 1 - b, axis=1),
                                     pltpu.roll(cv_ref[b, g], WINDOW - 1, axis=1))

    sinks = [sink_ref[g][:, 0:1] for g in range(N_KV)]
    shift_caches(0)
    s_old = [_dot(q_ref[b, g].astype(bf16), ck_ref[b, g].astype(bf16)) for b, g in pairs]
    shift_caches(1)
    s_new = [jnp.sum(q_ref[b, g] * kn_ref[b, g], axis=-1, keepdims=True) for b, g in pairs]
    shift_caches(2)
    m = [jnp.maximum(jnp.maximum(jnp.max(so, axis=-1, keepdims=True), sn), sinks[g])
         for (b, g), so, sn in zip(pairs, s_old, s_new)]
    shift_caches(3)
    e_old = [jnp.exp(so - mm) for so, mm in zip(s_old, m)]
    e_new = [jnp.exp(sn - mm) for sn, mm in zip(s_new, m)]
    den = [jnp.sum(eo, axis=-1, keepdims=True) + en + jnp.exp(sinks[g] - mm)
           for (b, g), eo, en, mm in zip(pairs, e_old, e_new, m)]
    shift_caches(4)
    o_old = [lax.dot_general(eo.astype(bf16), cv_ref[b, g].astype(bf16), (((1,), (1,)), ((), ())),
                             preferred_element_type=f32) for (b, g), eo in zip(pairs, e_old)]
    shift_caches(5)
    for (b, g), oo, en, dd in zip(pairs, o_old, e_new, den):
        att_ref[b, g] = (oo + en * vn_ref[b, g]) / dd
    shift_caches(6)


def _attn_decode(q4, kn4, vn4, knt, vnt, ck, cv, sink_b, *, bb):
    n = q4.shape[0]
    assert n % bb == 0 and n == LANES and WINDOW == LANES
    cache_spec = pl.BlockSpec((bb, N_KV, HEAD_DIM, WINDOW), lambda b: (b, 0, 0, 0))
    row_spec = pl.BlockSpec((bb, N_KV, 1, HEAD_DIM), lambda b: (b, 0, 0, 0))
    q_spec = pl.BlockSpec((bb, N_KV, GROUP, HEAD_DIM), lambda b: (b, 0, 0, 0))
    new_t_spec = pl.BlockSpec((N_KV, HEAD_DIM, n), lambda b: (0, 0, 0))
    return pl.pallas_call(
        functools.partial(_attn_decode_kernel, bb=bb),
        grid=(n // bb,),
        in_specs=[q_spec, row_spec, row_spec, new_t_spec, new_t_spec, cache_spec, cache_spec,
                  pl.BlockSpec((N_KV, GROUP, LANES), lambda b: (0, 0, 0))],
        out_specs=[q_spec, cache_spec, cache_spec],
        out_shape=[jax.ShapeDtypeStruct((n, N_KV, GROUP, HEAD_DIM), f32),
                   jax.ShapeDtypeStruct((n, N_KV, HEAD_DIM, WINDOW), f32),
                   jax.ShapeDtypeStruct((n, N_KV, HEAD_DIM, WINDOW), f32)],
        compiler_params=pltpu.CompilerParams(dimension_semantics=("arbitrary",)),
        name="attn_decode",
    )(q4, kn4, vn4, knt, vnt, ck, cv, sink_b)


def _post_decode_kernel(x_ref, ada_ref, za_ref, sgb_ref, att_ref, w_b_ref, w_o_ref, g_ref, w_up_ref,
                        fcw_ref, fcb_ref, w_down_ref, gf_ref, fst_ref,
                        w_b_bf_ref, w_o_bf_ref, w_up_bf_ref, w_down_bf_ref, y_ref, fstn_ref,
                        yb_scr, x1_scr, h_scr, up_scr, act_scr, acc_scr, *, phases):
    j = pl.program_id(0)
    (b0, nb, cb), (o0, no, co), (u0, nu, cu), (d0, nd, cd) = phases

    def mod(k):
        return ada_ref[:, k * D_MODEL:(k + 1) * D_MODEL]

    for c in range(nb):
        @pl.when(j == b0 + c)
        def _(c=c):
            w = w_b_ref[...].astype(bf16)
            w_b_bf_ref[...] = w
            yb_scr[:, c * cb:(c + 1) * cb] = _dot(att_ref[...].astype(bf16), w)

    for c in range(no):
        @pl.when(j == o0 + c)
        def _(c=c):
            w = w_o_ref[...].astype(bf16)
            w_o_bf_ref[...] = w
            mix = (za_ref[...] + sgb_ref[...] * yb_scr[...]).astype(bf16)
            cols = slice(c * co, (c + 1) * co)
            x1_scr[:, cols] = x_ref[:, 0, cols] + mod(2)[:, cols] * _dot(mix, w)

    for c in range(nu):
        @pl.when(j == u0 + c)
        def _(c=c):
            if c == 0:
                h_scr[...] = (_rms(x1_scr[...], g_ref[...]) * (1.0 + mod(4)) + mod(3)).astype(bf16)
            w = w_up_ref[...].astype(bf16)
            w_up_bf_ref[...] = w
            up_scr[:, c * cu:(c + 1) * cu] = _dot(h_scr[...], w)

    for c in range(nd):
        @pl.when(j == d0 + c)
        def _(c=c):
            if c == 0:
                up = up_scr[...]
                prev0 = fst_ref[:, 0, :]
                prev1 = fst_ref[:, 1, :]
                conv = fcw_ref[0:1, :] * prev0 + fcw_ref[1:2, :] * prev1 + fcw_ref[2:3, :] * up + fcb_ref[...]
                fstn_ref[:, 0, :] = prev1
                fstn_ref[:, 1, :] = up
                act_scr[...] = (_silu(conv[:, 0:D_FF]) * conv[:, D_FF:2 * D_FF]).astype(bf16)
            w = w_down_ref[...].astype(bf16)
            w_down_bf_ref[...] = w
            cols = slice(c * cd, (c + 1) * cd)
            acc_scr[:, cols] = x1_scr[:, cols] + mod(5)[:, cols] * _dot(act_scr[...], w)
            if c == nd - 1:
                y_ref[:, 0, :] = _rms(acc_scr[...], gf_ref[...])


def _post_decode(x, ada, za, sgb, att, w_b, w_o, g_ffn, w_up, fcw, fcb, w_down, g_final, fstate,
                 *, proj_chunk, up_chunk, down_chunk):
    n = x.shape[0]
    assert D_MODEL % proj_chunk == 0 and (2 * D_FF) % up_chunk == 0 and D_MODEL % down_chunk == 0
    assert proj_chunk % LANES == 0 and up_chunk % LANES == 0 and down_chunk % LANES == 0
    nb = no = D_MODEL // proj_chunk
    nu, nd = 2 * D_FF // up_chunk, D_MODEL // down_chunk
    b0, o0, u0, d0 = 0, nb, nb + no, nb + no + nu
    phases = ((b0, nb, proj_chunk), (o0, no, proj_chunk), (u0, nu, up_chunk), (d0, nd, down_chunk))

    def chunk_index(start, count):
        return lambda j: jnp.clip(j - start, 0, count - 1)

    ib, io, iu, idn = (chunk_index(s0, cnt) for s0, cnt, _ in phases)
    const2 = lambda j: (0, 0)
    const3 = lambda j: (0, 0, 0)
    rows = lambda cols: pl.BlockSpec((n, cols), const2, pipeline_mode=pl.Buffered(1))
    state_block = pl.BlockSpec((n, CONV_W - 1, 2 * D_FF), const3, pipeline_mode=pl.Buffered(1))
    token_block = pl.BlockSpec((n, 1, D_MODEL), const3, pipeline_mode=pl.Buffered(1))
    col_chunk = lambda k, width, idx: pl.BlockSpec((k, width), lambda j: (0, idx(j)))
    w_down_block = col_chunk(D_FF, down_chunk, idn)
    return pl.pallas_call(
        functools.partial(_post_decode_kernel, phases=phases),
        grid=(d0 + nd,),
        in_specs=[token_block, rows(N_MOD * D_MODEL), rows(D_MODEL), rows(D_MODEL), rows(N_HEADS * HEAD_DIM),
                  col_chunk(N_HEADS * HEAD_DIM, proj_chunk, ib), col_chunk(D_MODEL, proj_chunk, io),
                  pl.BlockSpec((1, D_MODEL), const2), col_chunk(D_MODEL, up_chunk, iu),
                  pl.BlockSpec((CONV_W, 2 * D_FF), const2), pl.BlockSpec((1, 2 * D_FF), const2),
                  w_down_block, pl.BlockSpec((1, D_MODEL), const2), state_block],
        out_specs=[col_chunk(N_HEADS * HEAD_DIM, proj_chunk, ib), col_chunk(D_MODEL, proj_chunk, io),
                   col_chunk(D_MODEL, up_chunk, iu), w_down_block, token_block, state_block],
        out_shape=[jax.ShapeDtypeStruct((N_HEADS * HEAD_DIM, D_MODEL), bf16),
                   jax.ShapeDtypeStruct((D_MODEL, D_MODEL), bf16),
                   jax.ShapeDtypeStruct((D_MODEL, 2 * D_FF), bf16),
                   jax.ShapeDtypeStruct((D_FF, D_MODEL), bf16),
                   jax.ShapeDtypeStruct((n, 1, D_MODEL), f32),
                   jax.ShapeDtypeStruct((n, CONV_W - 1, 2 * D_FF), f32)],
        scratch_shapes=[pltpu.VMEM((n, D_MODEL), f32), pltpu.VMEM((n, D_MODEL), f32), pltpu.VMEM((n, D_MODEL), bf16),
                        pltpu.VMEM((n, 2 * D_FF), f32), pltpu.VMEM((n, D_FF), bf16), pltpu.VMEM((n, D_MODEL), f32)],
        compiler_params=pltpu.CompilerParams(
            dimension_semantics=("arbitrary",),
            vmem_limit_bytes=VMEM_LIMIT_LARGE),
        name="post_decode",
    )(x, ada, za, sgb, att, w_b, w_o, g_ffn, w_up, fcw, fcb, w_down, g_final, fstate)


PROMPT_MIXER_ROWS = 512
PROMPT_FFN_ROWS = 1024
PROMPT_FFN_COLS = 256
PROMPT_FFN_DOWN_COLS = 1024
DECODE_ATTN_BATCH = 32
ADA_PER_STEP = 2
DECODE_IN_CHUNK = 1664
DECODE_PROJ_CHUNK = 512
DECODE_UP_CHUNK = 512
DECODE_DOWN_CHUNK = 256


def kernel(x_prompt, x_sample, c_prompt, c_sample, state_conv_a, cache_k_win, cache_v_win, state_ffn_conv, w_ada, b_ada, g_mix, w_in, conv_a_w, attn_sinks, w_a_out, w_b_out, w_o, g_ffn, w_up, ffn_conv_w, ffn_conv_b, w_down, g_final):
    depth = w_in.shape[0]
    n_p, seq, _ = x_prompt.shape
    n_s, t_s, _ = x_sample.shape
    assert n_p == 1 and t_s == 1, "one prompt sequence and single-token decode only"
    xp = x_prompt.reshape(seq, D_MODEL)
    xs = x_sample
    gf = g_final.reshape(1, D_MODEL)
    outs = [[] for _ in range(8)]
    for l in range(depth):
        ada_s, ada_p = _ada(c_sample, c_prompt, w_ada[l], b_ada[l], per_step=ADA_PER_STEP)
        gm, gn = g_mix[l].reshape(1, D_MODEL), g_ffn[l].reshape(1, D_MODEL)
        fcb = ffn_conv_b[l].reshape(1, 2 * D_FF)

        w_in_b, w_a_b, q, k_n, v_n, k_nt, v_nt, za, sgb, conv_s = _mixer_decode_pre(
            xs, ada_s, gm, w_in[l], conv_a_w[l], state_conv_a[l], w_a_out[l], chunk=DECODE_IN_CHUNK)
        to_native = lambda c: c.transpose(0, 2, 3, 1)
        from_native = lambda c: c.transpose(0, 3, 1, 2)
        sink_b = jnp.broadcast_to(attn_sinks[l].reshape(N_KV, GROUP, 1), (N_KV, GROUP, LANES))
        att4, k_s, v_s = _attn_decode(
            q.reshape(n_s, N_KV, GROUP, HEAD_DIM),
            k_n.reshape(n_s, N_KV, 1, HEAD_DIM), v_n.reshape(n_s, N_KV, 1, HEAD_DIM),
            k_nt.reshape(N_KV, HEAD_DIM, n_s), v_nt.reshape(N_KV, HEAD_DIM, n_s),
            to_native(cache_k_win[l]), to_native(cache_v_win[l]), sink_b, bb=DECODE_ATTN_BATCH)
        k_s, v_s = from_native(k_s), from_native(v_s)
        w_b_b, w_o_b, w_up_b, w_down_b, xs, ffn_s = _post_decode(
            xs, ada_s, za, sgb, att4.reshape(n_s, N_HEADS * HEAD_DIM), w_b_out[l], w_o[l], gn, w_up[l],
            ffn_conv_w[l], fcb, w_down[l], gf, state_ffn_conv[l],
            proj_chunk=DECODE_PROJ_CHUNK, up_chunk=DECODE_UP_CHUNK, down_chunk=DECODE_DOWN_CHUNK)

        x1, conv_p, k_p, v_p = _mixer_prompt(xp, ada_p, gm, w_in_b, conv_a_w[l], attn_sinks[l],
                                             w_a_b, w_b_b, w_o_b, tb=PROMPT_MIXER_ROWS)
        xp, ffn_p = _ffn_prompt(x1, ada_p, gn, w_up_b, ffn_conv_w[l], fcb, w_down_b, gf,
                                tb=PROMPT_FFN_ROWS, ch=PROMPT_FFN_COLS, nw=PROMPT_FFN_DOWN_COLS)

        for lst, val in zip(outs, (
                conv_p.reshape(n_p, CONV_W - 1, D_CONV), conv_s,
                from_native(k_p.reshape(n_p, N_KV, HEAD_DIM, WINDOW)), k_s,
                from_native(v_p.reshape(n_p, N_KV, HEAD_DIM, WINDOW)), v_s,
                ffn_p.reshape(n_p, CONV_W - 1, 2 * D_FF), ffn_s)):
            lst.append(val)
    assert depth == 1, "final RMSNorm is fused into the single layer's FFN kernels"
    return (xp.reshape(n_p, seq, D_MODEL), xs) + tuple(jnp.stack(o) for o in outs)
```

```python
import functools

import jax
import jax.numpy as jnp
from jax import lax
from jax.experimental import pallas as pl
from jax.experimental.pallas import tpu as pltpu

f32 = jnp.float32
bf16 = jnp.bfloat16

D_MODEL = 1024
D_CONV = D_MODEL
CONV_W = 3
N_HEADS = 16
N_KV = 4
GROUP = N_HEADS // N_KV
HEAD_DIM = 64
WINDOW = 128
D_FF = 2816
EPS = 1e-6
N_MOD = 6
ATTN_SCALE = HEAD_DIM ** -0.5
KV_COLS = N_KV * HEAD_DIM
C_XIN, C_B, C_C = 0, D_CONV, 2 * D_CONV
C_Q = 3 * D_CONV
C_K = C_Q + N_HEADS * HEAD_DIM
C_V = C_K + KV_COLS
C_GA = C_V + KV_COLS
C_GB = C_GA + D_MODEL
IN_COLS = C_GB + D_MODEL

LANES = 128
SUBLANES = 8
Q_SUB = 128
ATTN_LOOKAHEAD = 3
FFN_LOOKAHEAD = 2
MXU_COLS = 256
VMEM_BYTES_V7X = 64 * 1024 * 1024
VMEM_LIMIT_LARGE = VMEM_BYTES_V7X * 7 // 8
VMEM_LIMIT_MEDIUM = VMEM_BYTES_V7X * 5 // 8
VMEM_LIMIT_SMALL = VMEM_BYTES_V7X // 2


def _rms(x, g):
    ms = jnp.mean(x * x, axis=-1, keepdims=True)
    return x * lax.rsqrt(ms + EPS) * g


def _silu(x):
    return x * jax.nn.sigmoid(x)


def _dot(a, b):
    return jnp.dot(a, b, preferred_element_type=f32)


def _const_spec(shape):
    nd = len(shape)
    return pl.BlockSpec(shape, lambda i: (0,) * nd, pipeline_mode=pl.Buffered(1))


def _ada_kernel(cs_ref, cp_ref, w_ref, b_ref, os_ref, op_ref, *, per_step):
    w = w_ref[...].astype(bf16)
    os_ref[...] = _dot(_silu(cs_ref[...]).astype(bf16), w) + b_ref[...]
    cp = jnp.broadcast_to(_silu(cp_ref[...]), (SUBLANES, D_MODEL)).astype(bf16)
    mod_p = _dot(cp, w)[0:1, :] + b_ref[...]
    for m in range(per_step):
        op_ref[m] = mod_p[:, m * D_MODEL:(m + 1) * D_MODEL]


def _ada(c_sample, c_prompt, w_ada, b_ada, *, per_step):
    n_s = c_sample.shape[0]
    assert c_prompt.shape[0] == 1 and N_MOD % per_step == 0
    cols = per_step * D_MODEL
    return pl.pallas_call(
        functools.partial(_ada_kernel, per_step=per_step),
        grid=(N_MOD // per_step,),
        in_specs=[
            pl.BlockSpec((n_s, D_MODEL), lambda j: (0, 0)),
            pl.BlockSpec((1, D_MODEL), lambda j: (0, 0)),
            pl.BlockSpec((D_MODEL, cols), lambda j: (0, j)),
            pl.BlockSpec((1, cols), lambda j: (0, j)),
        ],
        out_specs=[pl.BlockSpec((n_s, cols), lambda j: (0, j)),
                   pl.BlockSpec((per_step, 1, D_MODEL), lambda j: (j, 0, 0))],
        out_shape=[jax.ShapeDtypeStruct((n_s, N_MOD * D_MODEL), f32),
                   jax.ShapeDtypeStruct((N_MOD, 1, D_MODEL), f32)],
        compiler_params=pltpu.CompilerParams(
            dimension_semantics=("arbitrary",),
            vmem_limit_bytes=VMEM_LIMIT_SMALL),
        name="ada",
    )(c_sample, c_prompt, w_ada, b_ada.reshape(1, -1))


def _mixer_prompt_kernel(sinks_ref, x_ref, ada_ref, g_ref, w_in_ref, cw_ref, w_a_ref, w_b_ref, w_o_ref,
                         x1_ref, conv_ref, knew_ref, vnew_ref,
                         ubuf, klo, khi, vt, attbuf, *, tb):
    i = pl.program_id(0)

    @pl.when(i == 0)
    def _():
        ubuf[:, 0:SUBLANES, :] = jnp.zeros((D_CONV // LANES, SUBLANES, LANES), f32)
        for r in (klo, khi):
            r[:, 0:WINDOW, :] = jnp.zeros((N_KV, WINDOW, LANES), bf16)
        vt[:, 0:WINDOW] = jnp.zeros((KV_COLS, WINDOW), bf16)

    x = x_ref[...]
    sh1, sc1, gt1 = ada_ref[0], ada_ref[1], ada_ref[2]
    h = (_rms(x, g_ref[...]) * (1.0 + sc1) + sh1).astype(bf16)

    def proj(c0, n):
        return _dot(h, w_in_ref[:, c0:c0 + n])

    k = proj(C_K, KV_COLS)
    v = proj(C_V, KV_COLS)
    u = proj(C_C, D_CONV) * proj(C_XIN, D_CONV)
    for s in range(D_CONV // LANES):
        ubuf[s, SUBLANES:SUBLANES + tb, :] = u[:, s * LANES:(s + 1) * LANES]
    b_gate = proj(C_B, D_CONV)
    q = (proj(C_Q, N_HEADS * HEAD_DIM) * ATTN_SCALE).astype(bf16)
    conv = jnp.concatenate(
        [cw_ref[0:1, s * LANES:(s + 1) * LANES] * ubuf[s, SUBLANES - 2:SUBLANES - 2 + tb, :]
         + cw_ref[1:2, s * LANES:(s + 1) * LANES] * ubuf[s, SUBLANES - 1:SUBLANES - 1 + tb, :]
         + cw_ref[2:3, s * LANES:(s + 1) * LANES] * u[:, s * LANES:(s + 1) * LANES]
         for s in range(D_CONV // LANES)], axis=1)
    conv_in = (b_gate * conv).astype(bf16)
    knew_ref[...] = k[tb - WINDOW:, :].T
    vnew_ref[...] = v[tb - WINDOW:, :].T

    lo = lax.broadcasted_iota(jnp.int32, (tb, LANES), 1) < HEAD_DIM
    for j in range(N_KV // 2):
        pair = k[:, LANES * j:LANES * (j + 1)]
        rolled = pltpu.roll(pair, HEAD_DIM, axis=1)
        zero = jnp.zeros_like(pair)
        klo[2 * j, WINDOW:WINDOW + tb, :] = jnp.where(lo, pair, zero).astype(bf16)
        khi[2 * j, WINDOW:WINDOW + tb, :] = jnp.where(lo, zero, rolled).astype(bf16)
        klo[2 * j + 1, WINDOW:WINDOW + tb, :] = jnp.where(lo, rolled, zero).astype(bf16)
        khi[2 * j + 1, WINDOW:WINDOW + tb, :] = jnp.where(lo, zero, pair).astype(bf16)
    vt[:, WINDOW:WINDOW + tb] = v.T.astype(bf16)

    cc = lax.broadcasted_iota(jnp.int32, (Q_SUB + WINDOW, 2 * Q_SUB), 0)
    col = lax.broadcasted_iota(jnp.int32, (Q_SUB + WINDOW, 2 * Q_SUB), 1)
    rr = col & (Q_SUB - 1)
    band = (cc >= rr) & (cc <= rr + WINDOW)
    first_head = lax.broadcasted_iota(jnp.int32, (1, 2 * Q_SUB), 1) < Q_SUB
    mask0 = band & (cc >= jnp.where(i == 0, WINDOW, 0))

    def scores(j, g, t):
        r0 = Q_SUB * j
        qs = jnp.concatenate([q[r0:r0 + Q_SUB, 2 * LANES * g:2 * LANES * g + LANES],
                              q[r0:r0 + Q_SUB, 2 * LANES * g + LANES:2 * LANES * (g + 1)]], axis=0)
        kr = (klo, khi)[t]
        return lax.dot_general(kr[g, r0:r0 + Q_SUB + WINDOW, :], qs, (((1,), (1,)), ((), ())),
                               preferred_element_type=f32)

    def finish(j, g, t, st):
        r0 = Q_SUB * j
        h0, h1 = GROUP * g + t, GROUP * g + 2 + t
        sink = jnp.where(first_head, sinks_ref[h0], sinks_ref[h1])
        st = jnp.where(mask0 if j == 0 else band, st, -jnp.inf)
        m = jnp.maximum(jnp.max(st, axis=0, keepdims=True), sink)
        e = jnp.exp(st - m)
        linv = 1.0 / (jnp.sum(e, axis=0, keepdims=True) + jnp.exp(sink - m))
        vtg = vt[HEAD_DIM * g:HEAD_DIM * (g + 1), r0:r0 + Q_SUB + WINDOW]
        ot = _dot(vtg, e.astype(bf16)) * linv
        attbuf[HEAD_DIM * h0:HEAD_DIM * (h0 + 1), r0:r0 + Q_SUB] = ot[:, 0:Q_SUB]
        attbuf[HEAD_DIM * h1:HEAD_DIM * (h1 + 1), r0:r0 + Q_SUB] = ot[:, Q_SUB:2 * Q_SUB]

    chains = [(j, g, t) for j in range(tb // Q_SUB) for g in range(N_KV) for t in range(2)]
    fillers = ([functools.partial(proj, c, MXU_COLS) for c in range(C_GA, C_GB + D_MODEL, MXU_COLS)]
               + [(lambda c=c: _dot(conv_in, w_a_ref[:, c:c + MXU_COLS])) for c in range(0, D_MODEL, MXU_COLS)])
    fill_every = -(-len(chains) // len(fillers))
    filled = []
    pending = [scores(*c) for c in chains[:ATTN_LOOKAHEAD]]
    for n, chain in enumerate(chains):
        if n + ATTN_LOOKAHEAD < len(chains):
            pending.append(scores(*chains[n + ATTN_LOOKAHEAD]))
        if n % fill_every == 0:
            filled.extend(f() for f in fillers[len(filled):len(filled) + 1])
        finish(*chain, pending.pop(0))
    filled.extend(f() for f in fillers[len(filled):])
    ga, gb, ya = (jnp.concatenate(filled[j:j + D_MODEL // MXU_COLS], axis=1)
                  for j in range(0, len(filled), D_MODEL // MXU_COLS))

    yb = _dot(attbuf[...].T.astype(bf16), w_b_ref[...])
    mix = (jax.nn.sigmoid(ga) * ya + jax.nn.sigmoid(gb) * yb).astype(bf16)
    x1_ref[...] = x + gt1 * _dot(mix, w_o_ref[...])

    for s in range(D_CONV // LANES):
        conv_ref[:, s * LANES:(s + 1) * LANES] = ubuf[s, SUBLANES + tb - (CONV_W - 1):SUBLANES + tb, :]
        ubuf[s, 0:SUBLANES, :] = ubuf[s, tb:tb + SUBLANES, :]
    for r in (klo, khi):
        r[:, 0:WINDOW, :] = r[:, tb:tb + WINDOW, :]
    vt[:, 0:WINDOW] = vt[:, tb:tb + WINDOW]


def _mixer_prompt(x, ada, g_mix, w_in, conv_w, sinks, w_a, w_b, w_o, *, tb):
    s = x.shape[0]
    assert s % tb == 0 and tb % Q_SUB == 0 and tb >= WINDOW
    kv_scratch = pltpu.VMEM((N_KV, WINDOW + tb, LANES), bf16)
    grid_spec = pltpu.PrefetchScalarGridSpec(
        num_scalar_prefetch=1,
        grid=(s // tb,),
        in_specs=[
            pl.BlockSpec((tb, D_MODEL), lambda i, sk: (i, 0)),
            pl.BlockSpec((N_MOD, 1, D_MODEL), lambda i, sk: (0, 0, 0)),
            pl.BlockSpec((1, D_MODEL), lambda i, sk: (0, 0)),
            pl.BlockSpec((D_MODEL, IN_COLS), lambda i, sk: (0, 0), pipeline_mode=pl.Buffered(1)),
            pl.BlockSpec((CONV_W, D_CONV), lambda i, sk: (0, 0)),
            pl.BlockSpec((D_CONV, D_MODEL), lambda i, sk: (0, 0), pipeline_mode=pl.Buffered(1)),
            pl.BlockSpec((N_HEADS * HEAD_DIM, D_MODEL), lambda i, sk: (0, 0), pipeline_mode=pl.Buffered(1)),
            pl.BlockSpec((D_MODEL, D_MODEL), lambda i, sk: (0, 0), pipeline_mode=pl.Buffered(1)),
        ],
        out_specs=[
            pl.BlockSpec((tb, D_MODEL), lambda i, sk: (i, 0)),
            pl.BlockSpec((CONV_W - 1, D_CONV), lambda i, sk: (0, 0)),
            pl.BlockSpec((KV_COLS, WINDOW), lambda i, sk: (0, 0)),
            pl.BlockSpec((KV_COLS, WINDOW), lambda i, sk: (0, 0)),
        ],
        scratch_shapes=[
            pltpu.VMEM((D_CONV // LANES, SUBLANES + tb, LANES), f32),
            kv_scratch, kv_scratch,
            pltpu.VMEM((KV_COLS, WINDOW + tb), bf16),
            pltpu.VMEM((N_HEADS * HEAD_DIM, tb), f32),
        ],
    )
    return pl.pallas_call(
        functools.partial(_mixer_prompt_kernel, tb=tb),
        grid_spec=grid_spec,
        out_shape=[
            jax.ShapeDtypeStruct((s, D_MODEL), f32),
            jax.ShapeDtypeStruct((CONV_W - 1, D_CONV), f32),
            jax.ShapeDtypeStruct((KV_COLS, WINDOW), f32),
            jax.ShapeDtypeStruct((KV_COLS, WINDOW), f32),
        ],
        compiler_params=pltpu.CompilerParams(
            dimension_semantics=("arbitrary",),
            vmem_limit_bytes=VMEM_LIMIT_LARGE),
        name="mixer_prompt",
    )(sinks, x, ada, g_mix, w_in, conv_w, w_a, w_b, w_o)


def _ffn_prompt_kernel(x_ref, ada_ref, g_ref, w_up_ref, fcw_ref, fcb_ref, w_down_ref, gf_ref,
                       y_ref, fst_ref, upbuf, actbuf, *, tb, ch, nw):
    i = pl.program_id(0)

    @pl.when(i == 0)
    def _():
        upbuf[:, 0:SUBLANES, :] = jnp.zeros((2 * D_FF // LANES, SUBLANES, LANES), f32)

    hb = tb // 2
    sh2, sc2, gt2 = ada_ref[3], ada_ref[4], ada_ref[5]
    h = (_rms(x_ref[...], g_ref[...]) * (1.0 + sc2) + sh2).astype(bf16)

    def up_cols(half, c0):
        up = _dot(h[half * hb:(half + 1) * hb, :], w_up_ref[:, c0:c0 + ch])
        for s in range(ch // LANES):
            upbuf[c0 // LANES + s, SUBLANES:SUBLANES + hb, :] = up[:, s * LANES:(s + 1) * LANES]
        return up

    def conv_cols(c0, up):
        pieces = []
        for s in range(ch // LANES):
            slab = c0 // LANES + s
            cols = slice(c0 + s * LANES, c0 + (s + 1) * LANES)
            pieces.append(fcw_ref[0:1, cols] * upbuf[slab, SUBLANES - 2:SUBLANES - 2 + hb, :]
                          + fcw_ref[1:2, cols] * upbuf[slab, SUBLANES - 1:SUBLANES - 1 + hb, :]
                          + fcw_ref[2:3, cols] * up[:, s * LANES:(s + 1) * LANES]
                          + fcb_ref[0:1, cols])
            upbuf[slab, 0:SUBLANES, :] = upbuf[slab, hb:hb + SUBLANES, :]
        return jnp.concatenate(pieces, axis=1)

    def down_cols(half, n0):
        return _dot(actbuf[half], w_down_ref[:, n0:n0 + nw])

    def finish(half, parts):
        rows = slice(half * hb, (half + 1) * hb)
        x2 = x_ref[rows, :] + gt2 * jnp.concatenate(parts, axis=1)
        y_ref[rows, :] = _rms(x2, gf_ref[...])

    chunks = list(range(0, D_FF, ch))
    down_starts = list(range(0, D_MODEL, nw))
    down_at = {len(chunks) * (k + 1) // (len(down_starts) + 1): n0 for k, n0 in enumerate(down_starts)}
    assert len(down_at) == len(down_starts)
    for half in range(2):
        parts = []
        pending = [(up_cols(half, c0), up_cols(half, D_FF + c0)) for c0 in chunks[:FFN_LOOKAHEAD]]
        for n, c0 in enumerate(chunks):
            if n + FFN_LOOKAHEAD < len(chunks):
                c1 = chunks[n + FFN_LOOKAHEAD]
                pending.append((up_cols(half, c1), up_cols(half, D_FF + c1)))
            if half == 1 and n in down_at:
                parts.append(down_cols(0, down_at[n]))
            up_g, up_v = pending.pop(0)
            actbuf[half, :, c0:c0 + ch] = (_silu(conv_cols(c0, up_g))
                                           * conv_cols(D_FF + c0, up_v)).astype(bf16)
        if half == 1:
            finish(0, parts)
    finish(1, [down_cols(1, n0) for n0 in down_starts])

    for slab in range(2 * D_FF // LANES):
        fst_ref[:, slab * LANES:(slab + 1) * LANES] = upbuf[slab, SUBLANES - (CONV_W - 1):SUBLANES, :]


def _ffn_prompt(x1, ada, g_ffn, w_up, fcw, fcb, w_down, g_final, *, tb, ch, nw):
    s = x1.shape[0]
    assert s % tb == 0 and tb % (4 * SUBLANES) == 0
    assert D_FF % ch == 0 and ch % LANES == 0 and D_MODEL % nw == 0 and nw % LANES == 0
    return pl.pallas_call(
        functools.partial(_ffn_prompt_kernel, tb=tb, ch=ch, nw=nw),
        grid=(s // tb,),
        in_specs=[
            pl.BlockSpec((tb, D_MODEL), lambda i: (i, 0)),
            pl.BlockSpec((N_MOD, 1, D_MODEL), lambda i: (0, 0, 0)),
            pl.BlockSpec((1, D_MODEL), lambda i: (0, 0)),
            _const_spec((D_MODEL, 2 * D_FF)),
            pl.BlockSpec((CONV_W, 2 * D_FF), lambda i: (0, 0)),
            pl.BlockSpec((1, 2 * D_FF), lambda i: (0, 0)),
            _const_spec((D_FF, D_MODEL)),
            pl.BlockSpec((1, D_MODEL), lambda i: (0, 0)),
        ],
        out_specs=[
            pl.BlockSpec((tb, D_MODEL), lambda i: (i, 0)),
            pl.BlockSpec((CONV_W - 1, 2 * D_FF), lambda i: (0, 0)),
        ],
        out_shape=[
            jax.ShapeDtypeStruct((s, D_MODEL), f32),
            jax.ShapeDtypeStruct((CONV_W - 1, 2 * D_FF), f32),
        ],
        scratch_shapes=[
            pltpu.VMEM((2 * D_FF // LANES, SUBLANES + tb // 2, LANES), f32),
            pltpu.VMEM((2, tb // 2, D_FF), bf16),
        ],
        compiler_params=pltpu.CompilerParams(
            dimension_semantics=("arbitrary",),
            vmem_limit_bytes=VMEM_LIMIT_LARGE),
        name="ffn_prompt",
    )(x1, ada, g_ffn, w_up, fcw, fcb, w_down, g_final)


def _mixer_decode_pre_kernel(x_ref, ada_ref, g_ref, w_in_ref, cw_ref, st_ref, w_a_ref,
                             w_in_bf_ref, w_a_bf_ref, q_ref, k_ref, v_ref, knt_ref, vnt_ref, za_ref, sgb_ref, stn_ref,
                             h_scr, proj_scr, *, chunk):
    j = pl.program_id(0)
    n_chunks = IN_COLS // chunk

    @pl.when(j == 0)
    def _():
        sh1 = ada_ref[:, 0:D_MODEL]
        sc1 = ada_ref[:, D_MODEL:2 * D_MODEL]
        h_scr[...] = (_rms(x_ref[:, 0, :], g_ref[...]) * (1.0 + sc1) + sh1).astype(bf16)

    w_chunk = w_in_ref[...].astype(bf16)
    w_in_bf_ref[...] = w_chunk
    part = _dot(h_scr[...], w_chunk)
    for c in range(n_chunks):
        @pl.when(j == c)
        def _(c=c):
            proj_scr[:, c * chunk:(c + 1) * chunk] = part

    @pl.when(j == n_chunks - 1)
    def _():
        def proj(c0, n):
            return proj_scr[:, c0:c0 + n]

        w_a = w_a_ref[...].astype(bf16)
        w_a_bf_ref[...] = w_a
        u = proj(C_C, D_CONV) * proj(C_XIN, D_CONV)
        prev0 = st_ref[:, 0, :]
        prev1 = st_ref[:, 1, :]
        conv = cw_ref[0:1, :] * prev0 + cw_ref[1:2, :] * prev1 + cw_ref[2:3, :] * u
        stn_ref[:, 0, :] = prev1
        stn_ref[:, 1, :] = u
        ya = _dot((proj(C_B, D_CONV) * conv).astype(bf16), w_a)
        q_ref[...] = proj(C_Q, N_HEADS * HEAD_DIM) * ATTN_SCALE
        k = proj(C_K, KV_COLS)
        v = proj(C_V, KV_COLS)
        k_ref[...] = k
        v_ref[...] = v
        knt_ref[...] = k.T
        vnt_ref[...] = v.T
        za_ref[...] = jax.nn.sigmoid(proj(C_GA, D_MODEL)) * ya
        sgb_ref[...] = jax.nn.sigmoid(proj(C_GB, D_MODEL))


def _mixer_decode_pre(x, ada, g_mix, w_in, conv_w, state, w_a, *, chunk):
    n = x.shape[0]
    assert IN_COLS % chunk == 0 and chunk % LANES == 0
    const2 = lambda j: (0, 0)
    const3 = lambda j: (0, 0, 0)
    row_block = lambda cols: pl.BlockSpec((n, cols), const2)
    state_block = pl.BlockSpec((n, CONV_W - 1, D_CONV), const3)
    shapes = [((D_MODEL, IN_COLS), bf16), ((D_CONV, D_MODEL), bf16),
              ((n, N_HEADS * HEAD_DIM), f32), ((n, KV_COLS), f32), ((n, KV_COLS), f32),
              ((KV_COLS, n), f32), ((KV_COLS, n), f32),
              ((n, D_MODEL), f32), ((n, D_MODEL), f32), ((n, CONV_W - 1, D_CONV), f32)]
    return pl.pallas_call(
        functools.partial(_mixer_decode_pre_kernel, chunk=chunk),
        grid=(IN_COLS // chunk,),
        in_specs=[pl.BlockSpec((n, 1, D_MODEL), const3), row_block(2 * D_MODEL), pl.BlockSpec((1, D_MODEL), const2),
                  pl.BlockSpec((D_MODEL, chunk), lambda j: (0, j)),
                  pl.BlockSpec((CONV_W, D_CONV), const2), state_block,
                  pl.BlockSpec((D_CONV, D_MODEL), const2)],
        out_specs=[pl.BlockSpec((D_MODEL, chunk), lambda j: (0, j)), pl.BlockSpec((D_CONV, D_MODEL), const2),
                   row_block(N_HEADS * HEAD_DIM), row_block(KV_COLS), row_block(KV_COLS),
                   pl.BlockSpec((KV_COLS, n), const2), pl.BlockSpec((KV_COLS, n), const2),
                   row_block(D_MODEL), row_block(D_MODEL), state_block],
        out_shape=[jax.ShapeDtypeStruct(shp, dt) for shp, dt in shapes],
        scratch_shapes=[pltpu.VMEM((n, D_MODEL), bf16), pltpu.VMEM((n, IN_COLS), f32)],
        compiler_params=pltpu.CompilerParams(
            dimension_semantics=("arbitrary",),
            vmem_limit_bytes=VMEM_LIMIT_MEDIUM),
        name="mixer_decode_pre",
    )(x, ada, g_mix, w_in, conv_w, state, w_a)


def _attn_decode_kernel(q_ref, kn_ref, vn_ref, knt_ref, vnt_ref, ck_ref, cv_ref, sink_ref,
                        att_ref, ok_ref, ov_ref, *, bb):
    step = pl.program_id(0)
    last = lax.broadcasted_iota(jnp.int32, (HEAD_DIM, WINDOW), 1) == WINDOW - 1
    to_front = (LANES - step * bb) % LANES
    kstep = [pltpu.roll(knt_ref[g], to_front, axis=1) for g in range(N_KV)]
    vstep = [pltpu.roll(vnt_ref[g], to_front, axis=1) for g in range(N_KV)]

    pairs = [(b, g) for b in range(bb) for g in range(N_KV)]
    n_stage = 7
    cuts = [len(pairs) * s // n_stage for s in range(n_stage + 1)]

    def shift_caches(stage):
        for b, g in pairs[cuts[stage]:cuts[stage + 1]]:
            ok_ref[b, g] = jnp.where(last, pltpu.roll(kstep[g], WINDOW - 1 - b, axis=1),
                                     pltpu.roll(ck_ref[b, g], WINDOW - 1, axis=1))
            ov_ref[b, g] = jnp.where(last, pltpu.roll(vstep[g], WINDOW - 1 - b, axis=1),
                                     pltpu.roll(cv_ref[b, g], WINDOW - 1, axis=1))

    sinks = [sink_ref[g][:, 0:1] for g in range(N_KV)]
    shift_caches(0)
    s_old = [_dot(q_ref[b, g].astype(bf16), ck_ref[b, g].astype(bf16)) for b, g in pairs]
    shift_caches(1)
    s_new = [jnp.sum(q_ref[b, g] * kn_ref[b, g], axis=-1, keepdims=True) for b, g in pairs]
    shift_caches(2)
    m = [jnp.maximum(jnp.maximum(jnp.max(so, axis=-1, keepdims=True), sn), sinks[g])
         for (b, g), so, sn in zip(pairs, s_old, s_new)]
    shift_caches(3)
    e_old = [jnp.exp(so - mm) for so, mm in zip(s_old, m)]
    e_new = [jnp.exp(sn - mm) for sn, mm in zip(s_new, m)]
    den = [jnp.sum(eo, axis=-1, keepdims=True) + en + jnp.exp(sinks[g] - mm)
           for (b, g), eo, en, mm in zip(pairs, e_old, e_new, m)]
    shift_caches(4)
    o_old = [lax.dot_general(eo.astype(bf16), cv_ref[b, g].astype(bf16), (((1,), (1,)), ((), ())),
                             preferred_element_type=f32) for (b, g), eo in zip(pairs, e_old)]
    shift_caches(5)
    for (b, g), oo, en, dd in zip(pairs, o_old, e_new, den):
        att_ref[b, g] = (oo + en * vn_ref[b, g]) / dd
    shift_caches(6)


def _attn_decode(q4, kn4, vn4, knt, vnt, ck, cv, sink_b, *, bb):
    n = q4.shape[0]
    assert n % bb == 0 and n == LANES and WINDOW == LANES
    cache_spec = pl.BlockSpec((bb, N_KV, HEAD_DIM, WINDOW), lambda b: (b, 0, 0, 0))
    row_spec = pl.BlockSpec((bb, N_KV, 1, HEAD_DIM), lambda b: (b, 0, 0, 0))
    q_spec = pl.BlockSpec((bb, N_KV, GROUP, HEAD_DIM), lambda b: (b, 0, 0, 0))
    new_t_spec = pl.BlockSpec((N_KV, HEAD_DIM, n), lambda b: (0, 0, 0))
    return pl.pallas_call(
        functools.partial(_attn_decode_kernel, bb=bb),
        grid=(n // bb,),
        in_specs=[q_spec, row_spec, row_spec, new_t_spec, new_t_spec, cache_spec, cache_spec,
                  pl.BlockSpec((N_KV, GROUP, LANES), lambda b: (0, 0, 0))],
        out_specs=[q_spec, cache_spec, cache_spec],
        out_shape=[jax.ShapeDtypeStruct((n, N_KV, GROUP, HEAD_DIM), f32),
                   jax.ShapeDtypeStruct((n, N_KV, HEAD_DIM, WINDOW), f32),
                   jax.ShapeDtypeStruct((n, N_KV, HEAD_DIM, WINDOW), f32)],
        compiler_params=pltpu.CompilerParams(dimension_semantics=("arbitrary",)),
        name="attn_decode",
    )(q4, kn4, vn4, knt, vnt, ck, cv, sink_b)


def _post_decode_kernel(x_ref, ada_ref, za_ref, sgb_ref, att_ref, w_b_ref, w_o_ref, g_ref, w_up_ref,
                        fcw_ref, fcb_ref, w_down_ref, gf_ref, fst_ref,
                        w_b_bf_ref, w_o_bf_ref, w_up_bf_ref, w_down_bf_ref, y_ref, fstn_ref,
                        yb_scr, x1_scr, h_scr, up_scr, act_scr, acc_scr, *, phases):
    j = pl.program_id(0)
    (b0, nb, cb), (o0, no, co), (u0, nu, cu), (d0, nd, cd) = phases

    def mod(k):
        return ada_ref[:, k * D_MODEL:(k + 1) * D_MODEL]

    for c in range(nb):
        @pl.when(j == b0 + c)
        def _(c=c):
            w = w_b_ref[...].astype(bf16)
            w_b_bf_ref[...] = w
            yb_scr[:, c * cb:(c + 1) * cb] = _dot(att_ref[...].astype(bf16), w)

    for c in range(no):
        @pl.when(j == o0 + c)
        def _(c=c):
            w = w_o_ref[...].astype(bf16)
            w_o_bf_ref[...] = w
            mix = (za_ref[...] + sgb_ref[...] * yb_scr[...]).astype(bf16)
            cols = slice(c * co, (c + 1) * co)
            x1_scr[:, cols] = x_ref[:, 0, cols] + mod(2)[:, cols] * _dot(mix, w)

    for c in range(nu):
        @pl.when(j == u0 + c)
        def _(c=c):
            if c == 0:
                h_scr[...] = (_rms(x1_scr[...], g_ref[...]) * (1.0 + mod(4)) + mod(3)).astype(bf16)
            w = w_up_ref[...].astype(bf16)
            w_up_bf_ref[...] = w
            up_scr[:, c * cu:(c + 1) * cu] = _dot(h_scr[...], w)

    for c in range(nd):
        @pl.when(j == d0 + c)
        def _(c=c):
            if c == 0:
                up = up_scr[...]
                prev0 = fst_ref[:, 0, :]
                prev1 = fst_ref[:, 1, :]
                conv = fcw_ref[0:1, :] * prev0 + fcw_ref[1:2, :] * prev1 + fcw_ref[2:3, :] * up + fcb_ref[...]
                fstn_ref[:, 0, :] = prev1
                fstn_ref[:, 1, :] = up
                act_scr[...] = (_silu(conv[:, 0:D_FF]) * conv[:, D_FF:2 * D_FF]).astype(bf16)
            w = w_down_ref[...].astype(bf16)
            w_down_bf_ref[...] = w
            cols = slice(c * cd, (c + 1) * cd)
            acc_scr[:, cols] = x1_scr[:, cols] + mod(5)[:, cols] * _dot(act_scr[...], w)
            if c == nd - 1:
                y_ref[:, 0, :] = _rms(acc_scr[...], gf_ref[...])


def _post_decode(x, ada, za, sgb, att, w_b, w_o, g_ffn, w_up, fcw, fcb, w_down, g_final, fstate,
                 *, proj_chunk, up_chunk, down_chunk):
    n = x.shape[0]
    assert D_MODEL % proj_chunk == 0 and (2 * D_FF) % up_chunk == 0 and D_MODEL % down_chunk == 0
    assert proj_chunk % LANES == 0 and up_chunk % LANES == 0 and down_chunk % LANES == 0
    nb = no = D_MODEL // proj_chunk
    nu, nd = 2 * D_FF // up_chunk, D_MODEL // down_chunk
    b0, o0, u0, d0 = 0, nb, nb + no, nb + no + nu
    phases = ((b0, nb, proj_chunk), (o0, no, proj_chunk), (u0, nu, up_chunk), (d0, nd, down_chunk))

    def chunk_index(start, count):
        return lambda j: jnp.clip(j - start, 0, count - 1)

    ib, io, iu, idn = (chunk_index(s0, cnt) for s0, cnt, _ in phases)
    const2 = lambda j: (0, 0)
    const3 = lambda j: (0, 0, 0)
    rows = lambda cols: pl.BlockSpec((n, cols), const2, pipeline_mode=pl.Buffered(1))
    state_block = pl.BlockSpec((n, CONV_W - 1, 2 * D_FF), const3, pipeline_mode=pl.Buffered(1))
    token_block = pl.BlockSpec((n, 1, D_MODEL), const3, pipeline_mode=pl.Buffered(1))
    col_chunk = lambda k, width, idx: pl.BlockSpec((k, width), lambda j: (0, idx(j)))
    w_down_block = col_chunk(D_FF, down_chunk, idn)
    return pl.pallas_call(
        functools.partial(_post_decode_kernel, phases=phases),
        grid=(d0 + nd,),
        in_specs=[token_block, rows(N_MOD * D_MODEL), rows(D_MODEL), rows(D_MODEL), rows(N_HEADS * HEAD_DIM),
                  col_chunk(N_HEADS * HEAD_DIM, proj_chunk, ib), col_chunk(D_MODEL, proj_chunk, io),
                  pl.BlockSpec((1, D_MODEL), const2), col_chunk(D_MODEL, up_chunk, iu),
                  pl.BlockSpec((CONV_W, 2 * D_FF), const2), pl.BlockSpec((1, 2 * D_FF), const2),
                  w_down_block, pl.BlockSpec((1, D_MODEL), const2), state_block],
        out_specs=[col_chunk(N_HEADS * HEAD_DIM, proj_chunk, ib), col_chunk(D_MODEL, proj_chunk, io),
                   col_chunk(D_MODEL, up_chunk, iu), w_down_block, token_block, state_block],
        out_shape=[jax.ShapeDtypeStruct((N_HEADS * HEAD_DIM, D_MODEL), bf16),
                   jax.ShapeDtypeStruct((D_MODEL, D_MODEL), bf16),
                   jax.ShapeDtypeStruct((D_MODEL, 2 * D_FF), bf16),
                   jax.ShapeDtypeStruct((D_FF, D_MODEL), bf16),
                   jax.ShapeDtypeStruct((n, 1, D_MODEL), f32),
                   jax.ShapeDtypeStruct((n, CONV_W - 1, 2 * D_FF), f32)],
        scratch_shapes=[pltpu.VMEM((n, D_MODEL), f32), pltpu.VMEM((n, D_MODEL), f32), pltpu.VMEM((n, D_MODEL), bf16),
                        pltpu.VMEM((n, 2 * D_FF), f32), pltpu.VMEM((n, D_FF), bf16), pltpu.VMEM((n, D_MODEL), f32)],
        compiler_params=pltpu.CompilerParams(
            dimension_semantics=("arbitrary",),
            vmem_limit_bytes=VMEM_LIMIT_LARGE),
        name="post_decode",
    )(x, ada, za, sgb, att, w_b, w_o, g_ffn, w_up, fcw, fcb, w_down, g_final, fstate)


PROMPT_MIXER_ROWS = 512
PROMPT_FFN_ROWS = 1024
PROMPT_FFN_COLS = 256
PROMPT_FFN_DOWN_COLS = 1024
DECODE_ATTN_BATCH = 32
ADA_PER_STEP = 2
DECODE_IN_CHUNK = 1664
DECODE_PROJ_CHUNK = 512
DECODE_UP_CHUNK = 512
DECODE_DOWN_CHUNK = 256


def kernel(x_prompt, x_sample, c_prompt, c_sample, state_conv_a, cache_k_win, cache_v_win, state_ffn_conv, w_ada, b_ada, g_mix, w_in, conv_a_w, attn_sinks, w_a_out, w_b_out, w_o, g_ffn, w_up, ffn_conv_w, ffn_conv_b, w_down, g_final):
    depth = w_in.shape[0]
    n_p, seq, _ = x_prompt.shape
    n_s, t_s, _ = x_sample.shape
    assert n_p == 1 and t_s == 1, "one prompt sequence and single-token decode only"
    xp = x_prompt.reshape(seq, D_MODEL)
    xs = x_sample
    gf = g_final.reshape(1, D_MODEL)
    outs = [[] for _ in range(8)]
    for l in range(depth):
        ada_s, ada_p = _ada(c_sample, c_prompt, w_ada[l], b_ada[l], per_step=ADA_PER_STEP)
        gm, gn = g_mix[l].reshape(1, D_MODEL), g_ffn[l].reshape(1, D_MODEL)
        fcb = ffn_conv_b[l].reshape(1, 2 * D_FF)

        w_in_b, w_a_b, q, k_n, v_n, k_nt, v_nt, za, sgb, conv_s = _mixer_decode_pre(
            xs, ada_s, gm, w_in[l], conv_a_w[l], state_conv_a[l], w_a_out[l], chunk=DECODE_IN_CHUNK)
        to_native = lambda c: c.transpose(0, 2, 3, 1)
        from_native = lambda c: c.transpose(0, 3, 1, 2)
        sink_b = jnp.broadcast_to(attn_sinks[l].reshape(N_KV, GROUP, 1), (N_KV, GROUP, LANES))
        att4, k_s, v_s = _attn_decode(
            q.reshape(n_s, N_KV, GROUP, HEAD_DIM),
            k_n.reshape(n_s, N_KV, 1, HEAD_DIM), v_n.reshape(n_s, N_KV, 1, HEAD_DIM),
            k_nt.reshape(N_KV, HEAD_DIM, n_s), v_nt.reshape(N_KV, HEAD_DIM, n_s),
            to_native(cache_k_win[l]), to_native(cache_v_win[l]), sink_b, bb=DECODE_ATTN_BATCH)
        k_s, v_s = from_native(k_s), from_native(v_s)
        w_b_b, w_o_b, w_up_b, w_down_b, xs, ffn_s = _post_decode(
            xs, ada_s, za, sgb, att4.reshape(n_s, N_HEADS * HEAD_DIM), w_b_out[l], w_o[l], gn, w_up[l],
            ffn_conv_w[l], fcb, w_down[l], gf, state_ffn_conv[l],
            proj_chunk=DECODE_PROJ_CHUNK, up_chunk=DECODE_UP_CHUNK, down_chunk=DECODE_DOWN_CHUNK)

        x1, conv_p, k_p, v_p = _mixer_prompt(xp, ada_p, gm, w_in_b, conv_a_w[l], attn_sinks[l],
                                             w_a_b, w_b_b, w_o_b, tb=PROMPT_MIXER_ROWS)
        xp, ffn_p = _ffn_prompt(x1, ada_p, gn, w_up_b, ffn_conv_w[l], fcb, w_down_b, gf,
                                tb=PROMPT_FFN_ROWS, ch=PROMPT_FFN_COLS, nw=PROMPT_FFN_DOWN_COLS)

        for lst, val in zip(outs, (
                conv_p.reshape(n_p, CONV_W - 1, D_CONV), conv_s,
                from_native(k_p.reshape(n_p, N_KV, HEAD_DIM, WINDOW)), k_s,
                from_native(v_p.reshape(n_p, N_KV, HEAD_DIM, WINDOW)), v_s,
                ffn_p.reshape(n_p, CONV_W - 1, 2 * D_FF), ffn_s)):
            lst.append(val)
    assert depth == 1, "final RMSNorm is fused into the single layer's FFN kernels"
    return (xp.reshape(n_p, seq, D_MODEL), xs) + tuple(jnp.stack(o) for o in outs)
```
